```python
import math
import jax, jax.numpy as jnp
from jax import lax
import numpy as np

D_MODEL = 1024
BATCH = 16
SEQ = 2048
DEPTH = 1

POOL_GROUPS = 4
POOL_WINDOWS = (2, 4, 8, 16)
POOL_GROUP_DIM = D_MODEL // 8
POOL_WIDTH = POOL_GROUPS * POOL_GROUP_DIM
ATTN_HEADS = 8
HEAD_DIM = D_MODEL // 16
ATTN_WIDTH = ATTN_HEADS * HEAD_DIM
MOBA_BLOCK = 256
MOBA_TOPK = 3
Q_CHUNK = 128
REL_BUCKETS = 32
REL_MAX_DIST = 128
MEM_LEN = 256
MEM_HEADS = 4
MEM_HEAD_DIM = D_MODEL // 8
MEM_WIDTH = MEM_HEADS * MEM_HEAD_DIM
MOE_GROUPS = 4
MOE_EXPERTS_PER_GROUP = 8
MOE_EXPERTS = MOE_GROUPS * MOE_EXPERTS_PER_GROUP
MOE_TOPK = 2
EXPERT_FF = D_MODEL // 4
N_BRANCH = 2
IN_WIDTH = POOL_WIDTH + 3 * ATTN_WIDTH + N_BRANCH * D_MODEL
DN_ALPHA = (2.0 * DEPTH) ** 0.25
DN_BETA = (8.0 * DEPTH) ** -0.25
LN_EPS = 1e-5

kernel_name = "hybrid_pool_moba_hmoe_deepnorm"


def layer_norm(x, g, b):
    xf = x.astype(jnp.float32)
    mu = jnp.mean(xf, axis=-1, keepdims=True)
    var = jnp.mean(jnp.square(xf - mu), axis=-1, keepdims=True)
    return ((xf - mu) * lax.rsqrt(var + LN_EPS)).astype(x.dtype) * g + b


def rel_bucket(dist):
    max_exact = REL_BUCKETS // 2
    d = jnp.maximum(dist, 0)
    large = max_exact + (jnp.log(jnp.maximum(d, 1).astype(jnp.float32) / max_exact)
                         / math.log(REL_MAX_DIST / max_exact) * (REL_BUCKETS - max_exact)).astype(jnp.int32)
    large = jnp.minimum(large, REL_BUCKETS - 1)
    return jnp.where(d < max_exact, d, large)


def pool_mixer(u, w_grp, scale):
    B, S, C = u.shape
    win = jnp.repeat(jnp.array(POOL_WINDOWS, jnp.int32), POOL_GROUP_DIM)
    uf = u.astype(jnp.float32)
    cs = jnp.pad(jnp.cumsum(uf, axis=1), ((0, 0), (1, 0), (0, 0)))
    t = jnp.arange(S, dtype=jnp.int32)[:, None]
    lo = jnp.maximum(t + 1 - win[None, :], 0)
    win_sum = cs[:, 1:, :] - cs[:, lo, jnp.arange(C)[None, :]]
    count = jnp.minimum(t + 1, win[None, :]).astype(jnp.float32)
    y = (win_sum / count - uf).astype(u.dtype)
    y = jnp.einsum('bsgc,gce->bsge', y.reshape(B, S, POOL_GROUPS, POOL_GROUP_DIM), w_grp)
    return y.reshape(B, S, C) * scale


def moba_attention(q, k, v, rel_table):
    B, S, H, Dh = q.shape
    nb = -(-S // MOBA_BLOCK)
    sp = nb * MOBA_BLOCK
    pad = ((0, 0), (0, sp - S), (0, 0), (0, 0))
    q, k, v = [jnp.pad(a, pad).transpose(0, 2, 1, 3) for a in (q, k, v)]
    kb = k.reshape(B, H, nb, MOBA_BLOCK, Dh)
    vb = v.reshape(B, H, nb, MOBA_BLOCK, Dh)
    n_sel = min(MOBA_TOPK, nb - 1)
    qblk = jnp.arange(sp, dtype=jnp.int32) // MOBA_BLOCK
    if n_sel > 0:
        kbar = jnp.mean(kb.astype(jnp.float32), axis=3)
        gate = jnp.einsum('bhsd,bhnd->bhsn', q.astype(jnp.float32), kbar)
        past = jnp.arange(nb, dtype=jnp.int32)[None, :] < qblk[:, None]
        gate = jnp.where(past, gate, -jnp.inf)
        _, sel = lax.top_k(gate, n_sel)
        sel = sel.astype(jnp.int32)
    else:
        sel = jnp.zeros((B, H, sp, 0), jnp.int32)
    table_h = rel_table.T
    scale = HEAD_DIM ** -0.5
    n_chunks = sp // Q_CHUNK
    hidx = jnp.arange(H)[:, None, None]
    l_idx = jnp.arange(MOBA_BLOCK, dtype=jnp.int32)

    def one_batch(args):
        qb_, kb_, vb_, sel_ = args

        def one_chunk(c):
            start = c * Q_CHUNK
            qc = lax.dynamic_slice_in_dim(qb_, start, Q_CHUNK, axis=1)
            sc = lax.dynamic_slice_in_dim(sel_, start, Q_CHUNK, axis=1)
            tq = start + jnp.arange(Q_CHUNK, dtype=jnp.int32)
            blk = start // MOBA_BLOCK
            ks = kb_[hidx, sc]
            vs = vb_[hidx, sc]
            kpos_s = sc[..., None] * MOBA_BLOCK + l_idx
            bias_s = table_h[hidx[..., None], rel_bucket(tq[None, :, None, None] - kpos_s)]
            ls = jnp.einsum('hqd,hqnld->hqnl', qc, ks).astype(jnp.float32) * scale + bias_s
            ls = jnp.where((sc < blk)[..., None], ls, -jnp.inf).reshape(H, Q_CHUNK, n_sel * MOBA_BLOCK)
            ko = lax.dynamic_index_in_dim(kb_, blk, axis=1, keepdims=False)
            vo = lax.dynamic_index_in_dim(vb_, blk, axis=1, keepdims=False)
            dist_o = tq[:, None] - (blk * MOBA_BLOCK + l_idx)[None, :]
            bias_o = table_h[:, rel_bucket(dist_o)]
            lo = jnp.einsum('hqd,hld->hql', qc, ko).astype(jnp.float32) * scale + bias_o
            lo = jnp.where(dist_o[None] >= 0, lo, -jnp.inf)
            p = jax.nn.softmax(jnp.concatenate([ls, lo], axis=-1), axis=-1).astype(qc.dtype)
            ps = p[..., :n_sel * MOBA_BLOCK].reshape(H, Q_CHUNK, n_sel, MOBA_BLOCK)
            po = p[..., n_sel * MOBA_BLOCK:]
            return jnp.einsum('hqnl,hqnld->hqd', ps, vs) + jnp.einsum('hql,hld->hqd', po, vo)

        outs = lax.map(one_chunk, jnp.arange(n_chunks, dtype=jnp.int32))
        return outs.transpose(1, 0, 2, 3).reshape(H, sp, Dh)

    out = lax.map(one_batch, (q, kb, vb, sel))
    return out.transpose(0, 2, 1, 3)[:, :S].reshape(B, S, H * Dh)


def hybrid_mixer(h, w_in, b_gate, w_pool_grp, pool_scale, w_pool_up, w_attn_up, w_out, rel_table):
    B, S, _ = h.shape
    z = h @ w_in
    P, A = POOL_WIDTH, ATTN_WIDTH
    u, q, k, v, gl = jnp.split(z, [P, P + A, P + 2 * A, P + 3 * A], axis=-1)
    gates = jax.nn.sigmoid(gl + b_gate).reshape(B, S, N_BRANCH, D_MODEL)
    y_pool = pool_mixer(u, w_pool_grp, pool_scale) @ w_pool_up
    hd = (B, S, ATTN_HEADS, HEAD_DIM)
    y_attn = moba_attention(q.reshape(hd), k.reshape(hd), v.reshape(hd), rel_table) @ w_attn_up
    merged = gates[:, :, 0] * y_pool + gates[:, :, 1] * y_attn
    return merged @ w_out


def memory_attention(h, mem, wq, wk, wv, wo):
    B, S, _ = h.shape
    M = mem.shape[1]
    q = (h @ wq).reshape(B, S, MEM_HEADS, MEM_HEAD_DIM)
    k = (mem @ wk).reshape(B, M, MEM_HEADS, MEM_HEAD_DIM)
    v = (mem @ wv).reshape(B, M, MEM_HEADS, MEM_HEAD_DIM)
    logits = jnp.einsum('bshd,bmhd->bhsm', q, k).astype(jnp.float32) * (MEM_HEAD_DIM ** -0.5)
    p = jax.nn.softmax(logits, axis=-1).astype(h.dtype)
    o = jnp.einsum('bhsm,bmhd->bshd', p, v).reshape(B, S, MEM_WIDTH)
    return o @ wo


def hier_moe(h, w_coarse, b_coarse, w_fine, b_fine, w_gate, w_up, w_down):
    B, S, _ = h.shape
    hf = h.astype(jnp.float32)
    coarse = jax.nn.softmax(jnp.einsum('bsd,dg->bsg', hf, w_coarse) + b_coarse, axis=-1)
    g_prob, g_idx = lax.top_k(coarse, 1)
    fine_logits = jnp.einsum('bsd,dge->bsge', hf, w_fine) + b_fine
    fine_sel = jnp.take_along_axis(fine_logits, g_idx[..., None], axis=2)[:, :, 0]
    e_prob, e_idx = lax.top_k(jax.nn.softmax(fine_sel, axis=-1), MOE_TOPK)
    e_prob = e_prob / jnp.sum(e_prob, axis=-1, keepdims=True)
    w_within = jnp.sum(jax.nn.one_hot(e_idx, MOE_EXPERTS_PER_GROUP) * e_prob[..., None], axis=-2)
    combine = jax.nn.one_hot(g_idx[..., 0], MOE_GROUPS)[..., None] * (g_prob * w_within)[..., None, :]
    combine = combine.reshape(B, S, MOE_EXPERTS).astype(h.dtype)

    def per_row(args):
        xr, cr = args
        a = jnp.einsum('sd,edf->sef', xr, w_gate)
        b = jnp.einsum('sd,edf->sef', xr, w_up)
        hid = jax.nn.silu(a) * b * cr[..., None]
        return jnp.einsum('sef,efd->sd', hid, w_down)

    return lax.map(per_row, (h, combine))


def setup_inputs(seed: int = 0) -> dict:
    key = jax.random.key(seed)
    ks = iter(jax.random.split(key, 40))
    f32 = jnp.float32
    L, D = DEPTH, D_MODEL

    def nrm(shape, s):
        return jax.random.normal(next(ks), shape, f32) * s

    def gain(shape):
        return 1.0 + nrm(shape, 0.02)

    x = nrm((BATCH, SEQ, D), 1.0)
    mem = nrm((BATCH, MEM_LEN, D), 1.0)
    col_scale = jnp.concatenate([
        jnp.ones((POOL_WIDTH + 2 * ATTN_WIDTH,), f32),
        jnp.full((ATTN_WIDTH,), DN_BETA, f32),
        jnp.ones((N_BRANCH * D,), f32)])
    w_in = nrm((L, D, IN_WIDTH), D ** -0.5) * col_scale
    return {
        "x": x,
        "mem": mem,
        "ln_in_g": gain((D,)),
        "ln_in_b": nrm((D,), 0.02),
        "rel_bias": nrm((REL_BUCKETS, ATTN_HEADS), 0.2),
        "w_in": w_in,
        "b_gate": nrm((L, N_BRANCH * D), 0.02),
        "w_pool_grp": nrm((L, POOL_GROUPS, POOL_GROUP_DIM, POOL_GROUP_DIM), POOL_GROUP_DIM ** -0.5),
        "pool_scale": 1.0 + nrm((L, POOL_WIDTH), 0.1),
        "w_pool_up": nrm((L, POOL_WIDTH, D), POOL_WIDTH ** -0.5),
        "w_attn_up": nrm((L, ATTN_WIDTH, D), ATTN_WIDTH ** -0.5),
        "w_mix_out": nrm((L, D, D), D ** -0.5 * DN_BETA),
        "ln1_g": gain((L, D)),
        "ln1_b": nrm((L, D), 0.02),
        "w_mq": nrm((L, D, MEM_WIDTH), D ** -0.5),
        "w_mk": nrm((L, D, MEM_WIDTH), D ** -0.5),
        "w_mv": nrm((L, D, MEM_WIDTH), D ** -0.5 * DN_BETA),
        "w_mo": nrm((L, MEM_WIDTH, D), MEM_WIDTH ** -0.5 * DN_BETA),
        "ln2_g": gain((L, D)),
        "ln2_b": nrm((L, D), 0.02),
        "w_coarse": nrm((L, D, MOE_GROUPS), D ** -0.5),
        "b_coarse": nrm((L, MOE_GROUPS), 0.01),
        "w_fine": nrm((L, D, MOE_GROUPS, MOE_EXPERTS_PER_GROUP), D ** -0.5),
        "b_fine": nrm((L, MOE_GROUPS, MOE_EXPERTS_PER_GROUP), 0.01),
        "w_gate": nrm((L, MOE_EXPERTS, D, EXPERT_FF), D ** -0.5),
        "w_up": nrm((L, MOE_EXPERTS, D, EXPERT_FF), D ** -0.5),
        "w_down": nrm((L, MOE_EXPERTS, EXPERT_FF, D), EXPERT_FF ** -0.5 * DN_BETA),
        "ln3_g": gain((L, D)),
        "ln3_b": nrm((L, D), 0.02),
    }


def reference(x, mem, ln_in_g, ln_in_b, rel_bias, w_in, b_gate, w_pool_grp, pool_scale, w_pool_up, w_attn_up,
              w_mix_out, ln1_g, ln1_b, w_mq, w_mk, w_mv, w_mo, ln2_g, ln2_b, w_coarse, b_coarse, w_fine, b_fine,
              w_gate, w_up, w_down, ln3_g, ln3_b):
    h = layer_norm(x, ln_in_g, ln_in_b)
    for l in range(DEPTH):
        mix = hybrid_mixer(h, w_in[l], b_gate[l], w_pool_grp[l], pool_scale[l], w_pool_up[l], w_attn_up[l],
                           w_mix_out[l], rel_bias)
        h = layer_norm(DN_ALPHA * h + mix, ln1_g[l], ln1_b[l])
        xa = memory_attention(h, mem, w_mq[l], w_mk[l], w_mv[l], w_mo[l])
        h = layer_norm(DN_ALPHA * h + xa, ln2_g[l], ln2_b[l])
        ff = hier_moe(h, w_coarse[l], b_coarse[l], w_fine[l], b_fine[l], w_gate[l], w_up[l], w_down[l])
        h = layer_norm(DN_ALPHA * h + ff, ln3_g[l], ln3_b[l])
    return h
```

```python
import functools
import math

import jax
import jax.numpy as jnp
from jax import lax
from jax.experimental import pallas as pl
from jax.experimental.pallas import tpu as pltpu

D = 1024
POOL_WINDOWS = (2, 4, 8, 16)
POOL_W = 512
N_HEADS = 8
HEAD_DIM = 64
ATTN_W = 512
BLK = 256
N_BLK = 8
TOPK = 3
REL_BUCKETS = 32
REL_MAX_DIST = 128
MEM_HEADS = 4
MEM_HD = 128
MEM_W = 512
N_GROUPS = 4
EPG = 8
N_EXPERTS = 32
FF = 256
DN_ALPHA = 2.0 ** 0.25
LN_EPS = 1e-5

LANE = 128
HALO = 16
NEG = -1e30
VMEM_LIMIT = 56 * 1024 * 1024

F32 = jnp.float32
BF16 = jnp.bfloat16

_NT = (((1,), (1,)), ((), ()))


def _dot(a, b):
    return jnp.dot(a, b, preferred_element_type=F32)


def _dot_nt(a, b):
    return lax.dot_general(a, b, _NT, preferred_element_type=F32)


def _split(a):
    hi = a.astype(BF16)
    lo = (a - hi.astype(F32)).astype(BF16)
    return hi, lo


def _ln(x, g, b):
    mu = jnp.mean(x, axis=-1, keepdims=True)
    xc = x - mu
    var = jnp.mean(xc * xc, axis=-1, keepdims=True)
    return xc * lax.rsqrt(var + LN_EPS) * g + b


def _proj_kernel(x_ref, g_ref, b_ref, w_ref, wgrp_ref, pscale_ref,
                 ypool_ref, q_ref, k_ref, v_ref, ubuf, kbt, *, tm):
    s = pl.program_id(1)
    nb_tile = tm // BLK

    h = _ln(x_ref[0], g_ref[...], b_ref[...])
    hb = h.astype(BF16)
    zu = _dot(hb, w_ref[:, 0:POOL_W])
    zq = _dot(hb, w_ref[:, POOL_W:POOL_W + N_HEADS * LANE])
    zk = _dot(hb, w_ref[:, POOL_W + N_HEADS * LANE:POOL_W + 2 * N_HEADS * LANE])
    zv = _dot(hb, w_ref[:, POOL_W + 2 * N_HEADS * LANE:])

    @pl.when(s == 0)
    def _():
        ubuf[0:HALO, :] = jnp.zeros((HALO, POOL_W), F32)
        kbt[...] = jnp.zeros_like(kbt)

    ubuf[HALO:HALO + tm, :] = zu
    t_pos = s * tm + lax.broadcasted_iota(jnp.int32, (tm, LANE), 0)
    for g, w in enumerate(POOL_WINDOWS):
        cols = slice(g * LANE, (g + 1) * LANE)
        ws = ubuf[HALO:HALO + tm, cols]
        for kk in range(1, w):
            ws = ws + ubuf[HALO - kk:HALO - kk + tm, cols]
        cnt = jnp.minimum(t_pos + 1, w).astype(F32)
        y = ws / cnt - ubuf[HALO:HALO + tm, cols]
        yg = _dot(y.astype(BF16), wgrp_ref[g]) * pscale_ref[:, cols]
        ypool_ref[0, :, cols] = yg.astype(ypool_ref.dtype)
    ubuf[0:HALO, :] = ubuf[tm:tm + HALO, :]

    r_io = lax.broadcasted_iota(jnp.int32, kbt.shape, 0)
    c_io = lax.broadcasted_iota(jnp.int32, kbt.shape, 1)
    head_match = (r_io >> 3) == (c_io >> 7)
    for bi in range(nb_tile):
        n = s * nb_tile + bi
        kmean = jnp.mean(zk[bi * BLK:(bi + 1) * BLK], axis=0, keepdims=True)
        kbt[...] = jnp.where(head_match & ((r_io & 7) == n), kmean, kbt[...])

    q_hi, q_lo = _split(zq)
    kb_hi, kb_lo = _split(kbt[...])
    gate = _dot_nt(q_hi, kb_hi) + _dot_nt(q_lo, kb_hi) + _dot_nt(q_hi, kb_lo)

    lane = lax.broadcasted_iota(jnp.int32, (tm, LANE), 1)
    row = lax.broadcasted_iota(jnp.int32, (tm, LANE), 0)
    n_l = lane & 7
    jrow = s * nb_tile + (row >> 8)
    past = n_l < jrow
    gt = jnp.where(past, gate, -jnp.inf)
    cnt = jnp.zeros((tm, LANE), F32)
    for sh in range(1, N_BLK):
        wrap = (n_l + sh) >= N_BLK
        gm = jnp.where(wrap, pltpu.roll(gt, N_BLK - sh, 1), pltpu.roll(gt, LANE - sh, 1))
        cnt = cnt + jnp.where(wrap, jnp.where(gm >= gt, 1.0, 0.0), jnp.where(gm > gt, 1.0, 0.0))
    keep = (past & (cnt < TOPK)) | (n_l == jrow)
    negmask = jnp.where(keep, 0.0, NEG)

    aug_lane = (lane >= HEAD_DIM) & (lane < HEAD_DIM + N_BLK)
    k_onehot = jnp.where(lane == HEAD_DIM + jrow, 1.0, 0.0)
    for hh in range(N_HEADS):
        cols = slice(hh * LANE, (hh + 1) * LANE)
        m_h = jnp.where(aug_lane, pltpu.roll(negmask, HEAD_DIM - N_BLK * hh, 1), 0.0)
        q_ref[0, hh] = (zq[:, cols] + m_h).astype(q_ref.dtype)
        k_ref[0, hh] = (zk[:, cols] + k_onehot).astype(k_ref.dtype)
    for p in range(ATTN_W // LANE):
        v_ref[0, p] = zv[:, p * LANE:(p + 1) * LANE].astype(v_ref.dtype)


def _attn_kernel(q_ref, k_ref, v_ref, town_ref, tadj_ref, o_ref):
    j = pl.program_id(1)
    lane = lax.broadcasted_iota(jnp.int32, (BLK, LANE), 1)

    def one_head(h, p, jj):
        q = q_ref[0, h]
        own0 = jj * BLK
        pieces = []
        s_own = _dot_nt(q, k_ref[0, h, own0:own0 + BLK, :]) + town_ref[h]
        pieces.append((s_own, own0, BLK))
        if jj >= 1:
            s_adj = _dot_nt(q, k_ref[0, h, own0 - BLK:own0, :]) + tadj_ref[h]
            pieces.append((s_adj, own0 - BLK, BLK))
        if jj >= 2:
            s_far = _dot_nt(q, k_ref[0, h, 0:own0 - BLK, :])
            pieces.append((s_far, 0, own0 - BLK))
        m = None
        for sc, _, _ in pieces:
            mm = jnp.max(sc, axis=1, keepdims=True)
            m = mm if m is None else jnp.maximum(m, mm)
        l = None
        acc = None
        for sc, start, size in pieces:
            e = jnp.exp(sc - m)
            ls = jnp.sum(e, axis=1, keepdims=True)
            l = ls if l is None else l + ls
            pv = _dot(e.astype(BF16), v_ref[0, p, start:start + size, :])
            acc = pv if acc is None else acc + pv
        return acc / l

    for jj in range(N_BLK):
        @pl.when(j == jj)
        def _(jj=jj):
            def pair(p, carry):
                o0 = one_head(2 * p, p, jj)
                o1 = one_head(2 * p + 1, p, jj)
                o_ref[0, p] = jnp.where(lane < HEAD_DIM, o0, o1).astype(o_ref.dtype)
                return carry
            lax.fori_loop(0, N_HEADS // 2, pair, 0)


def _memkv_kernel(mem_ref, wk_ref, wv_ref, k_ref, v_ref):
    mb = mem_ref[0].astype(BF16)
    k_ref[0] = _dot(mb, wk_ref[...]).astype(k_ref.dtype)
    v_ref[0] = _dot(mb, wv_ref[...]).astype(v_ref.dtype)


def _merge_kernel(x_ref, ypool_ref, o_ref, kmem_ref, vmem_ref,
                  lng_ref, lnb_ref, wgl_ref, bgate_ref, wpu_ref, wau_ref, wout_ref,
                  ln1g_ref, ln1b_ref, wmq_ref, wmo_ref, ln2g_ref, ln2b_ref,
                  wrh_ref, wrl_ref, br_ref,
                  h2_ref, h2b_ref, comb_ref, *, tm):
    h = _ln(x_ref[0], lng_ref[...], lnb_ref[...])
    hb = h.astype(BF16)
    gl = _dot(hb, wgl_ref[...]) + bgate_ref[...]
    gates = jax.nn.sigmoid(gl)
    y_pool = _dot(ypool_ref[0], wpu_ref[...])
    o_cat = jnp.concatenate([o_ref[0, p] for p in range(ATTN_W // LANE)], axis=1)
    y_attn = _dot(o_cat, wau_ref[...])
    merged = gates[:, 0:D] * y_pool + gates[:, D:2 * D] * y_attn
    mix = _dot(merged.astype(BF16), wout_ref[...])
    h1 = _ln(DN_ALPHA * h + mix, ln1g_ref[...], ln1b_ref[...])

    qm = _dot(h1.astype(BF16), wmq_ref[...]).astype(BF16)
    outs = []
    for hd in range(MEM_HEADS):
        cols = slice(hd * MEM_HD, (hd + 1) * MEM_HD)
        sc = _dot_nt(qm[:, cols], kmem_ref[0, :, cols])
        m = jnp.max(sc, axis=1, keepdims=True)
        e = jnp.exp(sc - m)
        l = jnp.sum(e, axis=1, keepdims=True)
        outs.append(_dot(e.astype(BF16), vmem_ref[0, :, cols]) / l)
    om = jnp.concatenate(outs, axis=1).astype(BF16)
    xa = _dot(om, wmo_ref[...])
    h2 = _ln(DN_ALPHA * h1 + xa, ln2g_ref[...], ln2b_ref[...])
    h2_ref[0] = h2
    h2b_ref[0] = h2.astype(BF16)

    x_hi, x_lo = _split(h2)
    r = _dot(x_hi, wrh_ref[...]) + _dot(x_lo, wrh_ref[...]) + _dot(x_hi, wrl_ref[...]) + br_ref[...]
    lane = lax.broadcasted_iota(jnp.int32, (tm, LANE), 1)
    lane_f = lane.astype(F32)
    cmask = (lane >= N_EXPERTS) & (lane < 2 * N_EXPERTS)
    c = jnp.where(cmask, r, -jnp.inf)
    cmax = jnp.max(c, axis=1, keepdims=True)
    ce = jnp.exp(c - cmax)
    csum = jnp.sum(ce, axis=1, keepdims=True) * (1.0 / EPG)
    g_prob = 1.0 / csum
    grp_lane = ((lane & (N_EXPERTS - 1)) >> 3).astype(F32)
    gidx = jnp.min(jnp.where(cmask & (c == cmax), grp_lane, 99.0), axis=1, keepdims=True)
    fmask = (lane < N_EXPERTS) & (grp_lane == gidx)
    f = jnp.where(fmask, r, -jnp.inf)
    fmax = jnp.max(f, axis=1, keepdims=True)
    fe = jnp.exp(f - fmax)
    fsum = jnp.sum(fe, axis=1, keepdims=True)
    prob = fe / fsum
    p1 = jnp.max(prob, axis=1, keepdims=True)
    i1 = jnp.min(jnp.where(fmask & (prob == p1), lane_f, 999.0), axis=1, keepdims=True)
    rest = fmask & (lane_f != i1)
    prob2 = jnp.where(rest, prob, -1.0)
    p2 = jnp.max(prob2, axis=1, keepdims=True)
    i2 = jnp.min(jnp.where(rest & (prob2 == p2), lane_f, 999.0), axis=1, keepdims=True)
    den = p1 + p2
    comb = jnp.where(lane_f == i1, g_prob * (p1 / den),
                     jnp.where(lane_f == i2, g_prob * (p2 / den), 0.0))
    comb_ref[0] = comb


def _moe_kernel(xb_ref, comb_ref, h2_ref, wg_ref, wu_ref, wd_ref, g_ref, b_ref, out_ref, acc_ref):
    e = pl.program_id(1)

    @pl.when(e == 0)
    def _():
        acc_ref[...] = jnp.zeros_like(acc_ref)

    xb = xb_ref[...]
    a = _dot(xb, wg_ref[0])
    b = _dot(xb, wu_ref[0])
    lane = lax.broadcasted_iota(jnp.int32, comb_ref.shape, 1)
    c = jnp.sum(jnp.where(lane == e, comb_ref[...], 0.0), axis=1, keepdims=True)
    hid = (a * jax.nn.sigmoid(a)) * b * c
    acc_ref[...] += _dot(hid.astype(BF16), wd_ref[0])

    @pl.when(e == pl.num_programs(1) - 1)
    def _():
        out_ref[...] = _ln(DN_ALPHA * h2_ref[...] + acc_ref[...], g_ref[...], b_ref[...])


def _rel_bucket_table(dist):
    max_exact = REL_BUCKETS // 2
    d = jnp.maximum(dist, 0)
    large = max_exact + (jnp.log(jnp.maximum(d, 1).astype(F32) / max_exact)
                         / math.log(REL_MAX_DIST / max_exact) * (REL_BUCKETS - max_exact)).astype(jnp.int32)
    large = jnp.minimum(large, REL_BUCKETS - 1)
    return jnp.where(d < max_exact, d, large)


def _const_spec(shape):
    nd = len(shape)
    return pl.BlockSpec(shape, lambda *_: (0,) * nd)


def kernel(x, mem, ln_in_g, ln_in_b, rel_bias, w_in, b_gate, w_pool_grp, pool_scale, w_pool_up, w_attn_up,
           w_mix_out, ln1_g, ln1_b, w_mq, w_mk, w_mv, w_mo, ln2_g, ln2_b, w_coarse, b_coarse, w_fine, b_fine,
           w_gate, w_up, w_down, ln3_g, ln3_b):
    B, S, _ = x.shape
    assert S == N_BLK * BLK and w_in.shape[0] == 1
    M = mem.shape[1]
    T = B * S
    tm = 512

    wi = w_in[0]
    w_u = wi[:, 0:POOL_W]
    w_q = wi[:, POOL_W:POOL_W + ATTN_W] * (HEAD_DIM ** -0.5)
    w_k = wi[:, POOL_W + ATTN_W:POOL_W + 2 * ATTN_W]
    w_v = wi[:, POOL_W + 2 * ATTN_W:POOL_W + 3 * ATTN_W]
    w_gl = wi[:, POOL_W + 3 * ATTN_W:]

    def pad_heads(w):
        w = w.reshape(D, N_HEADS, HEAD_DIM)
        return jnp.pad(w, ((0, 0), (0, 0), (0, LANE - HEAD_DIM))).reshape(D, N_HEADS * LANE)

    w1 = jnp.concatenate([w_u, pad_heads(w_q), pad_heads(w_k), w_v], axis=1).astype(BF16)
    row2 = lambda a: a.reshape(1, -1)

    iq = jnp.arange(BLK, dtype=jnp.int32)[:, None]
    ik = jnp.arange(BLK, dtype=jnp.int32)[None, :]
    tbl = rel_bias.T
    far = tbl[:, _rel_bucket_table(jnp.int32(2 * BLK))][:, None, None]
    d_own = iq - ik
    t_own = jnp.where(d_own[None] >= 0, tbl[:, _rel_bucket_table(d_own)] - far, NEG)
    t_adj = tbl[:, _rel_bucket_table(d_own + BLK)] - far

    n_w1 = w1.shape[1]
    ypool, q_aug, k_aug, v_p = pl.pallas_call(
        functools.partial(_proj_kernel, tm=tm),
        grid=(B, S // tm),
        in_specs=[
            pl.BlockSpec((1, tm, D), lambda b, s: (b, s, 0)),
            _const_spec((1, D)), _const_spec((1, D)),
            _const_spec((D, n_w1)),
            _const_spec((len(POOL_WINDOWS), LANE, LANE)),
            _const_spec((1, POOL_W)),
        ],
        out_specs=[
            pl.BlockSpec((1, tm, POOL_W), lambda b, s: (b, s, 0)),
            pl.BlockSpec((1, N_HEADS, tm, LANE), lambda b, s: (b, 0, s, 0)),
            pl.BlockSpec((1, N_HEADS, tm, LANE), lambda b, s: (b, 0, s, 0)),
            pl.BlockSpec((1, ATTN_W // LANE, tm, LANE), lambda b, s: (b, 0, s, 0)),
        ],
        out_shape=[
            jax.ShapeDtypeStruct((B, S, POOL_W), BF16),
            jax.ShapeDtypeStruct((B, N_HEADS, S, LANE), BF16),
            jax.ShapeDtypeStruct((B, N_HEADS, S, LANE), BF16),
            jax.ShapeDtypeStruct((B, ATTN_W // LANE, S, LANE), BF16),
        ],
        scratch_shapes=[pltpu.VMEM((HALO + tm, POOL_W), F32), pltpu.VMEM((LANE, N_HEADS * LANE), F32)],
        compiler_params=pltpu.CompilerParams(dimension_semantics=("arbitrary", "arbitrary"),
                                             vmem_limit_bytes=VMEM_LIMIT),
        name="proj_pool_gate",
    )(x, row2(ln_in_g), row2(ln_in_b), w1, w_pool_grp[0].astype(BF16), row2(pool_scale[0]))

    o_attn = pl.pallas_call(
        _attn_kernel,
        grid=(B, N_BLK),
        in_specs=[
            pl.BlockSpec((1, N_HEADS, BLK, LANE), lambda b, j: (b, 0, j, 0)),
            pl.BlockSpec((1, N_HEADS, S, LANE), lambda b, j: (b, 0, 0, 0)),
            pl.BlockSpec((1, ATTN_W // LANE, S, LANE), lambda b, j: (b, 0, 0, 0)),
            _const_spec((N_HEADS, BLK, BLK)),
            _const_spec((N_HEADS, BLK, BLK)),
        ],
        out_specs=pl.BlockSpec((1, ATTN_W // LANE, BLK, LANE), lambda b, j: (b, 0, j, 0)),
        out_shape=jax.ShapeDtypeStruct((B, ATTN_W // LANE, S, LANE), BF16),
        compiler_params=pltpu.CompilerParams(dimension_semantics=("arbitrary", "arbitrary"),
                                             vmem_limit_bytes=VMEM_LIMIT),
        name="moba_attn",
    )(q_aug, k_aug, v_p, t_own, t_adj)

    kmem, vmem = pl.pallas_call(
        _memkv_kernel,
        grid=(B,),
        in_specs=[pl.BlockSpec((1, M, D), lambda b: (b, 0, 0)),
                  _const_spec((D, MEM_W)), _const_spec((D, MEM_W))],
        out_specs=[pl.BlockSpec((1, M, MEM_W), lambda b: (b, 0, 0)),
                   pl.BlockSpec((1, M, MEM_W), lambda b: (b, 0, 0))],
        out_shape=[jax.ShapeDtypeStruct((B, M, MEM_W), BF16)] * 2,
        compiler_params=pltpu.CompilerParams(dimension_semantics=("arbitrary",)),
        name="mem_kv",
    )(mem, w_mk[0].astype(BF16), w_mv[0].astype(BF16))

    w_r = jnp.concatenate([
        w_fine[0].reshape(D, N_EXPERTS),
        jnp.repeat(w_coarse[0], EPG, axis=1),
        jnp.zeros((D, LANE - 2 * N_EXPERTS), F32)], axis=1)
    b_r = jnp.concatenate([
        b_fine[0].reshape(N_EXPERTS), jnp.repeat(b_coarse[0], EPG),
        jnp.zeros((LANE - 2 * N_EXPERTS,), F32)]).reshape(1, LANE)
    w_r_hi = w_r.astype(BF16)
    w_r_lo = (w_r - w_r_hi.astype(F32)).astype(BF16)

    h2, h2b, comb = pl.pallas_call(
        functools.partial(_merge_kernel, tm=tm),
        grid=(B, S // tm),
        in_specs=[
            pl.BlockSpec((1, tm, D), lambda b, s: (b, s, 0)),
            pl.BlockSpec((1, tm, POOL_W), lambda b, s: (b, s, 0)),
            pl.BlockSpec((1, ATTN_W // LANE, tm, LANE), lambda b, s: (b, 0, s, 0)),
            pl.BlockSpec((1, M, MEM_W), lambda b, s: (b, 0, 0)),
            pl.BlockSpec((1, M, MEM_W), lambda b, s: (b, 0, 0)),
            _const_spec((1, D)), _const_spec((1, D)),
            _const_spec((D, 2 * D)), _const_spec((1, 2 * D)),
            _const_spec((POOL_W, D)), _const_spec((ATTN_W, D)), _const_spec((D, D)),
            _const_spec((1, D)), _const_spec((1, D)),
            _const_spec((D, MEM_W)), _const_spec((MEM_W, D)),
            _const_spec((1, D)), _const_spec((1, D)),
            _const_spec((D, LANE)), _const_spec((D, LANE)), _const_spec((1, LANE)),
        ],
        out_specs=[
            pl.BlockSpec((1, tm, D), lambda b, s: (b, s, 0)),
            pl.BlockSpec((1, tm, D), lambda b, s: (b, s, 0)),
            pl.BlockSpec((1, tm, LANE), lambda b, s: (b, s, 0)),
        ],
        out_shape=[
            jax.ShapeDtypeStruct((B, S, D), F32),
            jax.ShapeDtypeStruct((B, S, D), BF16),
            jax.ShapeDtypeStruct((B, S, LANE), F32),
        ],
        compiler_params=pltpu.CompilerParams(dimension_semantics=("arbitrary", "arbitrary"),
                                             vmem_limit_bytes=VMEM_LIMIT),
        name="merge_memattn_router",
    )(x, ypool, o_attn, kmem, vmem,
      row2(ln_in_g), row2(ln_in_b), w_gl.astype(BF16), row2(b_gate[0]),
      w_pool_up[0].astype(BF16), w_attn_up[0].astype(BF16), w_mix_out[0].astype(BF16),
      row2(ln1_g[0]), row2(ln1_b[0]),
      (w_mq[0] * (MEM_HD ** -0.5)).astype(BF16), w_mo[0].astype(BF16),
      row2(ln2_g[0]), row2(ln2_b[0]),
      w_r_hi, w_r_lo, b_r)

    tmd = 1024
    out = pl.pallas_call(
        _moe_kernel,
        grid=(T // tmd, N_EXPERTS),
        in_specs=[
            pl.BlockSpec((tmd, D), lambda t, e: (t, 0)),
            pl.BlockSpec((tmd, LANE), lambda t, e: (t, 0)),
            pl.BlockSpec((tmd, D), lambda t, e: (t, 0)),
            pl.BlockSpec((1, D, FF), lambda t, e: (e, 0, 0)),
            pl.BlockSpec((1, D, FF), lambda t, e: (e, 0, 0)),
            pl.BlockSpec((1, FF, D), lambda t, e: (e, 0, 0)),
            _const_spec((1, D)), _const_spec((1, D)),
        ],
        out_specs=pl.BlockSpec((tmd, D), lambda t, e: (t, 0)),
        out_shape=jax.ShapeDtypeStruct((T, D), F32),
        scratch_shapes=[pltpu.VMEM((tmd, D), F32)],
        compiler_params=pltpu.CompilerParams(dimension_semantics=("arbitrary", "arbitrary"),
                                             vmem_limit_bytes=VMEM_LIMIT),
        name="moe_experts_ln3",
    )(h2b.reshape(T, D), comb.reshape(T, LANE), h2.reshape(T, D),
      w_gate[0].astype(BF16), w_up[0].astype(BF16), w_down[0].astype(BF16),
      row2(ln3_g[0]), row2(ln3_b[0]))
    return out.reshape(B, S, D)
```

```python
import functools
import math

import jax
import jax.numpy as jnp
from jax import lax
from jax.experimental import pallas as pl
from jax.experimental.pallas import tpu as pltpu

D = 1024
POOL_WINDOWS = (2, 4, 8, 16)
POOL_W = 512
N_HEADS = 8
HEAD_DIM = 64
ATTN_W = 512
BLK = 256
N_BLK = 8
TOPK = 3
REL_BUCKETS = 32
REL_MAX_DIST = 128
MEM_HEADS = 4
MEM_HD = 128
MEM_W = 512
N_GROUPS = 4
EPG = 8
N_EXPERTS = 32
FF = 256
DN_ALPHA = 2.0 ** 0.25
LN_EPS = 1e-5

LANE = 128
HALO = 16
NEG = -1e30
VMEM_LIMIT = 56 * 1024 * 1024

F32 = jnp.float32
BF16 = jnp.bfloat16

_NT = (((1,), (1,)), ((), ()))


def _dot(a, b):
    return jnp.dot(a, b, preferred_element_type=F32)


def _dot_nt(a, b):
    return lax.dot_general(a, b, _NT, preferred_element_type=F32)


def _split(a):
    hi = a.astype(BF16)
    lo = (a - hi.astype(F32)).astype(BF16)
    return hi, lo


def _ln(x, g, b):
    mu = jnp.mean(x, axis=-1, keepdims=True)
    xc = x - mu
    var = jnp.mean(xc * xc, axis=-1, keepdims=True)
    return xc * lax.rsqrt(var + LN_EPS) * g + b


def _proj_kernel(x_ref, g_ref, b_ref, w_ref, wgrp_ref, pscale_ref,
                 ypool_ref, q_ref, k_ref, v_ref, ubuf, kbt, *, tm):
    s = pl.program_id(1)
    nb_tile = tm // BLK

    h = _ln(x_ref[0], g_ref[...], b_ref[...])
    hb = h.astype(BF16)
    zu = _dot(hb, w_ref[:, 0:POOL_W])
    zq = _dot(hb, w_ref[:, POOL_W:POOL_W + N_HEADS * LANE])
    zk = _dot(hb, w_ref[:, POOL_W + N_HEADS * LANE:POOL_W + 2 * N_HEADS * LANE])
    zv = _dot(hb, w_ref[:, POOL_W + 2 * N_HEADS * LANE:])

    @pl.when(s == 0)
    def _():
        ubuf[0:HALO, :] = jnp.zeros((HALO, POOL_W), F32)
        kbt[...] = jnp.zeros_like(kbt)

    ubuf[HALO:HALO + tm, :] = zu
    t_pos = s * tm + lax.broadcasted_iota(jnp.int32, (tm, LANE), 0)
    for g, w in enumerate(POOL_WINDOWS):
        cols = slice(g * LANE, (g + 1) * LANE)
        ws = ubuf[HALO:HALO + tm, cols]
        for kk in range(1, w):
            ws = ws + ubuf[HALO - kk:HALO - kk + tm, cols]
        cnt = jnp.minimum(t_pos + 1, w).astype(F32)
        y = ws / cnt - ubuf[HALO:HALO + tm, cols]
        yg = _dot(y.astype(BF16), wgrp_ref[g]) * pscale_ref[:, cols]
        ypool_ref[0, :, cols] = yg.astype(ypool_ref.dtype)
    ubuf[0:HALO, :] = ubuf[tm:tm + HALO, :]

    r_io = lax.broadcasted_iota(jnp.int32, kbt.shape, 0)
    c_io = lax.broadcasted_iota(jnp.int32, kbt.shape, 1)
    head_match = (r_io >> 3) == (c_io >> 7)
    for bi in range(nb_tile):
        n = s * nb_tile + bi
        kmean = jnp.mean(zk[bi * BLK:(bi + 1) * BLK], axis=0, keepdims=True)
        kbt[...] = jnp.where(head_match & ((r_io & 7) == n), kmean, kbt[...])

    q_hi, q_lo = _split(zq)
    kb_hi, kb_lo = _split(kbt[...])
    gate = _dot_nt(q_hi, kb_hi) + _dot_nt(q_lo, kb_hi) + _dot_nt(q_hi, kb_lo)

    lane = lax.broadcasted_iota(jnp.int32, (tm, LANE), 1)
    row = lax.broadcasted_iota(jnp.int32, (tm, LANE), 0)
    n_l = lane & 7
    jrow = s * nb_tile + (row >> 8)
    past = n_l < jrow
    gt = jnp.where(past, gate, -jnp.inf)
    cnt = jnp.zeros((tm, LANE), F32)
    for sh in range(1, N_BLK):
        wrap = (n_l + sh) >= N_BLK
        gm = jnp.where(wrap, pltpu.roll(gt, N_BLK - sh, 1), pltpu.roll(gt, LANE - sh, 1))
        cnt = cnt + jnp.where(wrap, jnp.where(gm >= gt, 1.0, 0.0), jnp.where(gm > gt, 1.0, 0.0))
    keep = (past & (cnt < TOPK)) | (n_l == jrow)
    negmask = jnp.where(keep, 0.0, NEG)

    aug_lane = (lane >= HEAD_DIM) & (lane < HEAD_DIM + N_BLK)
    k_onehot = jnp.where(lane == HEAD_DIM + jrow, 1.0, 0.0)
    for hh in range(N_HEADS):
        cols = slice(hh * LANE, (hh + 1) * LANE)
        m_h = jnp.where(aug_lane, pltpu.roll(negmask, HEAD_DIM - N_BLK * hh, 1), 0.0)
        q_ref[0, hh] = (zq[:, cols] + m_h).astype(q_ref.dtype)
        k_ref[0, hh] = (zk[:, cols] + k_onehot).astype(k_ref.dtype)
    for p in range(ATTN_W // LANE):
        v_ref[0, p] = zv[:, p * LANE:(p + 1) * LANE].astype(v_ref.dtype)


def _attn_kernel(q_ref, k_ref, v_ref, tb_ref, o_ref):
    j = pl.program_id(1)
    lane = lax.broadcasted_iota(jnp.int32, (BLK, LANE), 1)

    def one_head(h, p, jj):
        q = q_ref[0, h]
        own0 = jj * BLK
        pieces = []
        s_own = _dot_nt(q, k_ref[0, h, own0:own0 + BLK, :]) + tb_ref[0, h]
        pieces.append((s_own, own0, BLK))
        if jj >= 1:
            s_adj = _dot_nt(q, k_ref[0, h, own0 - BLK:own0, :]) + tb_ref[1, h]
            pieces.append((s_adj, own0 - BLK, BLK))
        if jj >= 2:
            s_far = _dot_nt(q, k_ref[0, h, 0:own0 - BLK, :])
            pieces.append((s_far, 0, own0 - BLK))
        m = None
        for sc, _, _ in pieces:
            mm = jnp.max(sc, axis=1, keepdims=True)
            m = mm if m is None else jnp.maximum(m, mm)
        l = None
        acc = None
        for sc, start, size in pieces:
            e = jnp.exp(sc - m)
            ls = jnp.sum(e, axis=1, keepdims=True)
            l = ls if l is None else l + ls
            pv = _dot(e.astype(BF16), v_ref[0, p, start:start + size, :])
            acc = pv if acc is None else acc + pv
        return acc / l

    for jj in range(N_BLK):
        @pl.when(j == jj)
        def _(jj=jj):
            def pair(p, carry):
                o0 = one_head(2 * p, p, jj)
                o1 = one_head(2 * p + 1, p, jj)
                o_ref[0, p] = jnp.where(lane < HEAD_DIM, o0, o1).astype(o_ref.dtype)
                return carry
            lax.fori_loop(0, N_HEADS // 2, pair, 0)


def _memkv_kernel(mem_ref, wk_ref, wv_ref, k_ref, v_ref):
    mb = mem_ref[0].astype(BF16)
    k_ref[0] = _dot(mb, wk_ref[...]).astype(k_ref.dtype)
    v_ref[0] = _dot(mb, wv_ref[...]).astype(v_ref.dtype)


def _merge_kernel(x_ref, ypool_ref, o_ref, kmem_ref, vmem_ref,
                  lng_ref, lnb_ref, wgl_ref, bgate_ref, wpu_ref, wau_ref, wout_ref,
                  ln1g_ref, ln1b_ref, wmq_ref, wmo_ref, ln2g_ref, ln2b_ref,
                  wrh_ref, wrl_ref, br_ref,
                  h2_ref, h2b_ref, comb_ref, *, tm):
    h = _ln(x_ref[0], lng_ref[...], lnb_ref[...])
    hb = h.astype(BF16)
    gl = _dot(hb, wgl_ref[...]) + bgate_ref[...]
    gates = jax.nn.sigmoid(gl)
    y_pool = _dot(ypool_ref[0], wpu_ref[...])
    o_cat = jnp.concatenate([o_ref[0, p] for p in range(ATTN_W // LANE)], axis=1)
    y_attn = _dot(o_cat, wau_ref[...])
    merged = gates[:, 0:D] * y_pool + gates[:, D:2 * D] * y_attn
    mix = _dot(merged.astype(BF16), wout_ref[...])
    h1 = _ln(DN_ALPHA * h + mix, ln1g_ref[...], ln1b_ref[...])

    qm = _dot(h1.astype(BF16), wmq_ref[...]).astype(BF16)
    outs = []
    for hd in range(MEM_HEADS):
        cols = slice(hd * MEM_HD, (hd + 1) * MEM_HD)
        sc = _dot_nt(qm[:, cols], kmem_ref[0, :, cols])
        m = jnp.max(sc, axis=1, keepdims=True)
        e = jnp.exp(sc - m)
        l = jnp.sum(e, axis=1, keepdims=True)
        outs.append(_dot(e.astype(BF16), vmem_ref[0, :, cols]) / l)
    om = jnp.concatenate(outs, axis=1).astype(BF16)
    xa = _dot(om, wmo_ref[...])
    h2 = _ln(DN_ALPHA * h1 + xa, ln2g_ref[...], ln2b_ref[...])
    h2_ref[0] = h2
    h2b_ref[0] = h2.astype(BF16)

    x_hi, x_lo = _split(h2)
    r = _dot(x_hi, wrh_ref[...]) + _dot(x_lo, wrh_ref[...]) + _dot(x_hi, wrl_ref[...]) + br_ref[...]
    lane = lax.broadcasted_iota(jnp.int32, (tm, LANE), 1)
    lane_f = lane.astype(F32)
    cmask = (lane >= N_EXPERTS) & (lane < 2 * N_EXPERTS)
    c = jnp.where(cmask, r, -jnp.inf)
    cmax = jnp.max(c, axis=1, keepdims=True)
    ce = jnp.exp(c - cmax)
    csum = jnp.sum(ce, axis=1, keepdims=True) * (1.0 / EPG)
    g_prob = 1.0 / csum
    grp_lane = ((lane & (N_EXPERTS - 1)) >> 3).astype(F32)
    gidx = jnp.min(jnp.where(cmask & (c == cmax), grp_lane, 99.0), axis=1, keepdims=True)
    fmask = (lane < N_EXPERTS) & (grp_lane == gidx)
    f = jnp.where(fmask, r, -jnp.inf)
    fmax = jnp.max(f, axis=1, keepdims=True)
    fe = jnp.exp(f - fmax)
    fsum = jnp.sum(fe, axis=1, keepdims=True)
    prob = fe / fsum
    p1 = jnp.max(prob, axis=1, keepdims=True)
    i1 = jnp.min(jnp.where(fmask & (prob == p1), lane_f, 999.0), axis=1, keepdims=True)
    rest = fmask & (lane_f != i1)
    prob2 = jnp.where(rest, prob, -1.0)
    p2 = jnp.max(prob2, axis=1, keepdims=True)
    i2 = jnp.min(jnp.where(rest & (prob2 == p2), lane_f, 999.0), axis=1, keepdims=True)
    den = p1 + p2
    comb = jnp.where(lane_f == i1, g_prob * (p1 / den),
                     jnp.where(lane_f == i2, g_prob * (p2 / den), 0.0))
    comb_ref[0] = comb


def _moe_kernel(xb_ref, comb_ref, h2_ref, wg_ref, wu_ref, wd_ref, g_ref, b_ref, out_ref, acc_ref):
    e = pl.program_id(1)

    @pl.when(e == 0)
    def _():
        acc_ref[...] = jnp.zeros_like(acc_ref)

    xb = xb_ref[...]
    a = _dot(xb, wg_ref[0])
    b = _dot(xb, wu_ref[0])
    lane = lax.broadcasted_iota(jnp.int32, comb_ref.shape, 1)
    c = jnp.sum(jnp.where(lane == e, comb_ref[...], 0.0), axis=1, keepdims=True)
    hid = (a * jax.nn.sigmoid(a)) * b * c
    acc_ref[...] += _dot(hid.astype(BF16), wd_ref[0])

    @pl.when(e == pl.num_programs(1) - 1)
    def _():
        out_ref[...] = _ln(DN_ALPHA * h2_ref[...] + acc_ref[...], g_ref[...], b_ref[...])


def _bias_kernel(tbl_ref, bkt_ref, out_ref):
    h = pl.program_id(0)
    far = tbl_ref[h, REL_BUCKETS - 1]
    for which in range(2):
        bk = bkt_ref[which]
        acc = jnp.where(bk < 0, NEG, 0.0)
        for kk in range(REL_BUCKETS):
            acc = jnp.where(bk == kk, tbl_ref[h, kk] - far, acc)
        out_ref[which, 0] = acc


def _rel_bucket_table(dist):
    max_exact = REL_BUCKETS // 2
    d = jnp.maximum(dist, 0)
    large = max_exact + (jnp.log(jnp.maximum(d, 1).astype(F32) / max_exact)
                         / math.log(REL_MAX_DIST / max_exact) * (REL_BUCKETS - max_exact)).astype(jnp.int32)
    large = jnp.minimum(large, REL_BUCKETS - 1)
    return jnp.where(d < max_exact, d, large)


def _const_spec(shape):
    nd = len(shape)
    return pl.BlockSpec(shape, lambda *_: (0,) * nd)


def kernel(x, mem, ln_in_g, ln_in_b, rel_bias, w_in, b_gate, w_pool_grp, pool_scale, w_pool_up, w_attn_up,
           w_mix_out, ln1_g, ln1_b, w_mq, w_mk, w_mv, w_mo, ln2_g, ln2_b, w_coarse, b_coarse, w_fine, b_fine,
           w_gate, w_up, w_down, ln3_g, ln3_b):
    B, S, _ = x.shape
    assert S == N_BLK * BLK and w_in.shape[0] == 1
    M = mem.shape[1]
    T = B * S
    tm = 512

    wi = w_in[0]
    w_u = wi[:, 0:POOL_W]
    w_q = wi[:, POOL_W:POOL_W + ATTN_W] * (HEAD_DIM ** -0.5)
    w_k = wi[:, POOL_W + ATTN_W:POOL_W + 2 * ATTN_W]
    w_v = wi[:, POOL_W + 2 * ATTN_W:POOL_W + 3 * ATTN_W]
    w_gl = wi[:, POOL_W + 3 * ATTN_W:]

    def pad_heads(w):
        w = w.reshape(D, N_HEADS, HEAD_DIM)
        return jnp.pad(w, ((0, 0), (0, 0), (0, LANE - HEAD_DIM))).reshape(D, N_HEADS * LANE)

    w1 = jnp.concatenate([w_u, pad_heads(w_q), pad_heads(w_k), w_v], axis=1).astype(BF16)
    row2 = lambda a: a.reshape(1, -1)

    iq = jnp.arange(BLK, dtype=jnp.int32)[:, None]
    ik = jnp.arange(BLK, dtype=jnp.int32)[None, :]
    d_own = iq - ik
    bkt = jnp.stack([jnp.where(d_own >= 0, _rel_bucket_table(d_own), -1), _rel_bucket_table(d_own + BLK)])
    t_bias = pl.pallas_call(
        _bias_kernel,
        grid=(N_HEADS,),
        in_specs=[pl.BlockSpec(memory_space=pltpu.SMEM), _const_spec((2, BLK, BLK))],
        out_specs=pl.BlockSpec((2, 1, BLK, BLK), lambda h: (0, h, 0, 0)),
        out_shape=jax.ShapeDtypeStruct((2, N_HEADS, BLK, BLK), F32),
        name="relbias_tiles",
    )(rel_bias.T, bkt)

    n_w1 = w1.shape[1]
    ypool, q_aug, k_aug, v_p = pl.pallas_call(
        functools.partial(_proj_kernel, tm=tm),
        grid=(B, S // tm),
        in_specs=[
            pl.BlockSpec((1, tm, D), lambda b, s: (b, s, 0)),
            _const_spec((1, D)), _const_spec((1, D)),
            _const_spec((D, n_w1)),
            _const_spec((len(POOL_WINDOWS), LANE, LANE)),
            _const_spec((1, POOL_W)),
        ],
        out_specs=[
            pl.BlockSpec((1, tm, POOL_W), lambda b, s: (b, s, 0)),
            pl.BlockSpec((1, N_HEADS, tm, LANE), lambda b, s: (b, 0, s, 0)),
            pl.BlockSpec((1, N_HEADS, tm, LANE), lambda b, s: (b, 0, s, 0)),
            pl.BlockSpec((1, ATTN_W // LANE, tm, LANE), lambda b, s: (b, 0, s, 0)),
        ],
        out_shape=[
            jax.ShapeDtypeStruct((B, S, POOL_W), BF16),
            jax.ShapeDtypeStruct((B, N_HEADS, S, LANE), BF16),
            jax.ShapeDtypeStruct((B, N_HEADS, S, LANE), BF16),
            jax.ShapeDtypeStruct((B, ATTN_W // LANE, S, LANE), BF16),
        ],
        scratch_shapes=[pltpu.VMEM((HALO + tm, POOL_W), F32), pltpu.VMEM((LANE, N_HEADS * LANE), F32)],
        compiler_params=pltpu.CompilerParams(dimension_semantics=("arbitrary", "arbitrary"),
                                             vmem_limit_bytes=VMEM_LIMIT),
        name="proj_pool_gate",
    )(x, row2(ln_in_g), row2(ln_in_b), w1, w_pool_grp[0].astype(BF16), row2(pool_scale[0]))

    o_attn = pl.pallas_call(
        _attn_kernel,
        grid=(B, N_BLK),
        in_specs=[
            pl.BlockSpec((1, N_HEADS, BLK, LANE), lambda b, j: (b, 0, j, 0)),
            pl.BlockSpec((1, N_HEADS, S, LANE), lambda b, j: (b, 0, 0, 0)),
            pl.BlockSpec((1, ATTN_W // LANE, S, LANE), lambda b, j: (b, 0, 0, 0)),
            _const_spec((2, N_HEADS, BLK, BLK)),
        ],
        out_specs=pl.BlockSpec((1, ATTN_W // LANE, BLK, LANE), lambda b, j: (b, 0, j, 0)),
        out_shape=jax.ShapeDtypeStruct((B, ATTN_W // LANE, S, LANE), BF16),
        compiler_params=pltpu.CompilerParams(dimension_semantics=("arbitrary", "arbitrary"),
                                             vmem_limit_bytes=VMEM_LIMIT),
        name="moba_attn",
    )(q_aug, k_aug, v_p, t_bias)

    kmem, vmem = pl.pallas_call(
        _memkv_kernel,
        grid=(B,),
        in_specs=[pl.BlockSpec((1, M, D), lambda b: (b, 0, 0)),
                  _const_spec((D, MEM_W)), _const_spec((D, MEM_W))],
        out_specs=[pl.BlockSpec((1, M, MEM_W), lambda b: (b, 0, 0)),
                   pl.BlockSpec((1, M, MEM_W), lambda b: (b, 0, 0))],
        out_shape=[jax.ShapeDtypeStruct((B, M, MEM_W), BF16)] * 2,
        compiler_params=pltpu.CompilerParams(dimension_semantics=("arbitrary",)),
        name="mem_kv",
    )(mem, w_mk[0].astype(BF16), w_mv[0].astype(BF16))

    w_r = jnp.concatenate([
        w_fine[0].reshape(D, N_EXPERTS),
        jnp.repeat(w_coarse[0], EPG, axis=1),
        jnp.zeros((D, LANE - 2 * N_EXPERTS), F32)], axis=1)
    b_r = jnp.concatenate([
        b_fine[0].reshape(N_EXPERTS), jnp.repeat(b_coarse[0], EPG),
        jnp.zeros((LANE - 2 * N_EXPERTS,), F32)]).reshape(1, LANE)
    w_r_hi = w_r.astype(BF16)
    w_r_lo = (w_r - w_r_hi.astype(F32)).astype(BF16)

    h2, h2b, comb = pl.pallas_call(
        functools.partial(_merge_kernel, tm=tm),
        grid=(B, S // tm),
        in_specs=[
            pl.BlockSpec((1, tm, D), lambda b, s: (b, s, 0)),
            pl.BlockSpec((1, tm, POOL_W), lambda b, s: (b, s, 0)),
            pl.BlockSpec((1, ATTN_W // LANE, tm, LANE), lambda b, s: (b, 0, s, 0)),
            pl.BlockSpec((1, M, MEM_W), lambda b, s: (b, 0, 0)),
            pl.BlockSpec((1, M, MEM_W), lambda b, s: (b, 0, 0)),
            _const_spec((1, D)), _const_spec((1, D)),
            _const_spec((D, 2 * D)), _const_spec((1, 2 * D)),
            _const_spec((POOL_W, D)), _const_spec((ATTN_W, D)), _const_spec((D, D)),
            _const_spec((1, D)), _const_spec((1, D)),
            _const_spec((D, MEM_W)), _const_spec((MEM_W, D)),
            _const_spec((1, D)), _const_spec((1, D)),
            _const_spec((D, LANE)), _const_spec((D, LANE)), _const_spec((1, LANE)),
        ],
        out_specs=[
            pl.BlockSpec((1, tm, D), lambda b, s: (b, s, 0)),
            pl.BlockSpec((1, tm, D), lambda b, s: (b, s, 0)),
            pl.BlockSpec((1, tm, LANE), lambda b, s: (b, s, 0)),
        ],
        out_shape=[
            jax.ShapeDtypeStruct((B, S, D), F32),
            jax.ShapeDtypeStruct((B, S, D), BF16),
            jax.ShapeDtypeStruct((B, S, LANE), F32),
        ],
        compiler_params=pltpu.CompilerParams(dimension_semantics=("arbitrary", "arbitrary"),
                                             vmem_limit_bytes=VMEM_LIMIT),
        name="merge_memattn_router",
    )(x, ypool, o_attn, kmem, vmem,
      row2(ln_in_g), row2(ln_in_b), w_gl.astype(BF16), row2(b_gate[0]),
      w_pool_up[0].astype(BF16), w_attn_up[0].astype(BF16), w_mix_out[0].astype(BF16),
      row2(ln1_g[0]), row2(ln1_b[0]),
      (w_mq[0] * (MEM_HD ** -0.5)).astype(BF16), w_mo[0].astype(BF16),
      row2(ln2_g[0]), row2(ln2_b[0]),
      w_r_hi, w_r_lo, b_r)

    tmd = 1024
    out = pl.pallas_call(
        _moe_kernel,
        grid=(T // tmd, N_EXPERTS),
        in_specs=[
            pl.BlockSpec((tmd, D), lambda t, e: (t, 0)),
            pl.BlockSpec((tmd, LANE), lambda t, e: (t, 0)),
            pl.BlockSpec((tmd, D), lambda t, e: (t, 0)),
            pl.BlockSpec((1, D, FF), lambda t, e: (e, 0, 0)),
            pl.BlockSpec((1, D, FF), lambda t, e: (e, 0, 0)),
            pl.BlockSpec((1, FF, D), lambda t, e: (e, 0, 0)),
            _const_spec((1, D)), _const_spec((1, D)),
        ],
        out_specs=pl.BlockSpec((tmd, D), lambda t, e: (t, 0)),
        out_shape=jax.ShapeDtypeStruct((T, D), F32),
        scratch_shapes=[pltpu.VMEM((tmd, D), F32)],
        compiler_params=pltpu.CompilerParams(dimension_semantics=("arbitrary", "arbitrary"),
                                             vmem_limit_bytes=VMEM_LIMIT),
        name="moe_experts_ln3",
    )(h2b.reshape(T, D), comb.reshape(T, LANE), h2.reshape(T, D),
      w_gate[0].astype(BF16), w_up[0].astype(BF16), w_down[0].astype(BF16),
      row2(ln3_g[0]), row2(ln3_b[0]))
    return out.reshape(B, S, D)
```

```python
import functools
import math

import jax
import jax.numpy as jnp
from jax import lax
from jax.experimental import pallas as pl
from jax.experimental.pallas import tpu as pltpu

D = 1024
POOL_WINDOWS = (2, 4, 8, 16)
POOL_W = 512
N_HEADS = 8
HEAD_DIM = 64
ATTN_W = 512
BLK = 256
N_BLK = 8
TOPK = 3
REL_BUCKETS = 32
REL_MAX_DIST = 128
MEM_HEADS = 4
MEM_HD = 128
MEM_W = 512
N_GROUPS = 4
EPG = 8
N_EXPERTS = 32
FF = 256
DN_ALPHA = 2.0 ** 0.25
LN_EPS = 1e-5

LANE = 128
TOK_TILE = 512
SEG_ALIGN = 16
SEG_MAX = TOK_TILE
COMPACT_ROWS = 2 * TOK_TILE + N_EXPERTS * SEG_ALIGN
FFN_TILE = 256
HALO = 16
NEG = -1e30
VMEM_LIMIT = 56 * 1024 * 1024

F32 = jnp.float32
BF16 = jnp.bfloat16

_NT = (((1,), (1,)), ((), ()))


def _dot(a, b):
    return jnp.dot(a, b, preferred_element_type=F32)


def _dot_nt(a, b):
    return lax.dot_general(a, b, _NT, preferred_element_type=F32)


def _split(a):
    hi = a.astype(BF16)
    lo = (a - hi.astype(F32)).astype(BF16)
    return hi, lo


def _ln(x, g, b):
    mu = jnp.mean(x, axis=-1, keepdims=True)
    xc = x - mu
    var = jnp.mean(xc * xc, axis=-1, keepdims=True)
    return xc * lax.rsqrt(var + LN_EPS) * g + b


def _proj_kernel(x_ref, g_ref, b_ref, w_ref, wgrp_ref, pscale_ref,
                 ypool_ref, q_ref, k_ref, v_ref, ubuf, kbt, *, tm):
    s = pl.program_id(1)
    nb_tile = tm // BLK

    h = _ln(x_ref[0], g_ref[...], b_ref[...])
    hb = h.astype(BF16)
    zu = _dot(hb, w_ref[:, 0:POOL_W])
    zq = _dot(hb, w_ref[:, POOL_W:POOL_W + N_HEADS * LANE])
    zk = _dot(hb, w_ref[:, POOL_W + N_HEADS * LANE:POOL_W + 2 * N_HEADS * LANE])
    zv = _dot(hb, w_ref[:, POOL_W + 2 * N_HEADS * LANE:])

    @pl.when(s == 0)
    def _():
        ubuf[0:HALO, :] = jnp.zeros((HALO, POOL_W), F32)
        kbt[...] = jnp.zeros_like(kbt)

    ubuf[HALO:HALO + tm, :] = zu
    t_pos = s * tm + lax.broadcasted_iota(jnp.int32, (tm, LANE), 0)
    for g, w in enumerate(POOL_WINDOWS):
        cols = slice(g * LANE, (g + 1) * LANE)
        ws = ubuf[HALO:HALO + tm, cols]
        for kk in range(1, w):
            ws = ws + ubuf[HALO - kk:HALO - kk + tm, cols]
        cnt = jnp.minimum(t_pos + 1, w).astype(F32)
        y = ws / cnt - ubuf[HALO:HALO + tm, cols]
        yg = _dot(y.astype(BF16), wgrp_ref[g]) * pscale_ref[:, cols]
        ypool_ref[0, :, cols] = yg.astype(ypool_ref.dtype)
    ubuf[0:HALO, :] = ubuf[tm:tm + HALO, :]

    r_io = lax.broadcasted_iota(jnp.int32, kbt.shape, 0)
    c_io = lax.broadcasted_iota(jnp.int32, kbt.shape, 1)
    head_match = (r_io >> 3) == (c_io >> 7)
    for bi in range(nb_tile):
        n = s * nb_tile + bi
        kmean = jnp.mean(zk[bi * BLK:(bi + 1) * BLK], axis=0, keepdims=True)
        kbt[...] = jnp.where(head_match & ((r_io & 7) == n), kmean, kbt[...])

    q_hi, q_lo = _split(zq)
    kb_hi, kb_lo = _split(kbt[...])
    gate = _dot_nt(q_hi, kb_hi) + _dot_nt(q_lo, kb_hi) + _dot_nt(q_hi, kb_lo)

    lane = lax.broadcasted_iota(jnp.int32, (tm, LANE), 1)
    row = lax.broadcasted_iota(jnp.int32, (tm, LANE), 0)
    n_l = lane & 7
    jrow = s * nb_tile + (row >> 8)
    past = n_l < jrow
    gt = jnp.where(past, gate, -jnp.inf)
    cnt = jnp.zeros((tm, LANE), F32)
    for sh in range(1, N_BLK):
        wrap = (n_l + sh) >= N_BLK
        gm = jnp.where(wrap, pltpu.roll(gt, N_BLK - sh, 1), pltpu.roll(gt, LANE - sh, 1))
        cnt = cnt + jnp.where(wrap, jnp.where(gm >= gt, 1.0, 0.0), jnp.where(gm > gt, 1.0, 0.0))
    keep = (past & (cnt < TOPK)) | (n_l == jrow)
    negmask = jnp.where(keep, 0.0, NEG)

    aug_lane = (lane >= HEAD_DIM) & (lane < HEAD_DIM + N_BLK)
    k_onehot = jnp.where(lane == HEAD_DIM + jrow, 1.0, 0.0)
    for hh in range(N_HEADS):
        cols = slice(hh * LANE, (hh + 1) * LANE)
        m_h = jnp.where(aug_lane, pltpu.roll(negmask, HEAD_DIM - N_BLK * hh, 1), 0.0)
        q_ref[0, hh] = (zq[:, cols] + m_h).astype(q_ref.dtype)
        k_ref[0, hh] = (zk[:, cols] + k_onehot).astype(k_ref.dtype)
    for p in range(ATTN_W // LANE):
        v_ref[0, p] = zv[:, p * LANE:(p + 1) * LANE].astype(v_ref.dtype)


def _attn_kernel(q_ref, k_ref, v_ref, tb_ref, o_ref):
    j = pl.program_id(1)
    lane = lax.broadcasted_iota(jnp.int32, (BLK, LANE), 1)

    def one_head(h, p, jj):
        q = q_ref[0, h]
        own0 = jj * BLK
        pieces = []
        s_own = _dot_nt(q, k_ref[0, h, own0:own0 + BLK, :]) + tb_ref[0, h]
        pieces.append((s_own, own0, BLK))
        if jj >= 1:
            s_adj = _dot_nt(q, k_ref[0, h, own0 - BLK:own0, :]) + tb_ref[1, h]
            pieces.append((s_adj, own0 - BLK, BLK))
        if jj >= 2:
            s_far = _dot_nt(q, k_ref[0, h, 0:own0 - BLK, :])
            pieces.append((s_far, 0, own0 - BLK))
        m = None
        for sc, _, _ in pieces:
            mm = jnp.max(sc, axis=1, keepdims=True)
            m = mm if m is None else jnp.maximum(m, mm)
        l = None
        acc = None
        for sc, start, size in pieces:
            e = jnp.exp(sc - m)
            ls = jnp.sum(e, axis=1, keepdims=True)
            l = ls if l is None else l + ls
            pv = _dot(e.astype(BF16), v_ref[0, p, start:start + size, :])
            acc = pv if acc is None else acc + pv
        return acc / l

    for jj in range(N_BLK):
        @pl.when(j == jj)
        def _(jj=jj):
            def pair(p, carry):
                o0 = one_head(2 * p, p, jj)
                o1 = one_head(2 * p + 1, p, jj)
                o_ref[0, p] = jnp.where(lane < HEAD_DIM, o0, o1).astype(o_ref.dtype)
                return carry
            lax.fori_loop(0, N_HEADS // 2, pair, 0)


def _memkv_kernel(mem_ref, wk_ref, wv_ref, k_ref, v_ref):
    mb = mem_ref[0].astype(BF16)
    k_ref[0] = _dot(mb, wk_ref[...]).astype(k_ref.dtype)
    v_ref[0] = _dot(mb, wv_ref[...]).astype(v_ref.dtype)


def _merge_kernel(x_ref, ypool_ref, o_ref, kmem_ref, vmem_ref,
                  lng_ref, lnb_ref, wgl_ref, bgate_ref, wpu_ref, wau_ref, wout_ref,
                  ln1g_ref, ln1b_ref, wmq_ref, wmo_ref, ln2g_ref, ln2b_ref,
                  wrh_ref, wrl_ref, br_ref,
                  h2_ref, h2b_ref, comb_ref, route_ref, route_t_ref, cnt_ref, *, tm):
    h = _ln(x_ref[0], lng_ref[...], lnb_ref[...])
    hb = h.astype(BF16)
    gl = _dot(hb, wgl_ref[...]) + bgate_ref[...]
    gates = jax.nn.sigmoid(gl)
    y_pool = _dot(ypool_ref[0], wpu_ref[...])
    o_cat = jnp.concatenate([o_ref[0, p] for p in range(ATTN_W // LANE)], axis=1)
    y_attn = _dot(o_cat, wau_ref[...])
    merged = gates[:, 0:D] * y_pool + gates[:, D:2 * D] * y_attn
    mix = _dot(merged.astype(BF16), wout_ref[...])
    h1 = _ln(DN_ALPHA * h + mix, ln1g_ref[...], ln1b_ref[...])

    qm = _dot(h1.astype(BF16), wmq_ref[...]).astype(BF16)
    outs = []
    for hd in range(MEM_HEADS):
        cols = slice(hd * MEM_HD, (hd + 1) * MEM_HD)
        sc = _dot_nt(qm[:, cols], kmem_ref[0, :, cols])
        m = jnp.max(sc, axis=1, keepdims=True)
        e = jnp.exp(sc - m)
        l = jnp.sum(e, axis=1, keepdims=True)
        outs.append(_dot(e.astype(BF16), vmem_ref[0, :, cols]) / l)
    om = jnp.concatenate(outs, axis=1).astype(BF16)
    xa = _dot(om, wmo_ref[...])
    h2 = _ln(DN_ALPHA * h1 + xa, ln2g_ref[...], ln2b_ref[...])
    h2_ref[0] = h2
    h2b_ref[0] = h2.astype(BF16)

    x_hi, x_lo = _split(h2)
    r = _dot(x_hi, wrh_ref[...]) + _dot(x_lo, wrh_ref[...]) + _dot(x_hi, wrl_ref[...]) + br_ref[...]
    lane = lax.broadcasted_iota(jnp.int32, (tm, LANE), 1)
    lane_f = lane.astype(F32)
    cmask = (lane >= N_EXPERTS) & (lane < 2 * N_EXPERTS)
    c = jnp.where(cmask, r, -jnp.inf)
    cmax = jnp.max(c, axis=1, keepdims=True)
    ce = jnp.exp(c - cmax)
    csum = jnp.sum(ce, axis=1, keepdims=True) * (1.0 / EPG)
    g_prob = 1.0 / csum
    grp_lane = ((lane & (N_EXPERTS - 1)) >> 3).astype(F32)
    gidx = jnp.min(jnp.where(cmask & (c == cmax), grp_lane, 99.0), axis=1, keepdims=True)
    fmask = (lane < N_EXPERTS) & (grp_lane == gidx)
    f = jnp.where(fmask, r, -jnp.inf)
    fmax = jnp.max(f, axis=1, keepdims=True)
    fe = jnp.exp(f - fmax)
    fsum = jnp.sum(fe, axis=1, keepdims=True)
    prob = fe / fsum
    p1 = jnp.max(prob, axis=1, keepdims=True)
    i1 = jnp.min(jnp.where(fmask & (prob == p1), lane_f, 999.0), axis=1, keepdims=True)
    rest = fmask & (lane_f != i1)
    prob2 = jnp.where(rest, prob, -1.0)
    p2 = jnp.max(prob2, axis=1, keepdims=True)
    i2 = jnp.min(jnp.where(rest & (prob2 == p2), lane_f, 999.0), axis=1, keepdims=True)
    den = p1 + p2
    comb = jnp.where(lane_f == i1, g_prob * (p1 / den),
                     jnp.where(lane_f == i2, g_prob * (p2 / den), 0.0))
    comb_ref[0] = comb

    sel = jnp.where((lane_f == i1) | (lane_f == i2), 1.0, 0.0)
    cnt = jnp.sum(sel, axis=0, keepdims=True)
    pc = jnp.floor((cnt + (SEG_ALIGN - 1)) * (1.0 / SEG_ALIGN)) * SEG_ALIGN
    lane8 = lax.broadcasted_iota(jnp.int32, (8, LANE), 1)
    inc = jnp.broadcast_to(pc, (8, LANE))
    for sh in (1, 2, 4, 8, 16, 32, 64):
        inc = inc + jnp.where(lane8 >= sh, pltpu.roll(inc, sh, 1), 0.0)
    seg_start = inc[0:1] - pc
    t_row = lax.broadcasted_iota(jnp.int32, (tm, tm), 0)
    t_col = lax.broadcasted_iota(jnp.int32, (tm, tm), 1)
    earlier = jnp.where(t_row > t_col, 1.0, 0.0).astype(BF16)
    rank = _dot(earlier, sel.astype(BF16))
    pos = seg_start + rank
    d1 = jnp.sum(jnp.where(lane_f == i1, pos, 0.0), axis=1, keepdims=True)
    d2 = jnp.sum(jnp.where(lane_f == i2, pos, 0.0), axis=1, keepdims=True)
    route = jnp.where(lane == 0, d1, jnp.where(lane == 1, d2, 0.0))
    route_ref[0] = route
    r_hi, r_lo = _split(route)
    eye = jnp.where(lax.broadcasted_iota(jnp.int32, (8, LANE), 0) == lane8, 1.0, 0.0).astype(BF16)
    route_t_ref[0] = _dot_nt(eye, r_hi) + _dot_nt(eye, r_lo)
    cnt_ref[0] = jnp.broadcast_to(pc, (8, LANE))


def _segment_copies(sub, ls_ref, pc_ref, gs_ref, make_copy, act):
    def body(e, carry):
        idx = sub * N_EXPERTS + e
        lo = ls_ref[idx]
        go = gs_ref[idx]
        n = pc_ref[idx]
        off = jnp.int32(0)
        size = SEG_MAX
        while size >= SEG_ALIGN:
            bit = (n & size) != 0

            @pl.when(bit)
            def _(size=size, off=off):
                act(make_copy(pl.multiple_of(lo + off, SEG_ALIGN), pl.multiple_of(go + off, SEG_ALIGN), size))
            off = off + jnp.where(bit, size, 0)
            size //= 2
        return carry
    lax.fori_loop(0, N_EXPERTS, body, 0)


def _dispatch_kernel(ls_ref, pc_ref, gs_ref, tail_ref, xb_ref, comb_ref, route_t_ref, xs_ref,
                     cbuf, zbuf, sem, zsem, *, n_sub, rows):
    i = pl.program_id(0)
    slot = lax.rem(i, 2)

    def copies(sub, slot_, act):
        def mk(lo, go, size):
            return pltpu.make_async_copy(cbuf.at[slot_, pl.ds(lo, size), :], xs_ref.at[pl.ds(go, size), :],
                                         sem.at[slot_])
        _segment_copies(sub, ls_ref, pc_ref, gs_ref, mk, act)

    @pl.when(i >= 2)
    def _():
        copies(i - 2, slot, lambda c: c.wait())

    tm = xb_ref.shape[0]
    r_io = lax.broadcasted_iota(jnp.int32, (rows, tm), 0).astype(F32)
    d1 = route_t_ref[0, 0:1, :]
    d2 = route_t_ref[0, 1:2, :]
    p_mat = jnp.where((r_io == d1) | (r_io == d2), 1.0, 0.0).astype(BF16)
    c_hi, c_lo = _split(comb_ref[...])
    x_aug = jnp.concatenate([xb_ref[...], c_hi, c_lo], axis=1)
    cbuf[slot] = _dot(p_mat, x_aug).astype(BF16)
    copies(i, slot, lambda c: c.start())

    @pl.when(i == n_sub - 1)
    def _():
        if n_sub >= 2:
            copies(i - 1, 1 - slot, lambda c: c.wait())
        copies(i, slot, lambda c: c.wait())
        zbuf[...] = jnp.zeros_like(zbuf)

        def tails(act):
            def body(e, carry):
                st = tail_ref[e]
                n = tail_ref[N_EXPERTS + e]
                off = jnp.int32(0)
                size = zbuf.shape[0]
                while size >= SEG_ALIGN:
                    bit = (n & size) != 0

                    @pl.when(bit)
                    def _(size=size, off=off):
                        act(pltpu.make_async_copy(
                            zbuf.at[pl.ds(0, size), :],
                            xs_ref.at[pl.ds(pl.multiple_of(st + off, SEG_ALIGN), size), :], zsem))
                    off = off + jnp.where(bit, size, 0)
                    size //= 2
                return carry
            lax.fori_loop(0, N_EXPERTS, body, 0)
        tails(lambda c: c.start())
        tails(lambda c: c.wait())


def _ffn_kernel(texp_ref, nused_ref, xs_ref, wg_ref, wu_ref, wd_ref, ys_ref):
    t = pl.program_id(0)

    @pl.when(t < nused_ref[0])
    def _():
        e = texp_ref[t]
        xa = xs_ref[...]
        xrow = xa[:, 0:D]
        cw = xa[:, D:D + LANE].astype(F32) + xa[:, D + LANE:D + 2 * LANE].astype(F32)
        lane = lax.broadcasted_iota(jnp.int32, cw.shape, 1)
        c = jnp.sum(jnp.where(lane == e, cw, 0.0), axis=1, keepdims=True)
        a = _dot(xrow, wg_ref[0])
        b = _dot(xrow, wu_ref[0])
        hid = (a * jax.nn.sigmoid(a)) * b * c
        ys_ref[...] = _dot(hid.astype(BF16), wd_ref[0]).astype(ys_ref.dtype)


def _combine_kernel(ls_ref, pc_ref, gs_ref, route_ref, h2_ref, g_ref, b_ref, ys_ref, out_ref,
                    ybuf, sem, *, n_sub, rows):
    i = pl.program_id(0)
    slot = lax.rem(i, 2)

    def copies(sub, slot_, act):
        def mk(lo, go, size):
            return pltpu.make_async_copy(ys_ref.at[pl.ds(go, size), :], ybuf.at[slot_, pl.ds(lo, size), :],
                                         sem.at[slot_])
        _segment_copies(sub, ls_ref, pc_ref, gs_ref, mk, act)

    @pl.when(i == 0)
    def _():
        ybuf[...] = jnp.zeros_like(ybuf)
        copies(0, 0, lambda c: c.start())

    @pl.when(i + 1 < n_sub)
    def _():
        copies(i + 1, 1 - slot, lambda c: c.start())

    copies(i, slot, lambda c: c.wait())
    tm = h2_ref.shape[0]
    r_io = lax.broadcasted_iota(jnp.int32, (tm, rows), 1).astype(F32)
    d1 = route_ref[:, 0:1]
    d2 = route_ref[:, 1:2]
    p_t = jnp.where((r_io == d1) | (r_io == d2), 1.0, 0.0).astype(BF16)
    ff = _dot(p_t, ybuf[slot])
    out_ref[...] = _ln(DN_ALPHA * h2_ref[...] + ff, g_ref[...], b_ref[...])


def _bias_kernel(tbl_ref, bkt_ref, out_ref):
    h = pl.program_id(0)
    far = tbl_ref[h, REL_BUCKETS - 1]
    for which in range(2):
        bk = bkt_ref[which]
        acc = jnp.where(bk < 0, NEG, 0.0)
        for kk in range(REL_BUCKETS):
            acc = jnp.where(bk == kk, tbl_ref[h, kk] - far, acc)
        out_ref[which, 0] = acc


def _rel_bucket_table(dist):
    max_exact = REL_BUCKETS // 2
    d = jnp.maximum(dist, 0)
    large = max_exact + (jnp.log(jnp.maximum(d, 1).astype(F32) / max_exact)
                         / math.log(REL_MAX_DIST / max_exact) * (REL_BUCKETS - max_exact)).astype(jnp.int32)
    large = jnp.minimum(large, REL_BUCKETS - 1)
    return jnp.where(d < max_exact, d, large)


def _const_spec(shape):
    nd = len(shape)
    return pl.BlockSpec(shape, lambda *_: (0,) * nd)


def kernel(x, mem, ln_in_g, ln_in_b, rel_bias, w_in, b_gate, w_pool_grp, pool_scale, w_pool_up, w_attn_up,
           w_mix_out, ln1_g, ln1_b, w_mq, w_mk, w_mv, w_mo, ln2_g, ln2_b, w_coarse, b_coarse, w_fine, b_fine,
           w_gate, w_up, w_down, ln3_g, ln3_b):
    B, S, _ = x.shape
    assert S == N_BLK * BLK and w_in.shape[0] == 1
    M = mem.shape[1]
    T = B * S
    tm = 512

    wi = w_in[0]
    w_u = wi[:, 0:POOL_W]
    w_q = wi[:, POOL_W:POOL_W + ATTN_W] * (HEAD_DIM ** -0.5)
    w_k = wi[:, POOL_W + ATTN_W:POOL_W + 2 * ATTN_W]
    w_v = wi[:, POOL_W + 2 * ATTN_W:POOL_W + 3 * ATTN_W]
    w_gl = wi[:, POOL_W + 3 * ATTN_W:]

    def pad_heads(w):
        w = w.reshape(D, N_HEADS, HEAD_DIM)
        return jnp.pad(w, ((0, 0), (0, 0), (0, LANE - HEAD_DIM))).reshape(D, N_HEADS * LANE)

    w1 = jnp.concatenate([w_u, pad_heads(w_q), pad_heads(w_k), w_v], axis=1).astype(BF16)
    row2 = lambda a: a.reshape(1, -1)

    iq = jnp.arange(BLK, dtype=jnp.int32)[:, None]
    ik = jnp.arange(BLK, dtype=jnp.int32)[None, :]
    d_own = iq - ik
    bkt = jnp.stack([jnp.where(d_own >= 0, _rel_bucket_table(d_own), -1), _rel_bucket_table(d_own + BLK)])
    t_bias = pl.pallas_call(
        _bias_kernel,
        grid=(N_HEADS,),
        in_specs=[pl.BlockSpec(memory_space=pltpu.SMEM), _const_spec((2, BLK, BLK))],
        out_specs=pl.BlockSpec((2, 1, BLK, BLK), lambda h: (0, h, 0, 0)),
        out_shape=jax.ShapeDtypeStruct((2, N_HEADS, BLK, BLK), F32),
        name="relbias_tiles",
    )(rel_bias.T, bkt)

    n_w1 = w1.shape[1]
    ypool, q_aug, k_aug, v_p = pl.pallas_call(
        functools.partial(_proj_kernel, tm=tm),
        grid=(B, S // tm),
        in_specs=[
            pl.BlockSpec((1, tm, D), lambda b, s: (b, s, 0)),
            _const_spec((1, D)), _const_spec((1, D)),
            _const_spec((D, n_w1)),
            _const_spec((len(POOL_WINDOWS), LANE, LANE)),
            _const_spec((1, POOL_W)),
        ],
        out_specs=[
            pl.BlockSpec((1, tm, POOL_W), lambda b, s: (b, s, 0)),
            pl.BlockSpec((1, N_HEADS, tm, LANE), lambda b, s: (b, 0, s, 0)),
            pl.BlockSpec((1, N_HEADS, tm, LANE), lambda b, s: (b, 0, s, 0)),
            pl.BlockSpec((1, ATTN_W // LANE, tm, LANE), lambda b, s: (b, 0, s, 0)),
        ],
        out_shape=[
            jax.ShapeDtypeStruct((B, S, POOL_W), BF16),
            jax.ShapeDtypeStruct((B, N_HEADS, S, LANE), BF16),
            jax.ShapeDtypeStruct((B, N_HEADS, S, LANE), BF16),
            jax.ShapeDtypeStruct((B, ATTN_W // LANE, S, LANE), BF16),
        ],
        scratch_shapes=[pltpu.VMEM((HALO + tm, POOL_W), F32), pltpu.VMEM((LANE, N_HEADS * LANE), F32)],
        compiler_params=pltpu.CompilerParams(dimension_semantics=("arbitrary", "arbitrary"),
                                             vmem_limit_bytes=VMEM_LIMIT),
        name="proj_pool_gate",
    )(x, row2(ln_in_g), row2(ln_in_b), w1, w_pool_grp[0].astype(BF16), row2(pool_scale[0]))

    o_attn = pl.pallas_call(
        _attn_kernel,
        grid=(B, N_BLK),
        in_specs=[
            pl.BlockSpec((1, N_HEADS, BLK, LANE), lambda b, j: (b, 0, j, 0)),
            pl.BlockSpec((1, N_HEADS, S, LANE), lambda b, j: (b, 0, 0, 0)),
            pl.BlockSpec((1, ATTN_W // LANE, S, LANE), lambda b, j: (b, 0, 0, 0)),
            _const_spec((2, N_HEADS, BLK, BLK)),
        ],
        out_specs=pl.BlockSpec((1, ATTN_W // LANE, BLK, LANE), lambda b, j: (b, 0, j, 0)),
        out_shape=jax.ShapeDtypeStruct((B, ATTN_W // LANE, S, LANE), BF16),
        compiler_params=pltpu.CompilerParams(dimension_semantics=("arbitrary", "arbitrary"),
                                             vmem_limit_bytes=VMEM_LIMIT),
        name="moba_attn",
    )(q_aug, k_aug, v_p, t_bias)

    kmem, vmem = pl.pallas_call(
        _memkv_kernel,
        grid=(B,),
        in_specs=[pl.BlockSpec((1, M, D), lambda b: (b, 0, 0)),
                  _const_spec((D, MEM_W)), _const_spec((D, MEM_W))],
        out_specs=[pl.BlockSpec((1, M, MEM_W), lambda b: (b, 0, 0)),
                   pl.BlockSpec((1, M, MEM_W), lambda b: (b, 0, 0))],
        out_shape=[jax.ShapeDtypeStruct((B, M, MEM_W), BF16)] * 2,
        compiler_params=pltpu.CompilerParams(dimension_semantics=("arbitrary",)),
        name="mem_kv",
    )(mem, w_mk[0].astype(BF16), w_mv[0].astype(BF16))

    w_r = jnp.concatenate([
        w_fine[0].reshape(D, N_EXPERTS),
        jnp.repeat(w_coarse[0], EPG, axis=1),
        jnp.zeros((D, LANE - 2 * N_EXPERTS), F32)], axis=1)
    b_r = jnp.concatenate([
        b_fine[0].reshape(N_EXPERTS), jnp.repeat(b_coarse[0], EPG),
        jnp.zeros((LANE - 2 * N_EXPERTS,), F32)]).reshape(1, LANE)
    w_r_hi = w_r.astype(BF16)
    w_r_lo = (w_r - w_r_hi.astype(F32)).astype(BF16)

    n_sub = T // tm
    sub_idx = lambda b, s: b * (S // tm) + s
    h2, h2b, comb, route, route_t, seg_cnt = pl.pallas_call(
        functools.partial(_merge_kernel, tm=tm),
        grid=(B, S // tm),
        in_specs=[
            pl.BlockSpec((1, tm, D), lambda b, s: (b, s, 0)),
            pl.BlockSpec((1, tm, POOL_W), lambda b, s: (b, s, 0)),
            pl.BlockSpec((1, ATTN_W // LANE, tm, LANE), lambda b, s: (b, 0, s, 0)),
            pl.BlockSpec((1, M, MEM_W), lambda b, s: (b, 0, 0)),
            pl.BlockSpec((1, M, MEM_W), lambda b, s: (b, 0, 0)),
            _const_spec((1, D)), _const_spec((1, D)),
            _const_spec((D, 2 * D)), _const_spec((1, 2 * D)),
            _const_spec((POOL_W, D)), _const_spec((ATTN_W, D)), _const_spec((D, D)),
            _const_spec((1, D)), _const_spec((1, D)),
            _const_spec((D, MEM_W)), _const_spec((MEM_W, D)),
            _const_spec((1, D)), _const_spec((1, D)),
            _const_spec((D, LANE)), _const_spec((D, LANE)), _const_spec((1, LANE)),
        ],
        out_specs=[
            pl.BlockSpec((1, tm, D), lambda b, s: (b, s, 0)),
            pl.BlockSpec((1, tm, D), lambda b, s: (b, s, 0)),
            pl.BlockSpec((1, tm, LANE), lambda b, s: (b, s, 0)),
            pl.BlockSpec((1, tm, LANE), lambda b, s: (b, s, 0)),
            pl.BlockSpec((1, 8, tm), lambda b, s: (sub_idx(b, s), 0, 0)),
            pl.BlockSpec((1, 8, LANE), lambda b, s: (sub_idx(b, s), 0, 0)),
        ],
        out_shape=[
            jax.ShapeDtypeStruct((B, S, D), F32),
            jax.ShapeDtypeStruct((B, S, D), BF16),
            jax.ShapeDtypeStruct((B, S, LANE), F32),
            jax.ShapeDtypeStruct((B, S, LANE), F32),
            jax.ShapeDtypeStruct((n_sub, 8, tm), F32),
            jax.ShapeDtypeStruct((n_sub, 8, LANE), F32),
        ],
        compiler_params=pltpu.CompilerParams(dimension_semantics=("arbitrary", "arbitrary"),
                                             vmem_limit_bytes=VMEM_LIMIT),
        name="merge_memattn_router",
    )(x, ypool, o_attn, kmem, vmem,
      row2(ln_in_g), row2(ln_in_b), w_gl.astype(BF16), row2(b_gate[0]),
      w_pool_up[0].astype(BF16), w_attn_up[0].astype(BF16), w_mix_out[0].astype(BF16),
      row2(ln1_g[0]), row2(ln1_b[0]),
      (w_mq[0] * (MEM_HD ** -0.5)).astype(BF16), w_mo[0].astype(BF16),
      row2(ln2_g[0]), row2(ln2_b[0]),
      w_r_hi, w_r_lo, b_r)

    pcs = seg_cnt[:, 0, :N_EXPERTS].astype(jnp.int32)
    tot = jnp.sum(pcs, axis=0)
    cap = ((tot + FFN_TILE - 1) // FFN_TILE) * FFN_TILE
    ends = jnp.cumsum(cap)
    base = ends - cap
    gs = base[None, :] + jnp.cumsum(pcs, axis=0) - pcs
    ls = jnp.cumsum(pcs, axis=1) - pcs
    n_sorted = n_sub * COMPACT_ROWS + N_EXPERTS * FFN_TILE
    n_ffn_tiles = n_sorted // FFN_TILE
    n_used = (ends[-1] // FFN_TILE).astype(jnp.int32)
    tile_row = jnp.arange(n_ffn_tiles, dtype=jnp.int32) * FFN_TILE
    tile_exp = jnp.searchsorted(ends, jnp.minimum(tile_row, ends[-1] - 1), side="right").astype(jnp.int32)
    tails = jnp.concatenate([base + tot, cap - tot]).astype(jnp.int32)
    ls_f, pc_f, gs_f = ls.reshape(-1), pcs.reshape(-1), gs.reshape(-1).astype(jnp.int32)

    aug_w = D + 2 * LANE
    x_sorted = pl.pallas_call(
        functools.partial(_dispatch_kernel, n_sub=n_sub, rows=COMPACT_ROWS),
        grid_spec=pltpu.PrefetchScalarGridSpec(
            num_scalar_prefetch=4,
            grid=(n_sub,),
            in_specs=[
                pl.BlockSpec((tm, D), lambda i, *_: (i, 0)),
                pl.BlockSpec((tm, LANE), lambda i, *_: (i, 0)),
                pl.BlockSpec((1, 8, tm), lambda i, *_: (i, 0, 0)),
            ],
            out_specs=pl.BlockSpec(memory_space=pl.ANY),
            scratch_shapes=[
                pltpu.VMEM((2, COMPACT_ROWS, aug_w), BF16),
                pltpu.VMEM((FFN_TILE // 2, aug_w), BF16),
                pltpu.SemaphoreType.DMA((2,)),
                pltpu.SemaphoreType.DMA(()),
            ],
        ),
        out_shape=jax.ShapeDtypeStruct((n_sorted, aug_w), BF16),
        compiler_params=pltpu.CompilerParams(dimension_semantics=("arbitrary",), vmem_limit_bytes=VMEM_LIMIT),
        name="moe_dispatch",
    )(ls_f, pc_f, gs_f, tails, h2b.reshape(T, D), comb.reshape(T, LANE), route_t)

    used_tile = lambda t, texp, nused: (jnp.minimum(t, nused[0] - 1), 0)
    y_sorted = pl.pallas_call(
        _ffn_kernel,
        grid_spec=pltpu.PrefetchScalarGridSpec(
            num_scalar_prefetch=2,
            grid=(n_ffn_tiles,),
            in_specs=[
                pl.BlockSpec((FFN_TILE, aug_w), used_tile),
                pl.BlockSpec((1, D, FF), lambda t, texp, nused: (texp[t], 0, 0)),
                pl.BlockSpec((1, D, FF), lambda t, texp, nused: (texp[t], 0, 0)),
                pl.BlockSpec((1, FF, D), lambda t, texp, nused: (texp[t], 0, 0)),
            ],
            out_specs=pl.BlockSpec((FFN_TILE, D), used_tile),
        ),
        out_shape=jax.ShapeDtypeStruct((n_sorted, D), BF16),
        compiler_params=pltpu.CompilerParams(dimension_semantics=("arbitrary",), vmem_limit_bytes=VMEM_LIMIT),
        name="moe_expert_ffn",
    )(tile_exp, n_used.reshape(1), x_sorted,
      w_gate[0].astype(BF16), w_up[0].astype(BF16), w_down[0].astype(BF16))

    out = pl.pallas_call(
        functools.partial(_combine_kernel, n_sub=n_sub, rows=COMPACT_ROWS),
        grid_spec=pltpu.PrefetchScalarGridSpec(
            num_scalar_prefetch=3,
            grid=(n_sub,),
            in_specs=[
                pl.BlockSpec((tm, LANE), lambda i, *_: (i, 0)),
                pl.BlockSpec((tm, D), lambda i, *_: (i, 0)),
                pl.BlockSpec((1, D), lambda i, *_: (0, 0)),
                pl.BlockSpec((1, D), lambda i, *_: (0, 0)),
                pl.BlockSpec(memory_space=pl.ANY),
            ],
            out_specs=pl.BlockSpec((tm, D), lambda i, *_: (i, 0)),
            scratch_shapes=[
                pltpu.VMEM((2, COMPACT_ROWS, D), BF16),
                pltpu.SemaphoreType.DMA((2,)),
            ],
        ),
        out_shape=jax.ShapeDtypeStruct((T, D), F32),
        compiler_params=pltpu.CompilerParams(dimension_semantics=("arbitrary",), vmem_limit_bytes=VMEM_LIMIT),
        name="moe_combine_ln3",
    )(ls_f, pc_f, gs_f, route.reshape(T, LANE), h2.reshape(T, D), row2(ln3_g[0]), row2(ln3_b[0]), y_sorted)
    return out.reshape(B, S, D)
```

```python
import functools
import math

import jax
import jax.numpy as jnp
from jax import lax
from jax.experimental import pallas as pl
from jax.experimental.pallas import tpu as pltpu

D = 1024
POOL_WINDOWS = (2, 4, 8, 16)
POOL_W = 512
N_HEADS = 8
HEAD_DIM = 64
ATTN_W = 512
BLK = 256
N_BLK = 8
TOPK = 3
REL_BUCKETS = 32
REL_MAX_DIST = 128
MEM_HEADS = 4
MEM_HD = 128
MEM_W = 512
N_GROUPS = 4
EPG = 8
N_EXPERTS = 32
FF = 256
DN_ALPHA = 2.0 ** 0.25
LN_EPS = 1e-5

LANE = 128
TOK_TILE = 512
SEG_ALIGN = 16
SEG_MAX = TOK_TILE
COMPACT_ROWS = 2 * TOK_TILE + N_EXPERTS * SEG_ALIGN
FFN_TILE = 512
HALO = 16
NEG = -1e30
VMEM_LIMIT = 56 * 1024 * 1024

F32 = jnp.float32
BF16 = jnp.bfloat16

_NT = (((1,), (1,)), ((), ()))


def _dot(a, b):
    return jnp.dot(a, b, preferred_element_type=F32)


def _dot_nt(a, b):
    return lax.dot_general(a, b, _NT, preferred_element_type=F32)


def _split(a):
    hi = a.astype(BF16)
    lo = (a - hi.astype(F32)).astype(BF16)
    return hi, lo


def _ln(x, g, b):
    mu = jnp.mean(x, axis=-1, keepdims=True)
    xc = x - mu
    var = jnp.mean(xc * xc, axis=-1, keepdims=True)
    return xc * lax.rsqrt(var + LN_EPS) * g + b


def _proj_kernel(x_ref, g_ref, b_ref, w_ref, wgrp_ref, pscale_ref,
                 ypool_ref, q_ref, k_ref, v_ref, ubuf, kbt, *, tm):
    s = pl.program_id(1)
    nb_tile = tm // BLK

    h = _ln(x_ref[0], g_ref[...], b_ref[...])
    hb = h.astype(BF16)
    zu = _dot(hb, w_ref[:, 0:POOL_W])
    zq = _dot(hb, w_ref[:, POOL_W:POOL_W + N_HEADS * LANE])
    zk = _dot(hb, w_ref[:, POOL_W + N_HEADS * LANE:POOL_W + 2 * N_HEADS * LANE])
    zv = _dot(hb, w_ref[:, POOL_W + 2 * N_HEADS * LANE:])

    @pl.when(s == 0)
    def _():
        ubuf[0:HALO, :] = jnp.zeros((HALO, POOL_W), F32)
        kbt[...] = jnp.zeros_like(kbt)

    ubuf[HALO:HALO + tm, :] = zu
    t_pos = s * tm + lax.broadcasted_iota(jnp.int32, (tm, LANE), 0)
    for g, w in enumerate(POOL_WINDOWS):
        cols = slice(g * LANE, (g + 1) * LANE)
        ws = ubuf[HALO:HALO + tm, cols]
        for kk in range(1, w):
            ws = ws + ubuf[HALO - kk:HALO - kk + tm, cols]
        cnt = jnp.minimum(t_pos + 1, w).astype(F32)
        y = ws / cnt - ubuf[HALO:HALO + tm, cols]
        yg = _dot(y.astype(BF16), wgrp_ref[g]) * pscale_ref[:, cols]
        ypool_ref[0, :, cols] = yg.astype(ypool_ref.dtype)
    ubuf[0:HALO, :] = ubuf[tm:tm + HALO, :]

    r_io = lax.broadcasted_iota(jnp.int32, kbt.shape, 0)
    c_io = lax.broadcasted_iota(jnp.int32, kbt.shape, 1)
    head_match = (r_io >> 3) == (c_io >> 7)
    for bi in range(nb_tile):
        n = s * nb_tile + bi
        kmean = jnp.mean(zk[bi * BLK:(bi + 1) * BLK], axis=0, keepdims=True)
        kbt[...] = jnp.where(head_match & ((r_io & 7) == n), kmean, kbt[...])

    q_hi, q_lo = _split(zq)
    kb_hi, kb_lo = _split(kbt[...])
    gate = _dot_nt(q_hi, kb_hi) + _dot_nt(q_lo, kb_hi) + _dot_nt(q_hi, kb_lo)

    lane = lax.broadcasted_iota(jnp.int32, (tm, LANE), 1)
    row = lax.broadcasted_iota(jnp.int32, (tm, LANE), 0)
    n_l = lane & 7
    jrow = s * nb_tile + (row >> 8)
    past = n_l < jrow
    gt = jnp.where(past, gate, -jnp.inf)
    cnt = jnp.zeros((tm, LANE), F32)
    for sh in range(1, N_BLK):
        wrap = (n_l + sh) >= N_BLK
        gm = jnp.where(wrap, pltpu.roll(gt, N_BLK - sh, 1), pltpu.roll(gt, LANE - sh, 1))
        cnt = cnt + jnp.where(wrap, jnp.where(gm >= gt, 1.0, 0.0), jnp.where(gm > gt, 1.0, 0.0))
    keep = (past & (cnt < TOPK)) | (n_l == jrow)
    negmask = jnp.where(keep, 0.0, NEG)

    aug_lane = (lane >= HEAD_DIM) & (lane < HEAD_DIM + N_BLK)
    k_onehot = jnp.where(lane == HEAD_DIM + jrow, 1.0, 0.0)
    for hh in range(N_HEADS):
        cols = slice(hh * LANE, (hh + 1) * LANE)
        m_h = jnp.where(aug_lane, pltpu.roll(negmask, HEAD_DIM - N_BLK * hh, 1), 0.0)
        q_ref[0, hh] = (zq[:, cols] + m_h).astype(q_ref.dtype)
        k_ref[0, hh] = (zk[:, cols] + k_onehot).astype(k_ref.dtype)
    for p in range(ATTN_W // LANE):
        v_ref[0, p] = zv[:, p * LANE:(p + 1) * LANE].astype(v_ref.dtype)


def _attn_kernel(q_ref, k_ref, v_ref, tb_ref, o_ref):
    j = pl.program_id(1)
    lane = lax.broadcasted_iota(jnp.int32, (BLK, LANE), 1)

    def one_head(h, p, jj):
        q = q_ref[0, h]
        own0 = jj * BLK
        pieces = []
        s_own = _dot_nt(q, k_ref[0, h, own0:own0 + BLK, :]) + tb_ref[0, h]
        pieces.append((s_own, own0, BLK))
        if jj >= 1:
            s_adj = _dot_nt(q, k_ref[0, h, own0 - BLK:own0, :]) + tb_ref[1, h]
            pieces.append((s_adj, own0 - BLK, BLK))
        if jj >= 2:
            s_far = _dot_nt(q, k_ref[0, h, 0:own0 - BLK, :])
            pieces.append((s_far, 0, own0 - BLK))
        m = None
        for sc, _, _ in pieces:
            mm = jnp.max(sc, axis=1, keepdims=True)
            m = mm if m is None else jnp.maximum(m, mm)
        l = None
        acc = None
        for sc, start, size in pieces:
            e = jnp.exp(sc - m)
            ls = jnp.sum(e, axis=1, keepdims=True)
            l = ls if l is None else l + ls
            pv = _dot(e.astype(BF16), v_ref[0, p, start:start + size, :])
            acc = pv if acc is None else acc + pv
        return acc / l

    for jj in range(N_BLK):
        @pl.when(j == jj)
        def _(jj=jj):
            def pair(p, carry):
                o0 = one_head(2 * p, p, jj)
                o1 = one_head(2 * p + 1, p, jj)
                o_ref[0, p] = jnp.where(lane < HEAD_DIM, o0, o1).astype(o_ref.dtype)
                return carry
            lax.fori_loop(0, N_HEADS // 2, pair, 0)


def _memkv_kernel(mem_ref, wk_ref, wv_ref, k_ref, v_ref):
    mb = mem_ref[0].astype(BF16)
    k_ref[0] = _dot(mb, wk_ref[...]).astype(k_ref.dtype)
    v_ref[0] = _dot(mb, wv_ref[...]).astype(v_ref.dtype)


def _merge_kernel(x_ref, ypool_ref, o_ref, kmem_ref, vmem_ref,
                  lng_ref, lnb_ref, wgl_ref, bgate_ref, wpu_ref, wau_ref, wout_ref,
                  ln1g_ref, ln1b_ref, wmq_ref, wmo_ref, ln2g_ref, ln2b_ref,
                  wrh_ref, wrl_ref, br_ref,
                  h2_ref, h2b_ref, comb_ref, route_ref, route_t_ref, cnt_ref, *, tm):
    h = _ln(x_ref[0], lng_ref[...], lnb_ref[...])
    hb = h.astype(BF16)
    gl = _dot(hb, wgl_ref[...]) + bgate_ref[...]
    gates = jax.nn.sigmoid(gl)
    y_pool = _dot(ypool_ref[0], wpu_ref[...])
    o_cat = jnp.concatenate([o_ref[0, p] for p in range(ATTN_W // LANE)], axis=1)
    y_attn = _dot(o_cat, wau_ref[...])
    merged = gates[:, 0:D] * y_pool + gates[:, D:2 * D] * y_attn
    mix = _dot(merged.astype(BF16), wout_ref[...])
    h1 = _ln(DN_ALPHA * h + mix, ln1g_ref[...], ln1b_ref[...])

    qm = _dot(h1.astype(BF16), wmq_ref[...]).astype(BF16)
    outs = []
    for hd in range(MEM_HEADS):
        cols = slice(hd * MEM_HD, (hd + 1) * MEM_HD)
        sc = _dot_nt(qm[:, cols], kmem_ref[0, :, cols])
        m = jnp.max(sc, axis=1, keepdims=True)
        e = jnp.exp(sc - m)
        l = jnp.sum(e, axis=1, keepdims=True)
        outs.append(_dot(e.astype(BF16), vmem_ref[0, :, cols]) / l)
    om = jnp.concatenate(outs, axis=1).astype(BF16)
    xa = _dot(om, wmo_ref[...])
    h2 = _ln(DN_ALPHA * h1 + xa, ln2g_ref[...], ln2b_ref[...])
    h2_ref[0] = h2
    h2b_ref[0] = h2.astype(BF16)

    x_hi, x_lo = _split(h2)
    r = _dot(x_hi, wrh_ref[...]) + _dot(x_lo, wrh_ref[...]) + _dot(x_hi, wrl_ref[...]) + br_ref[...]
    lane = lax.broadcasted_iota(jnp.int32, (tm, LANE), 1)
    lane_f = lane.astype(F32)
    cmask = (lane >= N_EXPERTS) & (lane < 2 * N_EXPERTS)
    c = jnp.where(cmask, r, -jnp.inf)
    cmax = jnp.max(c, axis=1, keepdims=True)
    ce = jnp.exp(c - cmax)
    csum = jnp.sum(ce, axis=1, keepdims=True) * (1.0 / EPG)
    g_prob = 1.0 / csum
    grp_lane = ((lane & (N_EXPERTS - 1)) >> 3).astype(F32)
    gidx = jnp.min(jnp.where(cmask & (c == cmax), grp_lane, 99.0), axis=1, keepdims=True)
    fmask = (lane < N_EXPERTS) & (grp_lane == gidx)
    f = jnp.where(fmask, r, -jnp.inf)
    fmax = jnp.max(f, axis=1, keepdims=True)
    fe = jnp.exp(f - fmax)
    fsum = jnp.sum(fe, axis=1, keepdims=True)
    prob = fe / fsum
    p1 = jnp.max(prob, axis=1, keepdims=True)
    i1 = jnp.min(jnp.where(fmask & (prob == p1), lane_f, 999.0), axis=1, keepdims=True)
    rest = fmask & (lane_f != i1)
    prob2 = jnp.where(rest, prob, -1.0)
    p2 = jnp.max(prob2, axis=1, keepdims=True)
    i2 = jnp.min(jnp.where(rest & (prob2 == p2), lane_f, 999.0), axis=1, keepdims=True)
    den = p1 + p2
    comb = jnp.where(lane_f == i1, g_prob * (p1 / den),
                     jnp.where(lane_f == i2, g_prob * (p2 / den), 0.0))
    comb_ref[0] = comb

    sel = jnp.where((lane_f == i1) | (lane_f == i2), 1.0, 0.0)
    cnt = jnp.sum(sel, axis=0, keepdims=True)
    pc = jnp.floor((cnt + (SEG_ALIGN - 1)) * (1.0 / SEG_ALIGN)) * SEG_ALIGN
    lane8 = lax.broadcasted_iota(jnp.int32, (8, LANE), 1)
    inc = jnp.broadcast_to(pc, (8, LANE))
    for sh in (1, 2, 4, 8, 16, 32, 64):
        inc = inc + jnp.where(lane8 >= sh, pltpu.roll(inc, sh, 1), 0.0)
    seg_start = inc[0:1] - pc
    t_row = lax.broadcasted_iota(jnp.int32, (tm, tm), 0)
    t_col = lax.broadcasted_iota(jnp.int32, (tm, tm), 1)
    earlier = jnp.where(t_row > t_col, 1.0, 0.0).astype(BF16)
    rank = _dot(earlier, sel.astype(BF16))
    pos = seg_start + rank
    d1 = jnp.sum(jnp.where(lane_f == i1, pos, 0.0), axis=1, keepdims=True)
    d2 = jnp.sum(jnp.where(lane_f == i2, pos, 0.0), axis=1, keepdims=True)
    route = jnp.where(lane == 0, d1, jnp.where(lane == 1, d2, 0.0))
    route_ref[0] = route
    r_hi, r_lo = _split(route)
    eye = jnp.where(lax.broadcasted_iota(jnp.int32, (8, LANE), 0) == lane8, 1.0, 0.0).astype(BF16)
    route_t_ref[0] = _dot_nt(eye, r_hi) + _dot_nt(eye, r_lo)
    cnt_ref[0] = jnp.broadcast_to(pc, (8, LANE))


def _segment_copies(sub, ls_ref, pc_ref, gs_ref, make_copy, act):
    def body(e, carry):
        idx = sub * N_EXPERTS + e
        lo = ls_ref[idx]
        go = gs_ref[idx]
        n = pc_ref[idx]
        off = jnp.int32(0)
        size = SEG_MAX
        while size >= SEG_ALIGN:
            bit = (n & size) != 0

            @pl.when(bit)
            def _(size=size, off=off):
                act(make_copy(pl.multiple_of(lo + off, SEG_ALIGN), pl.multiple_of(go + off, SEG_ALIGN), size))
            off = off + jnp.where(bit, size, 0)
            size //= 2
        return carry
    lax.fori_loop(0, N_EXPERTS, body, 0)


def _dispatch_kernel(ls_ref, pc_ref, gs_ref, tail_ref, xb_ref, comb_ref, route_t_ref, xs_ref,
                     cbuf, zbuf, sem, zsem, *, n_sub, rows):
    i = pl.program_id(0)
    slot = lax.rem(i, 2)

    def copies(sub, slot_, act):
        def mk(lo, go, size):
            return pltpu.make_async_copy(cbuf.at[slot_, pl.ds(lo, size), :], xs_ref.at[pl.ds(go, size), :],
                                         sem.at[slot_])
        _segment_copies(sub, ls_ref, pc_ref, gs_ref, mk, act)

    @pl.when(i >= 2)
    def _():
        copies(i - 2, slot, lambda c: c.wait())

    tm = xb_ref.shape[0]
    r_io = lax.broadcasted_iota(jnp.int32, (rows, tm), 0).astype(F32)
    d1 = route_t_ref[0, 0:1, :]
    d2 = route_t_ref[0, 1:2, :]
    p_mat = jnp.where((r_io == d1) | (r_io == d2), 1.0, 0.0).astype(BF16)
    c_hi, c_lo = _split(comb_ref[...])
    x_aug = jnp.concatenate([xb_ref[...], c_hi, c_lo], axis=1)
    cbuf[slot] = _dot(p_mat, x_aug).astype(BF16)
    copies(i, slot, lambda c: c.start())

    @pl.when(i == n_sub - 1)
    def _():
        if n_sub >= 2:
            copies(i - 1, 1 - slot, lambda c: c.wait())
        copies(i, slot, lambda c: c.wait())
        zbuf[...] = jnp.zeros_like(zbuf)

        def tails(act):
            def body(e, carry):
                st = tail_ref[e]
                n = tail_ref[N_EXPERTS + e]
                off = jnp.int32(0)
                size = zbuf.shape[0]
                while size >= SEG_ALIGN:
                    bit = (n & size) != 0

                    @pl.when(bit)
                    def _(size=size, off=off):
                        act(pltpu.make_async_copy(
                            zbuf.at[pl.ds(0, size), :],
                            xs_ref.at[pl.ds(pl.multiple_of(st + off, SEG_ALIGN), size), :], zsem))
                    off = off + jnp.where(bit, size, 0)
                    size //= 2
                return carry
            lax.fori_loop(0, N_EXPERTS, body, 0)
        tails(lambda c: c.start())
        tails(lambda c: c.wait())


def _ffn_kernel(texp_ref, nused_ref, xs_ref, wg_ref, wu_ref, wd_ref, ys_ref):
    t = pl.program_id(0)

    @pl.when(t < nused_ref[0])
    def _():
        e = texp_ref[t]
        xa = xs_ref[...]
        xrow = xa[:, 0:D]
        cw = xa[:, D:D + LANE].astype(F32) + xa[:, D + LANE:D + 2 * LANE].astype(F32)
        lane = lax.broadcasted_iota(jnp.int32, cw.shape, 1)
        c = jnp.sum(jnp.where(lane == e, cw, 0.0), axis=1, keepdims=True)
        a = _dot(xrow, wg_ref[0])
        b = _dot(xrow, wu_ref[0])
        hid = (a * jax.nn.sigmoid(a)) * b * c
        ys_ref[...] = _dot(hid.astype(BF16), wd_ref[0]).astype(ys_ref.dtype)


def _combine_kernel(ls_ref, pc_ref, gs_ref, route_ref, h2_ref, g_ref, b_ref, ys_ref, out_ref,
                    ybuf, sem, *, n_sub, rows):
    i = pl.program_id(0)
    slot = lax.rem(i, 2)

    def copies(sub, slot_, act):
        def mk(lo, go, size):
            return pltpu.make_async_copy(ys_ref.at[pl.ds(go, size), :], ybuf.at[slot_, pl.ds(lo, size), :],
                                         sem.at[slot_])
        _segment_copies(sub, ls_ref, pc_ref, gs_ref, mk, act)

    @pl.when(i == 0)
    def _():
        ybuf[...] = jnp.zeros_like(ybuf)
        copies(0, 0, lambda c: c.start())

    @pl.when(i + 1 < n_sub)
    def _():
        copies(i + 1, 1 - slot, lambda c: c.start())

    copies(i, slot, lambda c: c.wait())
    tm = h2_ref.shape[0]
    r_io = lax.broadcasted_iota(jnp.int32, (tm, rows), 1).astype(F32)
    d1 = route_ref[:, 0:1]
    d2 = route_ref[:, 1:2]
    p_t = jnp.where((r_io == d1) | (r_io == d2), 1.0, 0.0).astype(BF16)
    ff = _dot(p_t, ybuf[slot])
    out_ref[...] = _ln(DN_ALPHA * h2_ref[...] + ff, g_ref[...], b_ref[...])


def _bias_kernel(tbl_ref, bkt_ref, out_ref):
    h = pl.program_id(0)
    far = tbl_ref[h, REL_BUCKETS - 1]
    for which in range(2):
        bk = bkt_ref[which]
        acc = jnp.where(bk < 0, NEG, 0.0)
        for kk in range(REL_BUCKETS):
            acc = jnp.where(bk == kk, tbl_ref[h, kk] - far, acc)
        out_ref[which, 0] = acc


def _rel_bucket_table(dist):
    max_exact = REL_BUCKETS // 2
    d = jnp.maximum(dist, 0)
    large = max_exact + (jnp.log(jnp.maximum(d, 1).astype(F32) / max_exact)
                         / math.log(REL_MAX_DIST / max_exact) * (REL_BUCKETS - max_exact)).astype(jnp.int32)
    large = jnp.minimum(large, REL_BUCKETS - 1)
    return jnp.where(d < max_exact, d, large)


def _const_spec(shape):
    nd = len(shape)
    return pl.BlockSpec(shape, lambda *_: (0,) * nd)


def kernel(x, mem, ln_in_g, ln_in_b, rel_bias, w_in, b_gate, w_pool_grp, pool_scale, w_pool_up, w_attn_up,
           w_mix_out, ln1_g, ln1_b, w_mq, w_mk, w_mv, w_mo, ln2_g, ln2_b, w_coarse, b_coarse, w_fine, b_fine,
           w_gate, w_up, w_down, ln3_g, ln3_b):
    B, S, _ = x.shape
    assert S == N_BLK * BLK and w_in.shape[0] == 1
    M = mem.shape[1]
    T = B * S
    tm = 512

    wi = w_in[0]
    w_u = wi[:, 0:POOL_W]
    w_q = wi[:, POOL_W:POOL_W + ATTN_W] * (HEAD_DIM ** -0.5)
    w_k = wi[:, POOL_W + ATTN_W:POOL_W + 2 * ATTN_W]
    w_v = wi[:, POOL_W + 2 * ATTN_W:POOL_W + 3 * ATTN_W]
    w_gl = wi[:, POOL_W + 3 * ATTN_W:]

    def pad_heads(w):
        w = w.reshape(D, N_HEADS, HEAD_DIM)
        return jnp.pad(w, ((0, 0), (0, 0), (0, LANE - HEAD_DIM))).reshape(D, N_HEADS * LANE)

    w1 = jnp.concatenate([w_u, pad_heads(w_q), pad_heads(w_k), w_v], axis=1).astype(BF16)
    row2 = lambda a: a.reshape(1, -1)

    iq = jnp.arange(BLK, dtype=jnp.int32)[:, None]
    ik = jnp.arange(BLK, dtype=jnp.int32)[None, :]
    d_own = iq - ik
    bkt = jnp.stack([jnp.where(d_own >= 0, _rel_bucket_table(d_own), -1), _rel_bucket_table(d_own + BLK)])
    t_bias = pl.pallas_call(
        _bias_kernel,
        grid=(N_HEADS,),
        in_specs=[pl.BlockSpec(memory_space=pltpu.SMEM), _const_spec((2, BLK, BLK))],
        out_specs=pl.BlockSpec((2, 1, BLK, BLK), lambda h: (0, h, 0, 0)),
        out_shape=jax.ShapeDtypeStruct((2, N_HEADS, BLK, BLK), F32),
        name="relbias_tiles",
    )(rel_bias.T, bkt)

    n_w1 = w1.shape[1]
    ypool, q_aug, k_aug, v_p = pl.pallas_call(
        functools.partial(_proj_kernel, tm=tm),
        grid=(B, S // tm),
        in_specs=[
            pl.BlockSpec((1, tm, D), lambda b, s: (b, s, 0)),
            _const_spec((1, D)), _const_spec((1, D)),
            _const_spec((D, n_w1)),
            _const_spec((len(POOL_WINDOWS), LANE, LANE)),
            _const_spec((1, POOL_W)),
        ],
        out_specs=[
            pl.BlockSpec((1, tm, POOL_W), lambda b, s: (b, s, 0)),
            pl.BlockSpec((1, N_HEADS, tm, LANE), lambda b, s: (b, 0, s, 0)),
            pl.BlockSpec((1, N_HEADS, tm, LANE), lambda b, s: (b, 0, s, 0)),
            pl.BlockSpec((1, ATTN_W // LANE, tm, LANE), lambda b, s: (b, 0, s, 0)),
        ],
        out_shape=[
            jax.ShapeDtypeStruct((B, S, POOL_W), BF16),
            jax.ShapeDtypeStruct((B, N_HEADS, S, LANE), BF16),
            jax.ShapeDtypeStruct((B, N_HEADS, S, LANE), BF16),
            jax.ShapeDtypeStruct((B, ATTN_W // LANE, S, LANE), BF16),
        ],
        scratch_shapes=[pltpu.VMEM((HALO + tm, POOL_W), F32), pltpu.VMEM((LANE, N_HEADS * LANE), F32)],
        compiler_params=pltpu.CompilerParams(dimension_semantics=("arbitrary", "arbitrary"),
                                             vmem_limit_bytes=VMEM_LIMIT),
        name="proj_pool_gate",
    )(x, row2(ln_in_g), row2(ln_in_b), w1, w_pool_grp[0].astype(BF16), row2(pool_scale[0]))

    o_attn = pl.pallas_call(
        _attn_kernel,
        grid=(B, N_BLK),
        in_specs=[
            pl.BlockSpec((1, N_HEADS, BLK, LANE), lambda b, j: (b, 0, j, 0)),
            pl.BlockSpec((1, N_HEADS, S, LANE), lambda b, j: (b, 0, 0, 0)),
            pl.BlockSpec((1, ATTN_W // LANE, S, LANE), lambda b, j: (b, 0, 0, 0)),
            _const_spec((2, N_HEADS, BLK, BLK)),
        ],
        out_specs=pl.BlockSpec((1, ATTN_W // LANE, BLK, LANE), lambda b, j: (b, 0, j, 0)),
        out_shape=jax.ShapeDtypeStruct((B, ATTN_W // LANE, S, LANE), BF16),
        compiler_params=pltpu.CompilerParams(dimension_semantics=("arbitrary", "arbitrary"),
                                             vmem_limit_bytes=VMEM_LIMIT),
        name="moba_attn",
    )(q_aug, k_aug, v_p, t_bias)

    kmem, vmem = pl.pallas_call(
        _memkv_kernel,
        grid=(B,),
        in_specs=[pl.BlockSpec((1, M, D), lambda b: (b, 0, 0)),
                  _const_spec((D, MEM_W)), _const_spec((D, MEM_W))],
        out_specs=[pl.BlockSpec((1, M, MEM_W), lambda b: (b, 0, 0)),
                   pl.BlockSpec((1, M, MEM_W), lambda b: (b, 0, 0))],
        out_shape=[jax.ShapeDtypeStruct((B, M, MEM_W), BF16)] * 2,
        compiler_params=pltpu.CompilerParams(dimension_semantics=("arbitrary",)),
        name="mem_kv",
    )(mem, w_mk[0].astype(BF16), w_mv[0].astype(BF16))

    w_r = jnp.concatenate([
        w_fine[0].reshape(D, N_EXPERTS),
        jnp.repeat(w_coarse[0], EPG, axis=1),
        jnp.zeros((D, LANE - 2 * N_EXPERTS), F32)], axis=1)
    b_r = jnp.concatenate([
        b_fine[0].reshape(N_EXPERTS), jnp.repeat(b_coarse[0], EPG),
        jnp.zeros((LANE - 2 * N_EXPERTS,), F32)]).reshape(1, LANE)
    w_r_hi = w_r.astype(BF16)
    w_r_lo = (w_r - w_r_hi.astype(F32)).astype(BF16)

    n_sub = T // tm
    sub_idx = lambda b, s: b * (S // tm) + s
    h2, h2b, comb, route, route_t, seg_cnt = pl.pallas_call(
        functools.partial(_merge_kernel, tm=tm),
        grid=(B, S // tm),
        in_specs=[
            pl.BlockSpec((1, tm, D), lambda b, s: (b, s, 0)),
            pl.BlockSpec((1, tm, POOL_W), lambda b, s: (b, s, 0)),
            pl.BlockSpec((1, ATTN_W // LANE, tm, LANE), lambda b, s: (b, 0, s, 0)),
            pl.BlockSpec((1, M, MEM_W), lambda b, s: (b, 0, 0)),
            pl.BlockSpec((1, M, MEM_W), lambda b, s: (b, 0, 0)),
            _const_spec((1, D)), _const_spec((1, D)),
            _const_spec((D, 2 * D)), _const_spec((1, 2 * D)),
            _const_spec((POOL_W, D)), _const_spec((ATTN_W, D)), _const_spec((D, D)),
            _const_spec((1, D)), _const_spec((1, D)),
            _const_spec((D, MEM_W)), _const_spec((MEM_W, D)),
            _const_spec((1, D)), _const_spec((1, D)),
            _const_spec((D, LANE)), _const_spec((D, LANE)), _const_spec((1, LANE)),
        ],
        out_specs=[
            pl.BlockSpec((1, tm, D), lambda b, s: (b, s, 0)),
            pl.BlockSpec((1, tm, D), lambda b, s: (b, s, 0)),
            pl.BlockSpec((1, tm, LANE), lambda b, s: (b, s, 0)),
            pl.BlockSpec((1, tm, LANE), lambda b, s: (b, s, 0)),
            pl.BlockSpec((1, 8, tm), lambda b, s: (sub_idx(b, s), 0, 0)),
            pl.BlockSpec((1, 8, LANE), lambda b, s: (sub_idx(b, s), 0, 0)),
        ],
        out_shape=[
            jax.ShapeDtypeStruct((B, S, D), F32),
            jax.ShapeDtypeStruct((B, S, D), BF16),
            jax.ShapeDtypeStruct((B, S, LANE), F32),
            jax.ShapeDtypeStruct((B, S, LANE), F32),
            jax.ShapeDtypeStruct((n_sub, 8, tm), F32),
            jax.ShapeDtypeStruct((n_sub, 8, LANE), F32),
        ],
        compiler_params=pltpu.CompilerParams(dimension_semantics=("arbitrary", "arbitrary"),
                                             vmem_limit_bytes=VMEM_LIMIT),
        name="merge_memattn_router",
    )(x, ypool, o_attn, kmem, vmem,
      row2(ln_in_g), row2(ln_in_b), w_gl.astype(BF16), row2(b_gate[0]),
      w_pool_up[0].astype(BF16), w_attn_up[0].astype(BF16), w_mix_out[0].astype(BF16),
      row2(ln1_g[0]), row2(ln1_b[0]),
      (w_mq[0] * (MEM_HD ** -0.5)).astype(BF16), w_mo[0].astype(BF16),
      row2(ln2_g[0]), row2(ln2_b[0]),
      w_r_hi, w_r_lo, b_r)

    pcs = seg_cnt[:, 0, :N_EXPERTS].astype(jnp.int32)
    tot = jnp.sum(pcs, axis=0)
    cap = ((tot + FFN_TILE - 1) // FFN_TILE) * FFN_TILE
    ends = jnp.cumsum(cap)
    base = ends - cap
    gs = base[None, :] + jnp.cumsum(pcs, axis=0) - pcs
    ls = jnp.cumsum(pcs, axis=1) - pcs
    n_sorted = n_sub * COMPACT_ROWS + N_EXPERTS * FFN_TILE
    n_ffn_tiles = n_sorted // FFN_TILE
    n_used = (ends[-1] // FFN_TILE).astype(jnp.int32)
    tile_row = jnp.arange(n_ffn_tiles, dtype=jnp.int32) * FFN_TILE
    tile_exp = jnp.sum(jnp.minimum(tile_row, ends[-1] - 1)[:, None] >= ends[None, :], axis=1).astype(jnp.int32)
    tails = jnp.concatenate([base + tot, cap - tot]).astype(jnp.int32)
    ls_f, pc_f, gs_f = ls.reshape(-1), pcs.reshape(-1), gs.reshape(-1).astype(jnp.int32)

    aug_w = D + 2 * LANE
    x_sorted = pl.pallas_call(
        functools.partial(_dispatch_kernel, n_sub=n_sub, rows=COMPACT_ROWS),
        grid_spec=pltpu.PrefetchScalarGridSpec(
            num_scalar_prefetch=4,
            grid=(n_sub,),
            in_specs=[
                pl.BlockSpec((tm, D), lambda i, *_: (i, 0)),
                pl.BlockSpec((tm, LANE), lambda i, *_: (i, 0)),
                pl.BlockSpec((1, 8, tm), lambda i, *_: (i, 0, 0)),
            ],
            out_specs=pl.BlockSpec(memory_space=pl.ANY),
            scratch_shapes=[
                pltpu.VMEM((2, COMPACT_ROWS, aug_w), BF16),
                pltpu.VMEM((FFN_TILE // 2, aug_w), BF16),
                pltpu.SemaphoreType.DMA((2,)),
                pltpu.SemaphoreType.DMA(()),
            ],
        ),
        out_shape=jax.ShapeDtypeStruct((n_sorted, aug_w), BF16),
        compiler_params=pltpu.CompilerParams(dimension_semantics=("arbitrary",), vmem_limit_bytes=VMEM_LIMIT),
        name="moe_dispatch",
    )(ls_f, pc_f, gs_f, tails, h2b.reshape(T, D), comb.reshape(T, LANE), route_t)

    used_tile = lambda t, texp, nused: (jnp.minimum(t, nused[0] - 1), 0)
    y_sorted = pl.pallas_call(
        _ffn_kernel,
        grid_spec=pltpu.PrefetchScalarGridSpec(
            num_scalar_prefetch=2,
            grid=(n_ffn_tiles,),
            in_specs=[
                pl.BlockSpec((FFN_TILE, aug_w), used_tile),
                pl.BlockSpec((1, D, FF), lambda t, texp, nused: (texp[t], 0, 0)),
                pl.BlockSpec((1, D, FF), lambda t, texp, nused: (texp[t], 0, 0)),
                pl.BlockSpec((1, FF, D), lambda t, texp, nused: (texp[t], 0, 0)),
            ],
            out_specs=pl.BlockSpec((FFN_TILE, D), used_tile),
        ),
        out_shape=jax.ShapeDtypeStruct((n_sorted, D), BF16),
        compiler_params=pltpu.CompilerParams(dimension_semantics=("arbitrary",), vmem_limit_bytes=VMEM_LIMIT),
        name="moe_expert_ffn",
    )(tile_exp, n_used.reshape(1), x_sorted,
      w_gate[0].astype(BF16), w_up[0].astype(BF16), w_down[0].astype(BF16))

    out = pl.pallas_call(
        functools.partial(_combine_kernel, n_sub=n_sub, rows=COMPACT_ROWS),
        grid_spec=pltpu.PrefetchScalarGridSpec(
            num_scalar_prefetch=3,
            grid=(n_sub,),
            in_specs=[
                pl.BlockSpec((tm, LANE), lambda i, *_: (i, 0)),
                pl.BlockSpec((tm, D), lambda i, *_: (i, 0)),
                pl.BlockSpec((1, D), lambda i, *_: (0, 0)),
                pl.BlockSpec((1, D), lambda i, *_: (0, 0)),
                pl.BlockSpec(memory_space=pl.ANY),
            ],
            out_specs=pl.BlockSpec((tm, D), lambda i, *_: (i, 0)),
            scratch_shapes=[
                pltpu.VMEM((2, COMPACT_ROWS, D), BF16),
                pltpu.SemaphoreType.DMA((2,)),
            ],
        ),
        out_shape=jax.ShapeDtypeStruct((T, D), F32),
        compiler_params=pltpu.CompilerParams(dimension_semantics=("arbitrary",), vmem_limit_bytes=VMEM_LIMIT),
        name="moe_combine_ln3",
    )(ls_f, pc_f, gs_f, route.reshape(T, LANE), h2.reshape(T, D), row2(ln3_g[0]), row2(ln3_b[0]), y_sorted)
    return out.reshape(B, S, D)
```

```python
import functools
import math

import jax
import jax.numpy as jnp
from jax import lax
from jax.experimental import pallas as pl
from jax.experimental.pallas import tpu as pltpu

D = 1024
POOL_WINDOWS = (2, 4, 8, 16)
POOL_W = 512
N_HEADS = 8
HEAD_DIM = 64
ATTN_W = 512
BLK = 256
N_BLK = 8
TOPK = 3
REL_BUCKETS = 32
REL_MAX_DIST = 128
MEM_HEADS = 4
MEM_HD = 128
MEM_W = 512
N_GROUPS = 4
EPG = 8
N_EXPERTS = 32
FF = 256
DN_ALPHA = 2.0 ** 0.25
LN_EPS = 1e-5

LANE = 128
TOK_TILE = 512
SEG_ALIGN = 16
COMPACT_ROWS = 2 * TOK_TILE + N_EXPERTS * SEG_ALIGN
MAX_CHUNKS = COMPACT_ROWS // SEG_ALIGN
FFN_TILE = 512
CHAIN_ROWS = 512
HALO = 16
NEG = -1e30
VMEM_LIMIT = 56 * 1024 * 1024

F32 = jnp.float32
BF16 = jnp.bfloat16

_NT = (((1,), (1,)), ((), ()))


def _dot(a, b):
    return jnp.dot(a, b, preferred_element_type=F32)


def _dot_nt(a, b):
    return lax.dot_general(a, b, _NT, preferred_element_type=F32)


def _split(a):
    hi = a.astype(BF16)
    lo = (a - hi.astype(F32)).astype(BF16)
    return hi, lo


def _ln(x, g, b):
    mu = jnp.mean(x, axis=-1, keepdims=True)
    xc = x - mu
    var = jnp.mean(xc * xc, axis=-1, keepdims=True)
    return xc * lax.rsqrt(var + LN_EPS) * g + b


def _proj_kernel(x_ref, g_ref, b_ref, w_ref, wgrp_ref, pscale_ref,
                 ypool_ref, q_ref, k_ref, v_ref, ubuf, kbt, *, tm):
    s = pl.program_id(1)
    nb_tile = tm // BLK

    h = _ln(x_ref[0], g_ref[...], b_ref[...])
    hb = h.astype(BF16)
    zu = _dot(hb, w_ref[:, 0:POOL_W])
    zq = _dot(hb, w_ref[:, POOL_W:POOL_W + N_HEADS * LANE])
    zk = _dot(hb, w_ref[:, POOL_W + N_HEADS * LANE:POOL_W + 2 * N_HEADS * LANE])
    zv = _dot(hb, w_ref[:, POOL_W + 2 * N_HEADS * LANE:])

    @pl.when(s == 0)
    def _():
        ubuf[0:HALO, :] = jnp.zeros((HALO, POOL_W), F32)
        kbt[...] = jnp.zeros_like(kbt)

    ubuf[HALO:HALO + tm, :] = zu
    t_pos = s * tm + lax.broadcasted_iota(jnp.int32, (tm, LANE), 0)
    for g, w in enumerate(POOL_WINDOWS):
        cols = slice(g * LANE, (g + 1) * LANE)
        ws = ubuf[HALO:HALO + tm, cols]
        for kk in range(1, w):
            ws = ws + ubuf[HALO - kk:HALO - kk + tm, cols]
        cnt = jnp.minimum(t_pos + 1, w).astype(F32)
        y = ws / cnt - ubuf[HALO:HALO + tm, cols]
        yg = _dot(y.astype(BF16), wgrp_ref[g]) * pscale_ref[:, cols]
        ypool_ref[0, :, cols] = yg.astype(ypool_ref.dtype)
    ubuf[0:HALO, :] = ubuf[tm:tm + HALO, :]

    r_io = lax.broadcasted_iota(jnp.int32, kbt.shape, 0)
    c_io = lax.broadcasted_iota(jnp.int32, kbt.shape, 1)
    head_match = (r_io >> 3) == (c_io >> 7)
    for bi in range(nb_tile):
        n = s * nb_tile + bi
        kmean = jnp.mean(zk[bi * BLK:(bi + 1) * BLK], axis=0, keepdims=True)
        kbt[...] = jnp.where(head_match & ((r_io & 7) == n), kmean, kbt[...])

    q_hi, q_lo = _split(zq)
    kb_hi, kb_lo = _split(kbt[...])
    gate = _dot_nt(q_hi, kb_hi) + _dot_nt(q_lo, kb_hi) + _dot_nt(q_hi, kb_lo)

    lane = lax.broadcasted_iota(jnp.int32, (tm, LANE), 1)
    row = lax.broadcasted_iota(jnp.int32, (tm, LANE), 0)
    n_l = lane & 7
    jrow = s * nb_tile + (row >> 8)
    past = n_l < jrow
    gt = jnp.where(past, gate, -jnp.inf)
    cnt = jnp.zeros((tm, LANE), F32)
    for sh in range(1, N_BLK):
        wrap = (n_l + sh) >= N_BLK
        gm = jnp.where(wrap, pltpu.roll(gt, N_BLK - sh, 1), pltpu.roll(gt, LANE - sh, 1))
        cnt = cnt + jnp.where(wrap, jnp.where(gm >= gt, 1.0, 0.0), jnp.where(gm > gt, 1.0, 0.0))
    keep = (past & (cnt < TOPK)) | (n_l == jrow)
    negmask = jnp.where(keep, 0.0, NEG)

    aug_lane = (lane >= HEAD_DIM) & (lane < HEAD_DIM + N_BLK)
    k_onehot = jnp.where(lane == HEAD_DIM + jrow, 1.0, 0.0)
    for hh in range(N_HEADS):
        cols = slice(hh * LANE, (hh + 1) * LANE)
        m_h = jnp.where(aug_lane, pltpu.roll(negmask, HEAD_DIM - N_BLK * hh, 1), 0.0)
        q_ref[0, hh] = (zq[:, cols] + m_h).astype(q_ref.dtype)
        k_ref[0, hh] = (zk[:, cols] + k_onehot).astype(k_ref.dtype)
    for p in range(ATTN_W // LANE):
        v_ref[0, p] = zv[:, p * LANE:(p + 1) * LANE].astype(v_ref.dtype)


def _attn_kernel(q_ref, k_ref, v_ref, tb_ref, o_ref):
    j = pl.program_id(1)
    lane = lax.broadcasted_iota(jnp.int32, (BLK, LANE), 1)

    def one_head(h, p, jj):
        q = q_ref[0, h]
        own0 = jj * BLK
        pieces = []
        s_own = _dot_nt(q, k_ref[0, h, own0:own0 + BLK, :]) + tb_ref[0, h]
        pieces.append((s_own, own0, BLK))
        if jj >= 1:
            s_adj = _dot_nt(q, k_ref[0, h, own0 - BLK:own0, :]) + tb_ref[1, h]
            pieces.append((s_adj, own0 - BLK, BLK))
        if jj >= 2:
            s_far = _dot_nt(q, k_ref[0, h, 0:own0 - BLK, :])
            pieces.append((s_far, 0, own0 - BLK))
        m = None
        for sc, _, _ in pieces:
            mm = jnp.max(sc, axis=1, keepdims=True)
            m = mm if m is None else jnp.maximum(m, mm)
        l = None
        acc = None
        for sc, start, size in pieces:
            e = jnp.exp(sc - m)
            ls = jnp.sum(e, axis=1, keepdims=True)
            l = ls if l is None else l + ls
            pv = _dot(e.astype(BF16), v_ref[0, p, start:start + size, :])
            acc = pv if acc is None else acc + pv
        return acc / l

    for jj in range(N_BLK):
        @pl.when(j == jj)
        def _(jj=jj):
            def pair(p, carry):
                o0 = one_head(2 * p, p, jj)
                o1 = one_head(2 * p + 1, p, jj)
                o_ref[0, p] = jnp.where(lane < HEAD_DIM, o0, o1).astype(o_ref.dtype)
                return carry
            lax.fori_loop(0, N_HEADS // 2, pair, 0)


def _memkv_kernel(mem_ref, wk_ref, wv_ref, k_ref, v_ref):
    mb = mem_ref[0].astype(BF16)
    k_ref[0] = _dot(mb, wk_ref[...]).astype(k_ref.dtype)
    v_ref[0] = _dot(mb, wv_ref[...]).astype(v_ref.dtype)


def _merge_kernel(x_ref, ypool_ref, o_ref, kmem_ref, vmem_ref,
                  lng_ref, lnb_ref, wgl_ref, bgate_ref, wpu_ref, wau_ref, wout_ref,
                  ln1g_ref, ln1b_ref, wmq_ref, wmo_ref, ln2g_ref, ln2b_ref,
                  wrh_ref, wrl_ref, br_ref,
                  h2_ref, h2b_ref, comb_ref, route_ref, route_t_ref, cnt_ref, *, tm):
    n_chain = tm // CHAIN_ROWS
    parts = [_merge_chain(slice(ci * CHAIN_ROWS, (ci + 1) * CHAIN_ROWS),
                          x_ref, ypool_ref, o_ref, kmem_ref, vmem_ref,
                          lng_ref, lnb_ref, wgl_ref, bgate_ref, wpu_ref, wau_ref, wout_ref,
                          ln1g_ref, ln1b_ref, wmq_ref, wmo_ref, ln2g_ref, ln2b_ref,
                          wrh_ref, wrl_ref, br_ref, h2_ref, h2b_ref, comb_ref)
             for ci in range(n_chain)]
    i1 = jnp.concatenate([p[0] for p in parts], axis=0)
    i2 = jnp.concatenate([p[1] for p in parts], axis=0)
    lane = lax.broadcasted_iota(jnp.int32, (tm, LANE), 1)
    lane_f = lane.astype(F32)

    sel = jnp.where((lane_f == i1) | (lane_f == i2), 1.0, 0.0)
    cnt = jnp.sum(sel, axis=0, keepdims=True)
    pc = jnp.floor((cnt + (SEG_ALIGN - 1)) * (1.0 / SEG_ALIGN)) * SEG_ALIGN
    lane8 = lax.broadcasted_iota(jnp.int32, (8, LANE), 1)
    inc = jnp.broadcast_to(pc, (8, LANE))
    for sh in (1, 2, 4, 8, 16, 32, 64):
        inc = inc + jnp.where(lane8 >= sh, pltpu.roll(inc, sh, 1), 0.0)
    seg_start = inc[0:1] - pc
    t_row = lax.broadcasted_iota(jnp.int32, (tm, tm), 0)
    t_col = lax.broadcasted_iota(jnp.int32, (tm, tm), 1)
    earlier = jnp.where(t_row > t_col, 1.0, 0.0).astype(BF16)
    rank = _dot(earlier, sel.astype(BF16))
    pos = seg_start + rank
    d1 = jnp.sum(jnp.where(lane_f == i1, pos, 0.0), axis=1, keepdims=True)
    d2 = jnp.sum(jnp.where(lane_f == i2, pos, 0.0), axis=1, keepdims=True)
    route = jnp.where(lane == 0, d1, jnp.where(lane == 1, d2, 0.0))
    route_ref[0] = route
    r_hi, r_lo = _split(route)
    eye = jnp.where(lax.broadcasted_iota(jnp.int32, (8, LANE), 0) == lane8, 1.0, 0.0).astype(BF16)
    route_t_ref[0] = _dot_nt(eye, r_hi) + _dot_nt(eye, r_lo)
    cnt_ref[0] = jnp.broadcast_to(pc, (8, LANE))


def _merge_chain(rows, x_ref, ypool_ref, o_ref, kmem_ref, vmem_ref,
                 lng_ref, lnb_ref, wgl_ref, bgate_ref, wpu_ref, wau_ref, wout_ref,
                 ln1g_ref, ln1b_ref, wmq_ref, wmo_ref, ln2g_ref, ln2b_ref,
                 wrh_ref, wrl_ref, br_ref, h2_ref, h2b_ref, comb_ref):
    n_rows = rows.stop - rows.start
    h = _ln(x_ref[0, rows, :], lng_ref[...], lnb_ref[...])
    hb = h.astype(BF16)
    gl = _dot(hb, wgl_ref[...]) + bgate_ref[...]
    gates = 0.5 * jnp.tanh(0.5 * gl) + 0.5
    y_pool = _dot(ypool_ref[0, rows, :], wpu_ref[...])
    o_cat = jnp.concatenate([o_ref[0, p, rows, :] for p in range(ATTN_W // LANE)], axis=1)
    y_attn = _dot(o_cat, wau_ref[...])
    merged = gates[:, 0:D] * y_pool + gates[:, D:2 * D] * y_attn
    mix = _dot(merged.astype(BF16), wout_ref[...])
    h1 = _ln(DN_ALPHA * h + mix, ln1g_ref[...], ln1b_ref[...])

    qm = _dot(h1.astype(BF16), wmq_ref[...]).astype(BF16)
    outs = []
    for hd in range(MEM_HEADS):
        cols = slice(hd * MEM_HD, (hd + 1) * MEM_HD)
        sc = _dot_nt(qm[:, cols], kmem_ref[0, :, cols])
        m = jnp.max(sc, axis=1, keepdims=True)
        e = jnp.exp(sc - m)
        l = jnp.sum(e, axis=1, keepdims=True)
        outs.append(_dot(e.astype(BF16), vmem_ref[0, :, cols]) / l)
    om = jnp.concatenate(outs, axis=1).astype(BF16)
    xa = _dot(om, wmo_ref[...])
    h2 = _ln(DN_ALPHA * h1 + xa, ln2g_ref[...], ln2b_ref[...])
    h2_ref[0, rows, :] = h2
    h2b_ref[0, rows, :] = h2.astype(BF16)

    x_hi, x_lo = _split(h2)
    r = _dot(x_hi, wrh_ref[...]) + _dot(x_lo, wrh_ref[...]) + _dot(x_hi, wrl_ref[...]) + br_ref[...]
    lane = lax.broadcasted_iota(jnp.int32, (n_rows, LANE), 1)
    lane_f = lane.astype(F32)
    cmask = (lane >= N_EXPERTS) & (lane < 2 * N_EXPERTS)
    c = jnp.where(cmask, r, -jnp.inf)
    cmax = jnp.max(c, axis=1, keepdims=True)
    ce = jnp.exp(c - cmax)
    csum = jnp.sum(ce, axis=1, keepdims=True) * (1.0 / EPG)
    g_prob = 1.0 / csum
    grp_lane = ((lane & (N_EXPERTS - 1)) >> 3).astype(F32)
    gidx = jnp.min(jnp.where(cmask & (c == cmax), grp_lane, 99.0), axis=1, keepdims=True)
    fmask = (lane < N_EXPERTS) & (grp_lane == gidx)
    f = jnp.where(fmask, r, -jnp.inf)
    fmax = jnp.max(f, axis=1, keepdims=True)
    fe = jnp.exp(f - fmax)
    fsum = jnp.sum(fe, axis=1, keepdims=True)
    prob = fe / fsum
    p1 = jnp.max(prob, axis=1, keepdims=True)
    i1 = jnp.min(jnp.where(fmask & (prob == p1), lane_f, 999.0), axis=1, keepdims=True)
    rest = fmask & (lane_f != i1)
    prob2 = jnp.where(rest, prob, -1.0)
    p2 = jnp.max(prob2, axis=1, keepdims=True)
    i2 = jnp.min(jnp.where(rest & (prob2 == p2), lane_f, 999.0), axis=1, keepdims=True)
    den = p1 + p2
    comb = jnp.where(lane_f == i1, g_prob * (p1 / den),
                     jnp.where(lane_f == i2, g_prob * (p2 / den), 0.0))
    comb_ref[0, rows, :] = comb
    return i1, i2


def _chunk_copies(sub, nch_ref, go_ref, make_copy, act):
    def body(k, carry):
        go = pl.multiple_of(go_ref[sub * MAX_CHUNKS + k], SEG_ALIGN)
        act(make_copy(pl.multiple_of(k * SEG_ALIGN, SEG_ALIGN), go))
        return carry
    lax.fori_loop(0, nch_ref[sub], body, 0)


def _dispatch_kernel(nch_ref, go_ref, tail_ref, xb_ref, comb_ref, route_t_ref, xs_ref,
                     cbuf, zbuf, sem, zsem, *, n_sub, rows):
    i = pl.program_id(0)
    slot = lax.rem(i, 2)

    def copies(sub, slot_, act):
        def mk(lo, go):
            return pltpu.make_async_copy(cbuf.at[slot_, pl.ds(lo, SEG_ALIGN), :],
                                         xs_ref.at[pl.ds(go, SEG_ALIGN), :], sem.at[slot_])
        _chunk_copies(sub, nch_ref, go_ref, mk, act)

    @pl.when(i >= 2)
    def _():
        copies(i - 2, slot, lambda c: c.wait())

    tm = xb_ref.shape[0]
    r_io = lax.broadcasted_iota(jnp.int32, (rows, tm), 0).astype(F32)
    d1 = route_t_ref[0, 0:1, :]
    d2 = route_t_ref[0, 1:2, :]
    p_mat = jnp.where((r_io == d1) | (r_io == d2), 1.0, 0.0).astype(BF16)
    c_hi, c_lo = _split(comb_ref[...])
    x_aug = jnp.concatenate([xb_ref[...], c_hi, c_lo], axis=1)
    cbuf[slot] = _dot(p_mat, x_aug).astype(BF16)
    copies(i, slot, lambda c: c.start())

    @pl.when(i == n_sub - 1)
    def _():
        if n_sub >= 2:
            copies(i - 1, 1 - slot, lambda c: c.wait())
        copies(i, slot, lambda c: c.wait())
        zbuf[...] = jnp.zeros_like(zbuf)

        def tails(act):
            def body(e, carry):
                st = tail_ref[e]
                n = tail_ref[N_EXPERTS + e]
                off = jnp.int32(0)
                size = zbuf.shape[0]
                while size >= SEG_ALIGN:
                    bit = (n & size) != 0

                    @pl.when(bit)
                    def _(size=size, off=off):
                        act(pltpu.make_async_copy(
                            zbuf.at[pl.ds(0, size), :],
                            xs_ref.at[pl.ds(pl.multiple_of(st + off, SEG_ALIGN), size), :], zsem))
                    off = off + jnp.where(bit, size, 0)
                    size //= 2
                return carry
            lax.fori_loop(0, N_EXPERTS, body, 0)
        tails(lambda c: c.start())
        tails(lambda c: c.wait())


def _ffn_kernel(texp_ref, nused_ref, xs_ref, wg_ref, wu_ref, wd_ref, ys_ref):
    t = pl.program_id(0)

    @pl.when(t < nused_ref[0])
    def _():
        e = texp_ref[t]
        xa = xs_ref[...]
        xrow = xa[:, 0:D]
        cw = xa[:, D:D + LANE].astype(F32) + xa[:, D + LANE:D + 2 * LANE].astype(F32)
        lane = lax.broadcasted_iota(jnp.int32, cw.shape, 1)
        c = jnp.sum(jnp.where(lane == e, cw, 0.0), axis=1, keepdims=True)
        a = _dot(xrow, wg_ref[0])
        b = _dot(xrow, wu_ref[0])
        hid = (a * jax.nn.sigmoid(a)) * b * c
        ys_ref[...] = _dot(hid.astype(BF16), wd_ref[0]).astype(ys_ref.dtype)


def _combine_kernel(nch_ref, go_ref, route_ref, h2_ref, g_ref, b_ref, ys_ref, out_ref,
                    ybuf, sem, *, n_sub, rows):
    i = pl.program_id(0)
    slot = lax.rem(i, 2)

    def copies(sub, slot_, act):
        def mk(lo, go):
            return pltpu.make_async_copy(ys_ref.at[pl.ds(go, SEG_ALIGN), :],
                                         ybuf.at[slot_, pl.ds(lo, SEG_ALIGN), :], sem.at[slot_])
        _chunk_copies(sub, nch_ref, go_ref, mk, act)

    @pl.when(i == 0)
    def _():
        ybuf[...] = jnp.zeros_like(ybuf)
        copies(0, 0, lambda c: c.start())

    @pl.when(i + 1 < n_sub)
    def _():
        copies(i + 1, 1 - slot, lambda c: c.start())

    copies(i, slot, lambda c: c.wait())
    tm = h2_ref.shape[0]
    r_io = lax.broadcasted_iota(jnp.int32, (tm, rows), 1).astype(F32)
    d1 = route_ref[:, 0:1]
    d2 = route_ref[:, 1:2]
    p_t = jnp.where((r_io == d1) | (r_io == d2), 1.0, 0.0).astype(BF16)
    ff = _dot(p_t, ybuf[slot])
    out_ref[...] = _ln(DN_ALPHA * h2_ref[...] + ff, g_ref[...], b_ref[...])


def _bias_kernel(tbl_ref, bkt_ref, out_ref):
    h = pl.program_id(0)
    far = tbl_ref[h, REL_BUCKETS - 1]
    for which in range(2):
        bk = bkt_ref[which]
        acc = jnp.where(bk < 0, NEG, 0.0)
        for kk in range(REL_BUCKETS):
            acc = jnp.where(bk == kk, tbl_ref[h, kk] - far, acc)
        out_ref[which, 0] = acc


def _rel_bucket_table(dist):
    max_exact = REL_BUCKETS // 2
    d = jnp.maximum(dist, 0)
    large = max_exact + (jnp.log(jnp.maximum(d, 1).astype(F32) / max_exact)
                         / math.log(REL_MAX_DIST / max_exact) * (REL_BUCKETS - max_exact)).astype(jnp.int32)
    large = jnp.minimum(large, REL_BUCKETS - 1)
    return jnp.where(d < max_exact, d, large)


def _const_spec(shape):
    nd = len(shape)
    return pl.BlockSpec(shape, lambda *_: (0,) * nd)


def kernel(x, mem, ln_in_g, ln_in_b, rel_bias, w_in, b_gate, w_pool_grp, pool_scale, w_pool_up, w_attn_up,
           w_mix_out, ln1_g, ln1_b, w_mq, w_mk, w_mv, w_mo, ln2_g, ln2_b, w_coarse, b_coarse, w_fine, b_fine,
           w_gate, w_up, w_down, ln3_g, ln3_b):
    B, S, _ = x.shape
    assert S == N_BLK * BLK and w_in.shape[0] == 1
    M = mem.shape[1]
    T = B * S
    tm = 512

    wi = w_in[0]
    w_u = wi[:, 0:POOL_W]
    w_q = wi[:, POOL_W:POOL_W + ATTN_W] * (HEAD_DIM ** -0.5)
    w_k = wi[:, POOL_W + ATTN_W:POOL_W + 2 * ATTN_W]
    w_v = wi[:, POOL_W + 2 * ATTN_W:POOL_W + 3 * ATTN_W]
    w_gl = wi[:, POOL_W + 3 * ATTN_W:]

    def pad_heads(w):
        w = w.reshape(D, N_HEADS, HEAD_DIM)
        return jnp.pad(w, ((0, 0), (0, 0), (0, LANE - HEAD_DIM))).reshape(D, N_HEADS * LANE)

    w1 = jnp.concatenate([w_u, pad_heads(w_q), pad_heads(w_k), w_v], axis=1).astype(BF16)
    row2 = lambda a: a.reshape(1, -1)

    iq = jnp.arange(BLK, dtype=jnp.int32)[:, None]
    ik = jnp.arange(BLK, dtype=jnp.int32)[None, :]
    d_own = iq - ik
    bkt = jnp.stack([jnp.where(d_own >= 0, _rel_bucket_table(d_own), -1), _rel_bucket_table(d_own + BLK)])
    t_bias = pl.pallas_call(
        _bias_kernel,
        grid=(N_HEADS,),
        in_specs=[pl.BlockSpec(memory_space=pltpu.SMEM), _const_spec((2, BLK, BLK))],
        out_specs=pl.BlockSpec((2, 1, BLK, BLK), lambda h: (0, h, 0, 0)),
        out_shape=jax.ShapeDtypeStruct((2, N_HEADS, BLK, BLK), F32),
        name="relbias_tiles",
    )(rel_bias.T, bkt)

    n_w1 = w1.shape[1]
    ypool, q_aug, k_aug, v_p = pl.pallas_call(
        functools.partial(_proj_kernel, tm=tm),
        grid=(B, S // tm),
        in_specs=[
            pl.BlockSpec((1, tm, D), lambda b, s: (b, s, 0)),
            _const_spec((1, D)), _const_spec((1, D)),
            _const_spec((D, n_w1)),
            _const_spec((len(POOL_WINDOWS), LANE, LANE)),
            _const_spec((1, POOL_W)),
        ],
        out_specs=[
            pl.BlockSpec((1, tm, POOL_W), lambda b, s: (b, s, 0)),
            pl.BlockSpec((1, N_HEADS, tm, LANE), lambda b, s: (b, 0, s, 0)),
            pl.BlockSpec((1, N_HEADS, tm, LANE), lambda b, s: (b, 0, s, 0)),
            pl.BlockSpec((1, ATTN_W // LANE, tm, LANE), lambda b, s: (b, 0, s, 0)),
        ],
        out_shape=[
            jax.ShapeDtypeStruct((B, S, POOL_W), BF16),
            jax.ShapeDtypeStruct((B, N_HEADS, S, LANE), BF16),
            jax.ShapeDtypeStruct((B, N_HEADS, S, LANE), BF16),
            jax.ShapeDtypeStruct((B, ATTN_W // LANE, S, LANE), BF16),
        ],
        scratch_shapes=[pltpu.VMEM((HALO + tm, POOL_W), F32), pltpu.VMEM((LANE, N_HEADS * LANE), F32)],
        compiler_params=pltpu.CompilerParams(dimension_semantics=("arbitrary", "arbitrary"),
                                             vmem_limit_bytes=VMEM_LIMIT),
        name="proj_pool_gate",
    )(x, row2(ln_in_g), row2(ln_in_b), w1, w_pool_grp[0].astype(BF16), row2(pool_scale[0]))

    o_attn = pl.pallas_call(
        _attn_kernel,
        grid=(B, N_BLK),
        in_specs=[
            pl.BlockSpec((1, N_HEADS, BLK, LANE), lambda b, j: (b, 0, j, 0)),
            pl.BlockSpec((1, N_HEADS, S, LANE), lambda b, j: (b, 0, 0, 0)),
            pl.BlockSpec((1, ATTN_W // LANE, S, LANE), lambda b, j: (b, 0, 0, 0)),
            _const_spec((2, N_HEADS, BLK, BLK)),
        ],
        out_specs=pl.BlockSpec((1, ATTN_W // LANE, BLK, LANE), lambda b, j: (b, 0, j, 0)),
        out_shape=jax.ShapeDtypeStruct((B, ATTN_W // LANE, S, LANE), BF16),
        compiler_params=pltpu.CompilerParams(dimension_semantics=("arbitrary", "arbitrary"),
                                             vmem_limit_bytes=VMEM_LIMIT),
        name="moba_attn",
    )(q_aug, k_aug, v_p, t_bias)

    kmem, vmem = pl.pallas_call(
        _memkv_kernel,
        grid=(B,),
        in_specs=[pl.BlockSpec((1, M, D), lambda b: (b, 0, 0)),
                  _const_spec((D, MEM_W)), _const_spec((D, MEM_W))],
        out_specs=[pl.BlockSpec((1, M, MEM_W), lambda b: (b, 0, 0)),
                   pl.BlockSpec((1, M, MEM_W), lambda b: (b, 0, 0))],
        out_shape=[jax.ShapeDtypeStruct((B, M, MEM_W), BF16)] * 2,
        compiler_params=pltpu.CompilerParams(dimension_semantics=("arbitrary",)),
        name="mem_kv",
    )(mem, w_mk[0].astype(BF16), w_mv[0].astype(BF16))

    w_r = jnp.concatenate([
        w_fine[0].reshape(D, N_EXPERTS),
        jnp.repeat(w_coarse[0], EPG, axis=1),
        jnp.zeros((D, LANE - 2 * N_EXPERTS), F32)], axis=1)
    b_r = jnp.concatenate([
        b_fine[0].reshape(N_EXPERTS), jnp.repeat(b_coarse[0], EPG),
        jnp.zeros((LANE - 2 * N_EXPERTS,), F32)]).reshape(1, LANE)
    w_r_hi = w_r.astype(BF16)
    w_r_lo = (w_r - w_r_hi.astype(F32)).astype(BF16)

    n_sub = T // tm
    sub_idx = lambda b, s: b * (S // tm) + s
    h2, h2b, comb, route, route_t, seg_cnt = pl.pallas_call(
        functools.partial(_merge_kernel, tm=tm),
        grid=(B, S // tm),
        in_specs=[
            pl.BlockSpec((1, tm, D), lambda b, s: (b, s, 0)),
            pl.BlockSpec((1, tm, POOL_W), lambda b, s: (b, s, 0)),
            pl.BlockSpec((1, ATTN_W // LANE, tm, LANE), lambda b, s: (b, 0, s, 0)),
            pl.BlockSpec((1, M, MEM_W), lambda b, s: (b, 0, 0)),
            pl.BlockSpec((1, M, MEM_W), lambda b, s: (b, 0, 0)),
            _const_spec((1, D)), _const_spec((1, D)),
            _const_spec((D, 2 * D)), _const_spec((1, 2 * D)),
            _const_spec((POOL_W, D)), _const_spec((ATTN_W, D)), _const_spec((D, D)),
            _const_spec((1, D)), _const_spec((1, D)),
            _const_spec((D, MEM_W)), _const_spec((MEM_W, D)),
            _const_spec((1, D)), _const_spec((1, D)),
            _const_spec((D, LANE)), _const_spec((D, LANE)), _const_spec((1, LANE)),
        ],
        out_specs=[
            pl.BlockSpec((1, tm, D), lambda b, s: (b, s, 0)),
            pl.BlockSpec((1, tm, D), lambda b, s: (b, s, 0)),
            pl.BlockSpec((1, tm, LANE), lambda b, s: (b, s, 0)),
            pl.BlockSpec((1, tm, LANE), lambda b, s: (b, s, 0)),
            pl.BlockSpec((1, 8, tm), lambda b, s: (sub_idx(b, s), 0, 0)),
            pl.BlockSpec((1, 8, LANE), lambda b, s: (sub_idx(b, s), 0, 0)),
        ],
        out_shape=[
            jax.ShapeDtypeStruct((B, S, D), F32),
            jax.ShapeDtypeStruct((B, S, D), BF16),
            jax.ShapeDtypeStruct((B, S, LANE), F32),
            jax.ShapeDtypeStruct((B, S, LANE), F32),
            jax.ShapeDtypeStruct((n_sub, 8, tm), F32),
            jax.ShapeDtypeStruct((n_sub, 8, LANE), F32),
        ],
        compiler_params=pltpu.CompilerParams(dimension_semantics=("arbitrary", "arbitrary"),
                                             vmem_limit_bytes=VMEM_LIMIT),
        name="merge_memattn_router",
    )(x, ypool, o_attn, kmem, vmem,
      row2(ln_in_g), row2(ln_in_b), w_gl.astype(BF16), row2(b_gate[0]),
      w_pool_up[0].astype(BF16), w_attn_up[0].astype(BF16), w_mix_out[0].astype(BF16),
      row2(ln1_g[0]), row2(ln1_b[0]),
      (w_mq[0] * (MEM_HD ** -0.5)).astype(BF16), w_mo[0].astype(BF16),
      row2(ln2_g[0]), row2(ln2_b[0]),
      w_r_hi, w_r_lo, b_r)

    pcs = seg_cnt[:, 0, :N_EXPERTS].astype(jnp.int32)
    tot = jnp.sum(pcs, axis=0)
    cap = ((tot + FFN_TILE - 1) // FFN_TILE) * FFN_TILE
    ends = jnp.cumsum(cap)
    base = ends - cap
    gs = base[None, :] + jnp.cumsum(pcs, axis=0) - pcs
    ls = jnp.cumsum(pcs, axis=1) - pcs
    n_sorted = n_sub * COMPACT_ROWS + N_EXPERTS * FFN_TILE
    n_ffn_tiles = n_sorted // FFN_TILE
    n_used = (ends[-1] // FFN_TILE).astype(jnp.int32)
    tile_row = jnp.arange(n_ffn_tiles, dtype=jnp.int32) * FFN_TILE
    tile_exp = jnp.sum(jnp.minimum(tile_row, ends[-1] - 1)[:, None] >= ends[None, :], axis=1).astype(jnp.int32)
    tails = jnp.concatenate([base + tot, cap - tot]).astype(jnp.int32)
    chunk_row = jnp.arange(MAX_CHUNKS, dtype=jnp.int32) * SEG_ALIGN
    chunk_exp = jnp.sum(chunk_row[None, :, None] >= (ls + pcs)[:, None, :], axis=2)
    shift = jnp.sum(jnp.where(chunk_exp[:, :, None] == jnp.arange(N_EXPERTS)[None, None, :],
                              (gs - ls)[:, None, :], 0), axis=2)
    chunk_go = (chunk_row[None, :] + shift).astype(jnp.int32).reshape(-1)
    n_chunks = (jnp.sum(pcs, axis=1) // SEG_ALIGN).astype(jnp.int32)

    aug_w = D + 2 * LANE
    x_sorted = pl.pallas_call(
        functools.partial(_dispatch_kernel, n_sub=n_sub, rows=COMPACT_ROWS),
        grid_spec=pltpu.PrefetchScalarGridSpec(
            num_scalar_prefetch=3,
            grid=(n_sub,),
            in_specs=[
                pl.BlockSpec((tm, D), lambda i, *_: (i, 0)),
                pl.BlockSpec((tm, LANE), lambda i, *_: (i, 0)),
                pl.BlockSpec((1, 8, tm), lambda i, *_: (i, 0, 0)),
            ],
            out_specs=pl.BlockSpec(memory_space=pl.ANY),
            scratch_shapes=[
                pltpu.VMEM((2, COMPACT_ROWS, aug_w), BF16),
                pltpu.VMEM((FFN_TILE // 2, aug_w), BF16),
                pltpu.SemaphoreType.DMA((2,)),
                pltpu.SemaphoreType.DMA(()),
            ],
        ),
        out_shape=jax.ShapeDtypeStruct((n_sorted, aug_w), BF16),
        compiler_params=pltpu.CompilerParams(dimension_semantics=("arbitrary",), vmem_limit_bytes=VMEM_LIMIT),
        name="moe_dispatch",
    )(n_chunks, chunk_go, tails, h2b.reshape(T, D), comb.reshape(T, LANE), route_t)

    used_tile = lambda t, texp, nused: (jnp.minimum(t, nused[0] - 1), 0)
    y_sorted = pl.pallas_call(
        _ffn_kernel,
        grid_spec=pltpu.PrefetchScalarGridSpec(
            num_scalar_prefetch=2,
            grid=(n_ffn_tiles,),
            in_specs=[
                pl.BlockSpec((FFN_TILE, aug_w), used_tile),
                pl.BlockSpec((1, D, FF), lambda t, texp, nused: (texp[t], 0, 0)),
                pl.BlockSpec((1, D, FF), lambda t, texp, nused: (texp[t], 0, 0)),
                pl.BlockSpec((1, FF, D), lambda t, texp, nused: (texp[t], 0, 0)),
            ],
            out_specs=pl.BlockSpec((FFN_TILE, D), used_tile),
        ),
        out_shape=jax.ShapeDtypeStruct((n_sorted, D), BF16),
        compiler_params=pltpu.CompilerParams(dimension_semantics=("arbitrary",), vmem_limit_bytes=VMEM_LIMIT),
        name="moe_expert_ffn",
    )(tile_exp, n_used.reshape(1), x_sorted,
      w_gate[0].astype(BF16), w_up[0].astype(BF16), w_down[0].astype(BF16))

    out = pl.pallas_call(
        functools.partial(_combine_kernel, n_sub=n_sub, rows=COMPACT_ROWS),
        grid_spec=pltpu.PrefetchScalarGridSpec(
            num_scalar_prefetch=2,
            grid=(n_sub,),
            in_specs=[
                pl.BlockSpec((tm, LANE), lambda i, *_: (i, 0)),
                pl.BlockSpec((tm, D), lambda i, *_: (i, 0)),
                pl.BlockSpec((1, D), lambda i, *_: (0, 0)),
                pl.BlockSpec((1, D), lambda i, *_: (0, 0)),
                pl.BlockSpec(memory_space=pl.ANY),
            ],
            out_specs=pl.BlockSpec((tm, D), lambda i, *_: (i, 0)),
            scratch_shapes=[
                pltpu.VMEM((2, COMPACT_ROWS, D), BF16),
                pltpu.SemaphoreType.DMA((2,)),
            ],
        ),
        out_shape=jax.ShapeDtypeStruct((T, D), F32),
        compiler_params=pltpu.CompilerParams(dimension_semantics=("arbitrary",), vmem_limit_bytes=VMEM_LIMIT),
        name="moe_combine_ln3",
    )(n_chunks, chunk_go, route.reshape(T, LANE), h2.reshape(T, D), row2(ln3_g[0]), row2(ln3_b[0]), y_sorted)
    return out.reshape(B, S, D)
```

```python
import functools
import math

import jax
import jax.numpy as jnp
from jax import lax
from jax.experimental import pallas as pl
from jax.experimental.pallas import tpu as pltpu

D = 1024
POOL_WINDOWS = (2, 4, 8, 16)
POOL_W = 512
N_HEADS = 8
HEAD_DIM = 64
ATTN_W = 512
BLK = 256
N_BLK = 8
TOPK = 3
REL_BUCKETS = 32
REL_MAX_DIST = 128
MEM_HEADS = 4
MEM_HD = 128
MEM_W = 512
N_GROUPS = 4
EPG = 8
N_EXPERTS = 32
FF = 256
DN_ALPHA = 2.0 ** 0.25
LN_EPS = 1e-5

LANE = 128
TOK_TILE = 512
SEG_ALIGN = 16
COMPACT_ROWS = 2 * TOK_TILE + N_EXPERTS * SEG_ALIGN
MAX_CHUNKS = COMPACT_ROWS // SEG_ALIGN
FFN_TILE = 512
CHAIN_ROWS = 512
HALO = 16
NEG = -1e30
VMEM_LIMIT = 56 * 1024 * 1024

F32 = jnp.float32
BF16 = jnp.bfloat16

_NT = (((1,), (1,)), ((), ()))


def _dot(a, b):
    return jnp.dot(a, b, preferred_element_type=F32)


def _dot_nt(a, b):
    return lax.dot_general(a, b, _NT, preferred_element_type=F32)


def _split(a):
    hi = a.astype(BF16)
    lo = (a - hi.astype(F32)).astype(BF16)
    return hi, lo


def _interleave(chains):
    results = [None] * len(chains)
    live = list(range(len(chains)))
    while live:
        for ci in list(live):
            try:
                next(chains[ci])
            except StopIteration as done:
                results[ci] = done.value
                live.remove(ci)
    return results


def _ln(x, g, b):
    mu = jnp.mean(x, axis=-1, keepdims=True)
    xc = x - mu
    var = jnp.mean(xc * xc, axis=-1, keepdims=True)
    return xc * lax.rsqrt(var + LN_EPS) * g + b


def _proj_kernel(x_ref, g_ref, b_ref, w_ref, wgrp_ref, pscale_ref,
                 ypool_ref, q_ref, k_ref, v_ref, ubuf, kbt, *, tm):
    s = pl.program_id(1)
    nb_tile = tm // BLK

    h = _ln(x_ref[0], g_ref[...], b_ref[...])
    hb = h.astype(BF16)
    zu = _dot(hb, w_ref[:, 0:POOL_W])
    zq = _dot(hb, w_ref[:, POOL_W:POOL_W + ATTN_W])
    zk = _dot(hb, w_ref[:, POOL_W + ATTN_W:POOL_W + 2 * ATTN_W])
    zv = _dot(hb, w_ref[:, POOL_W + 2 * ATTN_W:])

    @pl.when(s == 0)
    def _():
        ubuf[0:HALO, :] = jnp.zeros((HALO, POOL_W), F32)
        kbt[...] = jnp.zeros_like(kbt)

    ubuf[HALO:HALO + tm, :] = zu
    t_pos = s * tm + lax.broadcasted_iota(jnp.int32, (tm, LANE), 0)
    for g, w in enumerate(POOL_WINDOWS):
        cols = slice(g * LANE, (g + 1) * LANE)
        ws = ubuf[HALO:HALO + tm, cols]
        for kk in range(1, w):
            ws = ws + ubuf[HALO - kk:HALO - kk + tm, cols]
        cnt = jnp.minimum(t_pos + 1, w).astype(F32)
        y = ws / cnt - ubuf[HALO:HALO + tm, cols]
        yg = _dot(y.astype(BF16), wgrp_ref[g]) * pscale_ref[:, cols]
        ypool_ref[0, :, cols] = yg.astype(ypool_ref.dtype)
    ubuf[0:HALO, :] = ubuf[tm:tm + HALO, :]

    r_io = lax.broadcasted_iota(jnp.int32, kbt.shape, 0)
    c_io = lax.broadcasted_iota(jnp.int32, kbt.shape, 1)
    head_match = (r_io >> 3) == (c_io >> 6)
    for bi in range(nb_tile):
        n = s * nb_tile + bi
        kmean = jnp.mean(zk[bi * BLK:(bi + 1) * BLK], axis=0, keepdims=True)
        kbt[...] = jnp.where(head_match & ((r_io & 7) == n), kmean, kbt[...])

    q_hi, q_lo = _split(zq)
    kb_hi, kb_lo = _split(kbt[...])
    gate = _dot_nt(q_hi, kb_hi) + _dot_nt(q_lo, kb_hi) + _dot_nt(q_hi, kb_lo)

    lane = lax.broadcasted_iota(jnp.int32, (tm, LANE), 1)
    row = lax.broadcasted_iota(jnp.int32, (tm, LANE), 0)
    n_l = lane & 7
    jrow = s * nb_tile + (row >> 8)
    past = n_l < jrow
    gt = jnp.where(past, gate, -jnp.inf)
    cnt = jnp.zeros((tm, LANE), F32)
    for sh in range(1, N_BLK):
        wrap = (n_l + sh) >= N_BLK
        gm = jnp.where(wrap, pltpu.roll(gt, N_BLK - sh, 1), pltpu.roll(gt, LANE - sh, 1))
        cnt = cnt + jnp.where(wrap, jnp.where(gm >= gt, 1.0, 0.0), jnp.where(gm > gt, 1.0, 0.0))
    keep = (past & (cnt < TOPK)) | (n_l == jrow)
    negmask = jnp.where(keep, 0.0, NEG)

    aug_lane = (lane >= HEAD_DIM) & (lane < HEAD_DIM + N_BLK)
    k_onehot = jnp.where(lane == HEAD_DIM + jrow, 1.0, 0.0)
    head_lane = lane < HEAD_DIM
    for hh in range(N_HEADS):
        cols = slice((hh // 2) * LANE, (hh // 2 + 1) * LANE)
        q_h, k_h = zq[:, cols], zk[:, cols]
        if hh % 2:
            q_h, k_h = pltpu.roll(q_h, HEAD_DIM, 1), pltpu.roll(k_h, HEAD_DIM, 1)
        m_h = jnp.where(aug_lane, pltpu.roll(negmask, HEAD_DIM - N_BLK * hh, 1), 0.0)
        q_ref[0, hh] = jnp.where(head_lane, q_h, m_h).astype(q_ref.dtype)
        k_ref[0, hh] = jnp.where(head_lane, k_h, k_onehot).astype(k_ref.dtype)
    for p in range(ATTN_W // LANE):
        v_ref[0, p] = zv[:, p * LANE:(p + 1) * LANE].astype(v_ref.dtype)


def _attn_kernel(q_ref, k_ref, v_ref, tb_ref, o_ref):
    j = pl.program_id(1)
    lane = lax.broadcasted_iota(jnp.int32, (BLK, LANE), 1)

    def one_head(h, p, jj):
        q = q_ref[0, h]
        own0 = jj * BLK
        pieces = []
        s_own = _dot_nt(q, k_ref[0, h, own0:own0 + BLK, :]) + tb_ref[0, h]
        pieces.append((s_own, own0, BLK))
        if jj >= 1:
            s_adj = _dot_nt(q, k_ref[0, h, own0 - BLK:own0, :]) + tb_ref[1, h]
            pieces.append((s_adj, own0 - BLK, BLK))
        if jj >= 2:
            s_far = _dot_nt(q, k_ref[0, h, 0:own0 - BLK, :])
            pieces.append((s_far, 0, own0 - BLK))
        m = None
        for sc, _, _ in pieces:
            mm = jnp.max(sc, axis=1, keepdims=True)
            m = mm if m is None else jnp.maximum(m, mm)
        l = None
        acc = None
        for sc, start, size in pieces:
            e = jnp.exp(sc - m)
            ls = jnp.sum(e, axis=1, keepdims=True)
            l = ls if l is None else l + ls
            pv = _dot(e.astype(BF16), v_ref[0, p, start:start + size, :])
            acc = pv if acc is None else acc + pv
        return acc / l

    for jj in range(N_BLK):
        @pl.when(j == jj)
        def _(jj=jj):
            def pair(p, carry):
                o0 = one_head(2 * p, p, jj)
                o1 = one_head(2 * p + 1, p, jj)
                o_ref[0, p] = jnp.where(lane < HEAD_DIM, o0, o1).astype(o_ref.dtype)
                return carry
            lax.fori_loop(0, N_HEADS // 2, pair, 0)


def _memkv_kernel(mem_ref, wk_ref, wv_ref, k_ref, v_ref):
    mb = mem_ref[0].astype(BF16)
    k_ref[0] = _dot(mb, wk_ref[...]).astype(k_ref.dtype)
    v_ref[0] = _dot(mb, wv_ref[...]).astype(v_ref.dtype)


def _merge_kernel(x_ref, ypool_ref, o_ref, kmem_ref, vmem_ref,
                  lng_ref, lnb_ref, wgl_ref, bgate_ref, wpu_ref, wau_ref, wout_ref,
                  ln1g_ref, ln1b_ref, wmq_ref, wmo_ref, ln2g_ref, ln2b_ref,
                  wrh_ref, wrl_ref, br_ref,
                  h2_ref, h2b_ref, comb_ref, route_ref, route_t_ref, cnt_ref, *, tm):
    n_chain = tm // CHAIN_ROWS
    parts = _interleave([_merge_chain(slice(ci * CHAIN_ROWS, (ci + 1) * CHAIN_ROWS),
                                      x_ref, ypool_ref, o_ref, kmem_ref, vmem_ref,
                                      lng_ref, lnb_ref, wgl_ref, bgate_ref, wpu_ref, wau_ref, wout_ref,
                                      ln1g_ref, ln1b_ref, wmq_ref, wmo_ref, ln2g_ref, ln2b_ref,
                                      wrh_ref, wrl_ref, br_ref, h2_ref, h2b_ref, comb_ref)
                         for ci in range(n_chain)])
    i1 = jnp.concatenate([p[0] for p in parts], axis=0)
    i2 = jnp.concatenate([p[1] for p in parts], axis=0)
    lane = lax.broadcasted_iota(jnp.int32, (tm, LANE), 1)
    lane_f = lane.astype(F32)

    sel = jnp.where((lane_f == i1) | (lane_f == i2), 1.0, 0.0)
    cnt = jnp.sum(sel, axis=0, keepdims=True)
    pc = jnp.floor((cnt + (SEG_ALIGN - 1)) * (1.0 / SEG_ALIGN)) * SEG_ALIGN
    lane8 = lax.broadcasted_iota(jnp.int32, (8, LANE), 1)
    inc = jnp.broadcast_to(pc, (8, LANE))
    for sh in (1, 2, 4, 8, 16, 32, 64):
        inc = inc + jnp.where(lane8 >= sh, pltpu.roll(inc, sh, 1), 0.0)
    seg_start = inc[0:1] - pc
    t_row = lax.broadcasted_iota(jnp.int32, (tm, tm), 0)
    t_col = lax.broadcasted_iota(jnp.int32, (tm, tm), 1)
    earlier = jnp.where(t_row > t_col, 1.0, 0.0).astype(BF16)
    rank = _dot(earlier, sel.astype(BF16))
    pos = seg_start + rank
    d1 = jnp.sum(jnp.where(lane_f == i1, pos, 0.0), axis=1, keepdims=True)
    d2 = jnp.sum(jnp.where(lane_f == i2, pos, 0.0), axis=1, keepdims=True)
    route = jnp.where(lane == 0, d1, jnp.where(lane == 1, d2, 0.0))
    route_ref[0] = route
    r_hi, r_lo = _split(route)
    eye = jnp.where(lax.broadcasted_iota(jnp.int32, (8, LANE), 0) == lane8, 1.0, 0.0).astype(BF16)
    route_t_ref[0] = _dot_nt(eye, r_hi) + _dot_nt(eye, r_lo)
    cnt_ref[0] = jnp.broadcast_to(pc, (8, LANE))


def _merge_chain(rows, x_ref, ypool_ref, o_ref, kmem_ref, vmem_ref,
                 lng_ref, lnb_ref, wgl_ref, bgate_ref, wpu_ref, wau_ref, wout_ref,
                 ln1g_ref, ln1b_ref, wmq_ref, wmo_ref, ln2g_ref, ln2b_ref,
                 wrh_ref, wrl_ref, br_ref, h2_ref, h2b_ref, comb_ref):
    n_rows = rows.stop - rows.start
    h = _ln(x_ref[0, rows, :], lng_ref[...], lnb_ref[...])
    hb = h.astype(BF16)
    gl = _dot(hb, wgl_ref[...]) + bgate_ref[...]
    yield
    gates = 0.5 * jnp.tanh(0.5 * gl) + 0.5
    y_pool = _dot(ypool_ref[0, rows, :], wpu_ref[...])
    o_cat = jnp.concatenate([o_ref[0, p, rows, :] for p in range(ATTN_W // LANE)], axis=1)
    y_attn = _dot(o_cat, wau_ref[...])
    yield
    merged = gates[:, 0:D] * y_pool + gates[:, D:2 * D] * y_attn
    mix = _dot(merged.astype(BF16), wout_ref[...])
    yield
    h1 = _ln(DN_ALPHA * h + mix, ln1g_ref[...], ln1b_ref[...])

    qm = _dot(h1.astype(BF16), wmq_ref[...]).astype(BF16)
    yield
    outs = []
    for hd in range(MEM_HEADS):
        cols = slice(hd * MEM_HD, (hd + 1) * MEM_HD)
        sc = _dot_nt(qm[:, cols], kmem_ref[0, :, cols])
        m = jnp.max(sc, axis=1, keepdims=True)
        e = jnp.exp(sc - m)
        l = jnp.sum(e, axis=1, keepdims=True)
        outs.append(_dot(e.astype(BF16), vmem_ref[0, :, cols]) / l)
    om = jnp.concatenate(outs, axis=1).astype(BF16)
    xa = _dot(om, wmo_ref[...])
    yield
    h2 = _ln(DN_ALPHA * h1 + xa, ln2g_ref[...], ln2b_ref[...])
    h2_ref[0, rows, :] = h2
    h2b_ref[0, rows, :] = h2.astype(BF16)

    x_hi, x_lo = _split(h2)
    r = _dot(x_hi, wrh_ref[...]) + _dot(x_lo, wrh_ref[...]) + _dot(x_hi, wrl_ref[...]) + br_ref[...]
    yield
    lane = lax.broadcasted_iota(jnp.int32, (n_rows, LANE), 1)
    lane_f = lane.astype(F32)
    cmask = (lane >= N_EXPERTS) & (lane < 2 * N_EXPERTS)
    c = jnp.where(cmask, r, -jnp.inf)
    cmax = jnp.max(c, axis=1, keepdims=True)
    ce = jnp.exp(c - cmax)
    csum = jnp.sum(ce, axis=1, keepdims=True) * (1.0 / EPG)
    g_prob = 1.0 / csum
    grp_lane = ((lane & (N_EXPERTS - 1)) >> 3).astype(F32)
    gidx = jnp.min(jnp.where(cmask & (c == cmax), grp_lane, 99.0), axis=1, keepdims=True)
    fmask = (lane < N_EXPERTS) & (grp_lane == gidx)
    f = jnp.where(fmask, r, -jnp.inf)
    fmax = jnp.max(f, axis=1, keepdims=True)
    fe = jnp.exp(f - fmax)
    fsum = jnp.sum(fe, axis=1, keepdims=True)
    prob = fe / fsum
    p1 = jnp.max(prob, axis=1, keepdims=True)
    i1 = jnp.min(jnp.where(fmask & (prob == p1), lane_f, 999.0), axis=1, keepdims=True)
    rest = fmask & (lane_f != i1)
    prob2 = jnp.where(rest, prob, -1.0)
    p2 = jnp.max(prob2, axis=1, keepdims=True)
    i2 = jnp.min(jnp.where(rest & (prob2 == p2), lane_f, 999.0), axis=1, keepdims=True)
    den = p1 + p2
    comb = jnp.where(lane_f == i1, g_prob * (p1 / den),
                     jnp.where(lane_f == i2, g_prob * (p2 / den), 0.0))
    comb_ref[0, rows, :] = comb
    return i1, i2


def _chunk_copies(sub, nch_ref, go_ref, make_copy, act):
    def body(k, carry):
        go = pl.multiple_of(go_ref[sub * MAX_CHUNKS + k], SEG_ALIGN)
        act(make_copy(pl.multiple_of(k * SEG_ALIGN, SEG_ALIGN), go))
        return carry
    lax.fori_loop(0, nch_ref[sub], body, 0)


def _dispatch_kernel(nch_ref, go_ref, tail_ref, xb_ref, comb_ref, route_t_ref, xs_ref,
                     cbuf, zbuf, sem, zsem, *, n_sub, rows):
    i = pl.program_id(0)
    slot = lax.rem(i, 2)

    def copies(sub, slot_, act):
        def mk(lo, go):
            return pltpu.make_async_copy(cbuf.at[slot_, pl.ds(lo, SEG_ALIGN), :],
                                         xs_ref.at[pl.ds(go, SEG_ALIGN), :], sem.at[slot_])
        _chunk_copies(sub, nch_ref, go_ref, mk, act)

    @pl.when(i >= 2)
    def _():
        copies(i - 2, slot, lambda c: c.wait())

    tm = xb_ref.shape[0]
    r_io = lax.broadcasted_iota(jnp.int32, (rows, tm), 0).astype(F32)
    d1 = route_t_ref[0, 0:1, :]
    d2 = route_t_ref[0, 1:2, :]
    p_mat = jnp.where((r_io == d1) | (r_io == d2), 1.0, 0.0).astype(BF16)
    c_hi, c_lo = _split(comb_ref[...])
    x_aug = jnp.concatenate([xb_ref[...], c_hi, c_lo], axis=1)
    cbuf[slot] = _dot(p_mat, x_aug).astype(BF16)
    copies(i, slot, lambda c: c.start())

    @pl.when(i == n_sub - 1)
    def _():
        if n_sub >= 2:
            copies(i - 1, 1 - slot, lambda c: c.wait())
        copies(i, slot, lambda c: c.wait())
        zbuf[...] = jnp.zeros_like(zbuf)

        def tails(act):
            def body(e, carry):
                st = tail_ref[e]
                n = tail_ref[N_EXPERTS + e]
                off = jnp.int32(0)
                size = zbuf.shape[0]
                while size >= SEG_ALIGN:
                    bit = (n & size) != 0

                    @pl.when(bit)
                    def _(size=size, off=off):
                        act(pltpu.make_async_copy(
                            zbuf.at[pl.ds(0, size), :],
                            xs_ref.at[pl.ds(pl.multiple_of(st + off, SEG_ALIGN), size), :], zsem))
                    off = off + jnp.where(bit, size, 0)
                    size //= 2
                return carry
            lax.fori_loop(0, N_EXPERTS, body, 0)
        tails(lambda c: c.start())
        tails(lambda c: c.wait())


def _ffn_kernel(texp_ref, nused_ref, xs_ref, wg_ref, wu_ref, wd_ref, ys_ref, wg_b, wu_b, wd_b):
    t = pl.program_id(0)
    e = texp_ref[t]

    @pl.when((t == 0) | (e != texp_ref[jnp.maximum(t - 1, 0)]))
    def _():
        wg_b[...] = wg_ref[0].astype(BF16)
        wu_b[...] = wu_ref[0].astype(BF16)
        wd_b[...] = wd_ref[0].astype(BF16)

    @pl.when(t < nused_ref[0])
    def _():
        xa = xs_ref[...]
        xrow = xa[:, 0:D]
        cw = xa[:, D:D + LANE].astype(F32) + xa[:, D + LANE:D + 2 * LANE].astype(F32)
        lane = lax.broadcasted_iota(jnp.int32, cw.shape, 1)
        c = jnp.sum(jnp.where(lane == e, cw, 0.0), axis=1, keepdims=True)
        a = _dot(xrow, wg_b[...])
        b = _dot(xrow, wu_b[...])
        hid = (a * jax.nn.sigmoid(a)) * b * c
        ys_ref[...] = _dot(hid.astype(BF16), wd_b[...]).astype(ys_ref.dtype)


def _combine_kernel(nch_ref, go_ref, route_ref, h2_ref, g_ref, b_ref, ys_ref, out_ref,
                    ybuf, sem, *, n_sub, rows):
    i = pl.program_id(0)
    slot = lax.rem(i, 2)

    def copies(sub, slot_, act):
        def mk(lo, go):
            return pltpu.make_async_copy(ys_ref.at[pl.ds(go, SEG_ALIGN), :],
                                         ybuf.at[slot_, pl.ds(lo, SEG_ALIGN), :], sem.at[slot_])
        _chunk_copies(sub, nch_ref, go_ref, mk, act)

    @pl.when(i == 0)
    def _():
        ybuf[...] = jnp.zeros_like(ybuf)
        copies(0, 0, lambda c: c.start())

    @pl.when(i + 1 < n_sub)
    def _():
        copies(i + 1, 1 - slot, lambda c: c.start())

    copies(i, slot, lambda c: c.wait())
    tm = h2_ref.shape[0]
    r_io = lax.broadcasted_iota(jnp.int32, (tm, rows), 1).astype(F32)
    d1 = route_ref[:, 0:1]
    d2 = route_ref[:, 1:2]
    p_t = jnp.where((r_io == d1) | (r_io == d2), 1.0, 0.0).astype(BF16)
    ff = _dot(p_t, ybuf[slot])
    out_ref[...] = _ln(DN_ALPHA * h2_ref[...] + ff, g_ref[...], b_ref[...])


def _bias_kernel(tbl_ref, bkt_ref, out_ref):
    h = pl.program_id(0)
    far = tbl_ref[h, REL_BUCKETS - 1]
    for which in range(2):
        bk = bkt_ref[which]
        acc = jnp.where(bk < 0, NEG, 0.0)
        for kk in range(REL_BUCKETS):
            acc = jnp.where(bk == kk, tbl_ref[h, kk] - far, acc)
        out_ref[which, 0] = acc


def _rel_bucket_table(dist):
    max_exact = REL_BUCKETS // 2
    d = jnp.maximum(dist, 0)
    large = max_exact + (jnp.log(jnp.maximum(d, 1).astype(F32) / max_exact)
                         / math.log(REL_MAX_DIST / max_exact) * (REL_BUCKETS - max_exact)).astype(jnp.int32)
    large = jnp.minimum(large, REL_BUCKETS - 1)
    return jnp.where(d < max_exact, d, large)


def _const_spec(shape):
    nd = len(shape)
    return pl.BlockSpec(shape, lambda *_: (0,) * nd)


def kernel(x, mem, ln_in_g, ln_in_b, rel_bias, w_in, b_gate, w_pool_grp, pool_scale, w_pool_up, w_attn_up,
           w_mix_out, ln1_g, ln1_b, w_mq, w_mk, w_mv, w_mo, ln2_g, ln2_b, w_coarse, b_coarse, w_fine, b_fine,
           w_gate, w_up, w_down, ln3_g, ln3_b):
    B, S, _ = x.shape
    assert S == N_BLK * BLK and w_in.shape[0] == 1
    M = mem.shape[1]
    T = B * S
    tm = 512

    wi = w_in[0]
    w_u = wi[:, 0:POOL_W]
    w_q = wi[:, POOL_W:POOL_W + ATTN_W] * (HEAD_DIM ** -0.5)
    w_k = wi[:, POOL_W + ATTN_W:POOL_W + 2 * ATTN_W]
    w_v = wi[:, POOL_W + 2 * ATTN_W:POOL_W + 3 * ATTN_W]
    w_gl = wi[:, POOL_W + 3 * ATTN_W:]

    w1 = jnp.concatenate([w_u, w_q, w_k, w_v], axis=1).astype(BF16)
    row2 = lambda a: a.reshape(1, -1)

    iq = jnp.arange(BLK, dtype=jnp.int32)[:, None]
    ik = jnp.arange(BLK, dtype=jnp.int32)[None, :]
    d_own = iq - ik
    bkt = jnp.stack([jnp.where(d_own >= 0, _rel_bucket_table(d_own), -1), _rel_bucket_table(d_own + BLK)])
    t_bias = pl.pallas_call(
        _bias_kernel,
        grid=(N_HEADS,),
        in_specs=[pl.BlockSpec(memory_space=pltpu.SMEM), _const_spec((2, BLK, BLK))],
        out_specs=pl.BlockSpec((2, 1, BLK, BLK), lambda h: (0, h, 0, 0)),
        out_shape=jax.ShapeDtypeStruct((2, N_HEADS, BLK, BLK), F32),
        name="relbias_tiles",
    )(rel_bias.T, bkt)

    n_w1 = w1.shape[1]
    ypool, q_aug, k_aug, v_p = pl.pallas_call(
        functools.partial(_proj_kernel, tm=tm),
        grid=(B, S // tm),
        in_specs=[
            pl.BlockSpec((1, tm, D), lambda b, s: (b, s, 0)),
            _const_spec((1, D)), _const_spec((1, D)),
            _const_spec((D, n_w1)),
            _const_spec((len(POOL_WINDOWS), LANE, LANE)),
            _const_spec((1, POOL_W)),
        ],
        out_specs=[
            pl.BlockSpec((1, tm, POOL_W), lambda b, s: (b, s, 0)),
            pl.BlockSpec((1, N_HEADS, tm, LANE), lambda b, s: (b, 0, s, 0)),
            pl.BlockSpec((1, N_HEADS, tm, LANE), lambda b, s: (b, 0, s, 0)),
            pl.BlockSpec((1, ATTN_W // LANE, tm, LANE), lambda b, s: (b, 0, s, 0)),
        ],
        out_shape=[
            jax.ShapeDtypeStruct((B, S, POOL_W), BF16),
            jax.ShapeDtypeStruct((B, N_HEADS, S, LANE), BF16),
            jax.ShapeDtypeStruct((B, N_HEADS, S, LANE), BF16),
            jax.ShapeDtypeStruct((B, ATTN_W // LANE, S, LANE), BF16),
        ],
        scratch_shapes=[pltpu.VMEM((HALO + tm, POOL_W), F32), pltpu.VMEM((LANE, ATTN_W), F32)],
        compiler_params=pltpu.CompilerParams(dimension_semantics=("arbitrary", "arbitrary"),
                                             vmem_limit_bytes=VMEM_LIMIT),
        name="proj_pool_gate",
    )(x, row2(ln_in_g), row2(ln_in_b), w1, w_pool_grp[0].astype(BF16), row2(pool_scale[0]))

    o_attn = pl.pallas_call(
        _attn_kernel,
        grid=(B, N_BLK),
        in_specs=[
            pl.BlockSpec((1, N_HEADS, BLK, LANE), lambda b, j: (b, 0, j, 0)),
            pl.BlockSpec((1, N_HEADS, S, LANE), lambda b, j: (b, 0, 0, 0)),
            pl.BlockSpec((1, ATTN_W // LANE, S, LANE), lambda b, j: (b, 0, 0, 0)),
            _const_spec((2, N_HEADS, BLK, BLK)),
        ],
        out_specs=pl.BlockSpec((1, ATTN_W // LANE, BLK, LANE), lambda b, j: (b, 0, j, 0)),
        out_shape=jax.ShapeDtypeStruct((B, ATTN_W // LANE, S, LANE), BF16),
        compiler_params=pltpu.CompilerParams(dimension_semantics=("arbitrary", "arbitrary"),
                                             vmem_limit_bytes=VMEM_LIMIT),
        name="moba_attn",
    )(q_aug, k_aug, v_p, t_bias)

    kmem, vmem = pl.pallas_call(
        _memkv_kernel,
        grid=(B,),
        in_specs=[pl.BlockSpec((1, M, D), lambda b: (b, 0, 0)),
                  _const_spec((D, MEM_W)), _const_spec((D, MEM_W))],
        out_specs=[pl.BlockSpec((1, M, MEM_W), lambda b: (b, 0, 0)),
                   pl.BlockSpec((1, M, MEM_W), lambda b: (b, 0, 0))],
        out_shape=[jax.ShapeDtypeStruct((B, M, MEM_W), BF16)] * 2,
        compiler_params=pltpu.CompilerParams(dimension_semantics=("arbitrary",)),
        name="mem_kv",
    )(mem, w_mk[0].astype(BF16), w_mv[0].astype(BF16))

    w_r = jnp.concatenate([
        w_fine[0].reshape(D, N_EXPERTS),
        jnp.repeat(w_coarse[0], EPG, axis=1),
        jnp.zeros((D, LANE - 2 * N_EXPERTS), F32)], axis=1)
    b_r = jnp.concatenate([
        b_fine[0].reshape(N_EXPERTS), jnp.repeat(b_coarse[0], EPG),
        jnp.zeros((LANE - 2 * N_EXPERTS,), F32)]).reshape(1, LANE)
    w_r_hi = w_r.astype(BF16)
    w_r_lo = (w_r - w_r_hi.astype(F32)).astype(BF16)

    n_sub = T // tm
    sub_idx = lambda b, s: b * (S // tm) + s
    h2, h2b, comb, route, route_t, seg_cnt = pl.pallas_call(
        functools.partial(_merge_kernel, tm=tm),
        grid=(B, S // tm),
        in_specs=[
            pl.BlockSpec((1, tm, D), lambda b, s: (b, s, 0)),
            pl.BlockSpec((1, tm, POOL_W), lambda b, s: (b, s, 0)),
            pl.BlockSpec((1, ATTN_W // LANE, tm, LANE), lambda b, s: (b, 0, s, 0)),
            pl.BlockSpec((1, M, MEM_W), lambda b, s: (b, 0, 0)),
            pl.BlockSpec((1, M, MEM_W), lambda b, s: (b, 0, 0)),
            _const_spec((1, D)), _const_spec((1, D)),
            _const_spec((D, 2 * D)), _const_spec((1, 2 * D)),
            _const_spec((POOL_W, D)), _const_spec((ATTN_W, D)), _const_spec((D, D)),
            _const_spec((1, D)), _const_spec((1, D)),
            _const_spec((D, MEM_W)), _const_spec((MEM_W, D)),
            _const_spec((1, D)), _const_spec((1, D)),
            _const_spec((D, LANE)), _const_spec((D, LANE)), _const_spec((1, LANE)),
        ],
        out_specs=[
            pl.BlockSpec((1, tm, D), lambda b, s: (b, s, 0)),
            pl.BlockSpec((1, tm, D), lambda b, s: (b, s, 0)),
            pl.BlockSpec((1, tm, LANE), lambda b, s: (b, s, 0)),
            pl.BlockSpec((1, tm, LANE), lambda b, s: (b, s, 0)),
            pl.BlockSpec((1, 8, tm), lambda b, s: (sub_idx(b, s), 0, 0)),
            pl.BlockSpec((1, 8, LANE), lambda b, s: (sub_idx(b, s), 0, 0)),
        ],
        out_shape=[
            jax.ShapeDtypeStruct((B, S, D), F32),
            jax.ShapeDtypeStruct((B, S, D), BF16),
            jax.ShapeDtypeStruct((B, S, LANE), F32),
            jax.ShapeDtypeStruct((B, S, LANE), F32),
            jax.ShapeDtypeStruct((n_sub, 8, tm), F32),
            jax.ShapeDtypeStruct((n_sub, 8, LANE), F32),
        ],
        compiler_params=pltpu.CompilerParams(dimension_semantics=("arbitrary", "arbitrary"),
                                             vmem_limit_bytes=VMEM_LIMIT),
        name="merge_memattn_router",
    )(x, ypool, o_attn, kmem, vmem,
      row2(ln_in_g), row2(ln_in_b), w_gl.astype(BF16), row2(b_gate[0]),
      w_pool_up[0].astype(BF16), w_attn_up[0].astype(BF16), w_mix_out[0].astype(BF16),
      row2(ln1_g[0]), row2(ln1_b[0]),
      (w_mq[0] * (MEM_HD ** -0.5)).astype(BF16), w_mo[0].astype(BF16),
      row2(ln2_g[0]), row2(ln2_b[0]),
      w_r_hi, w_r_lo, b_r)

    pcs = seg_cnt[:, 0, :N_EXPERTS].astype(jnp.int32)
    tot = jnp.sum(pcs, axis=0)
    cap = ((tot + FFN_TILE - 1) // FFN_TILE) * FFN_TILE
    ends = jnp.cumsum(cap)
    base = ends - cap
    gs = base[None, :] + jnp.cumsum(pcs, axis=0) - pcs
    ls = jnp.cumsum(pcs, axis=1) - pcs
    n_sorted = n_sub * COMPACT_ROWS + N_EXPERTS * FFN_TILE
    n_ffn_tiles = n_sorted // FFN_TILE
    n_used = (ends[-1] // FFN_TILE).astype(jnp.int32)
    tile_row = jnp.arange(n_ffn_tiles, dtype=jnp.int32) * FFN_TILE
    tile_exp = jnp.sum(jnp.minimum(tile_row, ends[-1] - 1)[:, None] >= ends[None, :], axis=1).astype(jnp.int32)
    tails = jnp.concatenate([base + tot, cap - tot]).astype(jnp.int32)
    chunk_row = jnp.arange(MAX_CHUNKS, dtype=jnp.int32) * SEG_ALIGN
    chunk_exp = jnp.sum(chunk_row[None, :, None] >= (ls + pcs)[:, None, :], axis=2)
    shift = jnp.sum(jnp.where(chunk_exp[:, :, None] == jnp.arange(N_EXPERTS)[None, None, :],
                              (gs - ls)[:, None, :], 0), axis=2)
    chunk_go = (chunk_row[None, :] + shift).astype(jnp.int32).reshape(-1)
    n_chunks = (jnp.sum(pcs, axis=1) // SEG_ALIGN).astype(jnp.int32)

    aug_w = D + 2 * LANE
    x_sorted = pl.pallas_call(
        functools.partial(_dispatch_kernel, n_sub=n_sub, rows=COMPACT_ROWS),
        grid_spec=pltpu.PrefetchScalarGridSpec(
            num_scalar_prefetch=3,
            grid=(n_sub,),
            in_specs=[
                pl.BlockSpec((tm, D), lambda i, *_: (i, 0)),
                pl.BlockSpec((tm, LANE), lambda i, *_: (i, 0)),
                pl.BlockSpec((1, 8, tm), lambda i, *_: (i, 0, 0)),
            ],
            out_specs=pl.BlockSpec(memory_space=pl.ANY),
            scratch_shapes=[
                pltpu.VMEM((2, COMPACT_ROWS, aug_w), BF16),
                pltpu.VMEM((FFN_TILE // 2, aug_w), BF16),
                pltpu.SemaphoreType.DMA((2,)),
                pltpu.SemaphoreType.DMA(()),
            ],
        ),
        out_shape=jax.ShapeDtypeStruct((n_sorted, aug_w), BF16),
        compiler_params=pltpu.CompilerParams(dimension_semantics=("arbitrary",), vmem_limit_bytes=VMEM_LIMIT),
        name="moe_dispatch",
    )(n_chunks, chunk_go, tails, h2b.reshape(T, D), comb.reshape(T, LANE), route_t)

    used_tile = lambda t, texp, nused: (jnp.minimum(t, nused[0] - 1), 0)
    y_sorted = pl.pallas_call(
        _ffn_kernel,
        grid_spec=pltpu.PrefetchScalarGridSpec(
            num_scalar_prefetch=2,
            grid=(n_ffn_tiles,),
            in_specs=[
                pl.BlockSpec((FFN_TILE, aug_w), used_tile),
                pl.BlockSpec((1, D, FF), lambda t, texp, nused: (texp[t], 0, 0)),
                pl.BlockSpec((1, D, FF), lambda t, texp, nused: (texp[t], 0, 0)),
                pl.BlockSpec((1, FF, D), lambda t, texp, nused: (texp[t], 0, 0)),
            ],
            out_specs=pl.BlockSpec((FFN_TILE, D), used_tile),
            scratch_shapes=[pltpu.VMEM((D, FF), BF16), pltpu.VMEM((D, FF), BF16), pltpu.VMEM((FF, D), BF16)],
        ),
        out_shape=jax.ShapeDtypeStruct((n_sorted, D), BF16),
        compiler_params=pltpu.CompilerParams(dimension_semantics=("arbitrary",), vmem_limit_bytes=VMEM_LIMIT),
        name="moe_expert_ffn",
    )(tile_exp, n_used.reshape(1), x_sorted, w_gate[0], w_up[0], w_down[0])

    out = pl.pallas_call(
        functools.partial(_combine_kernel, n_sub=n_sub, rows=COMPACT_ROWS),
        grid_spec=pltpu.PrefetchScalarGridSpec(
            num_scalar_prefetch=2,
            grid=(n_sub,),
            in_specs=[
                pl.BlockSpec((tm, LANE), lambda i, *_: (i, 0)),
                pl.BlockSpec((tm, D), lambda i, *_: (i, 0)),
                pl.BlockSpec((1, D), lambda i, *_: (0, 0)),
                pl.BlockSpec((1, D), lambda i, *_: (0, 0)),
                pl.BlockSpec(memory_space=pl.ANY),
            ],
            out_specs=pl.BlockSpec((tm, D), lambda i, *_: (i, 0)),
            scratch_shapes=[
                pltpu.VMEM((2, COMPACT_ROWS, D), BF16),
                pltpu.SemaphoreType.DMA((2,)),
            ],
        ),
        out_shape=jax.ShapeDtypeStruct((T, D), F32),
        compiler_params=pltpu.CompilerParams(dimension_semantics=("arbitrary",), vmem_limit_bytes=VMEM_LIMIT),
        name="moe_combine_ln3",
    )(n_chunks, chunk_go, route.reshape(T, LANE), h2.reshape(T, D), row2(ln3_g[0]), row2(ln3_b[0]), y_sorted)
    return out.reshape(B, S, D)
```

```python
import functools
import math

import jax
import jax.numpy as jnp
from jax import lax
from jax.experimental import pallas as pl
from jax.experimental.pallas import tpu as pltpu

D = 1024
POOL_WINDOWS = (2, 4, 8, 16)
POOL_W = 512
N_HEADS = 8
HEAD_DIM = 64
ATTN_W = 512
BLK = 256
N_BLK = 8
TOPK = 3
REL_BUCKETS = 32
REL_MAX_DIST = 128
MEM_HEADS = 4
MEM_HD = 128
MEM_W = 512
N_GROUPS = 4
EPG = 8
N_EXPERTS = 32
FF = 256
DN_ALPHA = 2.0 ** 0.25
LN_EPS = 1e-5

LANE = 128
TOK_TILE = 512
SEG_ALIGN = 16
COMPACT_ROWS = 2 * TOK_TILE + N_EXPERTS * SEG_ALIGN
MAX_CHUNKS = COMPACT_ROWS // SEG_ALIGN
FFN_TILE = 512
MERGE_TILE = 1024
HALO = 16
NEG = -1e30
VMEM_LIMIT = 56 * 1024 * 1024

F32 = jnp.float32
BF16 = jnp.bfloat16

_NT = (((1,), (1,)), ((), ()))


def _dot(a, b):
    return jnp.dot(a, b, preferred_element_type=F32)


def _dot_nt(a, b):
    return lax.dot_general(a, b, _NT, preferred_element_type=F32)


def _split(a):
    hi = a.astype(BF16)
    lo = (a - hi.astype(F32)).astype(BF16)
    return hi, lo


def _interleave(chains):
    results = [None] * len(chains)
    live = list(range(len(chains)))
    while live:
        for ci in list(live):
            try:
                next(chains[ci])
            except StopIteration as done:
                results[ci] = done.value
                live.remove(ci)
    return results


def _ln(x, g, b):
    mu = jnp.mean(x, axis=-1, keepdims=True)
    xc = x - mu
    var = jnp.mean(xc * xc, axis=-1, keepdims=True)
    return xc * lax.rsqrt(var + LN_EPS) * g + b


def _proj_kernel(x_ref, g_ref, b_ref, w_ref, wgrp_ref, pscale_ref,
                 ypool_ref, q_ref, k_ref, v_ref, ubuf, kbt, *, tm):
    s = pl.program_id(1)
    nb_tile = tm // BLK

    h = _ln(x_ref[0], g_ref[...], b_ref[...])
    hb = h.astype(BF16)
    zu = _dot(hb, w_ref[:, 0:POOL_W])
    zq = _dot(hb, w_ref[:, POOL_W:POOL_W + ATTN_W])
    zk = _dot(hb, w_ref[:, POOL_W + ATTN_W:POOL_W + 2 * ATTN_W])
    zv = _dot(hb, w_ref[:, POOL_W + 2 * ATTN_W:])

    @pl.when(s == 0)
    def _():
        ubuf[0:HALO, :] = jnp.zeros((HALO, POOL_W), F32)
        kbt[...] = jnp.zeros_like(kbt)

    ubuf[HALO:HALO + tm, :] = zu
    t_pos = s * tm + lax.broadcasted_iota(jnp.int32, (tm, LANE), 0)
    for g, w in enumerate(POOL_WINDOWS):
        cols = slice(g * LANE, (g + 1) * LANE)
        ws = ubuf[HALO:HALO + tm, cols]
        for kk in range(1, w):
            ws = ws + ubuf[HALO - kk:HALO - kk + tm, cols]
        cnt = jnp.minimum(t_pos + 1, w).astype(F32)
        y = ws / cnt - ubuf[HALO:HALO + tm, cols]
        yg = _dot(y.astype(BF16), wgrp_ref[g]) * pscale_ref[:, cols]
        ypool_ref[0, :, cols] = yg.astype(ypool_ref.dtype)
    ubuf[0:HALO, :] = ubuf[tm:tm + HALO, :]

    r_io = lax.broadcasted_iota(jnp.int32, kbt.shape, 0)
    c_io = lax.broadcasted_iota(jnp.int32, kbt.shape, 1)
    head_match = (r_io >> 3) == (c_io >> 6)
    for bi in range(nb_tile):
        n = s * nb_tile + bi
        kmean = jnp.mean(zk[bi * BLK:(bi + 1) * BLK], axis=0, keepdims=True)
        kbt[...] = jnp.where(head_match & ((r_io & 7) == n), kmean, kbt[...])

    q_hi, q_lo = _split(zq)
    kb_hi, kb_lo = _split(kbt[...])
    gate = _dot_nt(q_hi, kb_hi) + _dot_nt(q_lo, kb_hi) + _dot_nt(q_hi, kb_lo)

    lane = lax.broadcasted_iota(jnp.int32, (tm, LANE), 1)
    row = lax.broadcasted_iota(jnp.int32, (tm, LANE), 0)
    n_l = lane & 7
    jrow = s * nb_tile + (row >> 8)
    past = n_l < jrow
    gt = jnp.where(past, gate, -jnp.inf)
    cnt = jnp.zeros((tm, LANE), F32)
    for sh in range(1, N_BLK):
        wrap = (n_l + sh) >= N_BLK
        gm = jnp.where(wrap, pltpu.roll(gt, N_BLK - sh, 1), pltpu.roll(gt, LANE - sh, 1))
        cnt = cnt + jnp.where(wrap, jnp.where(gm >= gt, 1.0, 0.0), jnp.where(gm > gt, 1.0, 0.0))
    keep = (past & (cnt < TOPK)) | (n_l == jrow)
    negmask = jnp.where(keep, 0.0, NEG)

    aug_lane = (lane >= HEAD_DIM) & (lane < HEAD_DIM + N_BLK)
    k_onehot = jnp.where(lane == HEAD_DIM + jrow, 1.0, 0.0)
    head_lane = lane < HEAD_DIM
    for hh in range(N_HEADS):
        cols = slice((hh // 2) * LANE, (hh // 2 + 1) * LANE)
        q_h, k_h = zq[:, cols], zk[:, cols]
        if hh % 2:
            q_h, k_h = pltpu.roll(q_h, HEAD_DIM, 1), pltpu.roll(k_h, HEAD_DIM, 1)
        m_h = jnp.where(aug_lane, pltpu.roll(negmask, HEAD_DIM - N_BLK * hh, 1), 0.0)
        q_ref[0, hh] = jnp.where(head_lane, q_h, m_h).astype(q_ref.dtype)
        k_ref[0, hh] = jnp.where(head_lane, k_h, k_onehot).astype(k_ref.dtype)
    for p in range(ATTN_W // LANE):
        v_ref[0, p] = zv[:, p * LANE:(p + 1) * LANE].astype(v_ref.dtype)


def _attn_kernel(q_ref, k_ref, v_ref, tb_ref, o_ref):
    j = pl.program_id(1)
    lane = lax.broadcasted_iota(jnp.int32, (BLK, LANE), 1)

    def one_head(h, p, jj):
        q = q_ref[0, h]
        own0 = jj * BLK
        pieces = []
        s_own = _dot_nt(q, k_ref[0, h, own0:own0 + BLK, :]) + tb_ref[0, h]
        pieces.append((s_own, own0, BLK))
        if jj >= 1:
            s_adj = _dot_nt(q, k_ref[0, h, own0 - BLK:own0, :]) + tb_ref[1, h]
            pieces.append((s_adj, own0 - BLK, BLK))
        if jj >= 2:
            s_far = _dot_nt(q, k_ref[0, h, 0:own0 - BLK, :])
            pieces.append((s_far, 0, own0 - BLK))
        m = None
        for sc, _, _ in pieces:
            mm = jnp.max(sc, axis=1, keepdims=True)
            m = mm if m is None else jnp.maximum(m, mm)
        l = None
        acc = None
        for sc, start, size in pieces:
            e = jnp.exp(sc - m)
            ls = jnp.sum(e, axis=1, keepdims=True)
            l = ls if l is None else l + ls
            pv = _dot(e.astype(BF16), v_ref[0, p, start:start + size, :])
            acc = pv if acc is None else acc + pv
        return acc / l

    for jj in range(N_BLK):
        @pl.when(j == jj)
        def _(jj=jj):
            def pair(p, carry):
                o0 = one_head(2 * p, p, jj)
                o1 = one_head(2 * p + 1, p, jj)
                o_ref[0, p] = jnp.where(lane < HEAD_DIM, o0, o1).astype(o_ref.dtype)
                return carry
            lax.fori_loop(0, N_HEADS // 2, pair, 0)


def _memkv_kernel(mem_ref, wk_ref, wv_ref, k_ref, v_ref):
    mb = mem_ref[0].astype(BF16)
    k_ref[0] = _dot(mb, wk_ref[...]).astype(k_ref.dtype)
    v_ref[0] = _dot(mb, wv_ref[...]).astype(v_ref.dtype)


def _merge_kernel(x_ref, ypool_ref, o_ref, kmem_ref, vmem_ref,
                  lng_ref, lnb_ref, wgl_ref, bgate_ref, wpu_ref, wau_ref, wout_ref,
                  ln1g_ref, ln1b_ref, wmq_ref, wmo_ref, ln2g_ref, ln2b_ref,
                  wrh_ref, wrl_ref, br_ref,
                  h2_ref, h2b_ref, comb_ref, route_ref, route_t_ref, cnt_ref, *, tm):
    _interleave([_merge_chain(ci, x_ref, ypool_ref, o_ref, kmem_ref, vmem_ref,
                              lng_ref, lnb_ref, wgl_ref, bgate_ref, wpu_ref, wau_ref, wout_ref,
                              ln1g_ref, ln1b_ref, wmq_ref, wmo_ref, ln2g_ref, ln2b_ref,
                              wrh_ref, wrl_ref, br_ref, h2_ref, h2b_ref, comb_ref, route_ref, route_t_ref, cnt_ref)
                 for ci in range(tm // TOK_TILE)])


def _merge_chain(ci, x_ref, ypool_ref, o_ref, kmem_ref, vmem_ref,
                 lng_ref, lnb_ref, wgl_ref, bgate_ref, wpu_ref, wau_ref, wout_ref,
                 ln1g_ref, ln1b_ref, wmq_ref, wmo_ref, ln2g_ref, ln2b_ref,
                 wrh_ref, wrl_ref, br_ref, h2_ref, h2b_ref, comb_ref, route_ref, route_t_ref, cnt_ref):
    rows = slice(ci * TOK_TILE, (ci + 1) * TOK_TILE)
    n_rows = TOK_TILE
    h = _ln(x_ref[0, rows, :], lng_ref[...], lnb_ref[...])
    hb = h.astype(BF16)
    gl = _dot(hb, wgl_ref[...]) + bgate_ref[...]
    yield
    gates = 0.5 * jnp.tanh(0.5 * gl) + 0.5
    y_pool = _dot(ypool_ref[0, rows, :], wpu_ref[...])
    o_cat = jnp.concatenate([o_ref[0, p, rows, :] for p in range(ATTN_W // LANE)], axis=1)
    y_attn = _dot(o_cat, wau_ref[...])
    yield
    merged = gates[:, 0:D] * y_pool + gates[:, D:2 * D] * y_attn
    mix = _dot(merged.astype(BF16), wout_ref[...])
    yield
    h1 = _ln(DN_ALPHA * h + mix, ln1g_ref[...], ln1b_ref[...])

    qm = _dot(h1.astype(BF16), wmq_ref[...]).astype(BF16)
    yield
    outs = []
    for hd in range(MEM_HEADS):
        cols = slice(hd * MEM_HD, (hd + 1) * MEM_HD)
        sc = _dot_nt(qm[:, cols], kmem_ref[0, :, cols])
        m = jnp.max(sc, axis=1, keepdims=True)
        e = jnp.exp(sc - m)
        l = jnp.sum(e, axis=1, keepdims=True)
        outs.append(_dot(e.astype(BF16), vmem_ref[0, :, cols]) / l)
    om = jnp.concatenate(outs, axis=1).astype(BF16)
    xa = _dot(om, wmo_ref[...])
    yield
    h2 = _ln(DN_ALPHA * h1 + xa, ln2g_ref[...], ln2b_ref[...])
    h2_ref[0, rows, :] = h2
    h2b_ref[0, rows, :] = h2.astype(BF16)

    x_hi, x_lo = _split(h2)
    r = _dot(x_hi, wrh_ref[...]) + _dot(x_lo, wrh_ref[...]) + _dot(x_hi, wrl_ref[...]) + br_ref[...]
    yield
    lane = lax.broadcasted_iota(jnp.int32, (n_rows, LANE), 1)
    lane_f = lane.astype(F32)
    cmask = (lane >= N_EXPERTS) & (lane < 2 * N_EXPERTS)
    c = jnp.where(cmask, r, -jnp.inf)
    cmax = jnp.max(c, axis=1, keepdims=True)
    ce = jnp.exp(c - cmax)
    csum = jnp.sum(ce, axis=1, keepdims=True) * (1.0 / EPG)
    g_prob = 1.0 / csum
    grp_lane = ((lane & (N_EXPERTS - 1)) >> 3).astype(F32)
    gidx = jnp.min(jnp.where(cmask & (c == cmax), grp_lane, 99.0), axis=1, keepdims=True)
    fmask = (lane < N_EXPERTS) & (grp_lane == gidx)
    f = jnp.where(fmask, r, -jnp.inf)
    fmax = jnp.max(f, axis=1, keepdims=True)
    fe = jnp.exp(f - fmax)
    fsum = jnp.sum(fe, axis=1, keepdims=True)
    prob = fe / fsum
    p1 = jnp.max(prob, axis=1, keepdims=True)
    i1 = jnp.min(jnp.where(fmask & (prob == p1), lane_f, 999.0), axis=1, keepdims=True)
    rest = fmask & (lane_f != i1)
    prob2 = jnp.where(rest, prob, -1.0)
    p2 = jnp.max(prob2, axis=1, keepdims=True)
    i2 = jnp.min(jnp.where(rest & (prob2 == p2), lane_f, 999.0), axis=1, keepdims=True)
    den = p1 + p2
    comb = jnp.where(lane_f == i1, g_prob * (p1 / den),
                     jnp.where(lane_f == i2, g_prob * (p2 / den), 0.0))
    comb_ref[0, rows, :] = comb
    yield

    sel = jnp.where((lane_f == i1) | (lane_f == i2), 1.0, 0.0)
    cnt = jnp.sum(sel, axis=0, keepdims=True)
    pc = jnp.floor((cnt + (SEG_ALIGN - 1)) * (1.0 / SEG_ALIGN)) * SEG_ALIGN
    lane8 = lax.broadcasted_iota(jnp.int32, (8, LANE), 1)
    inc = jnp.broadcast_to(pc, (8, LANE))
    for sh in (1, 2, 4, 8, 16, 32, 64):
        inc = inc + jnp.where(lane8 >= sh, pltpu.roll(inc, sh, 1), 0.0)
    seg_start = inc[0:1] - pc
    t_row = lax.broadcasted_iota(jnp.int32, (n_rows, n_rows), 0)
    t_col = lax.broadcasted_iota(jnp.int32, (n_rows, n_rows), 1)
    earlier = jnp.where(t_row > t_col, 1.0, 0.0).astype(BF16)
    rank = _dot(earlier, sel.astype(BF16))
    pos = seg_start + rank
    d1 = jnp.sum(jnp.where(lane_f == i1, pos, 0.0), axis=1, keepdims=True)
    d2 = jnp.sum(jnp.where(lane_f == i2, pos, 0.0), axis=1, keepdims=True)
    route = jnp.where(lane == 0, d1, jnp.where(lane == 1, d2, 0.0))
    route_ref[0, rows, :] = route
    r_hi, r_lo = _split(route)
    eye = jnp.where(lax.broadcasted_iota(jnp.int32, (8, LANE), 0) == lane8, 1.0, 0.0).astype(BF16)
    route_t_ref[ci] = _dot_nt(eye, r_hi) + _dot_nt(eye, r_lo)
    cnt_ref[ci] = jnp.broadcast_to(pc, (8, LANE))


def _chunk_copies(sub, nch_ref, go_ref, make_copy, act):
    def body(k, carry):
        go = pl.multiple_of(go_ref[sub * MAX_CHUNKS + k], SEG_ALIGN)
        act(make_copy(pl.multiple_of(k * SEG_ALIGN, SEG_ALIGN), go))
        return carry
    lax.fori_loop(0, nch_ref[sub], body, 0)


def _dispatch_kernel(nch_ref, go_ref, tail_ref, xb_ref, comb_ref, route_t_ref, xs_ref,
                     cbuf, zbuf, sem, zsem, *, n_sub, rows):
    i = pl.program_id(0)
    slot = lax.rem(i, 2)

    def copies(sub, slot_, act):
        def mk(lo, go):
            return pltpu.make_async_copy(cbuf.at[slot_, pl.ds(lo, SEG_ALIGN), :],
                                         xs_ref.at[pl.ds(go, SEG_ALIGN), :], sem.at[slot_])
        _chunk_copies(sub, nch_ref, go_ref, mk, act)

    @pl.when(i >= 2)
    def _():
        copies(i - 2, slot, lambda c: c.wait())

    tm = xb_ref.shape[0]
    r_io = lax.broadcasted_iota(jnp.int32, (rows, tm), 0).astype(F32)
    d1 = route_t_ref[0, 0:1, :]
    d2 = route_t_ref[0, 1:2, :]
    p_mat = jnp.where((r_io == d1) | (r_io == d2), 1.0, 0.0).astype(BF16)
    c_hi, c_lo = _split(comb_ref[...])
    x_aug = jnp.concatenate([xb_ref[...], c_hi, c_lo], axis=1)
    cbuf[slot] = _dot(p_mat, x_aug).astype(BF16)
    copies(i, slot, lambda c: c.start())

    @pl.when(i == n_sub - 1)
    def _():
        if n_sub >= 2:
            copies(i - 1, 1 - slot, lambda c: c.wait())
        copies(i, slot, lambda c: c.wait())
        zbuf[...] = jnp.zeros_like(zbuf)

        def tails(act):
            def body(e, carry):
                st = tail_ref[e]
                n = tail_ref[N_EXPERTS + e]
                off = jnp.int32(0)
                size = zbuf.shape[0]
                while size >= SEG_ALIGN:
                    bit = (n & size) != 0

                    @pl.when(bit)
                    def _(size=size, off=off):
                        act(pltpu.make_async_copy(
                            zbuf.at[pl.ds(0, size), :],
                            xs_ref.at[pl.ds(pl.multiple_of(st + off, SEG_ALIGN), size), :], zsem))
                    off = off + jnp.where(bit, size, 0)
                    size //= 2
                return carry
            lax.fori_loop(0, N_EXPERTS, body, 0)
        tails(lambda c: c.start())
        tails(lambda c: c.wait())


def _ffn_kernel(texp_ref, nused_ref, xs_ref, wg_ref, wu_ref, wd_ref, ys_ref, wg_b, wu_b, wd_b):
    t = pl.program_id(0)
    e = texp_ref[t]

    @pl.when((t == 0) | (e != texp_ref[jnp.maximum(t - 1, 0)]))
    def _():
        wg_b[...] = wg_ref[0].astype(BF16)
        wu_b[...] = wu_ref[0].astype(BF16)
        wd_b[...] = wd_ref[0].astype(BF16)

    @pl.when(t < nused_ref[0])
    def _():
        xa = xs_ref[...]
        xrow = xa[:, 0:D]
        cw = xa[:, D:D + LANE].astype(F32) + xa[:, D + LANE:D + 2 * LANE].astype(F32)
        lane = lax.broadcasted_iota(jnp.int32, cw.shape, 1)
        c = jnp.sum(jnp.where(lane == e, cw, 0.0), axis=1, keepdims=True)
        a = _dot(xrow, wg_b[...])
        b = _dot(xrow, wu_b[...])
        hid = (a * jax.nn.sigmoid(a)) * b * c
        ys_ref[...] = _dot(hid.astype(BF16), wd_b[...]).astype(ys_ref.dtype)


def _combine_kernel(nch_ref, go_ref, route_ref, h2_ref, g_ref, b_ref, ys_ref, out_ref,
                    ybuf, sem, *, n_sub, rows):
    i = pl.program_id(0)
    slot = lax.rem(i, 2)

    def copies(sub, slot_, act):
        def mk(lo, go):
            return pltpu.make_async_copy(ys_ref.at[pl.ds(go, SEG_ALIGN), :],
                                         ybuf.at[slot_, pl.ds(lo, SEG_ALIGN), :], sem.at[slot_])
        _chunk_copies(sub, nch_ref, go_ref, mk, act)

    @pl.when(i == 0)
    def _():
        ybuf[...] = jnp.zeros_like(ybuf)
        copies(0, 0, lambda c: c.start())

    @pl.when(i + 1 < n_sub)
    def _():
        copies(i + 1, 1 - slot, lambda c: c.start())

    copies(i, slot, lambda c: c.wait())
    tm = h2_ref.shape[0]
    r_io = lax.broadcasted_iota(jnp.int32, (tm, rows), 1).astype(F32)
    d1 = route_ref[:, 0:1]
    d2 = route_ref[:, 1:2]
    p_t = jnp.where((r_io == d1) | (r_io == d2), 1.0, 0.0).astype(BF16)
    ff = _dot(p_t, ybuf[slot])
    out_ref[...] = _ln(DN_ALPHA * h2_ref[...] + ff, g_ref[...], b_ref[...])


def _bias_kernel(tbl_ref, bkt_ref, out_ref):
    h = pl.program_id(0)
    far = tbl_ref[h, REL_BUCKETS - 1]
    for which in range(2):
        bk = bkt_ref[which]
        acc = jnp.where(bk < 0, NEG, 0.0)
        for kk in range(REL_BUCKETS):
            acc = jnp.where(bk == kk, tbl_ref[h, kk] - far, acc)
        out_ref[which, 0] = acc


def _rel_bucket_table(dist):
    max_exact = REL_BUCKETS // 2
    d = jnp.maximum(dist, 0)
    large = max_exact + (jnp.log(jnp.maximum(d, 1).astype(F32) / max_exact)
                         / math.log(REL_MAX_DIST / max_exact) * (REL_BUCKETS - max_exact)).astype(jnp.int32)
    large = jnp.minimum(large, REL_BUCKETS - 1)
    return jnp.where(d < max_exact, d, large)


def _const_spec(shape):
    nd = len(shape)
    return pl.BlockSpec(shape, lambda *_: (0,) * nd)


def kernel(x, mem, ln_in_g, ln_in_b, rel_bias, w_in, b_gate, w_pool_grp, pool_scale, w_pool_up, w_attn_up,
           w_mix_out, ln1_g, ln1_b, w_mq, w_mk, w_mv, w_mo, ln2_g, ln2_b, w_coarse, b_coarse, w_fine, b_fine,
           w_gate, w_up, w_down, ln3_g, ln3_b):
    B, S, _ = x.shape
    assert S == N_BLK * BLK and w_in.shape[0] == 1
    M = mem.shape[1]
    T = B * S
    tm = 512

    wi = w_in[0]
    w_u = wi[:, 0:POOL_W]
    w_q = wi[:, POOL_W:POOL_W + ATTN_W] * (HEAD_DIM ** -0.5)
    w_k = wi[:, POOL_W + ATTN_W:POOL_W + 2 * ATTN_W]
    w_v = wi[:, POOL_W + 2 * ATTN_W:POOL_W + 3 * ATTN_W]
    w_gl = wi[:, POOL_W + 3 * ATTN_W:]

    w1 = jnp.concatenate([w_u, w_q, w_k, w_v], axis=1).astype(BF16)
    row2 = lambda a: a.reshape(1, -1)

    iq = jnp.arange(BLK, dtype=jnp.int32)[:, None]
    ik = jnp.arange(BLK, dtype=jnp.int32)[None, :]
    d_own = iq - ik
    bkt = jnp.stack([jnp.where(d_own >= 0, _rel_bucket_table(d_own), -1), _rel_bucket_table(d_own + BLK)])
    t_bias = pl.pallas_call(
        _bias_kernel,
        grid=(N_HEADS,),
        in_specs=[pl.BlockSpec(memory_space=pltpu.SMEM), _const_spec((2, BLK, BLK))],
        out_specs=pl.BlockSpec((2, 1, BLK, BLK), lambda h: (0, h, 0, 0)),
        out_shape=jax.ShapeDtypeStruct((2, N_HEADS, BLK, BLK), F32),
        name="relbias_tiles",
    )(rel_bias.T, bkt)

    n_w1 = w1.shape[1]
    ypool, q_aug, k_aug, v_p = pl.pallas_call(
        functools.partial(_proj_kernel, tm=tm),
        grid=(B, S // tm),
        in_specs=[
            pl.BlockSpec((1, tm, D), lambda b, s: (b, s, 0)),
            _const_spec((1, D)), _const_spec((1, D)),
            _const_spec((D, n_w1)),
            _const_spec((len(POOL_WINDOWS), LANE, LANE)),
            _const_spec((1, POOL_W)),
        ],
        out_specs=[
            pl.BlockSpec((1, tm, POOL_W), lambda b, s: (b, s, 0)),
            pl.BlockSpec((1, N_HEADS, tm, LANE), lambda b, s: (b, 0, s, 0)),
            pl.BlockSpec((1, N_HEADS, tm, LANE), lambda b, s: (b, 0, s, 0)),
            pl.BlockSpec((1, ATTN_W // LANE, tm, LANE), lambda b, s: (b, 0, s, 0)),
        ],
        out_shape=[
            jax.ShapeDtypeStruct((B, S, POOL_W), BF16),
            jax.ShapeDtypeStruct((B, N_HEADS, S, LANE), BF16),
            jax.ShapeDtypeStruct((B, N_HEADS, S, LANE), BF16),
            jax.ShapeDtypeStruct((B, ATTN_W // LANE, S, LANE), BF16),
        ],
        scratch_shapes=[pltpu.VMEM((HALO + tm, POOL_W), F32), pltpu.VMEM((LANE, ATTN_W), F32)],
        compiler_params=pltpu.CompilerParams(dimension_semantics=("arbitrary", "arbitrary"),
                                             vmem_limit_bytes=VMEM_LIMIT),
        name="proj_pool_gate",
    )(x, row2(ln_in_g), row2(ln_in_b), w1, w_pool_grp[0].astype(BF16), row2(pool_scale[0]))

    o_attn = pl.pallas_call(
        _attn_kernel,
        grid=(B, N_BLK),
        in_specs=[
            pl.BlockSpec((1, N_HEADS, BLK, LANE), lambda b, j: (b, 0, j, 0)),
            pl.BlockSpec((1, N_HEADS, S, LANE), lambda b, j: (b, 0, 0, 0)),
            pl.BlockSpec((1, ATTN_W // LANE, S, LANE), lambda b, j: (b, 0, 0, 0)),
            _const_spec((2, N_HEADS, BLK, BLK)),
        ],
        out_specs=pl.BlockSpec((1, ATTN_W // LANE, BLK, LANE), lambda b, j: (b, 0, j, 0)),
        out_shape=jax.ShapeDtypeStruct((B, ATTN_W // LANE, S, LANE), BF16),
        compiler_params=pltpu.CompilerParams(dimension_semantics=("arbitrary", "arbitrary"),
                                             vmem_limit_bytes=VMEM_LIMIT),
        name="moba_attn",
    )(q_aug, k_aug, v_p, t_bias)

    kmem, vmem = pl.pallas_call(
        _memkv_kernel,
        grid=(B,),
        in_specs=[pl.BlockSpec((1, M, D), lambda b: (b, 0, 0)),
                  _const_spec((D, MEM_W)), _const_spec((D, MEM_W))],
        out_specs=[pl.BlockSpec((1, M, MEM_W), lambda b: (b, 0, 0)),
                   pl.BlockSpec((1, M, MEM_W), lambda b: (b, 0, 0))],
        out_shape=[jax.ShapeDtypeStruct((B, M, MEM_W), BF16)] * 2,
        compiler_params=pltpu.CompilerParams(dimension_semantics=("arbitrary",)),
        name="mem_kv",
    )(mem, w_mk[0].astype(BF16), w_mv[0].astype(BF16))

    w_r = jnp.concatenate([
        w_fine[0].reshape(D, N_EXPERTS),
        jnp.repeat(w_coarse[0], EPG, axis=1),
        jnp.zeros((D, LANE - 2 * N_EXPERTS), F32)], axis=1)
    b_r = jnp.concatenate([
        b_fine[0].reshape(N_EXPERTS), jnp.repeat(b_coarse[0], EPG),
        jnp.zeros((LANE - 2 * N_EXPERTS,), F32)]).reshape(1, LANE)
    w_r_hi = w_r.astype(BF16)
    w_r_lo = (w_r - w_r_hi.astype(F32)).astype(BF16)

    n_sub = T // tm
    tmm = MERGE_TILE
    per = tmm // tm
    sub_idx = lambda b, s: b * (S // tmm) + s
    h2, h2b, comb, route, route_t, seg_cnt = pl.pallas_call(
        functools.partial(_merge_kernel, tm=tmm),
        grid=(B, S // tmm),
        in_specs=[
            pl.BlockSpec((1, tmm, D), lambda b, s: (b, s, 0)),
            pl.BlockSpec((1, tmm, POOL_W), lambda b, s: (b, s, 0)),
            pl.BlockSpec((1, ATTN_W // LANE, tmm, LANE), lambda b, s: (b, 0, s, 0)),
            pl.BlockSpec((1, M, MEM_W), lambda b, s: (b, 0, 0)),
            pl.BlockSpec((1, M, MEM_W), lambda b, s: (b, 0, 0)),
            _const_spec((1, D)), _const_spec((1, D)),
            _const_spec((D, 2 * D)), _const_spec((1, 2 * D)),
            _const_spec((POOL_W, D)), _const_spec((ATTN_W, D)), _const_spec((D, D)),
            _const_spec((1, D)), _const_spec((1, D)),
            _const_spec((D, MEM_W)), _const_spec((MEM_W, D)),
            _const_spec((1, D)), _const_spec((1, D)),
            _const_spec((D, LANE)), _const_spec((D, LANE)), _const_spec((1, LANE)),
        ],
        out_specs=[
            pl.BlockSpec((1, tmm, D), lambda b, s: (b, s, 0)),
            pl.BlockSpec((1, tmm, D), lambda b, s: (b, s, 0)),
            pl.BlockSpec((1, tmm, LANE), lambda b, s: (b, s, 0)),
            pl.BlockSpec((1, tmm, LANE), lambda b, s: (b, s, 0)),
            pl.BlockSpec((per, 8, tm), lambda b, s: (sub_idx(b, s), 0, 0)),
            pl.BlockSpec((per, 8, LANE), lambda b, s: (sub_idx(b, s), 0, 0)),
        ],
        out_shape=[
            jax.ShapeDtypeStruct((B, S, D), F32),
            jax.ShapeDtypeStruct((B, S, D), BF16),
            jax.ShapeDtypeStruct((B, S, LANE), F32),
            jax.ShapeDtypeStruct((B, S, LANE), F32),
            jax.ShapeDtypeStruct((n_sub, 8, tm), F32),
            jax.ShapeDtypeStruct((n_sub, 8, LANE), F32),
        ],
        compiler_params=pltpu.CompilerParams(dimension_semantics=("arbitrary", "arbitrary"),
                                             vmem_limit_bytes=VMEM_LIMIT),
        name="merge_memattn_router",
    )(x, ypool, o_attn, kmem, vmem,
      row2(ln_in_g), row2(ln_in_b), w_gl.astype(BF16), row2(b_gate[0]),
      w_pool_up[0].astype(BF16), w_attn_up[0].astype(BF16), w_mix_out[0].astype(BF16),
      row2(ln1_g[0]), row2(ln1_b[0]),
      (w_mq[0] * (MEM_HD ** -0.5)).astype(BF16), w_mo[0].astype(BF16),
      row2(ln2_g[0]), row2(ln2_b[0]),
      w_r_hi, w_r_lo, b_r)

    pcs = seg_cnt[:, 0, :N_EXPERTS].astype(jnp.int32)
    tot = jnp.sum(pcs, axis=0)
    cap = ((tot + FFN_TILE - 1) // FFN_TILE) * FFN_TILE
    ends = jnp.cumsum(cap)
    base = ends - cap
    gs = base[None, :] + jnp.cumsum(pcs, axis=0) - pcs
    ls = jnp.cumsum(pcs, axis=1) - pcs
    n_sorted = n_sub * COMPACT_ROWS + N_EXPERTS * FFN_TILE
    n_ffn_tiles = n_sorted // FFN_TILE
    n_used = (ends[-1] // FFN_TILE).astype(jnp.int32)
    tile_row = jnp.arange(n_ffn_tiles, dtype=jnp.int32) * FFN_TILE
    tile_exp = jnp.sum(jnp.minimum(tile_row, ends[-1] - 1)[:, None] >= ends[None, :], axis=1).astype(jnp.int32)
    tails = jnp.concatenate([base + tot, cap - tot]).astype(jnp.int32)
    chunk_row = jnp.arange(MAX_CHUNKS, dtype=jnp.int32) * SEG_ALIGN
    chunk_exp = jnp.sum(chunk_row[None, :, None] >= (ls + pcs)[:, None, :], axis=2)
    shift = jnp.sum(jnp.where(chunk_exp[:, :, None] == jnp.arange(N_EXPERTS)[None, None, :],
                              (gs - ls)[:, None, :], 0), axis=2)
    chunk_go = (chunk_row[None, :] + shift).astype(jnp.int32).reshape(-1)
    n_chunks = (jnp.sum(pcs, axis=1) // SEG_ALIGN).astype(jnp.int32)

    aug_w = D + 2 * LANE
    x_sorted = pl.pallas_call(
        functools.partial(_dispatch_kernel, n_sub=n_sub, rows=COMPACT_ROWS),
        grid_spec=pltpu.PrefetchScalarGridSpec(
            num_scalar_prefetch=3,
            grid=(n_sub,),
            in_specs=[
                pl.BlockSpec((tm, D), lambda i, *_: (i, 0)),
                pl.BlockSpec((tm, LANE), lambda i, *_: (i, 0)),
                pl.BlockSpec((1, 8, tm), lambda i, *_: (i, 0, 0)),
            ],
            out_specs=pl.BlockSpec(memory_space=pl.ANY),
            scratch_shapes=[
                pltpu.VMEM((2, COMPACT_ROWS, aug_w), BF16),
                pltpu.VMEM((FFN_TILE // 2, aug_w), BF16),
                pltpu.SemaphoreType.DMA((2,)),
                pltpu.SemaphoreType.DMA(()),
            ],
        ),
        out_shape=jax.ShapeDtypeStruct((n_sorted, aug_w), BF16),
        compiler_params=pltpu.CompilerParams(dimension_semantics=("arbitrary",), vmem_limit_bytes=VMEM_LIMIT),
        name="moe_dispatch",
    )(n_chunks, chunk_go, tails, h2b.reshape(T, D), comb.reshape(T, LANE), route_t)

    used_tile = lambda t, texp, nused: (jnp.minimum(t, nused[0] - 1), 0)
    y_sorted = pl.pallas_call(
        _ffn_kernel,
        grid_spec=pltpu.PrefetchScalarGridSpec(
            num_scalar_prefetch=2,
            grid=(n_ffn_tiles,),
            in_specs=[
                pl.BlockSpec((FFN_TILE, aug_w), used_tile),
                pl.BlockSpec((1, D, FF), lambda t, texp, nused: (texp[t], 0, 0)),
                pl.BlockSpec((1, D, FF), lambda t, texp, nused: (texp[t], 0, 0)),
                pl.BlockSpec((1, FF, D), lambda t, texp, nused: (texp[t], 0, 0)),
            ],
            out_specs=pl.BlockSpec((FFN_TILE, D), used_tile),
            scratch_shapes=[pltpu.VMEM((D, FF), BF16), pltpu.VMEM((D, FF), BF16), pltpu.VMEM((FF, D), BF16)],
        ),
        out_shape=jax.ShapeDtypeStruct((n_sorted, D), BF16),
        compiler_params=pltpu.CompilerParams(dimension_semantics=("arbitrary",), vmem_limit_bytes=VMEM_LIMIT),
        name="moe_expert_ffn",
    )(tile_exp, n_used.reshape(1), x_sorted, w_gate[0], w_up[0], w_down[0])

    out = pl.pallas_call(
        functools.partial(_combine_kernel, n_sub=n_sub, rows=COMPACT_ROWS),
        grid_spec=pltpu.PrefetchScalarGridSpec(
            num_scalar_prefetch=2,
            grid=(n_sub,),
            in_specs=[
                pl.BlockSpec((tm, LANE), lambda i, *_: (i, 0)),
                pl.BlockSpec((tm, D), lambda i, *_: (i, 0)),
                pl.BlockSpec((1, D), lambda i, *_: (0, 0)),
                pl.BlockSpec((1, D), lambda i, *_: (0, 0)),
                pl.BlockSpec(memory_space=pl.ANY),
            ],
            out_specs=pl.BlockSpec((tm, D), lambda i, *_: (i, 0)),
            scratch_shapes=[
                pltpu.VMEM((2, COMPACT_ROWS, D), BF16),
                pltpu.SemaphoreType.DMA((2,)),
            ],
        ),
        out_shape=jax.ShapeDtypeStruct((T, D), F32),
        compiler_params=pltpu.CompilerParams(dimension_semantics=("arbitrary",), vmem_limit_bytes=VMEM_LIMIT),
        name="moe_combine_ln3",
    )(n_chunks, chunk_go, route.reshape(T, LANE), h2.reshape(T, D), row2(ln3_g[0]), row2(ln3_b[0]), y_sorted)
    return out.reshape(B, S, D)
```

```python
import functools
import math

import jax
import jax.numpy as jnp
from jax import lax
from jax.experimental import pallas as pl
from jax.experimental.pallas import tpu as pltpu

D = 1024
POOL_WINDOWS = (2, 4, 8, 16)
POOL_W = 512
N_HEADS = 8
HEAD_DIM = 64
ATTN_W = 512
BLK = 256
N_BLK = 8
TOPK = 3
REL_BUCKETS = 32
REL_MAX_DIST = 128
MEM_HEADS = 4
MEM_HD = 128
MEM_W = 512
N_GROUPS = 4
EPG = 8
N_EXPERTS = 32
FF = 256
DN_ALPHA = 2.0 ** 0.25
LN_EPS = 1e-5

LANE = 128
TOK_TILE = 512
SEG_ALIGN = 16
COMPACT_ROWS = 2 * TOK_TILE + N_EXPERTS * SEG_ALIGN
MAX_CHUNKS = COMPACT_ROWS // SEG_ALIGN
FFN_TILE = 512
MERGE_TILE = 1024
HALO = 16
NEG = -1e30
VMEM_LIMIT = 56 * 1024 * 1024

F32 = jnp.float32
BF16 = jnp.bfloat16

_NT = (((1,), (1,)), ((), ()))


def _dot(a, b):
    return jnp.dot(a, b, preferred_element_type=F32)


def _dot_nt(a, b):
    return lax.dot_general(a, b, _NT, preferred_element_type=F32)


def _split(a):
    hi = a.astype(BF16)
    lo = (a - hi.astype(F32)).astype(BF16)
    return hi, lo


def _interleave(chains):
    results = [None] * len(chains)
    live = list(range(len(chains)))
    while live:
        for ci in list(live):
            try:
                next(chains[ci])
            except StopIteration as done:
                results[ci] = done.value
                live.remove(ci)
    return results


def _ln(x, g, b):
    mu = jnp.mean(x, axis=-1, keepdims=True)
    xc = x - mu
    var = jnp.mean(xc * xc, axis=-1, keepdims=True)
    return xc * lax.rsqrt(var + LN_EPS) * g + b


def _proj_kernel(x_ref, g_ref, b_ref, w_ref, wgrp_ref, pscale_ref,
                 ypool_ref, q_ref, k_ref, v_ref, ubuf, kbt, *, tm):
    s = pl.program_id(1)

    @pl.when(s == 0)
    def _():
        ubuf[0:HALO, :] = jnp.zeros((HALO, POOL_W), F32)
        kbt[...] = jnp.zeros_like(kbt)

    _interleave([_proj_chain(ci, s, tm, x_ref, g_ref, b_ref, w_ref, wgrp_ref, pscale_ref,
                             ypool_ref, q_ref, k_ref, v_ref, ubuf, kbt)
                 for ci in range(tm // TOK_TILE)])
    ubuf[0:HALO, :] = ubuf[tm:tm + HALO, :]


def _proj_chain(ci, s, tm, x_ref, g_ref, b_ref, w_ref, wgrp_ref, pscale_ref,
                ypool_ref, q_ref, k_ref, v_ref, ubuf, kbt):
    n = TOK_TILE
    r0 = ci * n
    rows = slice(r0, r0 + n)
    blk0 = s * (tm // BLK) + ci * (n // BLK)

    h = _ln(x_ref[0, rows, :], g_ref[...], b_ref[...])
    hb = h.astype(BF16)
    zu = _dot(hb, w_ref[:, 0:POOL_W])
    zq = _dot(hb, w_ref[:, POOL_W:POOL_W + ATTN_W])
    zk = _dot(hb, w_ref[:, POOL_W + ATTN_W:POOL_W + 2 * ATTN_W])
    zv = _dot(hb, w_ref[:, POOL_W + 2 * ATTN_W:])
    ubuf[HALO + r0:HALO + r0 + n, :] = zu

    r_io = lax.broadcasted_iota(jnp.int32, kbt.shape, 0)
    c_io = lax.broadcasted_iota(jnp.int32, kbt.shape, 1)
    head_match = (r_io >> 3) == (c_io >> 6)
    for bi in range(n // BLK):
        kmean = jnp.mean(zk[bi * BLK:(bi + 1) * BLK], axis=0, keepdims=True)
        kbt[...] = jnp.where(head_match & ((r_io & 7) == blk0 + bi), kmean, kbt[...])
    yield

    t_pos = s * tm + r0 + lax.broadcasted_iota(jnp.int32, (n, LANE), 0)
    for g, w in enumerate(POOL_WINDOWS):
        cols = slice(g * LANE, (g + 1) * LANE)
        ws = ubuf[HALO + r0:HALO + r0 + n, cols]
        for kk in range(1, w):
            ws = ws + ubuf[HALO + r0 - kk:HALO + r0 - kk + n, cols]
        cnt = jnp.minimum(t_pos + 1, w).astype(F32)
        y = ws / cnt - ubuf[HALO + r0:HALO + r0 + n, cols]
        yg = _dot(y.astype(BF16), wgrp_ref[g]) * pscale_ref[:, cols]
        ypool_ref[0, rows, cols] = yg.astype(ypool_ref.dtype)
    yield

    q_hi, q_lo = _split(zq)
    kb_hi, kb_lo = _split(kbt[...])
    gate = _dot_nt(q_hi, kb_hi) + _dot_nt(q_lo, kb_hi) + _dot_nt(q_hi, kb_lo)
    yield

    lane = lax.broadcasted_iota(jnp.int32, (n, LANE), 1)
    row = lax.broadcasted_iota(jnp.int32, (n, LANE), 0)
    n_l = lane & 7
    jrow = blk0 + (row >> 8)
    past = n_l < jrow
    gt = jnp.where(past, gate, -jnp.inf)
    cnt = jnp.zeros((n, LANE), F32)
    for sh in range(1, N_BLK):
        wrap = (n_l + sh) >= N_BLK
        gm = jnp.where(wrap, pltpu.roll(gt, N_BLK - sh, 1), pltpu.roll(gt, LANE - sh, 1))
        cnt = cnt + jnp.where(wrap, jnp.where(gm >= gt, 1.0, 0.0), jnp.where(gm > gt, 1.0, 0.0))
    keep = (past & (cnt < TOPK)) | (n_l == jrow)
    negmask = jnp.where(keep, 0.0, NEG)

    aug_lane = (lane >= HEAD_DIM) & (lane < HEAD_DIM + N_BLK)
    k_onehot = jnp.where(lane == HEAD_DIM + jrow, 1.0, 0.0)
    head_lane = lane < HEAD_DIM
    for hh in range(N_HEADS):
        cols = slice((hh // 2) * LANE, (hh // 2 + 1) * LANE)
        q_h, k_h = zq[:, cols], zk[:, cols]
        if hh % 2:
            q_h, k_h = pltpu.roll(q_h, HEAD_DIM, 1), pltpu.roll(k_h, HEAD_DIM, 1)
        m_h = jnp.where(aug_lane, pltpu.roll(negmask, HEAD_DIM - N_BLK * hh, 1), 0.0)
        q_ref[0, hh, rows, :] = jnp.where(head_lane, q_h, m_h).astype(q_ref.dtype)
        k_ref[0, hh, rows, :] = jnp.where(head_lane, k_h, k_onehot).astype(k_ref.dtype)
    for p in range(ATTN_W // LANE):
        v_ref[0, p, rows, :] = zv[:, p * LANE:(p + 1) * LANE].astype(v_ref.dtype)


def _attn_kernel(q_ref, k_ref, v_ref, tb_ref, o_ref):
    j = pl.program_id(1)
    lane = lax.broadcasted_iota(jnp.int32, (BLK, LANE), 1)

    def one_head(h, p, jj):
        q = q_ref[0, h]
        own0 = jj * BLK
        pieces = []
        s_own = _dot_nt(q, k_ref[0, h, own0:own0 + BLK, :]) + tb_ref[0, h]
        pieces.append((s_own, own0, BLK))
        if jj >= 1:
            s_adj = _dot_nt(q, k_ref[0, h, own0 - BLK:own0, :]) + tb_ref[1, h]
            pieces.append((s_adj, own0 - BLK, BLK))
        if jj >= 2:
            s_far = _dot_nt(q, k_ref[0, h, 0:own0 - BLK, :])
            pieces.append((s_far, 0, own0 - BLK))
        m = None
        for sc, _, _ in pieces:
            mm = jnp.max(sc, axis=1, keepdims=True)
            m = mm if m is None else jnp.maximum(m, mm)
        l = None
        acc = None
        for sc, start, size in pieces:
            e = jnp.exp(sc - m)
            ls = jnp.sum(e, axis=1, keepdims=True)
            l = ls if l is None else l + ls
            pv = _dot(e.astype(BF16), v_ref[0, p, start:start + size, :])
            acc = pv if acc is None else acc + pv
        return acc / l

    for jj in range(N_BLK):
        @pl.when(j == jj)
        def _(jj=jj):
            def pair(p, carry):
                o0 = one_head(2 * p, p, jj)
                o1 = one_head(2 * p + 1, p, jj)
                o_ref[0, p] = jnp.where(lane < HEAD_DIM, o0, o1).astype(o_ref.dtype)
                return carry
            lax.fori_loop(0, N_HEADS // 2, pair, 0)


def _memkv_kernel(mem_ref, wk_ref, wv_ref, k_ref, v_ref):
    mb = mem_ref[0].astype(BF16)
    k_ref[0] = _dot(mb, wk_ref[...]).astype(k_ref.dtype)
    v_ref[0] = _dot(mb, wv_ref[...]).astype(v_ref.dtype)


def _merge_kernel(x_ref, ypool_ref, o_ref, kmem_ref, vmem_ref,
                  lng_ref, lnb_ref, wgl_ref, bgate_ref, wpu_ref, wau_ref, wout_ref,
                  ln1g_ref, ln1b_ref, wmq_ref, wmo_ref, ln2g_ref, ln2b_ref,
                  wrh_ref, wrl_ref, br_ref,
                  h2_ref, h2b_ref, comb_ref, route_ref, route_t_ref, cnt_ref, *, tm):
    _interleave([_merge_chain(ci, x_ref, ypool_ref, o_ref, kmem_ref, vmem_ref,
                              lng_ref, lnb_ref, wgl_ref, bgate_ref, wpu_ref, wau_ref, wout_ref,
                              ln1g_ref, ln1b_ref, wmq_ref, wmo_ref, ln2g_ref, ln2b_ref,
                              wrh_ref, wrl_ref, br_ref, h2_ref, h2b_ref, comb_ref, route_ref, route_t_ref, cnt_ref)
                 for ci in range(tm // TOK_TILE)])


def _merge_chain(ci, x_ref, ypool_ref, o_ref, kmem_ref, vmem_ref,
                 lng_ref, lnb_ref, wgl_ref, bgate_ref, wpu_ref, wau_ref, wout_ref,
                 ln1g_ref, ln1b_ref, wmq_ref, wmo_ref, ln2g_ref, ln2b_ref,
                 wrh_ref, wrl_ref, br_ref, h2_ref, h2b_ref, comb_ref, route_ref, route_t_ref, cnt_ref):
    rows = slice(ci * TOK_TILE, (ci + 1) * TOK_TILE)
    n_rows = TOK_TILE
    h = _ln(x_ref[0, rows, :], lng_ref[...], lnb_ref[...])
    hb = h.astype(BF16)
    gl = _dot(hb, wgl_ref[...]) + bgate_ref[...]
    yield
    gates = 0.5 * jnp.tanh(0.5 * gl) + 0.5
    y_pool = _dot(ypool_ref[0, rows, :], wpu_ref[...])
    o_cat = jnp.concatenate([o_ref[0, p, rows, :] for p in range(ATTN_W // LANE)], axis=1)
    y_attn = _dot(o_cat, wau_ref[...])
    yield
    merged = gates[:, 0:D] * y_pool + gates[:, D:2 * D] * y_attn
    mix = _dot(merged.astype(BF16), wout_ref[...])
    yield
    h1 = _ln(DN_ALPHA * h + mix, ln1g_ref[...], ln1b_ref[...])

    qm = _dot(h1.astype(BF16), wmq_ref[...]).astype(BF16)
    yield
    outs = []
    for hd in range(MEM_HEADS):
        cols = slice(hd * MEM_HD, (hd + 1) * MEM_HD)
        sc = _dot_nt(qm[:, cols], kmem_ref[0, :, cols])
        m = jnp.max(sc, axis=1, keepdims=True)
        e = jnp.exp(sc - m)
        l = jnp.sum(e, axis=1, keepdims=True)
        outs.append(_dot(e.astype(BF16), vmem_ref[0, :, cols]) / l)
    om = jnp.concatenate(outs, axis=1).astype(BF16)
    xa = _dot(om, wmo_ref[...])
    yield
    h2 = _ln(DN_ALPHA * h1 + xa, ln2g_ref[...], ln2b_ref[...])
    h2_ref[0, rows, :] = h2
    h2b_ref[0, rows, :] = h2.astype(BF16)

    x_hi, x_lo = _split(h2)
    r = _dot(x_hi, wrh_ref[...]) + _dot(x_lo, wrh_ref[...]) + _dot(x_hi, wrl_ref[...]) + br_ref[...]
    yield
    lane = lax.broadcasted_iota(jnp.int32, (n_rows, LANE), 1)
    lane_f = lane.astype(F32)
    cmask = (lane >= N_EXPERTS) & (lane < 2 * N_EXPERTS)
    c = jnp.where(cmask, r, -jnp.inf)
    cmax = jnp.max(c, axis=1, keepdims=True)
    ce = jnp.exp(c - cmax)
    csum = jnp.sum(ce, axis=1, keepdims=True) * (1.0 / EPG)
    g_prob = 1.0 / csum
    grp_lane = ((lane & (N_EXPERTS - 1)) >> 3).astype(F32)
    gidx = jnp.min(jnp.where(cmask & (c == cmax), grp_lane, 99.0), axis=1, keepdims=True)
    fmask = (lane < N_EXPERTS) & (grp_lane == gidx)
    f = jnp.where(fmask, r, -jnp.inf)
    fmax = jnp.max(f, axis=1, keepdims=True)
    fe = jnp.exp(f - fmax)
    fsum = jnp.sum(fe, axis=1, keepdims=True)
    prob = fe / fsum
    p1 = jnp.max(prob, axis=1, keepdims=True)
    i1 = jnp.min(jnp.where(fmask & (prob == p1), lane_f, 999.0), axis=1, keepdims=True)
    rest = fmask & (lane_f != i1)
    prob2 = jnp.where(rest, prob, -1.0)
    p2 = jnp.max(prob2, axis=1, keepdims=True)
    i2 = jnp.min(jnp.where(rest & (prob2 == p2), lane_f, 999.0), axis=1, keepdims=True)
    den = p1 + p2
    comb = jnp.where(lane_f == i1, g_prob * (p1 / den),
                     jnp.where(lane_f == i2, g_prob * (p2 / den), 0.0))
    comb_ref[0, rows, :] = comb
    yield

    sel = jnp.where((lane_f == i1) | (lane_f == i2), 1.0, 0.0)
    cnt = jnp.sum(sel, axis=0, keepdims=True)
    pc = jnp.floor((cnt + (SEG_ALIGN - 1)) * (1.0 / SEG_ALIGN)) * SEG_ALIGN
    lane8 = lax.broadcasted_iota(jnp.int32, (8, LANE), 1)
    inc = jnp.broadcast_to(pc, (8, LANE))
    for sh in (1, 2, 4, 8, 16, 32, 64):
        inc = inc + jnp.where(lane8 >= sh, pltpu.roll(inc, sh, 1), 0.0)
    seg_start = inc[0:1] - pc
    t_row = lax.broadcasted_iota(jnp.int32, (n_rows, n_rows), 0)
    t_col = lax.broadcasted_iota(jnp.int32, (n_rows, n_rows), 1)
    earlier = jnp.where(t_row > t_col, 1.0, 0.0).astype(BF16)
    rank = _dot(earlier, sel.astype(BF16))
    pos = seg_start + rank
    d1 = jnp.sum(jnp.where(lane_f == i1, pos, 0.0), axis=1, keepdims=True)
    d2 = jnp.sum(jnp.where(lane_f == i2, pos, 0.0), axis=1, keepdims=True)
    route = jnp.where(lane == 0, d1, jnp.where(lane == 1, d2, 0.0))
    route_ref[0, rows, :] = route
    r_hi, r_lo = _split(route)
    eye = jnp.where(lax.broadcasted_iota(jnp.int32, (8, LANE), 0) == lane8, 1.0, 0.0).astype(BF16)
    route_t_ref[ci] = _dot_nt(eye, r_hi) + _dot_nt(eye, r_lo)
    cnt_ref[ci] = jnp.broadcast_to(pc, (8, LANE))


def _chunk_copies(sub, nch_ref, go_ref, make_copy, act):
    def body(k, carry):
        go = pl.multiple_of(go_ref[sub * MAX_CHUNKS + k], SEG_ALIGN)
        act(make_copy(pl.multiple_of(k * SEG_ALIGN, SEG_ALIGN), go))
        return carry
    lax.fori_loop(0, nch_ref[sub], body, 0)


def _dispatch_kernel(nch_ref, go_ref, tail_ref, xb_ref, comb_ref, route_t_ref, xs_ref,
                     cbuf, zbuf, sem, zsem, *, n_sub, rows):
    i = pl.program_id(0)
    slot = lax.rem(i, 2)

    def copies(sub, slot_, act):
        def mk(lo, go):
            return pltpu.make_async_copy(cbuf.at[slot_, pl.ds(lo, SEG_ALIGN), :],
                                         xs_ref.at[pl.ds(go, SEG_ALIGN), :], sem.at[slot_])
        _chunk_copies(sub, nch_ref, go_ref, mk, act)

    @pl.when(i >= 2)
    def _():
        copies(i - 2, slot, lambda c: c.wait())

    tm = xb_ref.shape[0]
    r_io = lax.broadcasted_iota(jnp.int32, (rows, tm), 0).astype(F32)
    d1 = route_t_ref[0, 0:1, :]
    d2 = route_t_ref[0, 1:2, :]
    p_mat = jnp.where((r_io == d1) | (r_io == d2), 1.0, 0.0).astype(BF16)
    c_hi, c_lo = _split(comb_ref[...])
    x_aug = jnp.concatenate([xb_ref[...], c_hi, c_lo], axis=1)
    cbuf[slot] = _dot(p_mat, x_aug).astype(BF16)
    copies(i, slot, lambda c: c.start())

    @pl.when(i == n_sub - 1)
    def _():
        if n_sub >= 2:
            copies(i - 1, 1 - slot, lambda c: c.wait())
        copies(i, slot, lambda c: c.wait())
        zbuf[...] = jnp.zeros_like(zbuf)

        def tails(act):
            def body(e, carry):
                st = tail_ref[e]
                n = tail_ref[N_EXPERTS + e]
                off = jnp.int32(0)
                size = zbuf.shape[0]
                while size >= SEG_ALIGN:
                    bit = (n & size) != 0

                    @pl.when(bit)
                    def _(size=size, off=off):
                        act(pltpu.make_async_copy(
                            zbuf.at[pl.ds(0, size), :],
                            xs_ref.at[pl.ds(pl.multiple_of(st + off, SEG_ALIGN), size), :], zsem))
                    off = off + jnp.where(bit, size, 0)
                    size //= 2
                return carry
            lax.fori_loop(0, N_EXPERTS, body, 0)
        tails(lambda c: c.start())
        tails(lambda c: c.wait())


def _ffn_kernel(texp_ref, nused_ref, xs_ref, wg_ref, wu_ref, wd_ref, ys_ref, wg_b, wu_b, wd_b):
    t = pl.program_id(0)
    e = texp_ref[t]

    @pl.when((t == 0) | (e != texp_ref[jnp.maximum(t - 1, 0)]))
    def _():
        wg_b[...] = wg_ref[0].astype(BF16)
        wu_b[...] = wu_ref[0].astype(BF16)
        wd_b[...] = wd_ref[0].astype(BF16)

    @pl.when(t < nused_ref[0])
    def _():
        xa = xs_ref[...]
        xrow = xa[:, 0:D]
        cw = xa[:, D:D + LANE].astype(F32) + xa[:, D + LANE:D + 2 * LANE].astype(F32)
        lane = lax.broadcasted_iota(jnp.int32, cw.shape, 1)
        c = jnp.sum(jnp.where(lane == e, cw, 0.0), axis=1, keepdims=True)
        a = _dot(xrow, wg_b[...])
        b = _dot(xrow, wu_b[...])
        hid = (a * jax.nn.sigmoid(a)) * b * c
        ys_ref[...] = _dot(hid.astype(BF16), wd_b[...]).astype(ys_ref.dtype)


def _combine_kernel(nch_ref, go_ref, route_ref, h2_ref, g_ref, b_ref, ys_ref, out_ref,
                    ybuf, sem, *, n_sub, rows):
    i = pl.program_id(0)
    slot = lax.rem(i, 2)

    def copies(sub, slot_, act):
        def mk(lo, go):
            return pltpu.make_async_copy(ys_ref.at[pl.ds(go, SEG_ALIGN), :],
                                         ybuf.at[slot_, pl.ds(lo, SEG_ALIGN), :], sem.at[slot_])
        _chunk_copies(sub, nch_ref, go_ref, mk, act)

    @pl.when(i == 0)
    def _():
        ybuf[...] = jnp.zeros_like(ybuf)
        copies(0, 0, lambda c: c.start())

    @pl.when(i + 1 < n_sub)
    def _():
        copies(i + 1, 1 - slot, lambda c: c.start())

    copies(i, slot, lambda c: c.wait())
    tm = h2_ref.shape[0]
    r_io = lax.broadcasted_iota(jnp.int32, (tm, rows), 1).astype(F32)
    d1 = route_ref[:, 0:1]
    d2 = route_ref[:, 1:2]
    p_t = jnp.where((r_io == d1) | (r_io == d2), 1.0, 0.0).astype(BF16)
    ff = _dot(p_t, ybuf[slot])
    out_ref[...] = _ln(DN_ALPHA * h2_ref[...] + ff, g_ref[...], b_ref[...])


def _bias_kernel(tbl_ref, bkt_ref, out_ref):
    h = pl.program_id(0)
    far = tbl_ref[h, REL_BUCKETS - 1]
    for which in range(2):
        bk = bkt_ref[which]
        acc = jnp.where(bk < 0, NEG, 0.0)
        for kk in range(REL_BUCKETS):
            acc = jnp.where(bk == kk, tbl_ref[h, kk] - far, acc)
        out_ref[which, 0] = acc


def _rel_bucket_table(dist):
    max_exact = REL_BUCKETS // 2
    d = jnp.maximum(dist, 0)
    large = max_exact + (jnp.log(jnp.maximum(d, 1).astype(F32) / max_exact)
                         / math.log(REL_MAX_DIST / max_exact) * (REL_BUCKETS - max_exact)).astype(jnp.int32)
    large = jnp.minimum(large, REL_BUCKETS - 1)
    return jnp.where(d < max_exact, d, large)


def _const_spec(shape):
    nd = len(shape)
    return pl.BlockSpec(shape, lambda *_: (0,) * nd)


def kernel(x, mem, ln_in_g, ln_in_b, rel_bias, w_in, b_gate, w_pool_grp, pool_scale, w_pool_up, w_attn_up,
           w_mix_out, ln1_g, ln1_b, w_mq, w_mk, w_mv, w_mo, ln2_g, ln2_b, w_coarse, b_coarse, w_fine, b_fine,
           w_gate, w_up, w_down, ln3_g, ln3_b):
    B, S, _ = x.shape
    assert S == N_BLK * BLK and w_in.shape[0] == 1
    M = mem.shape[1]
    T = B * S
    tm = 512

    wi = w_in[0]
    w_u = wi[:, 0:POOL_W]
    w_q = wi[:, POOL_W:POOL_W + ATTN_W] * (HEAD_DIM ** -0.5)
    w_k = wi[:, POOL_W + ATTN_W:POOL_W + 2 * ATTN_W]
    w_v = wi[:, POOL_W + 2 * ATTN_W:POOL_W + 3 * ATTN_W]
    w_gl = wi[:, POOL_W + 3 * ATTN_W:]

    w1 = jnp.concatenate([w_u, w_q, w_k, w_v], axis=1).astype(BF16)
    row2 = lambda a: a.reshape(1, -1)

    iq = jnp.arange(BLK, dtype=jnp.int32)[:, None]
    ik = jnp.arange(BLK, dtype=jnp.int32)[None, :]
    d_own = iq - ik
    bkt = jnp.stack([jnp.where(d_own >= 0, _rel_bucket_table(d_own), -1), _rel_bucket_table(d_own + BLK)])
    t_bias = pl.pallas_call(
        _bias_kernel,
        grid=(N_HEADS,),
        in_specs=[pl.BlockSpec(memory_space=pltpu.SMEM), _const_spec((2, BLK, BLK))],
        out_specs=pl.BlockSpec((2, 1, BLK, BLK), lambda h: (0, h, 0, 0)),
        out_shape=jax.ShapeDtypeStruct((2, N_HEADS, BLK, BLK), F32),
        name="relbias_tiles",
    )(rel_bias.T, bkt)

    n_w1 = w1.shape[1]
    tmp = MERGE_TILE
    ypool, q_aug, k_aug, v_p = pl.pallas_call(
        functools.partial(_proj_kernel, tm=tmp),
        grid=(B, S // tmp),
        in_specs=[
            pl.BlockSpec((1, tmp, D), lambda b, s: (b, s, 0)),
            _const_spec((1, D)), _const_spec((1, D)),
            _const_spec((D, n_w1)),
            _const_spec((len(POOL_WINDOWS), LANE, LANE)),
            _const_spec((1, POOL_W)),
        ],
        out_specs=[
            pl.BlockSpec((1, tmp, POOL_W), lambda b, s: (b, s, 0)),
            pl.BlockSpec((1, N_HEADS, tmp, LANE), lambda b, s: (b, 0, s, 0)),
            pl.BlockSpec((1, N_HEADS, tmp, LANE), lambda b, s: (b, 0, s, 0)),
            pl.BlockSpec((1, ATTN_W // LANE, tmp, LANE), lambda b, s: (b, 0, s, 0)),
        ],
        out_shape=[
            jax.ShapeDtypeStruct((B, S, POOL_W), BF16),
            jax.ShapeDtypeStruct((B, N_HEADS, S, LANE), BF16),
            jax.ShapeDtypeStruct((B, N_HEADS, S, LANE), BF16),
            jax.ShapeDtypeStruct((B, ATTN_W // LANE, S, LANE), BF16),
        ],
        scratch_shapes=[pltpu.VMEM((HALO + tmp, POOL_W), F32), pltpu.VMEM((LANE, ATTN_W), F32)],
        compiler_params=pltpu.CompilerParams(dimension_semantics=("arbitrary", "arbitrary"),
                                             vmem_limit_bytes=VMEM_LIMIT),
        name="proj_pool_gate",
    )(x, row2(ln_in_g), row2(ln_in_b), w1, w_pool_grp[0].astype(BF16), row2(pool_scale[0]))

    o_attn = pl.pallas_call(
        _attn_kernel,
        grid=(B, N_BLK),
        in_specs=[
            pl.BlockSpec((1, N_HEADS, BLK, LANE), lambda b, j: (b, 0, j, 0)),
            pl.BlockSpec((1, N_HEADS, S, LANE), lambda b, j: (b, 0, 0, 0)),
            pl.BlockSpec((1, ATTN_W // LANE, S, LANE), lambda b, j: (b, 0, 0, 0)),
            _const_spec((2, N_HEADS, BLK, BLK)),
        ],
        out_specs=pl.BlockSpec((1, ATTN_W // LANE, BLK, LANE), lambda b, j: (b, 0, j, 0)),
        out_shape=jax.ShapeDtypeStruct((B, ATTN_W // LANE, S, LANE), BF16),
        compiler_params=pltpu.CompilerParams(dimension_semantics=("arbitrary", "arbitrary"),
                                             vmem_limit_bytes=VMEM_LIMIT),
        name="moba_attn",
    )(q_aug, k_aug, v_p, t_bias)

    kmem, vmem = pl.pallas_call(
        _memkv_kernel,
        grid=(B,),
        in_specs=[pl.BlockSpec((1, M, D), lambda b: (b, 0, 0)),
                  _const_spec((D, MEM_W)), _const_spec((D, MEM_W))],
        out_specs=[pl.BlockSpec((1, M, MEM_W), lambda b: (b, 0, 0)),
                   pl.BlockSpec((1, M, MEM_W), lambda b: (b, 0, 0))],
        out_shape=[jax.ShapeDtypeStruct((B, M, MEM_W), BF16)] * 2,
        compiler_params=pltpu.CompilerParams(dimension_semantics=("arbitrary",)),
        name="mem_kv",
    )(mem, w_mk[0].astype(BF16), w_mv[0].astype(BF16))

    w_r = jnp.concatenate([
        w_fine[0].reshape(D, N_EXPERTS),
        jnp.repeat(w_coarse[0], EPG, axis=1),
        jnp.zeros((D, LANE - 2 * N_EXPERTS), F32)], axis=1)
    b_r = jnp.concatenate([
        b_fine[0].reshape(N_EXPERTS), jnp.repeat(b_coarse[0], EPG),
        jnp.zeros((LANE - 2 * N_EXPERTS,), F32)]).reshape(1, LANE)
    w_r_hi = w_r.astype(BF16)
    w_r_lo = (w_r - w_r_hi.astype(F32)).astype(BF16)

    n_sub = T // tm
    tmm = MERGE_TILE
    per = tmm // tm
    sub_idx = lambda b, s: b * (S // tmm) + s
    h2, h2b, comb, route, route_t, seg_cnt = pl.pallas_call(
        functools.partial(_merge_kernel, tm=tmm),
        grid=(B, S // tmm),
        in_specs=[
            pl.BlockSpec((1, tmm, D), lambda b, s: (b, s, 0)),
            pl.BlockSpec((1, tmm, POOL_W), lambda b, s: (b, s, 0)),
            pl.BlockSpec((1, ATTN_W // LANE, tmm, LANE), lambda b, s: (b, 0, s, 0)),
            pl.BlockSpec((1, M, MEM_W), lambda b, s: (b, 0, 0)),
            pl.BlockSpec((1, M, MEM_W), lambda b, s: (b, 0, 0)),
            _const_spec((1, D)), _const_spec((1, D)),
            _const_spec((D, 2 * D)), _const_spec((1, 2 * D)),
            _const_spec((POOL_W, D)), _const_spec((ATTN_W, D)), _const_spec((D, D)),
            _const_spec((1, D)), _const_spec((1, D)),
            _const_spec((D, MEM_W)), _const_spec((MEM_W, D)),
            _const_spec((1, D)), _const_spec((1, D)),
            _const_spec((D, LANE)), _const_spec((D, LANE)), _const_spec((1, LANE)),
        ],
        out_specs=[
            pl.BlockSpec((1, tmm, D), lambda b, s: (b, s, 0)),
            pl.BlockSpec((1, tmm, D), lambda b, s: (b, s, 0)),
            pl.BlockSpec((1, tmm, LANE), lambda b, s: (b, s, 0)),
            pl.BlockSpec((1, tmm, LANE), lambda b, s: (b, s, 0)),
            pl.BlockSpec((per, 8, tm), lambda b, s: (sub_idx(b, s), 0, 0)),
            pl.BlockSpec((per, 8, LANE), lambda b, s: (sub_idx(b, s), 0, 0)),
        ],
        out_shape=[
            jax.ShapeDtypeStruct((B, S, D), F32),
            jax.ShapeDtypeStruct((B, S, D), BF16),
            jax.ShapeDtypeStruct((B, S, LANE), F32),
            jax.ShapeDtypeStruct((B, S, LANE), F32),
            jax.ShapeDtypeStruct((n_sub, 8, tm), F32),
            jax.ShapeDtypeStruct((n_sub, 8, LANE), F32),
        ],
        compiler_params=pltpu.CompilerParams(dimension_semantics=("arbitrary", "arbitrary"),
                                             vmem_limit_bytes=VMEM_LIMIT),
        name="merge_memattn_router",
    )(x, ypool, o_attn, kmem, vmem,
      row2(ln_in_g), row2(ln_in_b), w_gl.astype(BF16), row2(b_gate[0]),
      w_pool_up[0].astype(BF16), w_attn_up[0].astype(BF16), w_mix_out[0].astype(BF16),
      row2(ln1_g[0]), row2(ln1_b[0]),
      (w_mq[0] * (MEM_HD ** -0.5)).astype(BF16), w_mo[0].astype(BF16),
      row2(ln2_g[0]), row2(ln2_b[0]),
      w_r_hi, w_r_lo, b_r)

    pcs = seg_cnt[:, 0, :N_EXPERTS].astype(jnp.int32)
    tot = jnp.sum(pcs, axis=0)
    cap = ((tot + FFN_TILE - 1) // FFN_TILE) * FFN_TILE
    ends = jnp.cumsum(cap)
    base = ends - cap
    gs = base[None, :] + jnp.cumsum(pcs, axis=0) - pcs
    ls = jnp.cumsum(pcs, axis=1) - pcs
    n_sorted = n_sub * COMPACT_ROWS + N_EXPERTS * FFN_TILE
    n_ffn_tiles = n_sorted // FFN_TILE
    n_used = (ends[-1] // FFN_TILE).astype(jnp.int32)
    tile_row = jnp.arange(n_ffn_tiles, dtype=jnp.int32) * FFN_TILE
    tile_exp = jnp.sum(jnp.minimum(tile_row, ends[-1] - 1)[:, None] >= ends[None, :], axis=1).astype(jnp.int32)
    tails = jnp.concatenate([base + tot, cap - tot]).astype(jnp.int32)
    chunk_row = jnp.arange(MAX_CHUNKS, dtype=jnp.int32) * SEG_ALIGN
    chunk_exp = jnp.sum(chunk_row[None, :, None] >= (ls + pcs)[:, None, :], axis=2)
    shift = jnp.sum(jnp.where(chunk_exp[:, :, None] == jnp.arange(N_EXPERTS)[None, None, :],
                              (gs - ls)[:, None, :], 0), axis=2)
    chunk_go = (chunk_row[None, :] + shift).astype(jnp.int32).reshape(-1)
    n_chunks = (jnp.sum(pcs, axis=1) // SEG_ALIGN).astype(jnp.int32)

    aug_w = D + 2 * LANE
    x_sorted = pl.pallas_call(
        functools.partial(_dispatch_kernel, n_sub=n_sub, rows=COMPACT_ROWS),
        grid_spec=pltpu.PrefetchScalarGridSpec(
            num_scalar_prefetch=3,
            grid=(n_sub,),
            in_specs=[
                pl.BlockSpec((tm, D), lambda i, *_: (i, 0)),
                pl.BlockSpec((tm, LANE), lambda i, *_: (i, 0)),
                pl.BlockSpec((1, 8, tm), lambda i, *_: (i, 0, 0)),
            ],
            out_specs=pl.BlockSpec(memory_space=pl.ANY),
            scratch_shapes=[
                pltpu.VMEM((2, COMPACT_ROWS, aug_w), BF16),
                pltpu.VMEM((FFN_TILE // 2, aug_w), BF16),
                pltpu.SemaphoreType.DMA((2,)),
                pltpu.SemaphoreType.DMA(()),
            ],
        ),
        out_shape=jax.ShapeDtypeStruct((n_sorted, aug_w), BF16),
        compiler_params=pltpu.CompilerParams(dimension_semantics=("arbitrary",), vmem_limit_bytes=VMEM_LIMIT),
        name="moe_dispatch",
    )(n_chunks, chunk_go, tails, h2b.reshape(T, D), comb.reshape(T, LANE), route_t)

    used_tile = lambda t, texp, nused: (jnp.minimum(t, nused[0] - 1), 0)
    y_sorted = pl.pallas_call(
        _ffn_kernel,
        grid_spec=pltpu.PrefetchScalarGridSpec(
            num_scalar_prefetch=2,
            grid=(n_ffn_tiles,),
            in_specs=[
                pl.BlockSpec((FFN_TILE, aug_w), used_tile),
                pl.BlockSpec((1, D, FF), lambda t, texp, nused: (texp[t], 0, 0)),
                pl.BlockSpec((1, D, FF), lambda t, texp, nused: (texp[t], 0, 0)),
                pl.BlockSpec((1, FF, D), lambda t, texp, nused: (texp[t], 0, 0)),
            ],
            out_specs=pl.BlockSpec((FFN_TILE, D), used_tile),
            scratch_shapes=[pltpu.VMEM((D, FF), BF16), pltpu.VMEM((D, FF), BF16), pltpu.VMEM((FF, D), BF16)],
        ),
        out_shape=jax.ShapeDtypeStruct((n_sorted, D), BF16),
        compiler_params=pltpu.CompilerParams(dimension_semantics=("arbitrary",), vmem_limit_bytes=VMEM_LIMIT),
        name="moe_expert_ffn",
    )(tile_exp, n_used.reshape(1), x_sorted, w_gate[0], w_up[0], w_down[0])

    out = pl.pallas_call(
        functools.partial(_combine_kernel, n_sub=n_sub, rows=COMPACT_ROWS),
        grid_spec=pltpu.PrefetchScalarGridSpec(
            num_scalar_prefetch=2,
            grid=(n_sub,),
            in_specs=[
                pl.BlockSpec((tm, LANE), lambda i, *_: (i, 0)),
                pl.BlockSpec((tm, D), lambda i, *_: (i, 0)),
                pl.BlockSpec((1, D), lambda i, *_: (0, 0)),
                pl.BlockSpec((1, D), lambda i, *_: (0, 0)),
                pl.BlockSpec(memory_space=pl.ANY),
            ],
            out_specs=pl.BlockSpec((tm, D), lambda i, *_: (i, 0)),
            scratch_shapes=[
                pltpu.VMEM((2, COMPACT_ROWS, D), BF16),
                pltpu.SemaphoreType.DMA((2,)),
            ],
        ),
        out_shape=jax.ShapeDtypeStruct((T, D), F32),
        compiler_params=pltpu.CompilerParams(dimension_semantics=("arbitrary",), vmem_limit_bytes=VMEM_LIMIT),
        name="moe_combine_ln3",
    )(n_chunks, chunk_go, route.reshape(T, LANE), h2.reshape(T, D), row2(ln3_g[0]), row2(ln3_b[0]), y_sorted)
    return out.reshape(B, S, D)
```

```python
import functools
import math

import jax
import jax.numpy as jnp
from jax import lax
from jax.experimental import pallas as pl
from jax.experimental.pallas import tpu as pltpu

D = 1024
POOL_WINDOWS = (2, 4, 8, 16)
POOL_W = 512
N_HEADS = 8
HEAD_DIM = 64
ATTN_W = 512
BLK = 256
N_BLK = 8
TOPK = 3
REL_BUCKETS = 32
REL_MAX_DIST = 128
MEM_HEADS = 4
MEM_HD = 128
MEM_W = 512
N_GROUPS = 4
EPG = 8
N_EXPERTS = 32
FF = 256
DN_ALPHA = 2.0 ** 0.25
LN_EPS = 1e-5

LANE = 128
TOK_TILE = 512
SEG_ALIGN = 16
COMPACT_ROWS = 2 * TOK_TILE + N_EXPERTS * SEG_ALIGN
MAX_CHUNKS = COMPACT_ROWS // SEG_ALIGN
FFN_TILE = 512
ATTN_PAIRS_PER_TRIP = 2
MERGE_TILE = 1024
HALO = 16
NEG = -1e30
VMEM_LIMIT = 56 * 1024 * 1024

F32 = jnp.float32
BF16 = jnp.bfloat16

_NT = (((1,), (1,)), ((), ()))


def _dot(a, b):
    return jnp.dot(a, b, preferred_element_type=F32)


def _dot_nt(a, b):
    return lax.dot_general(a, b, _NT, preferred_element_type=F32)


def _split(a):
    hi = a.astype(BF16)
    lo = (a - hi.astype(F32)).astype(BF16)
    return hi, lo


def _interleave(chains):
    results = [None] * len(chains)
    live = list(range(len(chains)))
    while live:
        for ci in list(live):
            try:
                next(chains[ci])
            except StopIteration as done:
                results[ci] = done.value
                live.remove(ci)
    return results


def _ln(x, g, b):
    mu = jnp.mean(x, axis=-1, keepdims=True)
    xc = x - mu
    var = jnp.mean(xc * xc, axis=-1, keepdims=True)
    return xc * lax.rsqrt(var + LN_EPS) * g + b


def _proj_kernel(x_ref, g_ref, b_ref, w_ref, wgrp_ref, pscale_ref,
                 ypool_ref, q_ref, k_ref, v_ref, ubuf, kbt, *, tm):
    s = pl.program_id(1)

    @pl.when(s == 0)
    def _():
        ubuf[0:HALO, :] = jnp.zeros((HALO, POOL_W), F32)
        kbt[...] = jnp.zeros_like(kbt)

    _interleave([_proj_chain(ci, s, tm, x_ref, g_ref, b_ref, w_ref, wgrp_ref, pscale_ref,
                             ypool_ref, q_ref, k_ref, v_ref, ubuf, kbt)
                 for ci in range(tm // TOK_TILE)])
    ubuf[0:HALO, :] = ubuf[tm:tm + HALO, :]


def _proj_chain(ci, s, tm, x_ref, g_ref, b_ref, w_ref, wgrp_ref, pscale_ref,
                ypool_ref, q_ref, k_ref, v_ref, ubuf, kbt):
    n = TOK_TILE
    r0 = ci * n
    rows = slice(r0, r0 + n)
    blk0 = s * (tm // BLK) + ci * (n // BLK)

    h = _ln(x_ref[0, rows, :], g_ref[...], b_ref[...])
    hb = h.astype(BF16)
    zu = _dot(hb, w_ref[:, 0:POOL_W])
    zq = _dot(hb, w_ref[:, POOL_W:POOL_W + ATTN_W])
    zk = _dot(hb, w_ref[:, POOL_W + ATTN_W:POOL_W + 2 * ATTN_W])
    zv = _dot(hb, w_ref[:, POOL_W + 2 * ATTN_W:])
    ubuf[HALO + r0:HALO + r0 + n, :] = zu

    r_io = lax.broadcasted_iota(jnp.int32, kbt.shape, 0)
    c_io = lax.broadcasted_iota(jnp.int32, kbt.shape, 1)
    head_match = (r_io >> 3) == (c_io >> 6)
    for bi in range(n // BLK):
        kmean = jnp.mean(zk[bi * BLK:(bi + 1) * BLK], axis=0, keepdims=True)
        kbt[...] = jnp.where(head_match & ((r_io & 7) == blk0 + bi), kmean, kbt[...])
    yield

    t_pos = s * tm + r0 + lax.broadcasted_iota(jnp.int32, (n, LANE), 0)
    for g, w in enumerate(POOL_WINDOWS):
        cols = slice(g * LANE, (g + 1) * LANE)
        ws = ubuf[HALO + r0:HALO + r0 + n, cols]
        for kk in range(1, w):
            ws = ws + ubuf[HALO + r0 - kk:HALO + r0 - kk + n, cols]
        cnt = jnp.minimum(t_pos + 1, w).astype(F32)
        y = ws / cnt - ubuf[HALO + r0:HALO + r0 + n, cols]
        yg = _dot(y.astype(BF16), wgrp_ref[g]) * pscale_ref[:, cols]
        ypool_ref[0, rows, cols] = yg.astype(ypool_ref.dtype)
    yield

    q_hi, q_lo = _split(zq)
    kb_hi, kb_lo = _split(kbt[...])
    gate = _dot_nt(q_hi, kb_hi) + _dot_nt(q_lo, kb_hi) + _dot_nt(q_hi, kb_lo)
    yield

    lane = lax.broadcasted_iota(jnp.int32, (n, LANE), 1)
    row = lax.broadcasted_iota(jnp.int32, (n, LANE), 0)
    n_l = lane & 7
    jrow = blk0 + (row >> 8)
    past = n_l < jrow
    gt = jnp.where(past, gate, -jnp.inf)
    cnt = jnp.zeros((n, LANE), F32)
    for sh in range(1, N_BLK):
        wrap = (n_l + sh) >= N_BLK
        gm = jnp.where(wrap, pltpu.roll(gt, N_BLK - sh, 1), pltpu.roll(gt, LANE - sh, 1))
        cnt = cnt + jnp.where(wrap, jnp.where(gm >= gt, 1.0, 0.0), jnp.where(gm > gt, 1.0, 0.0))
    keep = (past & (cnt < TOPK)) | (n_l == jrow)
    negmask = jnp.where(keep, 0.0, NEG)

    aug_lane = (lane >= HEAD_DIM) & (lane < HEAD_DIM + N_BLK)
    k_onehot = jnp.where(lane == HEAD_DIM + jrow, 1.0, 0.0)
    head_lane = lane < HEAD_DIM
    for hh in range(N_HEADS):
        cols = slice((hh // 2) * LANE, (hh // 2 + 1) * LANE)
        q_h, k_h = zq[:, cols], zk[:, cols]
        if hh % 2:
            q_h, k_h = pltpu.roll(q_h, HEAD_DIM, 1), pltpu.roll(k_h, HEAD_DIM, 1)
        m_h = jnp.where(aug_lane, pltpu.roll(negmask, HEAD_DIM - N_BLK * hh, 1), 0.0)
        q_ref[0, hh, rows, :] = jnp.where(head_lane, q_h, m_h).astype(q_ref.dtype)
        k_ref[0, hh, rows, :] = jnp.where(head_lane, k_h, k_onehot).astype(k_ref.dtype)
    for p in range(ATTN_W // LANE):
        v_ref[0, p, rows, :] = zv[:, p * LANE:(p + 1) * LANE].astype(v_ref.dtype)


def _attn_kernel(q_ref, k_ref, v_ref, tb_ref, o_ref):
    j = pl.program_id(1)
    lane = lax.broadcasted_iota(jnp.int32, (BLK, LANE), 1)

    def one_head(h, p, jj):
        q = q_ref[0, h]
        own0 = jj * BLK
        pieces = []
        s_own = _dot_nt(q, k_ref[0, h, own0:own0 + BLK, :]) + tb_ref[0, h]
        pieces.append((s_own, own0, BLK))
        if jj >= 1:
            s_adj = _dot_nt(q, k_ref[0, h, own0 - BLK:own0, :]) + tb_ref[1, h]
            pieces.append((s_adj, own0 - BLK, BLK))
        if jj >= 2:
            s_far = _dot_nt(q, k_ref[0, h, 0:own0 - BLK, :])
            pieces.append((s_far, 0, own0 - BLK))
        yield
        m = None
        for sc, _, _ in pieces:
            mm = jnp.max(sc, axis=1, keepdims=True)
            m = mm if m is None else jnp.maximum(m, mm)
        l = None
        probs = []
        for sc, start, size in pieces:
            e = jnp.exp(sc - m)
            ls = jnp.sum(e, axis=1, keepdims=True)
            l = ls if l is None else l + ls
            probs.append((e.astype(BF16), start, size))
        yield
        acc = None
        for pb, start, size in probs:
            pv = _dot(pb, v_ref[0, p, start:start + size, :])
            acc = pv if acc is None else acc + pv
        return acc / l

    for jj in range(N_BLK):
        @pl.when(j == jj)
        def _(jj=jj):
            def group(gi, carry):
                pairs = [gi * ATTN_PAIRS_PER_TRIP + pi for pi in range(ATTN_PAIRS_PER_TRIP)]
                outs = _interleave([one_head(2 * p + hh, p, jj) for p in pairs for hh in range(2)])
                for pi, p in enumerate(pairs):
                    o_ref[0, p] = jnp.where(lane < HEAD_DIM, outs[2 * pi], outs[2 * pi + 1]).astype(o_ref.dtype)
                return carry
            lax.fori_loop(0, N_HEADS // 2 // ATTN_PAIRS_PER_TRIP, group, 0)


def _memkv_kernel(mem_ref, wk_ref, wv_ref, k_ref, v_ref):
    mb = mem_ref[0].astype(BF16)
    k_ref[0] = _dot(mb, wk_ref[...]).astype(k_ref.dtype)
    v_ref[0] = _dot(mb, wv_ref[...]).astype(v_ref.dtype)


def _merge_kernel(x_ref, ypool_ref, o_ref, kmem_ref, vmem_ref,
                  lng_ref, lnb_ref, wgl_ref, bgate_ref, wpu_ref, wau_ref, wout_ref,
                  ln1g_ref, ln1b_ref, wmq_ref, wmo_ref, ln2g_ref, ln2b_ref,
                  wrh_ref, wrl_ref, br_ref,
                  h2_ref, h2b_ref, comb_ref, route_ref, route_t_ref, cnt_ref, *, tm):
    _interleave([_merge_chain(ci, x_ref, ypool_ref, o_ref, kmem_ref, vmem_ref,
                              lng_ref, lnb_ref, wgl_ref, bgate_ref, wpu_ref, wau_ref, wout_ref,
                              ln1g_ref, ln1b_ref, wmq_ref, wmo_ref, ln2g_ref, ln2b_ref,
                              wrh_ref, wrl_ref, br_ref, h2_ref, h2b_ref, comb_ref, route_ref, route_t_ref, cnt_ref)
                 for ci in range(tm // TOK_TILE)])


def _merge_chain(ci, x_ref, ypool_ref, o_ref, kmem_ref, vmem_ref,
                 lng_ref, lnb_ref, wgl_ref, bgate_ref, wpu_ref, wau_ref, wout_ref,
                 ln1g_ref, ln1b_ref, wmq_ref, wmo_ref, ln2g_ref, ln2b_ref,
                 wrh_ref, wrl_ref, br_ref, h2_ref, h2b_ref, comb_ref, route_ref, route_t_ref, cnt_ref):
    rows = slice(ci * TOK_TILE, (ci + 1) * TOK_TILE)
    n_rows = TOK_TILE
    h = _ln(x_ref[0, rows, :], lng_ref[...], lnb_ref[...])
    hb = h.astype(BF16)
    gl = _dot(hb, wgl_ref[...]) + bgate_ref[...]
    yield
    gates = 0.5 * jnp.tanh(0.5 * gl) + 0.5
    y_pool = _dot(ypool_ref[0, rows, :], wpu_ref[...])
    o_cat = jnp.concatenate([o_ref[0, p, rows, :] for p in range(ATTN_W // LANE)], axis=1)
    y_attn = _dot(o_cat, wau_ref[...])
    yield
    merged = gates[:, 0:D] * y_pool + gates[:, D:2 * D] * y_attn
    mix = _dot(merged.astype(BF16), wout_ref[...])
    yield
    h1 = _ln(DN_ALPHA * h + mix, ln1g_ref[...], ln1b_ref[...])

    qm = _dot(h1.astype(BF16), wmq_ref[...]).astype(BF16)
    yield
    outs = []
    for hd in range(MEM_HEADS):
        cols = slice(hd * MEM_HD, (hd + 1) * MEM_HD)
        sc = _dot_nt(qm[:, cols], kmem_ref[0, :, cols])
        m = jnp.max(sc, axis=1, keepdims=True)
        e = jnp.exp(sc - m)
        l = jnp.sum(e, axis=1, keepdims=True)
        outs.append(_dot(e.astype(BF16), vmem_ref[0, :, cols]) / l)
    om = jnp.concatenate(outs, axis=1).astype(BF16)
    xa = _dot(om, wmo_ref[...])
    yield
    h2 = _ln(DN_ALPHA * h1 + xa, ln2g_ref[...], ln2b_ref[...])
    h2_ref[0, rows, :] = h2
    h2b_ref[0, rows, :] = h2.astype(BF16)

    x_hi, x_lo = _split(h2)
    r = _dot(x_hi, wrh_ref[...]) + _dot(x_lo, wrh_ref[...]) + _dot(x_hi, wrl_ref[...]) + br_ref[...]
    yield
    lane = lax.broadcasted_iota(jnp.int32, (n_rows, LANE), 1)
    lane_f = lane.astype(F32)
    cmask = (lane >= N_EXPERTS) & (lane < 2 * N_EXPERTS)
    c = jnp.where(cmask, r, -jnp.inf)
    cmax = jnp.max(c, axis=1, keepdims=True)
    ce = jnp.exp(c - cmax)
    csum = jnp.sum(ce, axis=1, keepdims=True) * (1.0 / EPG)
    g_prob = 1.0 / csum
    grp_lane = ((lane & (N_EXPERTS - 1)) >> 3).astype(F32)
    gidx = jnp.min(jnp.where(cmask & (c == cmax), grp_lane, 99.0), axis=1, keepdims=True)
    fmask = (lane < N_EXPERTS) & (grp_lane == gidx)
    f = jnp.where(fmask, r, -jnp.inf)
    fmax = jnp.max(f, axis=1, keepdims=True)
    fe = jnp.exp(f - fmax)
    fsum = jnp.sum(fe, axis=1, keepdims=True)
    prob = fe / fsum
    p1 = jnp.max(prob, axis=1, keepdims=True)
    i1 = jnp.min(jnp.where(fmask & (prob == p1), lane_f, 999.0), axis=1, keepdims=True)
    rest = fmask & (lane_f != i1)
    prob2 = jnp.where(rest, prob, -1.0)
    p2 = jnp.max(prob2, axis=1, keepdims=True)
    i2 = jnp.min(jnp.where(rest & (prob2 == p2), lane_f, 999.0), axis=1, keepdims=True)
    den = p1 + p2
    comb = jnp.where(lane_f == i1, g_prob * (p1 / den),
                     jnp.where(lane_f == i2, g_prob * (p2 / den), 0.0))
    comb_ref[0, rows, :] = comb
    yield

    sel = jnp.where((lane_f == i1) | (lane_f == i2), 1.0, 0.0)
    cnt = jnp.sum(sel, axis=0, keepdims=True)
    pc = jnp.floor((cnt + (SEG_ALIGN - 1)) * (1.0 / SEG_ALIGN)) * SEG_ALIGN
    lane8 = lax.broadcasted_iota(jnp.int32, (8, LANE), 1)
    inc = jnp.broadcast_to(pc, (8, LANE))
    for sh in (1, 2, 4, 8, 16, 32, 64):
        inc = inc + jnp.where(lane8 >= sh, pltpu.roll(inc, sh, 1), 0.0)
    seg_start = inc[0:1] - pc
    t_row = lax.broadcasted_iota(jnp.int32, (n_rows, n_rows), 0)
    t_col = lax.broadcasted_iota(jnp.int32, (n_rows, n_rows), 1)
    earlier = jnp.where(t_row > t_col, 1.0, 0.0).astype(BF16)
    rank = _dot(earlier, sel.astype(BF16))
    pos = seg_start + rank
    d1 = jnp.sum(jnp.where(lane_f == i1, pos, 0.0), axis=1, keepdims=True)
    d2 = jnp.sum(jnp.where(lane_f == i2, pos, 0.0), axis=1, keepdims=True)
    route = jnp.where(lane == 0, d1, jnp.where(lane == 1, d2, 0.0))
    route_ref[0, rows, :] = route
    r_hi, r_lo = _split(route)
    eye = jnp.where(lax.broadcasted_iota(jnp.int32, (8, LANE), 0) == lane8, 1.0, 0.0).astype(BF16)
    route_t_ref[ci] = _dot_nt(eye, r_hi) + _dot_nt(eye, r_lo)
    cnt_ref[ci] = jnp.broadcast_to(pc, (8, LANE))


def _chunk_copies(sub, nch_ref, go_ref, make_copy, act):
    def body(k, carry):
        go = pl.multiple_of(go_ref[sub * MAX_CHUNKS + k], SEG_ALIGN)
        act(make_copy(pl.multiple_of(k * SEG_ALIGN, SEG_ALIGN), go))
        return carry
    lax.fori_loop(0, nch_ref[sub], body, 0)


def _dispatch_kernel(nch_ref, go_ref, tail_ref, xb_ref, comb_ref, route_t_ref, xs_ref,
                     cbuf, zbuf, sem, zsem, *, n_sub, rows):
    i = pl.program_id(0)
    slot = lax.rem(i, 2)

    def copies(sub, slot_, act):
        def mk(lo, go):
            return pltpu.make_async_copy(cbuf.at[slot_, pl.ds(lo, SEG_ALIGN), :],
                                         xs_ref.at[pl.ds(go, SEG_ALIGN), :], sem.at[slot_])
        _chunk_copies(sub, nch_ref, go_ref, mk, act)

    @pl.when(i >= 2)
    def _():
        copies(i - 2, slot, lambda c: c.wait())

    tm = xb_ref.shape[0]
    r_io = lax.broadcasted_iota(jnp.int32, (rows, tm), 0).astype(F32)
    d1 = route_t_ref[0, 0:1, :]
    d2 = route_t_ref[0, 1:2, :]
    p_mat = jnp.where((r_io == d1) | (r_io == d2), 1.0, 0.0).astype(BF16)
    c_hi, c_lo = _split(comb_ref[...])
    x_aug = jnp.concatenate([xb_ref[...], c_hi, c_lo], axis=1)
    cbuf[slot] = _dot(p_mat, x_aug).astype(BF16)
    copies(i, slot, lambda c: c.start())

    @pl.when(i == n_sub - 1)
    def _():
        if n_sub >= 2:
            copies(i - 1, 1 - slot, lambda c: c.wait())
        copies(i, slot, lambda c: c.wait())
        zbuf[...] = jnp.zeros_like(zbuf)

        def tails(act):
            def body(e, carry):
                st = tail_ref[e]
                n = tail_ref[N_EXPERTS + e]
                off = jnp.int32(0)
                size = zbuf.shape[0]
                while size >= SEG_ALIGN:
                    bit = (n & size) != 0

                    @pl.when(bit)
                    def _(size=size, off=off):
                        act(pltpu.make_async_copy(
                            zbuf.at[pl.ds(0, size), :],
                            xs_ref.at[pl.ds(pl.multiple_of(st + off, SEG_ALIGN), size), :], zsem))
                    off = off + jnp.where(bit, size, 0)
                    size //= 2
                return carry
            lax.fori_loop(0, N_EXPERTS, body, 0)
        tails(lambda c: c.start())
        tails(lambda c: c.wait())


def _ffn_kernel(texp_ref, nused_ref, xs_ref, wg_ref, wu_ref, wd_ref, ys_ref, wg_b, wu_b, wd_b):
    t = pl.program_id(0)
    e = texp_ref[t]

    @pl.when((t == 0) | (e != texp_ref[jnp.maximum(t - 1, 0)]))
    def _():
        wg_b[...] = wg_ref[0].astype(BF16)
        wu_b[...] = wu_ref[0].astype(BF16)
        wd_b[...] = wd_ref[0].astype(BF16)

    @pl.when(t < nused_ref[0])
    def _():
        xa = xs_ref[...]
        xrow = xa[:, 0:D]
        cw = xa[:, D:D + LANE].astype(F32) + xa[:, D + LANE:D + 2 * LANE].astype(F32)
        lane = lax.broadcasted_iota(jnp.int32, cw.shape, 1)
        c = jnp.sum(jnp.where(lane == e, cw, 0.0), axis=1, keepdims=True)
        a = _dot(xrow, wg_b[...])
        b = _dot(xrow, wu_b[...])
        hid = (a * jax.nn.sigmoid(a)) * b * c
        ys_ref[...] = _dot(hid.astype(BF16), wd_b[...]).astype(ys_ref.dtype)


def _combine_kernel(nch_ref, go_ref, route_ref, h2_ref, g_ref, b_ref, ys_ref, out_ref,
                    ybuf, sem, *, n_sub, rows):
    i = pl.program_id(0)
    slot = lax.rem(i, 2)

    def copies(sub, slot_, act):
        def mk(lo, go):
            return pltpu.make_async_copy(ys_ref.at[pl.ds(go, SEG_ALIGN), :],
                                         ybuf.at[slot_, pl.ds(lo, SEG_ALIGN), :], sem.at[slot_])
        _chunk_copies(sub, nch_ref, go_ref, mk, act)

    @pl.when(i == 0)
    def _():
        ybuf[...] = jnp.zeros_like(ybuf)
        copies(0, 0, lambda c: c.start())

    @pl.when(i + 1 < n_sub)
    def _():
        copies(i + 1, 1 - slot, lambda c: c.start())

    copies(i, slot, lambda c: c.wait())
    tm = h2_ref.shape[0]
    r_io = lax.broadcasted_iota(jnp.int32, (tm, rows), 1).astype(F32)
    d1 = route_ref[:, 0:1]
    d2 = route_ref[:, 1:2]
    p_t = jnp.where((r_io == d1) | (r_io == d2), 1.0, 0.0).astype(BF16)
    ff = _dot(p_t, ybuf[slot])
    out_ref[...] = _ln(DN_ALPHA * h2_ref[...] + ff, g_ref[...], b_ref[...])


def _bias_kernel(tbl_ref, bkt_ref, out_ref):
    h = pl.program_id(0)
    far = tbl_ref[h, REL_BUCKETS - 1]
    for which in range(2):
        bk = bkt_ref[which]
        acc = jnp.where(bk < 0, NEG, 0.0)
        for kk in range(REL_BUCKETS):
            acc = jnp.where(bk == kk, tbl_ref[h, kk] - far, acc)
        out_ref[which, 0] = acc


def _rel_bucket_table(dist):
    max_exact = REL_BUCKETS // 2
    d = jnp.maximum(dist, 0)
    large = max_exact + (jnp.log(jnp.maximum(d, 1).astype(F32) / max_exact)
                         / math.log(REL_MAX_DIST / max_exact) * (REL_BUCKETS - max_exact)).astype(jnp.int32)
    large = jnp.minimum(large, REL_BUCKETS - 1)
    return jnp.where(d < max_exact, d, large)


def _const_spec(shape):
    nd = len(shape)
    return pl.BlockSpec(shape, lambda *_: (0,) * nd)


def kernel(x, mem, ln_in_g, ln_in_b, rel_bias, w_in, b_gate, w_pool_grp, pool_scale, w_pool_up, w_attn_up,
           w_mix_out, ln1_g, ln1_b, w_mq, w_mk, w_mv, w_mo, ln2_g, ln2_b, w_coarse, b_coarse, w_fine, b_fine,
           w_gate, w_up, w_down, ln3_g, ln3_b):
    B, S, _ = x.shape
    assert S == N_BLK * BLK and w_in.shape[0] == 1
    M = mem.shape[1]
    T = B * S
    tm = 512

    wi = w_in[0]
    w_u = wi[:, 0:POOL_W]
    w_q = wi[:, POOL_W:POOL_W + ATTN_W] * (HEAD_DIM ** -0.5)
    w_k = wi[:, POOL_W + ATTN_W:POOL_W + 2 * ATTN_W]
    w_v = wi[:, POOL_W + 2 * ATTN_W:POOL_W + 3 * ATTN_W]
    w_gl = wi[:, POOL_W + 3 * ATTN_W:]

    w1 = jnp.concatenate([w_u, w_q, w_k, w_v], axis=1).astype(BF16)
    row2 = lambda a: a.reshape(1, -1)

    iq = jnp.arange(BLK, dtype=jnp.int32)[:, None]
    ik = jnp.arange(BLK, dtype=jnp.int32)[None, :]
    d_own = iq - ik
    bkt = jnp.stack([jnp.where(d_own >= 0, _rel_bucket_table(d_own), -1), _rel_bucket_table(d_own + BLK)])
    t_bias = pl.pallas_call(
        _bias_kernel,
        grid=(N_HEADS,),
        in_specs=[pl.BlockSpec(memory_space=pltpu.SMEM), _const_spec((2, BLK, BLK))],
        out_specs=pl.BlockSpec((2, 1, BLK, BLK), lambda h: (0, h, 0, 0)),
        out_shape=jax.ShapeDtypeStruct((2, N_HEADS, BLK, BLK), F32),
        name="relbias_tiles",
    )(rel_bias.T, bkt)

    n_w1 = w1.shape[1]
    tmp = MERGE_TILE
    ypool, q_aug, k_aug, v_p = pl.pallas_call(
        functools.partial(_proj_kernel, tm=tmp),
        grid=(B, S // tmp),
        in_specs=[
            pl.BlockSpec((1, tmp, D), lambda b, s: (b, s, 0)),
            _const_spec((1, D)), _const_spec((1, D)),
            _const_spec((D, n_w1)),
            _const_spec((len(POOL_WINDOWS), LANE, LANE)),
            _const_spec((1, POOL_W)),
        ],
        out_specs=[
            pl.BlockSpec((1, tmp, POOL_W), lambda b, s: (b, s, 0)),
            pl.BlockSpec((1, N_HEADS, tmp, LANE), lambda b, s: (b, 0, s, 0)),
            pl.BlockSpec((1, N_HEADS, tmp, LANE), lambda b, s: (b, 0, s, 0)),
            pl.BlockSpec((1, ATTN_W // LANE, tmp, LANE), lambda b, s: (b, 0, s, 0)),
        ],
        out_shape=[
            jax.ShapeDtypeStruct((B, S, POOL_W), BF16),
            jax.ShapeDtypeStruct((B, N_HEADS, S, LANE), BF16),
            jax.ShapeDtypeStruct((B, N_HEADS, S, LANE), BF16),
            jax.ShapeDtypeStruct((B, ATTN_W // LANE, S, LANE), BF16),
        ],
        scratch_shapes=[pltpu.VMEM((HALO + tmp, POOL_W), F32), pltpu.VMEM((LANE, ATTN_W), F32)],
        compiler_params=pltpu.CompilerParams(dimension_semantics=("arbitrary", "arbitrary"),
                                             vmem_limit_bytes=VMEM_LIMIT),
        name="proj_pool_gate",
    )(x, row2(ln_in_g), row2(ln_in_b), w1, w_pool_grp[0].astype(BF16), row2(pool_scale[0]))

    o_attn = pl.pallas_call(
        _attn_kernel,
        grid=(B, N_BLK),
        in_specs=[
            pl.BlockSpec((1, N_HEADS, BLK, LANE), lambda b, j: (b, 0, j, 0)),
            pl.BlockSpec((1, N_HEADS, S, LANE), lambda b, j: (b, 0, 0, 0)),
            pl.BlockSpec((1, ATTN_W // LANE, S, LANE), lambda b, j: (b, 0, 0, 0)),
            _const_spec((2, N_HEADS, BLK, BLK)),
        ],
        out_specs=pl.BlockSpec((1, ATTN_W // LANE, BLK, LANE), lambda b, j: (b, 0, j, 0)),
        out_shape=jax.ShapeDtypeStruct((B, ATTN_W // LANE, S, LANE), BF16),
        compiler_params=pltpu.CompilerParams(dimension_semantics=("arbitrary", "arbitrary"),
                                             vmem_limit_bytes=VMEM_LIMIT),
        name="moba_attn",
    )(q_aug, k_aug, v_p, t_bias)

    kmem, vmem = pl.pallas_call(
        _memkv_kernel,
        grid=(B,),
        in_specs=[pl.BlockSpec((1, M, D), lambda b: (b, 0, 0)),
                  _const_spec((D, MEM_W)), _const_spec((D, MEM_W))],
        out_specs=[pl.BlockSpec((1, M, MEM_W), lambda b: (b, 0, 0)),
                   pl.BlockSpec((1, M, MEM_W), lambda b: (b, 0, 0))],
        out_shape=[jax.ShapeDtypeStruct((B, M, MEM_W), BF16)] * 2,
        compiler_params=pltpu.CompilerParams(dimension_semantics=("arbitrary",)),
        name="mem_kv",
    )(mem, w_mk[0].astype(BF16), w_mv[0].astype(BF16))

    w_r = jnp.concatenate([
        w_fine[0].reshape(D, N_EXPERTS),
        jnp.repeat(w_coarse[0], EPG, axis=1),
        jnp.zeros((D, LANE - 2 * N_EXPERTS), F32)], axis=1)
    b_r = jnp.concatenate([
        b_fine[0].reshape(N_EXPERTS), jnp.repeat(b_coarse[0], EPG),
        jnp.zeros((LANE - 2 * N_EXPERTS,), F32)]).reshape(1, LANE)
    w_r_hi = w_r.astype(BF16)
    w_r_lo = (w_r - w_r_hi.astype(F32)).astype(BF16)

    n_sub = T // tm
    tmm = MERGE_TILE
    per = tmm // tm
    sub_idx = lambda b, s: b * (S // tmm) + s
    h2, h2b, comb, route, route_t, seg_cnt = pl.pallas_call(
        functools.partial(_merge_kernel, tm=tmm),
        grid=(B, S // tmm),
        in_specs=[
            pl.BlockSpec((1, tmm, D), lambda b, s: (b, s, 0)),
            pl.BlockSpec((1, tmm, POOL_W), lambda b, s: (b, s, 0)),
            pl.BlockSpec((1, ATTN_W // LANE, tmm, LANE), lambda b, s: (b, 0, s, 0)),
            pl.BlockSpec((1, M, MEM_W), lambda b, s: (b, 0, 0)),
            pl.BlockSpec((1, M, MEM_W), lambda b, s: (b, 0, 0)),
            _const_spec((1, D)), _const_spec((1, D)),
            _const_spec((D, 2 * D)), _const_spec((1, 2 * D)),
            _const_spec((POOL_W, D)), _const_spec((ATTN_W, D)), _const_spec((D, D)),
            _const_spec((1, D)), _const_spec((1, D)),
            _const_spec((D, MEM_W)), _const_spec((MEM_W, D)),
            _const_spec((1, D)), _const_spec((1, D)),
            _const_spec((D, LANE)), _const_spec((D, LANE)), _const_spec((1, LANE)),
        ],
        out_specs=[
            pl.BlockSpec((1, tmm, D), lambda b, s: (b, s, 0)),
            pl.BlockSpec((1, tmm, D), lambda b, s: (b, s, 0)),
            pl.BlockSpec((1, tmm, LANE), lambda b, s: (b, s, 0)),
            pl.BlockSpec((1, tmm, LANE), lambda b, s: (b, s, 0)),
            pl.BlockSpec((per, 8, tm), lambda b, s: (sub_idx(b, s), 0, 0)),
            pl.BlockSpec((per, 8, LANE), lambda b, s: (sub_idx(b, s), 0, 0)),
        ],
        out_shape=[
            jax.ShapeDtypeStruct((B, S, D), F32),
            jax.ShapeDtypeStruct((B, S, D), BF16),
            jax.ShapeDtypeStruct((B, S, LANE), F32),
            jax.ShapeDtypeStruct((B, S, LANE), F32),
            jax.ShapeDtypeStruct((n_sub, 8, tm), F32),
            jax.ShapeDtypeStruct((n_sub, 8, LANE), F32),
        ],
        compiler_params=pltpu.CompilerParams(dimension_semantics=("arbitrary", "arbitrary"),
                                             vmem_limit_bytes=VMEM_LIMIT),
        name="merge_memattn_router",
    )(x, ypool, o_attn, kmem, vmem,
      row2(ln_in_g), row2(ln_in_b), w_gl.astype(BF16), row2(b_gate[0]),
      w_pool_up[0].astype(BF16), w_attn_up[0].astype(BF16), w_mix_out[0].astype(BF16),
      row2(ln1_g[0]), row2(ln1_b[0]),
      (w_mq[0] * (MEM_HD ** -0.5)).astype(BF16), w_mo[0].astype(BF16),
      row2(ln2_g[0]), row2(ln2_b[0]),
      w_r_hi, w_r_lo, b_r)

    pcs = seg_cnt[:, 0, :N_EXPERTS].astype(jnp.int32)
    tot = jnp.sum(pcs, axis=0)
    cap = ((tot + FFN_TILE - 1) // FFN_TILE) * FFN_TILE
    ends = jnp.cumsum(cap)
    base = ends - cap
    gs = base[None, :] + jnp.cumsum(pcs, axis=0) - pcs
    ls = jnp.cumsum(pcs, axis=1) - pcs
    n_sorted = n_sub * COMPACT_ROWS + N_EXPERTS * FFN_TILE
    n_ffn_tiles = n_sorted // FFN_TILE
    n_used = (ends[-1] // FFN_TILE).astype(jnp.int32)
    tile_row = jnp.arange(n_ffn_tiles, dtype=jnp.int32) * FFN_TILE
    tile_exp = jnp.sum(jnp.minimum(tile_row, ends[-1] - 1)[:, None] >= ends[None, :], axis=1).astype(jnp.int32)
    tails = jnp.concatenate([base + tot, cap - tot]).astype(jnp.int32)
    chunk_row = jnp.arange(MAX_CHUNKS, dtype=jnp.int32) * SEG_ALIGN
    chunk_exp = jnp.sum(chunk_row[None, :, None] >= (ls + pcs)[:, None, :], axis=2)
    shift = jnp.sum(jnp.where(chunk_exp[:, :, None] == jnp.arange(N_EXPERTS)[None, None, :],
                              (gs - ls)[:, None, :], 0), axis=2)
    chunk_go = (chunk_row[None, :] + shift).astype(jnp.int32).reshape(-1)
    n_chunks = (jnp.sum(pcs, axis=1) // SEG_ALIGN).astype(jnp.int32)

    aug_w = D + 2 * LANE
    x_sorted = pl.pallas_call(
        functools.partial(_dispatch_kernel, n_sub=n_sub, rows=COMPACT_ROWS),
        grid_spec=pltpu.PrefetchScalarGridSpec(
            num_scalar_prefetch=3,
            grid=(n_sub,),
            in_specs=[
                pl.BlockSpec((tm, D), lambda i, *_: (i, 0)),
                pl.BlockSpec((tm, LANE), lambda i, *_: (i, 0)),
                pl.BlockSpec((1, 8, tm), lambda i, *_: (i, 0, 0)),
            ],
            out_specs=pl.BlockSpec(memory_space=pl.ANY),
            scratch_shapes=[
                pltpu.VMEM((2, COMPACT_ROWS, aug_w), BF16),
                pltpu.VMEM((FFN_TILE // 2, aug_w), BF16),
                pltpu.SemaphoreType.DMA((2,)),
                pltpu.SemaphoreType.DMA(()),
            ],
        ),
        out_shape=jax.ShapeDtypeStruct((n_sorted, aug_w), BF16),
        compiler_params=pltpu.CompilerParams(dimension_semantics=("arbitrary",), vmem_limit_bytes=VMEM_LIMIT),
        name="moe_dispatch",
    )(n_chunks, chunk_go, tails, h2b.reshape(T, D), comb.reshape(T, LANE), route_t)

    used_tile = lambda t, texp, nused: (jnp.minimum(t, nused[0] - 1), 0)
    y_sorted = pl.pallas_call(
        _ffn_kernel,
        grid_spec=pltpu.PrefetchScalarGridSpec(
            num_scalar_prefetch=2,
            grid=(n_ffn_tiles,),
            in_specs=[
                pl.BlockSpec((FFN_TILE, aug_w), used_tile),
                pl.BlockSpec((1, D, FF), lambda t, texp, nused: (texp[t], 0, 0)),
                pl.BlockSpec((1, D, FF), lambda t, texp, nused: (texp[t], 0, 0)),
                pl.BlockSpec((1, FF, D), lambda t, texp, nused: (texp[t], 0, 0)),
            ],
            out_specs=pl.BlockSpec((FFN_TILE, D), used_tile),
            scratch_shapes=[pltpu.VMEM((D, FF), BF16), pltpu.VMEM((D, FF), BF16), pltpu.VMEM((FF, D), BF16)],
        ),
        out_shape=jax.ShapeDtypeStruct((n_sorted, D), BF16),
        compiler_params=pltpu.CompilerParams(dimension_semantics=("arbitrary",), vmem_limit_bytes=VMEM_LIMIT),
        name="moe_expert_ffn",
    )(tile_exp, n_used.reshape(1), x_sorted, w_gate[0], w_up[0], w_down[0])

    out = pl.pallas_call(
        functools.partial(_combine_kernel, n_sub=n_sub, rows=COMPACT_ROWS),
        grid_spec=pltpu.PrefetchScalarGridSpec(
            num_scalar_prefetch=2,
            grid=(n_sub,),
            in_specs=[
                pl.BlockSpec((tm, LANE), lambda i, *_: (i, 0)),
                pl.BlockSpec((tm, D), lambda i, *_: (i, 0)),
                pl.BlockSpec((1, D), lambda i, *_: (0, 0)),
                pl.BlockSpec((1, D), lambda i, *_: (0, 0)),
                pl.BlockSpec(memory_space=pl.ANY),
            ],
            out_specs=pl.BlockSpec((tm, D), lambda i, *_: (i, 0)),
            scratch_shapes=[
                pltpu.VMEM((2, COMPACT_ROWS, D), BF16),
                pltpu.SemaphoreType.DMA((2,)),
            ],
        ),
        out_shape=jax.ShapeDtypeStruct((T, D), F32),
        compiler_params=pltpu.CompilerParams(dimension_semantics=("arbitrary",), vmem_limit_bytes=VMEM_LIMIT),
        name="moe_combine_ln3",
    )(n_chunks, chunk_go, route.reshape(T, LANE), h2.reshape(T, D), row2(ln3_g[0]), row2(ln3_b[0]), y_sorted)
    return out.reshape(B, S, D)
```

```python
import functools
import math

import jax
import jax.numpy as jnp
from jax import lax
from jax.experimental import pallas as pl
from jax.experimental.pallas import tpu as pltpu

D = 1024
POOL_WINDOWS = (2, 4, 8, 16)
POOL_W = 512
N_HEADS = 8
HEAD_DIM = 64
ATTN_W = 512
BLK = 256
N_BLK = 8
TOPK = 3
REL_BUCKETS = 32
REL_MAX_DIST = 128
MEM_HEADS = 4
MEM_HD = 128
MEM_W = 512
N_GROUPS = 4
EPG = 8
N_EXPERTS = 32
FF = 256
DN_ALPHA = 2.0 ** 0.25
LN_EPS = 1e-5

LANE = 128
TOK_TILE = 512
SEG_ALIGN = 16
COMPACT_ROWS = 2 * TOK_TILE + N_EXPERTS * SEG_ALIGN
MAX_CHUNKS = COMPACT_ROWS // SEG_ALIGN
FFN_TILE = 512
ATTN_PAIRS_PER_TRIP = 2
MERGE_TILE = 1024
HALO = 16
NEG = -1e30
VMEM_LIMIT = 56 * 1024 * 1024

F32 = jnp.float32
BF16 = jnp.bfloat16

_NT = (((1,), (1,)), ((), ()))


def _dot(a, b):
    return jnp.dot(a, b, preferred_element_type=F32)


def _dot_nt(a, b):
    return lax.dot_general(a, b, _NT, preferred_element_type=F32)


def _split(a):
    hi = a.astype(BF16)
    lo = (a - hi.astype(F32)).astype(BF16)
    return hi, lo


def _interleave(chains):
    results = [None] * len(chains)
    live = list(range(len(chains)))
    while live:
        for ci in list(live):
            try:
                next(chains[ci])
            except StopIteration as done:
                results[ci] = done.value
                live.remove(ci)
    return results


def _ln(x, g, b):
    mu = jnp.mean(x, axis=-1, keepdims=True)
    xc = x - mu
    var = jnp.mean(xc * xc, axis=-1, keepdims=True)
    return xc * lax.rsqrt(var + LN_EPS) * g + b


def _proj_kernel(x_ref, g_ref, b_ref, w_ref, wgrp_ref, pscale_ref,
                 ypool_ref, q_ref, k_ref, v_ref, ubuf, kbt, *, tm):
    s = pl.program_id(1)

    @pl.when(s == 0)
    def _():
        ubuf[0:HALO, :] = jnp.zeros((HALO, POOL_W), F32)
        kbt[...] = jnp.zeros_like(kbt)

    _interleave([_proj_chain(ci, s, tm, x_ref, g_ref, b_ref, w_ref, wgrp_ref, pscale_ref,
                             ypool_ref, q_ref, k_ref, v_ref, ubuf, kbt)
                 for ci in range(tm // TOK_TILE)])
    ubuf[0:HALO, :] = ubuf[tm:tm + HALO, :]


def _proj_chain(ci, s, tm, x_ref, g_ref, b_ref, w_ref, wgrp_ref, pscale_ref,
                ypool_ref, q_ref, k_ref, v_ref, ubuf, kbt):
    n = TOK_TILE
    r0 = ci * n
    rows = slice(r0, r0 + n)
    blk0 = s * (tm // BLK) + ci * (n // BLK)

    h = _ln(x_ref[0, rows, :], g_ref[...], b_ref[...])
    hb = h.astype(BF16)
    zu = _dot(hb, w_ref[:, 0:POOL_W])
    zq = _dot(hb, w_ref[:, POOL_W:POOL_W + ATTN_W])
    zk = _dot(hb, w_ref[:, POOL_W + ATTN_W:POOL_W + 2 * ATTN_W])
    zv = _dot(hb, w_ref[:, POOL_W + 2 * ATTN_W:])
    ubuf[HALO + r0:HALO + r0 + n, :] = zu

    r_io = lax.broadcasted_iota(jnp.int32, kbt.shape, 0)
    c_io = lax.broadcasted_iota(jnp.int32, kbt.shape, 1)
    head_match = (r_io >> 3) == (c_io >> 6)
    for bi in range(n // BLK):
        kmean = jnp.mean(zk[bi * BLK:(bi + 1) * BLK], axis=0, keepdims=True)
        kbt[...] = jnp.where(head_match & ((r_io & 7) == blk0 + bi), kmean, kbt[...])
    yield

    t_pos = s * tm + r0 + lax.broadcasted_iota(jnp.int32, (n, LANE), 0)
    for g, w in enumerate(POOL_WINDOWS):
        cols = slice(g * LANE, (g + 1) * LANE)
        ws = ubuf[HALO + r0:HALO + r0 + n, cols]
        for kk in range(1, w):
            ws = ws + ubuf[HALO + r0 - kk:HALO + r0 - kk + n, cols]
        cnt = jnp.minimum(t_pos + 1, w).astype(F32)
        y = ws / cnt - ubuf[HALO + r0:HALO + r0 + n, cols]
        yg = _dot(y.astype(BF16), wgrp_ref[g]) * pscale_ref[:, cols]
        ypool_ref[0, rows, cols] = yg.astype(ypool_ref.dtype)
    yield

    q_hi, q_lo = _split(zq)
    kb_hi, kb_lo = _split(kbt[...])
    g2 = _dot_nt(q_hi, jnp.concatenate([kb_hi, kb_lo], axis=0))
    gate = g2[:, 0:LANE] + g2[:, LANE:2 * LANE] + _dot_nt(q_lo, kb_hi)
    yield

    lane = lax.broadcasted_iota(jnp.int32, (n, LANE), 1)
    row = lax.broadcasted_iota(jnp.int32, (n, LANE), 0)
    n_l = lane & 7
    jrow = blk0 + (row >> 8)
    past = n_l < jrow
    gt = jnp.where(past, gate, -jnp.inf)
    cnt = jnp.zeros((n, LANE), F32)
    for sh in range(1, N_BLK):
        wrap = (n_l + sh) >= N_BLK
        gm = jnp.where(wrap, pltpu.roll(gt, N_BLK - sh, 1), pltpu.roll(gt, LANE - sh, 1))
        cnt = cnt + jnp.where(wrap, jnp.where(gm >= gt, 1.0, 0.0), jnp.where(gm > gt, 1.0, 0.0))
    keep = (past & (cnt < TOPK)) | (n_l == jrow)
    negmask = jnp.where(keep, 0.0, NEG)

    aug_lane = (lane >= HEAD_DIM) & (lane < HEAD_DIM + N_BLK)
    k_onehot = jnp.where(lane == HEAD_DIM + jrow, 1.0, 0.0)
    head_lane = lane < HEAD_DIM
    for hh in range(N_HEADS):
        cols = slice((hh // 2) * LANE, (hh // 2 + 1) * LANE)
        q_h, k_h = zq[:, cols], zk[:, cols]
        if hh % 2:
            q_h, k_h = pltpu.roll(q_h, HEAD_DIM, 1), pltpu.roll(k_h, HEAD_DIM, 1)
        m_h = jnp.where(aug_lane, pltpu.roll(negmask, HEAD_DIM - N_BLK * hh, 1), 0.0)
        q_ref[0, hh, rows, :] = jnp.where(head_lane, q_h, m_h).astype(q_ref.dtype)
        k_ref[0, hh, rows, :] = jnp.where(head_lane, k_h, k_onehot).astype(k_ref.dtype)
    for p in range(ATTN_W // LANE):
        v_ref[0, p, rows, :] = zv[:, p * LANE:(p + 1) * LANE].astype(v_ref.dtype)


def _attn_kernel(q_ref, k_ref, v_ref, tb_ref, o_ref):
    j = pl.program_id(1)
    lane = lax.broadcasted_iota(jnp.int32, (BLK, LANE), 1)

    def one_head(h, p, jj):
        q = q_ref[0, h]
        own0 = jj * BLK
        pieces = []
        s_own = _dot_nt(q, k_ref[0, h, own0:own0 + BLK, :]) + tb_ref[0, h]
        pieces.append((s_own, own0, BLK))
        if jj >= 1:
            s_adj = _dot_nt(q, k_ref[0, h, own0 - BLK:own0, :]) + tb_ref[1, h]
            pieces.append((s_adj, own0 - BLK, BLK))
        if jj >= 2:
            s_far = _dot_nt(q, k_ref[0, h, 0:own0 - BLK, :])
            pieces.append((s_far, 0, own0 - BLK))
        yield
        m = None
        for sc, _, _ in pieces:
            mm = jnp.max(sc, axis=1, keepdims=True)
            m = mm if m is None else jnp.maximum(m, mm)
        l = None
        probs = []
        for sc, start, size in pieces:
            e = jnp.exp(sc - m)
            ls = jnp.sum(e, axis=1, keepdims=True)
            l = ls if l is None else l + ls
            probs.append((e.astype(BF16), start, size))
        yield
        acc = None
        for pb, start, size in probs:
            pv = _dot(pb, v_ref[0, p, start:start + size, :])
            acc = pv if acc is None else acc + pv
        return acc / l

    for jj in range(N_BLK):
        @pl.when(j == jj)
        def _(jj=jj):
            def group(gi, carry):
                pairs = [gi * ATTN_PAIRS_PER_TRIP + pi for pi in range(ATTN_PAIRS_PER_TRIP)]
                outs = _interleave([one_head(2 * p + hh, p, jj) for p in pairs for hh in range(2)])
                for pi, p in enumerate(pairs):
                    o_ref[0, p] = jnp.where(lane < HEAD_DIM, outs[2 * pi], outs[2 * pi + 1]).astype(o_ref.dtype)
                return carry
            lax.fori_loop(0, N_HEADS // 2 // ATTN_PAIRS_PER_TRIP, group, 0)


def _memkv_kernel(mem_ref, wk_ref, wv_ref, k_ref, v_ref):
    mb = mem_ref[0].astype(BF16)
    k_ref[0] = _dot(mb, wk_ref[...]).astype(k_ref.dtype)
    v_ref[0] = _dot(mb, wv_ref[...]).astype(v_ref.dtype)


def _merge_kernel(x_ref, ypool_ref, o_ref, kmem_ref, vmem_ref,
                  lng_ref, lnb_ref, wgl_ref, bgate_ref, wpu_ref, wau_ref, wout_ref,
                  ln1g_ref, ln1b_ref, wmq_ref, wmo_ref, ln2g_ref, ln2b_ref,
                  wrh_ref, br_ref,
                  h2_ref, h2b_ref, comb_ref, route_ref, route_t_ref, cnt_ref, *, tm):
    _interleave([_merge_chain(ci, x_ref, ypool_ref, o_ref, kmem_ref, vmem_ref,
                              lng_ref, lnb_ref, wgl_ref, bgate_ref, wpu_ref, wau_ref, wout_ref,
                              ln1g_ref, ln1b_ref, wmq_ref, wmo_ref, ln2g_ref, ln2b_ref,
                              wrh_ref, br_ref, h2_ref, h2b_ref, comb_ref, route_ref, route_t_ref, cnt_ref)
                 for ci in range(tm // TOK_TILE)])


def _merge_chain(ci, x_ref, ypool_ref, o_ref, kmem_ref, vmem_ref,
                 lng_ref, lnb_ref, wgl_ref, bgate_ref, wpu_ref, wau_ref, wout_ref,
                 ln1g_ref, ln1b_ref, wmq_ref, wmo_ref, ln2g_ref, ln2b_ref,
                 wrh_ref, br_ref, h2_ref, h2b_ref, comb_ref, route_ref, route_t_ref, cnt_ref):
    rows = slice(ci * TOK_TILE, (ci + 1) * TOK_TILE)
    n_rows = TOK_TILE
    h = _ln(x_ref[0, rows, :], lng_ref[...], lnb_ref[...])
    hb = h.astype(BF16)
    gl = _dot(hb, wgl_ref[...]) + bgate_ref[...]
    yield
    gates = 0.5 * jnp.tanh(0.5 * gl) + 0.5
    y_pool = _dot(ypool_ref[0, rows, :], wpu_ref[...])
    o_cat = jnp.concatenate([o_ref[0, p, rows, :] for p in range(ATTN_W // LANE)], axis=1)
    y_attn = _dot(o_cat, wau_ref[...])
    yield
    merged = gates[:, 0:D] * y_pool + gates[:, D:2 * D] * y_attn
    mix = _dot(merged.astype(BF16), wout_ref[...])
    yield
    h1 = _ln(DN_ALPHA * h + mix, ln1g_ref[...], ln1b_ref[...])

    qm = _dot(h1.astype(BF16), wmq_ref[...]).astype(BF16)
    yield
    outs = []
    for hd in range(MEM_HEADS):
        cols = slice(hd * MEM_HD, (hd + 1) * MEM_HD)
        sc = _dot_nt(qm[:, cols], kmem_ref[0, :, cols])
        m = jnp.max(sc, axis=1, keepdims=True)
        e = jnp.exp(sc - m)
        l = jnp.sum(e, axis=1, keepdims=True)
        outs.append(_dot(e.astype(BF16), vmem_ref[0, :, cols]) / l)
    om = jnp.concatenate(outs, axis=1).astype(BF16)
    xa = _dot(om, wmo_ref[...])
    yield
    h2 = _ln(DN_ALPHA * h1 + xa, ln2g_ref[...], ln2b_ref[...])
    h2_ref[0, rows, :] = h2
    h2b_ref[0, rows, :] = h2.astype(BF16)

    x_hi, x_lo = _split(h2)
    r2 = _dot(x_hi, wrh_ref[...])
    r = r2[:, 0:LANE] + r2[:, LANE:2 * LANE] + _dot(x_lo, wrh_ref[:, 0:LANE]) + br_ref[...]
    yield
    lane = lax.broadcasted_iota(jnp.int32, (n_rows, LANE), 1)
    lane_f = lane.astype(F32)
    cmask = (lane >= N_EXPERTS) & (lane < 2 * N_EXPERTS)
    c = jnp.where(cmask, r, -jnp.inf)
    cmax = jnp.max(c, axis=1, keepdims=True)
    ce = jnp.exp(c - cmax)
    csum = jnp.sum(ce, axis=1, keepdims=True) * (1.0 / EPG)
    g_prob = 1.0 / csum
    grp_lane = ((lane & (N_EXPERTS - 1)) >> 3).astype(F32)
    gidx = jnp.min(jnp.where(cmask & (c == cmax), grp_lane, 99.0), axis=1, keepdims=True)
    fmask = (lane < N_EXPERTS) & (grp_lane == gidx)
    f = jnp.where(fmask, r, -jnp.inf)
    fmax = jnp.max(f, axis=1, keepdims=True)
    fe = jnp.exp(f - fmax)
    fsum = jnp.sum(fe, axis=1, keepdims=True)
    prob = fe / fsum
    p1 = jnp.max(prob, axis=1, keepdims=True)
    i1 = jnp.min(jnp.where(fmask & (prob == p1), lane_f, 999.0), axis=1, keepdims=True)
    rest = fmask & (lane_f != i1)
    prob2 = jnp.where(rest, prob, -1.0)
    p2 = jnp.max(prob2, axis=1, keepdims=True)
    i2 = jnp.min(jnp.where(rest & (prob2 == p2), lane_f, 999.0), axis=1, keepdims=True)
    den = p1 + p2
    comb = jnp.where(lane_f == i1, g_prob * (p1 / den),
                     jnp.where(lane_f == i2, g_prob * (p2 / den), 0.0))
    comb_ref[0, rows, :] = comb
    yield

    sel = jnp.where((lane_f == i1) | (lane_f == i2), 1.0, 0.0)
    cnt = jnp.sum(sel, axis=0, keepdims=True)
    pc = jnp.floor((cnt + (SEG_ALIGN - 1)) * (1.0 / SEG_ALIGN)) * SEG_ALIGN
    lane8 = lax.broadcasted_iota(jnp.int32, (8, LANE), 1)
    inc = jnp.broadcast_to(pc, (8, LANE))
    for sh in (1, 2, 4, 8, 16, 32, 64):
        inc = inc + jnp.where(lane8 >= sh, pltpu.roll(inc, sh, 1), 0.0)
    seg_start = inc[0:1] - pc
    t_row = lax.broadcasted_iota(jnp.int32, (n_rows, n_rows), 0)
    t_col = lax.broadcasted_iota(jnp.int32, (n_rows, n_rows), 1)
    earlier = jnp.where(t_row > t_col, 1.0, 0.0).astype(BF16)
    rank = _dot(earlier, sel.astype(BF16))
    pos = seg_start + rank
    d1 = jnp.sum(jnp.where(lane_f == i1, pos, 0.0), axis=1, keepdims=True)
    d2 = jnp.sum(jnp.where(lane_f == i2, pos, 0.0), axis=1, keepdims=True)
    route = jnp.where(lane == 0, d1, jnp.where(lane == 1, d2, 0.0))
    route_ref[0, rows, :] = route
    r_hi, r_lo = _split(route)
    eye = jnp.where(lax.broadcasted_iota(jnp.int32, (8, LANE), 0) == lane8, 1.0, 0.0).astype(BF16)
    route_t_ref[ci] = _dot_nt(eye, r_hi) + _dot_nt(eye, r_lo)
    cnt_ref[ci] = jnp.broadcast_to(pc, (8, LANE))


def _chunk_copies(sub, nch_ref, go_ref, make_copy, act):
    def body(k, carry):
        go = pl.multiple_of(go_ref[sub * MAX_CHUNKS + k], SEG_ALIGN)
        act(make_copy(pl.multiple_of(k * SEG_ALIGN, SEG_ALIGN), go))
        return carry
    lax.fori_loop(0, nch_ref[sub], body, 0)


def _dispatch_kernel(nch_ref, go_ref, tail_ref, xb_ref, comb_ref, route_t_ref, xs_ref,
                     cbuf, zbuf, sem, zsem, *, n_sub, rows):
    i = pl.program_id(0)
    slot = lax.rem(i, 2)

    def copies(sub, slot_, act):
        def mk(lo, go):
            return pltpu.make_async_copy(cbuf.at[slot_, pl.ds(lo, SEG_ALIGN), :],
                                         xs_ref.at[pl.ds(go, SEG_ALIGN), :], sem.at[slot_])
        _chunk_copies(sub, nch_ref, go_ref, mk, act)

    @pl.when(i >= 2)
    def _():
        copies(i - 2, slot, lambda c: c.wait())

    tm = xb_ref.shape[0]
    r_io = lax.broadcasted_iota(jnp.int32, (rows, tm), 0).astype(F32)
    d1 = route_t_ref[0, 0:1, :]
    d2 = route_t_ref[0, 1:2, :]
    p_mat = jnp.where((r_io == d1) | (r_io == d2), 1.0, 0.0).astype(BF16)
    comb = comb_ref[...]
    c_hi = comb.astype(BF16).astype(F32)
    c_pack = (c_hi + pltpu.roll(comb - c_hi, N_EXPERTS, 1)).astype(BF16)
    x_aug = jnp.concatenate([xb_ref[...], c_pack], axis=1)
    cbuf[slot] = _dot(p_mat, x_aug).astype(BF16)
    copies(i, slot, lambda c: c.start())

    @pl.when(i == n_sub - 1)
    def _():
        if n_sub >= 2:
            copies(i - 1, 1 - slot, lambda c: c.wait())
        copies(i, slot, lambda c: c.wait())
        zbuf[...] = jnp.zeros_like(zbuf)

        def tails(act):
            def body(e, carry):
                st = tail_ref[e]
                n = tail_ref[N_EXPERTS + e]
                off = jnp.int32(0)
                size = zbuf.shape[0]
                while size >= SEG_ALIGN:
                    bit = (n & size) != 0

                    @pl.when(bit)
                    def _(size=size, off=off):
                        act(pltpu.make_async_copy(
                            zbuf.at[pl.ds(0, size), :],
                            xs_ref.at[pl.ds(pl.multiple_of(st + off, SEG_ALIGN), size), :], zsem))
                    off = off + jnp.where(bit, size, 0)
                    size //= 2
                return carry
            lax.fori_loop(0, N_EXPERTS, body, 0)
        tails(lambda c: c.start())
        tails(lambda c: c.wait())


def _ffn_kernel(texp_ref, nused_ref, xs_ref, wg_ref, wu_ref, wd_ref, ys_ref, wg_b, wu_b, wd_b):
    t = pl.program_id(0)
    e = texp_ref[t]

    @pl.when((t == 0) | (e != texp_ref[jnp.maximum(t - 1, 0)]))
    def _():
        wg_b[...] = wg_ref[0].astype(BF16)
        wu_b[...] = wu_ref[0].astype(BF16)
        wd_b[...] = wd_ref[0].astype(BF16)

    @pl.when(t < nused_ref[0])
    def _():
        xa = xs_ref[...]
        xrow = xa[:, 0:D]
        cw = xa[:, D:D + LANE].astype(F32)
        lane = lax.broadcasted_iota(jnp.int32, cw.shape, 1)
        c = jnp.sum(jnp.where((lane == e) | (lane == e + N_EXPERTS), cw, 0.0), axis=1, keepdims=True)
        a = _dot(xrow, wg_b[...])
        b = _dot(xrow, wu_b[...])
        hid = (a * jax.nn.sigmoid(a)) * b * c
        ys_ref[...] = _dot(hid.astype(BF16), wd_b[...]).astype(ys_ref.dtype)


def _combine_kernel(nch_ref, go_ref, route_ref, h2_ref, g_ref, b_ref, ys_ref, out_ref,
                    ybuf, sem, *, n_sub, rows):
    i = pl.program_id(0)
    slot = lax.rem(i, 2)

    def copies(sub, slot_, act):
        def mk(lo, go):
            return pltpu.make_async_copy(ys_ref.at[pl.ds(go, SEG_ALIGN), :],
                                         ybuf.at[slot_, pl.ds(lo, SEG_ALIGN), :], sem.at[slot_])
        _chunk_copies(sub, nch_ref, go_ref, mk, act)

    @pl.when(i == 0)
    def _():
        ybuf[...] = jnp.zeros_like(ybuf)
        copies(0, 0, lambda c: c.start())

    @pl.when(i + 1 < n_sub)
    def _():
        copies(i + 1, 1 - slot, lambda c: c.start())

    copies(i, slot, lambda c: c.wait())
    tm = h2_ref.shape[0]
    r_io = lax.broadcasted_iota(jnp.int32, (tm, rows), 1).astype(F32)
    d1 = route_ref[:, 0:1]
    d2 = route_ref[:, 1:2]
    p_t = jnp.where((r_io == d1) | (r_io == d2), 1.0, 0.0).astype(BF16)
    ff = _dot(p_t, ybuf[slot])
    out_ref[...] = _ln(DN_ALPHA * h2_ref[...] + ff, g_ref[...], b_ref[...])


def _bias_kernel(tbl_ref, bkt_ref, out_ref):
    h = pl.program_id(0)
    far = tbl_ref[h, REL_BUCKETS - 1]
    for which in range(2):
        bk = bkt_ref[which]
        acc = jnp.where(bk < 0, NEG, 0.0)
        for kk in range(REL_BUCKETS):
            acc = jnp.where(bk == kk, tbl_ref[h, kk] - far, acc)
        out_ref[which, 0] = acc


def _rel_bucket_table(dist):
    max_exact = REL_BUCKETS // 2
    d = jnp.maximum(dist, 0)
    large = max_exact + (jnp.log(jnp.maximum(d, 1).astype(F32) / max_exact)
                         / math.log(REL_MAX_DIST / max_exact) * (REL_BUCKETS - max_exact)).astype(jnp.int32)
    large = jnp.minimum(large, REL_BUCKETS - 1)
    return jnp.where(d < max_exact, d, large)


def _const_spec(shape):
    nd = len(shape)
    return pl.BlockSpec(shape, lambda *_: (0,) * nd)


def kernel(x, mem, ln_in_g, ln_in_b, rel_bias, w_in, b_gate, w_pool_grp, pool_scale, w_pool_up, w_attn_up,
           w_mix_out, ln1_g, ln1_b, w_mq, w_mk, w_mv, w_mo, ln2_g, ln2_b, w_coarse, b_coarse, w_fine, b_fine,
           w_gate, w_up, w_down, ln3_g, ln3_b):
    B, S, _ = x.shape
    assert S == N_BLK * BLK and w_in.shape[0] == 1
    M = mem.shape[1]
    T = B * S
    tm = 512

    wi = w_in[0]
    w_u = wi[:, 0:POOL_W]
    w_q = wi[:, POOL_W:POOL_W + ATTN_W] * (HEAD_DIM ** -0.5)
    w_k = wi[:, POOL_W + ATTN_W:POOL_W + 2 * ATTN_W]
    w_v = wi[:, POOL_W + 2 * ATTN_W:POOL_W + 3 * ATTN_W]
    w_gl = wi[:, POOL_W + 3 * ATTN_W:]

    w1 = jnp.concatenate([w_u, w_q, w_k, w_v], axis=1).astype(BF16)
    row2 = lambda a: a.reshape(1, -1)

    iq = jnp.arange(BLK, dtype=jnp.int32)[:, None]
    ik = jnp.arange(BLK, dtype=jnp.int32)[None, :]
    d_own = iq - ik
    bkt = jnp.stack([jnp.where(d_own >= 0, _rel_bucket_table(d_own), -1), _rel_bucket_table(d_own + BLK)])
    t_bias = pl.pallas_call(
        _bias_kernel,
        grid=(N_HEADS,),
        in_specs=[pl.BlockSpec(memory_space=pltpu.SMEM), _const_spec((2, BLK, BLK))],
        out_specs=pl.BlockSpec((2, 1, BLK, BLK), lambda h: (0, h, 0, 0)),
        out_shape=jax.ShapeDtypeStruct((2, N_HEADS, BLK, BLK), F32),
        name="relbias_tiles",
    )(rel_bias.T, bkt)

    n_w1 = w1.shape[1]
    tmp = MERGE_TILE
    ypool, q_aug, k_aug, v_p = pl.pallas_call(
        functools.partial(_proj_kernel, tm=tmp),
        grid=(B, S // tmp),
        in_specs=[
            pl.BlockSpec((1, tmp, D), lambda b, s: (b, s, 0)),
            _const_spec((1, D)), _const_spec((1, D)),
            _const_spec((D, n_w1)),
            _const_spec((len(POOL_WINDOWS), LANE, LANE)),
            _const_spec((1, POOL_W)),
        ],
        out_specs=[
            pl.BlockSpec((1, tmp, POOL_W), lambda b, s: (b, s, 0)),
            pl.BlockSpec((1, N_HEADS, tmp, LANE), lambda b, s: (b, 0, s, 0)),
            pl.BlockSpec((1, N_HEADS, tmp, LANE), lambda b, s: (b, 0, s, 0)),
            pl.BlockSpec((1, ATTN_W // LANE, tmp, LANE), lambda b, s: (b, 0, s, 0)),
        ],
        out_shape=[
            jax.ShapeDtypeStruct((B, S, POOL_W), BF16),
            jax.ShapeDtypeStruct((B, N_HEADS, S, LANE), BF16),
            jax.ShapeDtypeStruct((B, N_HEADS, S, LANE), BF16),
            jax.ShapeDtypeStruct((B, ATTN_W // LANE, S, LANE), BF16),
        ],
        scratch_shapes=[pltpu.VMEM((HALO + tmp, POOL_W), F32), pltpu.VMEM((LANE, ATTN_W), F32)],
        compiler_params=pltpu.CompilerParams(dimension_semantics=("arbitrary", "arbitrary"),
                                             vmem_limit_bytes=VMEM_LIMIT),
        name="proj_pool_gate",
    )(x, row2(ln_in_g), row2(ln_in_b), w1, w_pool_grp[0].astype(BF16), row2(pool_scale[0]))

    o_attn = pl.pallas_call(
        _attn_kernel,
        grid=(B, N_BLK),
        in_specs=[
            pl.BlockSpec((1, N_HEADS, BLK, LANE), lambda b, j: (b, 0, j, 0)),
            pl.BlockSpec((1, N_HEADS, S, LANE), lambda b, j: (b, 0, 0, 0)),
            pl.BlockSpec((1, ATTN_W // LANE, S, LANE), lambda b, j: (b, 0, 0, 0)),
            _const_spec((2, N_HEADS, BLK, BLK)),
        ],
        out_specs=pl.BlockSpec((1, ATTN_W // LANE, BLK, LANE), lambda b, j: (b, 0, j, 0)),
        out_shape=jax.ShapeDtypeStruct((B, ATTN_W // LANE, S, LANE), BF16),
        compiler_params=pltpu.CompilerParams(dimension_semantics=("arbitrary", "arbitrary"),
                                             vmem_limit_bytes=VMEM_LIMIT),
        name="moba_attn",
    )(q_aug, k_aug, v_p, t_bias)

    kmem, vmem = pl.pallas_call(
        _memkv_kernel,
        grid=(B,),
        in_specs=[pl.BlockSpec((1, M, D), lambda b: (b, 0, 0)),
                  _const_spec((D, MEM_W)), _const_spec((D, MEM_W))],
        out_specs=[pl.BlockSpec((1, M, MEM_W), lambda b: (b, 0, 0)),
                   pl.BlockSpec((1, M, MEM_W), lambda b: (b, 0, 0))],
        out_shape=[jax.ShapeDtypeStruct((B, M, MEM_W), BF16)] * 2,
        compiler_params=pltpu.CompilerParams(dimension_semantics=("arbitrary",)),
        name="mem_kv",
    )(mem, w_mk[0].astype(BF16), w_mv[0].astype(BF16))

    w_r = jnp.concatenate([
        w_fine[0].reshape(D, N_EXPERTS),
        jnp.repeat(w_coarse[0], EPG, axis=1),
        jnp.zeros((D, LANE - 2 * N_EXPERTS), F32)], axis=1)
    b_r = jnp.concatenate([
        b_fine[0].reshape(N_EXPERTS), jnp.repeat(b_coarse[0], EPG),
        jnp.zeros((LANE - 2 * N_EXPERTS,), F32)]).reshape(1, LANE)
    w_r_hi = w_r.astype(BF16)
    w_r_lo = (w_r - w_r_hi.astype(F32)).astype(BF16)
    w_r_cat = jnp.concatenate([w_r_hi, w_r_lo], axis=1)

    n_sub = T // tm
    tmm = MERGE_TILE
    per = tmm // tm
    sub_idx = lambda b, s: b * (S // tmm) + s
    h2, h2b, comb, route, route_t, seg_cnt = pl.pallas_call(
        functools.partial(_merge_kernel, tm=tmm),
        grid=(B, S // tmm),
        in_specs=[
            pl.BlockSpec((1, tmm, D), lambda b, s: (b, s, 0)),
            pl.BlockSpec((1, tmm, POOL_W), lambda b, s: (b, s, 0)),
            pl.BlockSpec((1, ATTN_W // LANE, tmm, LANE), lambda b, s: (b, 0, s, 0)),
            pl.BlockSpec((1, M, MEM_W), lambda b, s: (b, 0, 0)),
            pl.BlockSpec((1, M, MEM_W), lambda b, s: (b, 0, 0)),
            _const_spec((1, D)), _const_spec((1, D)),
            _const_spec((D, 2 * D)), _const_spec((1, 2 * D)),
            _const_spec((POOL_W, D)), _const_spec((ATTN_W, D)), _const_spec((D, D)),
            _const_spec((1, D)), _const_spec((1, D)),
            _const_spec((D, MEM_W)), _const_spec((MEM_W, D)),
            _const_spec((1, D)), _const_spec((1, D)),
            _const_spec((D, 2 * LANE)), _const_spec((1, LANE)),
        ],
        out_specs=[
            pl.BlockSpec((1, tmm, D), lambda b, s: (b, s, 0)),
            pl.BlockSpec((1, tmm, D), lambda b, s: (b, s, 0)),
            pl.BlockSpec((1, tmm, LANE), lambda b, s: (b, s, 0)),
            pl.BlockSpec((1, tmm, LANE), lambda b, s: (b, s, 0)),
            pl.BlockSpec((per, 8, tm), lambda b, s: (sub_idx(b, s), 0, 0)),
            pl.BlockSpec((per, 8, LANE), lambda b, s: (sub_idx(b, s), 0, 0)),
        ],
        out_shape=[
            jax.ShapeDtypeStruct((B, S, D), F32),
            jax.ShapeDtypeStruct((B, S, D), BF16),
            jax.ShapeDtypeStruct((B, S, LANE), F32),
            jax.ShapeDtypeStruct((B, S, LANE), F32),
            jax.ShapeDtypeStruct((n_sub, 8, tm), F32),
            jax.ShapeDtypeStruct((n_sub, 8, LANE), F32),
        ],
        compiler_params=pltpu.CompilerParams(dimension_semantics=("arbitrary", "arbitrary"),
                                             vmem_limit_bytes=VMEM_LIMIT),
        name="merge_memattn_router",
    )(x, ypool, o_attn, kmem, vmem,
      row2(ln_in_g), row2(ln_in_b), w_gl.astype(BF16), row2(b_gate[0]),
      w_pool_up[0].astype(BF16), w_attn_up[0].astype(BF16), w_mix_out[0].astype(BF16),
      row2(ln1_g[0]), row2(ln1_b[0]),
      (w_mq[0] * (MEM_HD ** -0.5)).astype(BF16), w_mo[0].astype(BF16),
      row2(ln2_g[0]), row2(ln2_b[0]),
      w_r_cat, b_r)

    pcs = seg_cnt[:, 0, :N_EXPERTS].astype(jnp.int32)
    tot = jnp.sum(pcs, axis=0)
    cap = ((tot + FFN_TILE - 1) // FFN_TILE) * FFN_TILE
    ends = jnp.cumsum(cap)
    base = ends - cap
    gs = base[None, :] + jnp.cumsum(pcs, axis=0) - pcs
    ls = jnp.cumsum(pcs, axis=1) - pcs
    n_sorted = n_sub * COMPACT_ROWS + N_EXPERTS * FFN_TILE
    n_ffn_tiles = n_sorted // FFN_TILE
    n_used = (ends[-1] // FFN_TILE).astype(jnp.int32)
    tile_row = jnp.arange(n_ffn_tiles, dtype=jnp.int32) * FFN_TILE
    tile_exp = jnp.sum(jnp.minimum(tile_row, ends[-1] - 1)[:, None] >= ends[None, :], axis=1).astype(jnp.int32)
    tails = jnp.concatenate([base + tot, cap - tot]).astype(jnp.int32)
    chunk_row = jnp.arange(MAX_CHUNKS, dtype=jnp.int32) * SEG_ALIGN
    chunk_exp = jnp.sum(chunk_row[None, :, None] >= (ls + pcs)[:, None, :], axis=2)
    shift = jnp.sum(jnp.where(chunk_exp[:, :, None] == jnp.arange(N_EXPERTS)[None, None, :],
                              (gs - ls)[:, None, :], 0), axis=2)
    chunk_go = (chunk_row[None, :] + shift).astype(jnp.int32).reshape(-1)
    n_chunks = (jnp.sum(pcs, axis=1) // SEG_ALIGN).astype(jnp.int32)

    aug_w = D + LANE
    x_sorted = pl.pallas_call(
        functools.partial(_dispatch_kernel, n_sub=n_sub, rows=COMPACT_ROWS),
        grid_spec=pltpu.PrefetchScalarGridSpec(
            num_scalar_prefetch=3,
            grid=(n_sub,),
            in_specs=[
                pl.BlockSpec((tm, D), lambda i, *_: (i, 0)),
                pl.BlockSpec((tm, LANE), lambda i, *_: (i, 0)),
                pl.BlockSpec((1, 8, tm), lambda i, *_: (i, 0, 0)),
            ],
            out_specs=pl.BlockSpec(memory_space=pl.ANY),
            scratch_shapes=[
                pltpu.VMEM((2, COMPACT_ROWS, aug_w), BF16),
                pltpu.VMEM((FFN_TILE // 2, aug_w), BF16),
                pltpu.SemaphoreType.DMA((2,)),
                pltpu.SemaphoreType.DMA(()),
            ],
        ),
        out_shape=jax.ShapeDtypeStruct((n_sorted, aug_w), BF16),
        compiler_params=pltpu.CompilerParams(dimension_semantics=("arbitrary",), vmem_limit_bytes=VMEM_LIMIT),
        name="moe_dispatch",
    )(n_chunks, chunk_go, tails, h2b.reshape(T, D), comb.reshape(T, LANE), route_t)

    used_tile = lambda t, texp, nused: (jnp.minimum(t, nused[0] - 1), 0)
    y_sorted = pl.pallas_call(
        _ffn_kernel,
        grid_spec=pltpu.PrefetchScalarGridSpec(
            num_scalar_prefetch=2,
            grid=(n_ffn_tiles,),
            in_specs=[
                pl.BlockSpec((FFN_TILE, aug_w), used_tile),
                pl.BlockSpec((1, D, FF), lambda t, texp, nused: (texp[t], 0, 0)),
                pl.BlockSpec((1, D, FF), lambda t, texp, nused: (texp[t], 0, 0)),
                pl.BlockSpec((1, FF, D), lambda t, texp, nused: (texp[t], 0, 0)),
            ],
            out_specs=pl.BlockSpec((FFN_TILE, D), used_tile),
            scratch_shapes=[pltpu.VMEM((D, FF), BF16), pltpu.VMEM((D, FF), BF16), pltpu.VMEM((FF, D), BF16)],
        ),
        out_shape=jax.ShapeDtypeStruct((n_sorted, D), BF16),
        compiler_params=pltpu.CompilerParams(dimension_semantics=("arbitrary",), vmem_limit_bytes=VMEM_LIMIT),
        name="moe_expert_ffn",
    )(tile_exp, n_used.reshape(1), x_sorted, w_gate[0], w_up[0], w_down[0])

    out = pl.pallas_call(
        functools.partial(_combine_kernel, n_sub=n_sub, rows=COMPACT_ROWS),
        grid_spec=pltpu.PrefetchScalarGridSpec(
            num_scalar_prefetch=2,
            grid=(n_sub,),
            in_specs=[
                pl.BlockSpec((tm, LANE), lambda i, *_: (i, 0)),
                pl.BlockSpec((tm, D), lambda i, *_: (i, 0)),
                pl.BlockSpec((1, D), lambda i, *_: (0, 0)),
                pl.BlockSpec((1, D), lambda i, *_: (0, 0)),
                pl.BlockSpec(memory_space=pl.ANY),
            ],
            out_specs=pl.BlockSpec((tm, D), lambda i, *_: (i, 0)),
            scratch_shapes=[
                pltpu.VMEM((2, COMPACT_ROWS, D), BF16),
                pltpu.SemaphoreType.DMA((2,)),
            ],
        ),
        out_shape=jax.ShapeDtypeStruct((T, D), F32),
        compiler_params=pltpu.CompilerParams(dimension_semantics=("arbitrary",), vmem_limit_bytes=VMEM_LIMIT),
        name="moe_combine_ln3",
    )(n_chunks, chunk_go, route.reshape(T, LANE), h2.reshape(T, D), row2(ln3_g[0]), row2(ln3_b[0]), y_sorted)
    return out.reshape(B, S, D)
```

```python
import functools
import math

import jax
import jax.numpy as jnp
from jax import lax
from jax.experimental import pallas as pl
from jax.experimental.pallas import tpu as pltpu

D = 1024
POOL_WINDOWS = (2, 4, 8, 16)
POOL_W = 512
N_HEADS = 8
HEAD_DIM = 64
ATTN_W = 512
BLK = 256
N_BLK = 8
TOPK = 3
REL_BUCKETS = 32
REL_MAX_DIST = 128
MEM_HEADS = 4
MEM_HD = 128
MEM_W = 512
N_GROUPS = 4
EPG = 8
N_EXPERTS = 32
FF = 256
DN_ALPHA = 2.0 ** 0.25
LN_EPS = 1e-5

LANE = 128
TOK_TILE = 512
SEG_ALIGN = 16
COMPACT_ROWS = 2 * TOK_TILE + N_EXPERTS * SEG_ALIGN
MAX_CHUNKS = COMPACT_ROWS // SEG_ALIGN
FFN_TILE = 512
ATTN_PAIRS_PER_TRIP = 2
MOE_SUBS_PER_STEP = 2
MERGE_TILE = 1024
HALO = 16
NEG = -1e30
VMEM_LIMIT = 56 * 1024 * 1024

F32 = jnp.float32
BF16 = jnp.bfloat16

_NT = (((1,), (1,)), ((), ()))


def _dot(a, b):
    return jnp.dot(a, b, preferred_element_type=F32)


def _dot_nt(a, b):
    return lax.dot_general(a, b, _NT, preferred_element_type=F32)


def _split(a):
    hi = a.astype(BF16)
    lo = (a - hi.astype(F32)).astype(BF16)
    return hi, lo


def _interleave(chains):
    results = [None] * len(chains)
    live = list(range(len(chains)))
    while live:
        for ci in list(live):
            try:
                next(chains[ci])
            except StopIteration as done:
                results[ci] = done.value
                live.remove(ci)
    return results


def _ln(x, g, b):
    mu = jnp.mean(x, axis=-1, keepdims=True)
    xc = x - mu
    var = jnp.mean(xc * xc, axis=-1, keepdims=True)
    return xc * lax.rsqrt(var + LN_EPS) * g + b


def _proj_kernel(x_ref, g_ref, b_ref, w_ref, wgrp_ref, pscale_ref,
                 ypool_ref, q_ref, k_ref, v_ref, ubuf, kbt, *, tm):
    s = pl.program_id(1)

    @pl.when(s == 0)
    def _():
        ubuf[0:HALO, :] = jnp.zeros((HALO, POOL_W), F32)
        kbt[...] = jnp.zeros_like(kbt)

    _interleave([_proj_chain(ci, s, tm, x_ref, g_ref, b_ref, w_ref, wgrp_ref, pscale_ref,
                             ypool_ref, q_ref, k_ref, v_ref, ubuf, kbt)
                 for ci in range(tm // TOK_TILE)])
    ubuf[0:HALO, :] = ubuf[tm:tm + HALO, :]


def _proj_chain(ci, s, tm, x_ref, g_ref, b_ref, w_ref, wgrp_ref, pscale_ref,
                ypool_ref, q_ref, k_ref, v_ref, ubuf, kbt):
    n = TOK_TILE
    r0 = ci * n
    rows = slice(r0, r0 + n)
    blk0 = s * (tm // BLK) + ci * (n // BLK)

    h = _ln(x_ref[0, rows, :], g_ref[...], b_ref[...])
    hb = h.astype(BF16)
    zu = _dot(hb, w_ref[:, 0:POOL_W])
    zq = _dot(hb, w_ref[:, POOL_W:POOL_W + ATTN_W])
    zk = _dot(hb, w_ref[:, POOL_W + ATTN_W:POOL_W + 2 * ATTN_W])
    zv = _dot(hb, w_ref[:, POOL_W + 2 * ATTN_W:])
    ubuf[HALO + r0:HALO + r0 + n, :] = zu

    r_io = lax.broadcasted_iota(jnp.int32, kbt.shape, 0)
    c_io = lax.broadcasted_iota(jnp.int32, kbt.shape, 1)
    head_match = (r_io >> 3) == (c_io >> 6)
    for bi in range(n // BLK):
        kmean = jnp.mean(zk[bi * BLK:(bi + 1) * BLK], axis=0, keepdims=True)
        kbt[...] = jnp.where(head_match & ((r_io & 7) == blk0 + bi), kmean, kbt[...])
    yield

    t_pos = s * tm + r0 + lax.broadcasted_iota(jnp.int32, (n, LANE), 0)
    for g, w in enumerate(POOL_WINDOWS):
        cols = slice(g * LANE, (g + 1) * LANE)
        ws = ubuf[HALO + r0:HALO + r0 + n, cols]
        for kk in range(1, w):
            ws = ws + ubuf[HALO + r0 - kk:HALO + r0 - kk + n, cols]
        cnt = jnp.minimum(t_pos + 1, w).astype(F32)
        y = ws / cnt - ubuf[HALO + r0:HALO + r0 + n, cols]
        yg = _dot(y.astype(BF16), wgrp_ref[g]) * pscale_ref[:, cols]
        ypool_ref[0, rows, cols] = yg.astype(ypool_ref.dtype)
    yield

    q_hi, q_lo = _split(zq)
    kb_hi, kb_lo = _split(kbt[...])
    g2 = _dot_nt(q_hi, jnp.concatenate([kb_hi, kb_lo], axis=0))
    gate = g2[:, 0:LANE] + g2[:, LANE:2 * LANE] + _dot_nt(q_lo, kb_hi)
    yield

    lane = lax.broadcasted_iota(jnp.int32, (n, LANE), 1)
    row = lax.broadcasted_iota(jnp.int32, (n, LANE), 0)
    n_l = lane & 7
    jrow = blk0 + (row >> 8)
    past = n_l < jrow
    gt = jnp.where(past, gate, -jnp.inf)
    cnt = jnp.zeros((n, LANE), F32)
    for sh in range(1, N_BLK):
        wrap = (n_l + sh) >= N_BLK
        gm = jnp.where(wrap, pltpu.roll(gt, N_BLK - sh, 1), pltpu.roll(gt, LANE - sh, 1))
        cnt = cnt + jnp.where(wrap, jnp.where(gm >= gt, 1.0, 0.0), jnp.where(gm > gt, 1.0, 0.0))
    keep = (past & (cnt < TOPK)) | (n_l == jrow)
    negmask = jnp.where(keep, 0.0, NEG)

    aug_lane = (lane >= HEAD_DIM) & (lane < HEAD_DIM + N_BLK)
    k_onehot = jnp.where(lane == HEAD_DIM + jrow, 1.0, 0.0)
    head_lane = lane < HEAD_DIM
    for hh in range(N_HEADS):
        cols = slice((hh // 2) * LANE, (hh // 2 + 1) * LANE)
        q_h, k_h = zq[:, cols], zk[:, cols]
        if hh % 2:
            q_h, k_h = pltpu.roll(q_h, HEAD_DIM, 1), pltpu.roll(k_h, HEAD_DIM, 1)
        m_h = jnp.where(aug_lane, pltpu.roll(negmask, HEAD_DIM - N_BLK * hh, 1), 0.0)
        q_ref[0, hh, rows, :] = jnp.where(head_lane, q_h, m_h).astype(q_ref.dtype)
        k_ref[0, hh, rows, :] = jnp.where(head_lane, k_h, k_onehot).astype(k_ref.dtype)
    for p in range(ATTN_W // LANE):
        v_ref[0, p, rows, :] = zv[:, p * LANE:(p + 1) * LANE].astype(v_ref.dtype)


def _attn_kernel(q_ref, k_ref, v_ref, tb_ref, o_ref):
    j = pl.program_id(1)
    lane = lax.broadcasted_iota(jnp.int32, (BLK, LANE), 1)

    def one_head(h, p, jj):
        q = q_ref[0, h]
        own0 = jj * BLK
        pieces = []
        s_own = _dot_nt(q, k_ref[0, h, own0:own0 + BLK, :]) + tb_ref[0, h]
        pieces.append((s_own, own0, BLK))
        if jj >= 1:
            s_adj = _dot_nt(q, k_ref[0, h, own0 - BLK:own0, :]) + tb_ref[1, h]
            pieces.append((s_adj, own0 - BLK, BLK))
        if jj >= 2:
            s_far = _dot_nt(q, k_ref[0, h, 0:own0 - BLK, :])
            pieces.append((s_far, 0, own0 - BLK))
        yield
        m = None
        for sc, _, _ in pieces:
            mm = jnp.max(sc, axis=1, keepdims=True)
            m = mm if m is None else jnp.maximum(m, mm)
        l = None
        probs = []
        for sc, start, size in pieces:
            e = jnp.exp(sc - m)
            ls = jnp.sum(e, axis=1, keepdims=True)
            l = ls if l is None else l + ls
            probs.append((e.astype(BF16), start, size))
        yield
        acc = None
        for pb, start, size in probs:
            pv = _dot(pb, v_ref[0, p, start:start + size, :])
            acc = pv if acc is None else acc + pv
        return acc / l

    for jj in range(N_BLK):
        @pl.when(j == jj)
        def _(jj=jj):
            def group(gi, carry):
                pairs = [gi * ATTN_PAIRS_PER_TRIP + pi for pi in range(ATTN_PAIRS_PER_TRIP)]
                outs = _interleave([one_head(2 * p + hh, p, jj) for p in pairs for hh in range(2)])
                for pi, p in enumerate(pairs):
                    o_ref[0, p] = jnp.where(lane < HEAD_DIM, outs[2 * pi], outs[2 * pi + 1]).astype(o_ref.dtype)
                return carry
            lax.fori_loop(0, N_HEADS // 2 // ATTN_PAIRS_PER_TRIP, group, 0)


def _memkv_kernel(mem_ref, wk_ref, wv_ref, k_ref, v_ref):
    mb = mem_ref[0].astype(BF16)
    k_ref[0] = _dot(mb, wk_ref[...]).astype(k_ref.dtype)
    v_ref[0] = _dot(mb, wv_ref[...]).astype(v_ref.dtype)


def _merge_kernel(x_ref, ypool_ref, o_ref, kmem_ref, vmem_ref,
                  lng_ref, lnb_ref, wgl_ref, bgate_ref, wpu_ref, wau_ref, wout_ref,
                  ln1g_ref, ln1b_ref, wmq_ref, wmo_ref, ln2g_ref, ln2b_ref,
                  wrh_ref, br_ref,
                  h2_ref, h2b_ref, comb_ref, route_ref, route_t_ref, cnt_ref, *, tm):
    _interleave([_merge_chain(ci, x_ref, ypool_ref, o_ref, kmem_ref, vmem_ref,
                              lng_ref, lnb_ref, wgl_ref, bgate_ref, wpu_ref, wau_ref, wout_ref,
                              ln1g_ref, ln1b_ref, wmq_ref, wmo_ref, ln2g_ref, ln2b_ref,
                              wrh_ref, br_ref, h2_ref, h2b_ref, comb_ref, route_ref, route_t_ref, cnt_ref)
                 for ci in range(tm // TOK_TILE)])


def _merge_chain(ci, x_ref, ypool_ref, o_ref, kmem_ref, vmem_ref,
                 lng_ref, lnb_ref, wgl_ref, bgate_ref, wpu_ref, wau_ref, wout_ref,
                 ln1g_ref, ln1b_ref, wmq_ref, wmo_ref, ln2g_ref, ln2b_ref,
                 wrh_ref, br_ref, h2_ref, h2b_ref, comb_ref, route_ref, route_t_ref, cnt_ref):
    rows = slice(ci * TOK_TILE, (ci + 1) * TOK_TILE)
    n_rows = TOK_TILE
    h = _ln(x_ref[0, rows, :], lng_ref[...], lnb_ref[...])
    hb = h.astype(BF16)
    gl = _dot(hb, wgl_ref[...]) + bgate_ref[...]
    yield
    gates = 0.5 * jnp.tanh(0.5 * gl) + 0.5
    y_pool = _dot(ypool_ref[0, rows, :], wpu_ref[...])
    o_cat = jnp.concatenate([o_ref[0, p, rows, :] for p in range(ATTN_W // LANE)], axis=1)
    y_attn = _dot(o_cat, wau_ref[...])
    yield
    merged = gates[:, 0:D] * y_pool + gates[:, D:2 * D] * y_attn
    mix = _dot(merged.astype(BF16), wout_ref[...])
    yield
    h1 = _ln(DN_ALPHA * h + mix, ln1g_ref[...], ln1b_ref[...])

    qm = _dot(h1.astype(BF16), wmq_ref[...]).astype(BF16)
    yield
    outs = []
    for hd in range(MEM_HEADS):
        cols = slice(hd * MEM_HD, (hd + 1) * MEM_HD)
        sc = _dot_nt(qm[:, cols], kmem_ref[0, :, cols])
        m = jnp.max(sc, axis=1, keepdims=True)
        e = jnp.exp(sc - m)
        l = jnp.sum(e, axis=1, keepdims=True)
        outs.append(_dot(e.astype(BF16), vmem_ref[0, :, cols]) / l)
    om = jnp.concatenate(outs, axis=1).astype(BF16)
    xa = _dot(om, wmo_ref[...])
    yield
    h2 = _ln(DN_ALPHA * h1 + xa, ln2g_ref[...], ln2b_ref[...])
    h2_ref[0, rows, :] = h2
    h2b_ref[0, rows, :] = h2.astype(BF16)

    x_hi, x_lo = _split(h2)
    r2 = _dot(x_hi, wrh_ref[...])
    r = r2[:, 0:LANE] + r2[:, LANE:2 * LANE] + _dot(x_lo, wrh_ref[:, 0:LANE]) + br_ref[...]
    yield
    lane = lax.broadcasted_iota(jnp.int32, (n_rows, LANE), 1)
    lane_f = lane.astype(F32)
    cmask = (lane >= N_EXPERTS) & (lane < 2 * N_EXPERTS)
    c = jnp.where(cmask, r, -jnp.inf)
    cmax = jnp.max(c, axis=1, keepdims=True)
    ce = jnp.exp(c - cmax)
    csum = jnp.sum(ce, axis=1, keepdims=True) * (1.0 / EPG)
    g_prob = 1.0 / csum
    grp_lane = ((lane & (N_EXPERTS - 1)) >> 3).astype(F32)
    gidx = jnp.min(jnp.where(cmask & (c == cmax), grp_lane, 99.0), axis=1, keepdims=True)
    fmask = (lane < N_EXPERTS) & (grp_lane == gidx)
    f = jnp.where(fmask, r, -jnp.inf)
    fmax = jnp.max(f, axis=1, keepdims=True)
    fe = jnp.exp(f - fmax)
    fsum = jnp.sum(fe, axis=1, keepdims=True)
    prob = fe / fsum
    p1 = jnp.max(prob, axis=1, keepdims=True)
    i1 = jnp.min(jnp.where(fmask & (prob == p1), lane_f, 999.0), axis=1, keepdims=True)
    rest = fmask & (lane_f != i1)
    prob2 = jnp.where(rest, prob, -1.0)
    p2 = jnp.max(prob2, axis=1, keepdims=True)
    i2 = jnp.min(jnp.where(rest & (prob2 == p2), lane_f, 999.0), axis=1, keepdims=True)
    den = p1 + p2
    comb = jnp.where(lane_f == i1, g_prob * (p1 / den),
                     jnp.where(lane_f == i2, g_prob * (p2 / den), 0.0))
    comb_ref[0, rows, :] = comb
    yield

    sel = jnp.where((lane_f == i1) | (lane_f == i2), 1.0, 0.0)
    cnt = jnp.sum(sel, axis=0, keepdims=True)
    pc = jnp.floor((cnt + (SEG_ALIGN - 1)) * (1.0 / SEG_ALIGN)) * SEG_ALIGN
    lane8 = lax.broadcasted_iota(jnp.int32, (8, LANE), 1)
    inc = jnp.broadcast_to(pc, (8, LANE))
    for sh in (1, 2, 4, 8, 16, 32, 64):
        inc = inc + jnp.where(lane8 >= sh, pltpu.roll(inc, sh, 1), 0.0)
    seg_start = inc[0:1] - pc
    t_row = lax.broadcasted_iota(jnp.int32, (n_rows, n_rows), 0)
    t_col = lax.broadcasted_iota(jnp.int32, (n_rows, n_rows), 1)
    earlier = jnp.where(t_row > t_col, 1.0, 0.0).astype(BF16)
    rank = _dot(earlier, sel.astype(BF16))
    pos = seg_start + rank
    d1 = jnp.sum(jnp.where(lane_f == i1, pos, 0.0), axis=1, keepdims=True)
    d2 = jnp.sum(jnp.where(lane_f == i2, pos, 0.0), axis=1, keepdims=True)
    route = jnp.where(lane == 0, d1, jnp.where(lane == 1, d2, 0.0))
    route_ref[0, rows, :] = route
    r_hi, r_lo = _split(route)
    eye = jnp.where(lax.broadcasted_iota(jnp.int32, (8, LANE), 0) == lane8, 1.0, 0.0).astype(BF16)
    route_t_ref[ci] = _dot_nt(eye, r_hi) + _dot_nt(eye, r_lo)
    cnt_ref[ci] = jnp.broadcast_to(pc, (8, LANE))


def _chunk_copies(sub, nch_ref, go_ref, make_copy, act):
    def body(k, carry):
        go = pl.multiple_of(go_ref[sub * MAX_CHUNKS + k], SEG_ALIGN)
        act(make_copy(pl.multiple_of(k * SEG_ALIGN, SEG_ALIGN), go))
        return carry
    lax.fori_loop(0, nch_ref[sub], body, 0)


def _dispatch_kernel(nch_ref, go_ref, tail_ref, xb_ref, comb_ref, route_t_ref, xs_ref,
                     cbuf, zbuf, sem, zsem, *, n_steps, rows):
    i = pl.program_id(0)
    par = lax.rem(i, 2)
    n_u = MOE_SUBS_PER_STEP
    tm = TOK_TILE

    def copies(step, par_, act):
        for u in range(n_u):
            slot_ = par_ * n_u + u

            def mk(lo, go, slot_=slot_):
                return pltpu.make_async_copy(cbuf.at[slot_, pl.ds(lo, SEG_ALIGN), :],
                                             xs_ref.at[pl.ds(go, SEG_ALIGN), :], sem.at[slot_])
            _chunk_copies(step * n_u + u, nch_ref, go_ref, mk, act)

    @pl.when(i >= 2)
    def _():
        copies(i - 2, par, lambda c: c.wait())

    def chain(u):
        rs = slice(u * tm, (u + 1) * tm)
        r_io = lax.broadcasted_iota(jnp.int32, (rows, tm), 0).astype(F32)
        d1 = route_t_ref[u, 0:1, :]
        d2 = route_t_ref[u, 1:2, :]
        p_mat = jnp.where((r_io == d1) | (r_io == d2), 1.0, 0.0).astype(BF16)
        comb = comb_ref[rs, :]
        c_hi = comb.astype(BF16).astype(F32)
        c_pack = (c_hi + pltpu.roll(comb - c_hi, N_EXPERTS, 1)).astype(BF16)
        x_aug = jnp.concatenate([xb_ref[rs, :], c_pack], axis=1)
        yield
        cbuf[par * n_u + u] = _dot(p_mat, x_aug).astype(BF16)

    _interleave([chain(u) for u in range(n_u)])
    copies(i, par, lambda c: c.start())

    @pl.when(i == n_steps - 1)
    def _():
        if n_steps >= 2:
            copies(i - 1, 1 - par, lambda c: c.wait())
        copies(i, par, lambda c: c.wait())
        zbuf[...] = jnp.zeros_like(zbuf)

        def tails(act):
            def body(e, carry):
                st = tail_ref[e]
                n = tail_ref[N_EXPERTS + e]
                off = jnp.int32(0)
                size = zbuf.shape[0]
                while size >= SEG_ALIGN:
                    bit = (n & size) != 0

                    @pl.when(bit)
                    def _(size=size, off=off):
                        act(pltpu.make_async_copy(
                            zbuf.at[pl.ds(0, size), :],
                            xs_ref.at[pl.ds(pl.multiple_of(st + off, SEG_ALIGN), size), :], zsem))
                    off = off + jnp.where(bit, size, 0)
                    size //= 2
                return carry
            lax.fori_loop(0, N_EXPERTS, body, 0)
        tails(lambda c: c.start())
        tails(lambda c: c.wait())


def _ffn_kernel(texp_ref, nused_ref, xs_ref, wg_ref, wu_ref, wd_ref, ys_ref, wg_b, wu_b, wd_b):
    t = pl.program_id(0)
    e = texp_ref[t]

    @pl.when((t == 0) | (e != texp_ref[jnp.maximum(t - 1, 0)]))
    def _():
        wg_b[...] = wg_ref[0].astype(BF16)
        wu_b[...] = wu_ref[0].astype(BF16)
        wd_b[...] = wd_ref[0].astype(BF16)

    @pl.when(t < nused_ref[0])
    def _():
        xa = xs_ref[...]
        xrow = xa[:, 0:D]
        cw = xa[:, D:D + LANE].astype(F32)
        lane = lax.broadcasted_iota(jnp.int32, cw.shape, 1)
        c = jnp.sum(jnp.where((lane == e) | (lane == e + N_EXPERTS), cw, 0.0), axis=1, keepdims=True)
        a = _dot(xrow, wg_b[...])
        b = _dot(xrow, wu_b[...])
        hid = (a * jax.nn.sigmoid(a)) * b * c
        ys_ref[...] = _dot(hid.astype(BF16), wd_b[...]).astype(ys_ref.dtype)


def _combine_kernel(nch_ref, go_ref, route_ref, h2_ref, g_ref, b_ref, ys_ref, out_ref,
                    ybuf, sem, *, n_steps, rows):
    i = pl.program_id(0)
    par = lax.rem(i, 2)
    n_u = MOE_SUBS_PER_STEP
    tm = TOK_TILE

    def copies(step, par_, act):
        for u in range(n_u):
            slot_ = par_ * n_u + u

            def mk(lo, go, slot_=slot_):
                return pltpu.make_async_copy(ys_ref.at[pl.ds(go, SEG_ALIGN), :],
                                             ybuf.at[slot_, pl.ds(lo, SEG_ALIGN), :], sem.at[slot_])
            _chunk_copies(step * n_u + u, nch_ref, go_ref, mk, act)

    @pl.when(i == 0)
    def _():
        ybuf[...] = jnp.zeros_like(ybuf)
        copies(0, 0, lambda c: c.start())

    @pl.when(i + 1 < n_steps)
    def _():
        copies(i + 1, 1 - par, lambda c: c.start())

    copies(i, par, lambda c: c.wait())

    def chain(u):
        rs = slice(u * tm, (u + 1) * tm)
        r_io = lax.broadcasted_iota(jnp.int32, (tm, rows), 1).astype(F32)
        d1 = route_ref[rs, 0:1]
        d2 = route_ref[rs, 1:2]
        p_t = jnp.where((r_io == d1) | (r_io == d2), 1.0, 0.0).astype(BF16)
        yield
        ff = _dot(p_t, ybuf[par * n_u + u])
        yield
        out_ref[rs, :] = _ln(DN_ALPHA * h2_ref[rs, :] + ff, g_ref[...], b_ref[...])

    _interleave([chain(u) for u in range(n_u)])


def _bias_kernel(tbl_ref, bkt_ref, out_ref):
    h = pl.program_id(0)
    far = tbl_ref[h, REL_BUCKETS - 1]
    for which in range(2):
        bk = bkt_ref[which]
        acc = jnp.where(bk < 0, NEG, 0.0)
        for kk in range(REL_BUCKETS):
            acc = jnp.where(bk == kk, tbl_ref[h, kk] - far, acc)
        out_ref[which, 0] = acc


def _rel_bucket_table(dist):
    max_exact = REL_BUCKETS // 2
    d = jnp.maximum(dist, 0)
    large = max_exact + (jnp.log(jnp.maximum(d, 1).astype(F32) / max_exact)
                         / math.log(REL_MAX_DIST / max_exact) * (REL_BUCKETS - max_exact)).astype(jnp.int32)
    large = jnp.minimum(large, REL_BUCKETS - 1)
    return jnp.where(d < max_exact, d, large)


def _const_spec(shape):
    nd = len(shape)
    return pl.BlockSpec(shape, lambda *_: (0,) * nd)


def kernel(x, mem, ln_in_g, ln_in_b, rel_bias, w_in, b_gate, w_pool_grp, pool_scale, w_pool_up, w_attn_up,
           w_mix_out, ln1_g, ln1_b, w_mq, w_mk, w_mv, w_mo, ln2_g, ln2_b, w_coarse, b_coarse, w_fine, b_fine,
           w_gate, w_up, w_down, ln3_g, ln3_b):
    B, S, _ = x.shape
    assert S == N_BLK * BLK and w_in.shape[0] == 1
    M = mem.shape[1]
    T = B * S
    tm = 512

    wi = w_in[0]
    w_u = wi[:, 0:POOL_W]
    w_q = wi[:, POOL_W:POOL_W + ATTN_W] * (HEAD_DIM ** -0.5)
    w_k = wi[:, POOL_W + ATTN_W:POOL_W + 2 * ATTN_W]
    w_v = wi[:, POOL_W + 2 * ATTN_W:POOL_W + 3 * ATTN_W]
    w_gl = wi[:, POOL_W + 3 * ATTN_W:]

    w1 = jnp.concatenate([w_u, w_q, w_k, w_v], axis=1).astype(BF16)
    row2 = lambda a: a.reshape(1, -1)

    iq = jnp.arange(BLK, dtype=jnp.int32)[:, None]
    ik = jnp.arange(BLK, dtype=jnp.int32)[None, :]
    d_own = iq - ik
    bkt = jnp.stack([jnp.where(d_own >= 0, _rel_bucket_table(d_own), -1), _rel_bucket_table(d_own + BLK)])
    t_bias = pl.pallas_call(
        _bias_kernel,
        grid=(N_HEADS,),
        in_specs=[pl.BlockSpec(memory_space=pltpu.SMEM), _const_spec((2, BLK, BLK))],
        out_specs=pl.BlockSpec((2, 1, BLK, BLK), lambda h: (0, h, 0, 0)),
        out_shape=jax.ShapeDtypeStruct((2, N_HEADS, BLK, BLK), F32),
        name="relbias_tiles",
    )(rel_bias.T, bkt)

    n_w1 = w1.shape[1]
    tmp = MERGE_TILE
    ypool, q_aug, k_aug, v_p = pl.pallas_call(
        functools.partial(_proj_kernel, tm=tmp),
        grid=(B, S // tmp),
        in_specs=[
            pl.BlockSpec((1, tmp, D), lambda b, s: (b, s, 0)),
            _const_spec((1, D)), _const_spec((1, D)),
            _const_spec((D, n_w1)),
            _const_spec((len(POOL_WINDOWS), LANE, LANE)),
            _const_spec((1, POOL_W)),
        ],
        out_specs=[
            pl.BlockSpec((1, tmp, POOL_W), lambda b, s: (b, s, 0)),
            pl.BlockSpec((1, N_HEADS, tmp, LANE), lambda b, s: (b, 0, s, 0)),
            pl.BlockSpec((1, N_HEADS, tmp, LANE), lambda b, s: (b, 0, s, 0)),
            pl.BlockSpec((1, ATTN_W // LANE, tmp, LANE), lambda b, s: (b, 0, s, 0)),
        ],
        out_shape=[
            jax.ShapeDtypeStruct((B, S, POOL_W), BF16),
            jax.ShapeDtypeStruct((B, N_HEADS, S, LANE), BF16),
            jax.ShapeDtypeStruct((B, N_HEADS, S, LANE), BF16),
            jax.ShapeDtypeStruct((B, ATTN_W // LANE, S, LANE), BF16),
        ],
        scratch_shapes=[pltpu.VMEM((HALO + tmp, POOL_W), F32), pltpu.VMEM((LANE, ATTN_W), F32)],
        compiler_params=pltpu.CompilerParams(dimension_semantics=("arbitrary", "arbitrary"),
                                             vmem_limit_bytes=VMEM_LIMIT),
        name="proj_pool_gate",
    )(x, row2(ln_in_g), row2(ln_in_b), w1, w_pool_grp[0].astype(BF16), row2(pool_scale[0]))

    o_attn = pl.pallas_call(
        _attn_kernel,
        grid=(B, N_BLK),
        in_specs=[
            pl.BlockSpec((1, N_HEADS, BLK, LANE), lambda b, j: (b, 0, j, 0)),
            pl.BlockSpec((1, N_HEADS, S, LANE), lambda b, j: (b, 0, 0, 0)),
            pl.BlockSpec((1, ATTN_W // LANE, S, LANE), lambda b, j: (b, 0, 0, 0)),
            _const_spec((2, N_HEADS, BLK, BLK)),
        ],
        out_specs=pl.BlockSpec((1, ATTN_W // LANE, BLK, LANE), lambda b, j: (b, 0, j, 0)),
        out_shape=jax.ShapeDtypeStruct((B, ATTN_W // LANE, S, LANE), BF16),
        compiler_params=pltpu.CompilerParams(dimension_semantics=("arbitrary", "arbitrary"),
                                             vmem_limit_bytes=VMEM_LIMIT),
        name="moba_attn",
    )(q_aug, k_aug, v_p, t_bias)

    kmem, vmem = pl.pallas_call(
        _memkv_kernel,
        grid=(B,),
        in_specs=[pl.BlockSpec((1, M, D), lambda b: (b, 0, 0)),
                  _const_spec((D, MEM_W)), _const_spec((D, MEM_W))],
        out_specs=[pl.BlockSpec((1, M, MEM_W), lambda b: (b, 0, 0)),
                   pl.BlockSpec((1, M, MEM_W), lambda b: (b, 0, 0))],
        out_shape=[jax.ShapeDtypeStruct((B, M, MEM_W), BF16)] * 2,
        compiler_params=pltpu.CompilerParams(dimension_semantics=("arbitrary",)),
        name="mem_kv",
    )(mem, w_mk[0].astype(BF16), w_mv[0].astype(BF16))

    w_r = jnp.concatenate([
        w_fine[0].reshape(D, N_EXPERTS),
        jnp.repeat(w_coarse[0], EPG, axis=1),
        jnp.zeros((D, LANE - 2 * N_EXPERTS), F32)], axis=1)
    b_r = jnp.concatenate([
        b_fine[0].reshape(N_EXPERTS), jnp.repeat(b_coarse[0], EPG),
        jnp.zeros((LANE - 2 * N_EXPERTS,), F32)]).reshape(1, LANE)
    w_r_hi = w_r.astype(BF16)
    w_r_lo = (w_r - w_r_hi.astype(F32)).astype(BF16)
    w_r_cat = jnp.concatenate([w_r_hi, w_r_lo], axis=1)

    n_sub = T // tm
    tmm = MERGE_TILE
    per = tmm // tm
    sub_idx = lambda b, s: b * (S // tmm) + s
    h2, h2b, comb, route, route_t, seg_cnt = pl.pallas_call(
        functools.partial(_merge_kernel, tm=tmm),
        grid=(B, S // tmm),
        in_specs=[
            pl.BlockSpec((1, tmm, D), lambda b, s: (b, s, 0)),
            pl.BlockSpec((1, tmm, POOL_W), lambda b, s: (b, s, 0)),
            pl.BlockSpec((1, ATTN_W // LANE, tmm, LANE), lambda b, s: (b, 0, s, 0)),
            pl.BlockSpec((1, M, MEM_W), lambda b, s: (b, 0, 0)),
            pl.BlockSpec((1, M, MEM_W), lambda b, s: (b, 0, 0)),
            _const_spec((1, D)), _const_spec((1, D)),
            _const_spec((D, 2 * D)), _const_spec((1, 2 * D)),
            _const_spec((POOL_W, D)), _const_spec((ATTN_W, D)), _const_spec((D, D)),
            _const_spec((1, D)), _const_spec((1, D)),
            _const_spec((D, MEM_W)), _const_spec((MEM_W, D)),
            _const_spec((1, D)), _const_spec((1, D)),
            _const_spec((D, 2 * LANE)), _const_spec((1, LANE)),
        ],
        out_specs=[
            pl.BlockSpec((1, tmm, D), lambda b, s: (b, s, 0)),
            pl.BlockSpec((1, tmm, D), lambda b, s: (b, s, 0)),
            pl.BlockSpec((1, tmm, LANE), lambda b, s: (b, s, 0)),
            pl.BlockSpec((1, tmm, LANE), lambda b, s: (b, s, 0)),
            pl.BlockSpec((per, 8, tm), lambda b, s: (sub_idx(b, s), 0, 0)),
            pl.BlockSpec((per, 8, LANE), lambda b, s: (sub_idx(b, s), 0, 0)),
        ],
        out_shape=[
            jax.ShapeDtypeStruct((B, S, D), F32),
            jax.ShapeDtypeStruct((B, S, D), BF16),
            jax.ShapeDtypeStruct((B, S, LANE), F32),
            jax.ShapeDtypeStruct((B, S, LANE), F32),
            jax.ShapeDtypeStruct((n_sub, 8, tm), F32),
            jax.ShapeDtypeStruct((n_sub, 8, LANE), F32),
        ],
        compiler_params=pltpu.CompilerParams(dimension_semantics=("arbitrary", "arbitrary"),
                                             vmem_limit_bytes=VMEM_LIMIT),
        name="merge_memattn_router",
    )(x, ypool, o_attn, kmem, vmem,
      row2(ln_in_g), row2(ln_in_b), w_gl.astype(BF16), row2(b_gate[0]),
      w_pool_up[0].astype(BF16), w_attn_up[0].astype(BF16), w_mix_out[0].astype(BF16),
      row2(ln1_g[0]), row2(ln1_b[0]),
      (w_mq[0] * (MEM_HD ** -0.5)).astype(BF16), w_mo[0].astype(BF16),
      row2(ln2_g[0]), row2(ln2_b[0]),
      w_r_cat, b_r)

    pcs = seg_cnt[:, 0, :N_EXPERTS].astype(jnp.int32)
    tot = jnp.sum(pcs, axis=0)
    cap = ((tot + FFN_TILE - 1) // FFN_TILE) * FFN_TILE
    ends = jnp.cumsum(cap)
    base = ends - cap
    gs = base[None, :] + jnp.cumsum(pcs, axis=0) - pcs
    ls = jnp.cumsum(pcs, axis=1) - pcs
    n_sorted = n_sub * COMPACT_ROWS + N_EXPERTS * FFN_TILE
    n_ffn_tiles = n_sorted // FFN_TILE
    n_used = (ends[-1] // FFN_TILE).astype(jnp.int32)
    tile_row = jnp.arange(n_ffn_tiles, dtype=jnp.int32) * FFN_TILE
    tile_exp = jnp.sum(jnp.minimum(tile_row, ends[-1] - 1)[:, None] >= ends[None, :], axis=1).astype(jnp.int32)
    tails = jnp.concatenate([base + tot, cap - tot]).astype(jnp.int32)
    chunk_row = jnp.arange(MAX_CHUNKS, dtype=jnp.int32) * SEG_ALIGN
    chunk_exp = jnp.sum(chunk_row[None, :, None] >= (ls + pcs)[:, None, :], axis=2)
    shift = jnp.sum(jnp.where(chunk_exp[:, :, None] == jnp.arange(N_EXPERTS)[None, None, :],
                              (gs - ls)[:, None, :], 0), axis=2)
    chunk_go = (chunk_row[None, :] + shift).astype(jnp.int32).reshape(-1)
    n_chunks = (jnp.sum(pcs, axis=1) // SEG_ALIGN).astype(jnp.int32)

    aug_w = D + LANE
    n_u = MOE_SUBS_PER_STEP
    n_steps = n_sub // n_u
    x_sorted = pl.pallas_call(
        functools.partial(_dispatch_kernel, n_steps=n_steps, rows=COMPACT_ROWS),
        grid_spec=pltpu.PrefetchScalarGridSpec(
            num_scalar_prefetch=3,
            grid=(n_steps,),
            in_specs=[
                pl.BlockSpec((n_u * tm, D), lambda i, *_: (i, 0)),
                pl.BlockSpec((n_u * tm, LANE), lambda i, *_: (i, 0)),
                pl.BlockSpec((n_u, 8, tm), lambda i, *_: (i, 0, 0)),
            ],
            out_specs=pl.BlockSpec(memory_space=pl.ANY),
            scratch_shapes=[
                pltpu.VMEM((2 * n_u, COMPACT_ROWS, aug_w), BF16),
                pltpu.VMEM((FFN_TILE // 2, aug_w), BF16),
                pltpu.SemaphoreType.DMA((2 * n_u,)),
                pltpu.SemaphoreType.DMA(()),
            ],
        ),
        out_shape=jax.ShapeDtypeStruct((n_sorted, aug_w), BF16),
        compiler_params=pltpu.CompilerParams(dimension_semantics=("arbitrary",), vmem_limit_bytes=VMEM_LIMIT),
        name="moe_dispatch",
    )(n_chunks, chunk_go, tails, h2b.reshape(T, D), comb.reshape(T, LANE), route_t)

    used_tile = lambda t, texp, nused: (jnp.minimum(t, nused[0] - 1), 0)
    y_sorted = pl.pallas_call(
        _ffn_kernel,
        grid_spec=pltpu.PrefetchScalarGridSpec(
            num_scalar_prefetch=2,
            grid=(n_ffn_tiles,),
            in_specs=[
                pl.BlockSpec((FFN_TILE, aug_w), used_tile),
                pl.BlockSpec((1, D, FF), lambda t, texp, nused: (texp[t], 0, 0)),
                pl.BlockSpec((1, D, FF), lambda t, texp, nused: (texp[t], 0, 0)),
                pl.BlockSpec((1, FF, D), lambda t, texp, nused: (texp[t], 0, 0)),
            ],
            out_specs=pl.BlockSpec((FFN_TILE, D), used_tile),
            scratch_shapes=[pltpu.VMEM((D, FF), BF16), pltpu.VMEM((D, FF), BF16), pltpu.VMEM((FF, D), BF16)],
        ),
        out_shape=jax.ShapeDtypeStruct((n_sorted, D), BF16),
        compiler_params=pltpu.CompilerParams(dimension_semantics=("arbitrary",), vmem_limit_bytes=VMEM_LIMIT),
        name="moe_expert_ffn",
    )(tile_exp, n_used.reshape(1), x_sorted, w_gate[0], w_up[0], w_down[0])

    out = pl.pallas_call(
        functools.partial(_combine_kernel, n_steps=n_steps, rows=COMPACT_ROWS),
        grid_spec=pltpu.PrefetchScalarGridSpec(
            num_scalar_prefetch=2,
            grid=(n_steps,),
            in_specs=[
                pl.BlockSpec((n_u * tm, LANE), lambda i, *_: (i, 0)),
                pl.BlockSpec((n_u * tm, D), lambda i, *_: (i, 0)),
                pl.BlockSpec((1, D), lambda i, *_: (0, 0)),
                pl.BlockSpec((1, D), lambda i, *_: (0, 0)),
                pl.BlockSpec(memory_space=pl.ANY),
            ],
            out_specs=pl.BlockSpec((n_u * tm, D), lambda i, *_: (i, 0)),
            scratch_shapes=[
                pltpu.VMEM((2 * n_u, COMPACT_ROWS, D), BF16),
                pltpu.SemaphoreType.DMA((2 * n_u,)),
            ],
        ),
        out_shape=jax.ShapeDtypeStruct((T, D), F32),
        compiler_params=pltpu.CompilerParams(dimension_semantics=("arbitrary",), vmem_limit_bytes=VMEM_LIMIT),
        name="moe_combine_ln3",
    )(n_chunks, chunk_go, route.reshape(T, LANE), h2.reshape(T, D), row2(ln3_g[0]), row2(ln3_b[0]), y_sorted)
    return out.reshape(B, S, D)
```

```python
import functools
import math

import jax
import jax.numpy as jnp
from jax import lax
from jax.experimental import pallas as pl
from jax.experimental.pallas import tpu as pltpu

D = 1024
POOL_WINDOWS = (2, 4, 8, 16)
POOL_W = 512
N_HEADS = 8
HEAD_DIM = 64
ATTN_W = 512
BLK = 256
N_BLK = 8
TOPK = 3
REL_BUCKETS = 32
REL_MAX_DIST = 128
MEM_HEADS = 4
MEM_HD = 128
MEM_W = 512
N_GROUPS = 4
EPG = 8
N_EXPERTS = 32
FF = 256
DN_ALPHA = 2.0 ** 0.25
LN_EPS = 1e-5

LANE = 128
TOK_TILE = 512
SEG_ALIGN = 16
COMPACT_ROWS = 2 * TOK_TILE + N_EXPERTS * SEG_ALIGN
MAX_CHUNKS = COMPACT_ROWS // SEG_ALIGN
FFN_TILE = 512
ATTN_PAIRS_PER_TRIP = 2
MOE_SUBS_PER_STEP = 2
XS_DEPTH = 3
MERGE_TILE = 1024
HALO = 16
NEG = -1e30
VMEM_LIMIT = 56 * 1024 * 1024

F32 = jnp.float32
BF16 = jnp.bfloat16

_NT = (((1,), (1,)), ((), ()))


def _dot(a, b):
    return jnp.dot(a, b, preferred_element_type=F32)


def _dot_nt(a, b):
    return lax.dot_general(a, b, _NT, preferred_element_type=F32)


def _split(a):
    hi = a.astype(BF16)
    lo = (a - hi.astype(F32)).astype(BF16)
    return hi, lo


def _interleave(chains):
    results = [None] * len(chains)
    live = list(range(len(chains)))
    while live:
        for ci in list(live):
            try:
                next(chains[ci])
            except StopIteration as done:
                results[ci] = done.value
                live.remove(ci)
    return results


def _ln(x, g, b):
    mu = jnp.mean(x, axis=-1, keepdims=True)
    xc = x - mu
    var = jnp.mean(xc * xc, axis=-1, keepdims=True)
    return xc * lax.rsqrt(var + LN_EPS) * g + b


def _proj_kernel(x_ref, g_ref, b_ref, w_ref, wgrp_ref, pscale_ref,
                 ypool_ref, q_ref, k_ref, v_ref, ubuf, kbt, *, tm):
    s = pl.program_id(1)

    @pl.when(s == 0)
    def _():
        ubuf[0:HALO, :] = jnp.zeros((HALO, POOL_W), F32)
        kbt[...] = jnp.zeros_like(kbt)

    _interleave([_proj_chain(ci, s, tm, x_ref, g_ref, b_ref, w_ref, wgrp_ref, pscale_ref,
                             ypool_ref, q_ref, k_ref, v_ref, ubuf, kbt)
                 for ci in range(tm // TOK_TILE)])
    ubuf[0:HALO, :] = ubuf[tm:tm + HALO, :]


def _proj_chain(ci, s, tm, x_ref, g_ref, b_ref, w_ref, wgrp_ref, pscale_ref,
                ypool_ref, q_ref, k_ref, v_ref, ubuf, kbt):
    n = TOK_TILE
    r0 = ci * n
    rows = slice(r0, r0 + n)
    blk0 = s * (tm // BLK) + ci * (n // BLK)

    h = _ln(x_ref[0, rows, :], g_ref[...], b_ref[...])
    hb = h.astype(BF16)
    zu = _dot(hb, w_ref[:, 0:POOL_W])
    zq = _dot(hb, w_ref[:, POOL_W:POOL_W + ATTN_W])
    zk = _dot(hb, w_ref[:, POOL_W + ATTN_W:POOL_W + 2 * ATTN_W])
    zv = _dot(hb, w_ref[:, POOL_W + 2 * ATTN_W:])
    ubuf[HALO + r0:HALO + r0 + n, :] = zu

    r_io = lax.broadcasted_iota(jnp.int32, kbt.shape, 0)
    c_io = lax.broadcasted_iota(jnp.int32, kbt.shape, 1)
    head_match = (r_io >> 3) == (c_io >> 6)
    for bi in range(n // BLK):
        kmean = jnp.mean(zk[bi * BLK:(bi + 1) * BLK], axis=0, keepdims=True)
        kbt[...] = jnp.where(head_match & ((r_io & 7) == blk0 + bi), kmean, kbt[...])
    yield

    t_pos = s * tm + r0 + lax.broadcasted_iota(jnp.int32, (n, LANE), 0)
    for g, w in enumerate(POOL_WINDOWS):
        cols = slice(g * LANE, (g + 1) * LANE)
        ws = ubuf[HALO + r0:HALO + r0 + n, cols]
        for kk in range(1, w):
            ws = ws + ubuf[HALO + r0 - kk:HALO + r0 - kk + n, cols]
        cnt = jnp.minimum(t_pos + 1, w).astype(F32)
        y = ws / cnt - ubuf[HALO + r0:HALO + r0 + n, cols]
        yg = _dot(y.astype(BF16), wgrp_ref[g]) * pscale_ref[:, cols]
        ypool_ref[0, rows, cols] = yg.astype(ypool_ref.dtype)
    yield

    q_hi, q_lo = _split(zq)
    kb_hi, kb_lo = _split(kbt[...])
    g2 = _dot_nt(q_hi, jnp.concatenate([kb_hi, kb_lo], axis=0))
    gate = g2[:, 0:LANE] + g2[:, LANE:2 * LANE] + _dot_nt(q_lo, kb_hi)
    yield

    lane = lax.broadcasted_iota(jnp.int32, (n, LANE), 1)
    row = lax.broadcasted_iota(jnp.int32, (n, LANE), 0)
    n_l = lane & 7
    jrow = blk0 + (row >> 8)
    past = n_l < jrow
    gt = jnp.where(past, gate, -jnp.inf)
    cnt = jnp.zeros((n, LANE), F32)
    for sh in range(1, N_BLK):
        wrap = (n_l + sh) >= N_BLK
        gm = jnp.where(wrap, pltpu.roll(gt, N_BLK - sh, 1), pltpu.roll(gt, LANE - sh, 1))
        cnt = cnt + jnp.where(wrap, jnp.where(gm >= gt, 1.0, 0.0), jnp.where(gm > gt, 1.0, 0.0))
    keep = (past & (cnt < TOPK)) | (n_l == jrow)
    negmask = jnp.where(keep, 0.0, NEG)

    aug_lane = (lane >= HEAD_DIM) & (lane < HEAD_DIM + N_BLK)
    k_onehot = jnp.where(lane == HEAD_DIM + jrow, 1.0, 0.0)
    head_lane = lane < HEAD_DIM
    for hh in range(N_HEADS):
        cols = slice((hh // 2) * LANE, (hh // 2 + 1) * LANE)
        q_h, k_h = zq[:, cols], zk[:, cols]
        if hh % 2:
            q_h, k_h = pltpu.roll(q_h, HEAD_DIM, 1), pltpu.roll(k_h, HEAD_DIM, 1)
        m_h = jnp.where(aug_lane, pltpu.roll(negmask, HEAD_DIM - N_BLK * hh, 1), 0.0)
        q_ref[0, hh, rows, :] = jnp.where(head_lane, q_h, m_h).astype(q_ref.dtype)
        k_ref[0, hh, rows, :] = jnp.where(head_lane, k_h, k_onehot).astype(k_ref.dtype)
    for p in range(ATTN_W // LANE):
        v_ref[0, p, rows, :] = zv[:, p * LANE:(p + 1) * LANE].astype(v_ref.dtype)


def _attn_kernel(q_ref, k_ref, v_ref, tb_ref, o_ref):
    j = pl.program_id(1)
    lane = lax.broadcasted_iota(jnp.int32, (BLK, LANE), 1)

    def one_head(h, p, jj):
        q = q_ref[0, h]
        own0 = jj * BLK
        pieces = []
        s_own = _dot_nt(q, k_ref[0, h, own0:own0 + BLK, :]) + tb_ref[0, h]
        pieces.append((s_own, own0, BLK))
        if jj >= 1:
            s_adj = _dot_nt(q, k_ref[0, h, own0 - BLK:own0, :]) + tb_ref[1, h]
            pieces.append((s_adj, own0 - BLK, BLK))
        if jj >= 2:
            s_far = _dot_nt(q, k_ref[0, h, 0:own0 - BLK, :])
            pieces.append((s_far, 0, own0 - BLK))
        yield
        m = None
        for sc, _, _ in pieces:
            mm = jnp.max(sc, axis=1, keepdims=True)
            m = mm if m is None else jnp.maximum(m, mm)
        l = None
        probs = []
        for sc, start, size in pieces:
            e = jnp.exp(sc - m)
            ls = jnp.sum(e, axis=1, keepdims=True)
            l = ls if l is None else l + ls
            probs.append((e.astype(BF16), start, size))
        yield
        acc = None
        for pb, start, size in probs:
            pv = _dot(pb, v_ref[0, p, start:start + size, :])
            acc = pv if acc is None else acc + pv
        return acc / l

    for jj in range(N_BLK):
        @pl.when(j == jj)
        def _(jj=jj):
            def group(gi, carry):
                pairs = [gi * ATTN_PAIRS_PER_TRIP + pi for pi in range(ATTN_PAIRS_PER_TRIP)]
                outs = _interleave([one_head(2 * p + hh, p, jj) for p in pairs for hh in range(2)])
                for pi, p in enumerate(pairs):
                    o_ref[0, p] = jnp.where(lane < HEAD_DIM, outs[2 * pi], outs[2 * pi + 1]).astype(o_ref.dtype)
                return carry
            lax.fori_loop(0, N_HEADS // 2 // ATTN_PAIRS_PER_TRIP, group, 0)


def _memkv_kernel(mem_ref, wk_ref, wv_ref, k_ref, v_ref):
    mb = mem_ref[0].astype(BF16)
    k_ref[0] = _dot(mb, wk_ref[...]).astype(k_ref.dtype)
    v_ref[0] = _dot(mb, wv_ref[...]).astype(v_ref.dtype)


def _merge_kernel(x_ref, ypool_ref, o_ref, kmem_ref, vmem_ref,
                  lng_ref, lnb_ref, wgl_ref, bgate_ref, wpu_ref, wau_ref, wout_ref,
                  ln1g_ref, ln1b_ref, wmq_ref, wmo_ref, ln2g_ref, ln2b_ref,
                  wrh_ref, br_ref,
                  h2_ref, h2b_ref, comb_ref, route_ref, route_t_ref, cnt_ref, *, tm):
    _interleave([_merge_chain(ci, x_ref, ypool_ref, o_ref, kmem_ref, vmem_ref,
                              lng_ref, lnb_ref, wgl_ref, bgate_ref, wpu_ref, wau_ref, wout_ref,
                              ln1g_ref, ln1b_ref, wmq_ref, wmo_ref, ln2g_ref, ln2b_ref,
                              wrh_ref, br_ref, h2_ref, h2b_ref, comb_ref, route_ref, route_t_ref, cnt_ref)
                 for ci in range(tm // TOK_TILE)])


def _merge_chain(ci, x_ref, ypool_ref, o_ref, kmem_ref, vmem_ref,
                 lng_ref, lnb_ref, wgl_ref, bgate_ref, wpu_ref, wau_ref, wout_ref,
                 ln1g_ref, ln1b_ref, wmq_ref, wmo_ref, ln2g_ref, ln2b_ref,
                 wrh_ref, br_ref, h2_ref, h2b_ref, comb_ref, route_ref, route_t_ref, cnt_ref):
    rows = slice(ci * TOK_TILE, (ci + 1) * TOK_TILE)
    n_rows = TOK_TILE
    h = _ln(x_ref[0, rows, :], lng_ref[...], lnb_ref[...])
    hb = h.astype(BF16)
    gl = _dot(hb, wgl_ref[...]) + bgate_ref[...]
    yield
    gates = 0.5 * jnp.tanh(0.5 * gl) + 0.5
    y_pool = _dot(ypool_ref[0, rows, :], wpu_ref[...])
    o_cat = jnp.concatenate([o_ref[0, p, rows, :] for p in range(ATTN_W // LANE)], axis=1)
    y_attn = _dot(o_cat, wau_ref[...])
    yield
    merged = gates[:, 0:D] * y_pool + gates[:, D:2 * D] * y_attn
    mix = _dot(merged.astype(BF16), wout_ref[...])
    yield
    h1 = _ln(DN_ALPHA * h + mix, ln1g_ref[...], ln1b_ref[...])

    qm = _dot(h1.astype(BF16), wmq_ref[...]).astype(BF16)
    yield
    outs = []
    for hd in range(MEM_HEADS):
        cols = slice(hd * MEM_HD, (hd + 1) * MEM_HD)
        sc = _dot_nt(qm[:, cols], kmem_ref[0, :, cols])
        m = jnp.max(sc, axis=1, keepdims=True)
        e = jnp.exp(sc - m)
        l = jnp.sum(e, axis=1, keepdims=True)
        outs.append(_dot(e.astype(BF16), vmem_ref[0, :, cols]) / l)
    om = jnp.concatenate(outs, axis=1).astype(BF16)
    xa = _dot(om, wmo_ref[...])
    yield
    h2 = _ln(DN_ALPHA * h1 + xa, ln2g_ref[...], ln2b_ref[...])
    h2_ref[0, rows, :] = h2
    h2b_ref[0, rows, :] = h2.astype(BF16)

    x_hi, x_lo = _split(h2)
    r2 = _dot(x_hi, wrh_ref[...])
    r = r2[:, 0:LANE] + r2[:, LANE:2 * LANE] + _dot(x_lo, wrh_ref[:, 0:LANE]) + br_ref[...]
    yield
    lane = lax.broadcasted_iota(jnp.int32, (n_rows, LANE), 1)
    lane_f = lane.astype(F32)
    cmask = (lane >= N_EXPERTS) & (lane < 2 * N_EXPERTS)
    c = jnp.where(cmask, r, -jnp.inf)
    cmax = jnp.max(c, axis=1, keepdims=True)
    ce = jnp.exp(c - cmax)
    csum = jnp.sum(ce, axis=1, keepdims=True) * (1.0 / EPG)
    g_prob = 1.0 / csum
    grp_lane = ((lane & (N_EXPERTS - 1)) >> 3).astype(F32)
    gidx = jnp.min(jnp.where(cmask & (c == cmax), grp_lane, 99.0), axis=1, keepdims=True)
    fmask = (lane < N_EXPERTS) & (grp_lane == gidx)
    f = jnp.where(fmask, r, -jnp.inf)
    fmax = jnp.max(f, axis=1, keepdims=True)
    fe = jnp.exp(f - fmax)
    fsum = jnp.sum(fe, axis=1, keepdims=True)
    prob = fe / fsum
    p1 = jnp.max(prob, axis=1, keepdims=True)
    i1 = jnp.min(jnp.where(fmask & (prob == p1), lane_f, 999.0), axis=1, keepdims=True)
    rest = fmask & (lane_f != i1)
    prob2 = jnp.where(rest, prob, -1.0)
    p2 = jnp.max(prob2, axis=1, keepdims=True)
    i2 = jnp.min(jnp.where(rest & (prob2 == p2), lane_f, 999.0), axis=1, keepdims=True)
    den = p1 + p2
    comb = jnp.where(lane_f == i1, g_prob * (p1 / den),
                     jnp.where(lane_f == i2, g_prob * (p2 / den), 0.0))
    comb_ref[0, rows, :] = comb
    yield

    sel = jnp.where((lane_f == i1) | (lane_f == i2), 1.0, 0.0)
    cnt = jnp.sum(sel, axis=0, keepdims=True)
    pc = jnp.floor((cnt + (SEG_ALIGN - 1)) * (1.0 / SEG_ALIGN)) * SEG_ALIGN
    lane8 = lax.broadcasted_iota(jnp.int32, (8, LANE), 1)
    inc = jnp.broadcast_to(pc, (8, LANE))
    for sh in (1, 2, 4, 8, 16, 32, 64):
        inc = inc + jnp.where(lane8 >= sh, pltpu.roll(inc, sh, 1), 0.0)
    seg_start = inc[0:1] - pc
    t_row = lax.broadcasted_iota(jnp.int32, (n_rows, n_rows), 0)
    t_col = lax.broadcasted_iota(jnp.int32, (n_rows, n_rows), 1)
    earlier = jnp.where(t_row > t_col, 1.0, 0.0).astype(BF16)
    rank = _dot(earlier, sel.astype(BF16))
    pos = seg_start + rank
    d1 = jnp.sum(jnp.where(lane_f == i1, pos, 0.0), axis=1, keepdims=True)
    d2 = jnp.sum(jnp.where(lane_f == i2, pos, 0.0), axis=1, keepdims=True)
    route = jnp.where(lane == 0, d1, jnp.where(lane == 1, d2, 0.0))
    route_ref[0, rows, :] = route
    r_hi, r_lo = _split(route)
    eye = jnp.where(lax.broadcasted_iota(jnp.int32, (8, LANE), 0) == lane8, 1.0, 0.0).astype(BF16)
    route_t_ref[ci] = _dot_nt(eye, r_hi) + _dot_nt(eye, r_lo)
    cnt_ref[ci] = jnp.broadcast_to(pc, (8, LANE))


def _chunk_copies(sub, nch_ref, go_ref, make_copy, act):
    def body(k, carry):
        go = pl.multiple_of(go_ref[sub * MAX_CHUNKS + k], SEG_ALIGN)
        act(make_copy(pl.multiple_of(k * SEG_ALIGN, SEG_ALIGN), go))
        return carry
    lax.fori_loop(0, nch_ref[sub], body, 0)


def _dispatch_kernel(nch_ref, go_ref, tail_ref, xb_ref, comb_ref, route_t_ref, xs_ref,
                     cbuf, zbuf, sem, zsem, *, n_steps, rows):
    i = pl.program_id(0)
    par = lax.rem(i, 2)
    n_u = MOE_SUBS_PER_STEP
    tm = TOK_TILE

    def copies(step, par_, act):
        for u in range(n_u):
            slot_ = par_ * n_u + u

            def mk(lo, go, slot_=slot_):
                return pltpu.make_async_copy(cbuf.at[slot_, pl.ds(lo, SEG_ALIGN), :],
                                             xs_ref.at[pl.ds(go, SEG_ALIGN), :], sem.at[slot_])
            _chunk_copies(step * n_u + u, nch_ref, go_ref, mk, act)

    @pl.when(i >= 2)
    def _():
        copies(i - 2, par, lambda c: c.wait())

    def chain(u):
        rs = slice(u * tm, (u + 1) * tm)
        r_io = lax.broadcasted_iota(jnp.int32, (rows, tm), 0).astype(F32)
        d1 = route_t_ref[u, 0:1, :]
        d2 = route_t_ref[u, 1:2, :]
        p_mat = jnp.where((r_io == d1) | (r_io == d2), 1.0, 0.0).astype(BF16)
        comb = comb_ref[rs, :]
        c_hi = comb.astype(BF16).astype(F32)
        c_pack = (c_hi + pltpu.roll(comb - c_hi, N_EXPERTS, 1)).astype(BF16)
        x_aug = jnp.concatenate([xb_ref[rs, :], c_pack], axis=1)
        yield
        cbuf[par * n_u + u] = _dot(p_mat, x_aug).astype(BF16)

    _interleave([chain(u) for u in range(n_u)])
    copies(i, par, lambda c: c.start())

    @pl.when(i == n_steps - 1)
    def _():
        if n_steps >= 2:
            copies(i - 1, 1 - par, lambda c: c.wait())
        copies(i, par, lambda c: c.wait())
        zbuf[...] = jnp.zeros_like(zbuf)

        def tails(act):
            def body(e, carry):
                st = tail_ref[e]
                n = tail_ref[N_EXPERTS + e]
                off = jnp.int32(0)
                size = zbuf.shape[0]
                while size >= SEG_ALIGN:
                    bit = (n & size) != 0

                    @pl.when(bit)
                    def _(size=size, off=off):
                        act(pltpu.make_async_copy(
                            zbuf.at[pl.ds(0, size), :],
                            xs_ref.at[pl.ds(pl.multiple_of(st + off, SEG_ALIGN), size), :], zsem))
                    off = off + jnp.where(bit, size, 0)
                    size //= 2
                return carry
            lax.fori_loop(0, N_EXPERTS, body, 0)
        tails(lambda c: c.start())
        tails(lambda c: c.wait())


def _ffn_kernel(texp_ref, nused_ref, xs_ref, wg_ref, wu_ref, wd_ref, ys_ref, wg_b, wu_b, wd_b, xbuf, xsem):
    t = pl.program_id(0)
    e = texp_ref[t]
    n_used = nused_ref[0]

    def fetch(u):
        slot = lax.rem(u, XS_DEPTH)
        return pltpu.make_async_copy(xs_ref.at[pl.ds(pl.multiple_of(u * FFN_TILE, FFN_TILE), FFN_TILE), :],
                                     xbuf.at[slot], xsem.at[slot])

    @pl.when(t == 0)
    def _():
        for u in range(XS_DEPTH - 1):
            @pl.when(u < n_used)
            def _(u=u):
                fetch(jnp.int32(u)).start()

    @pl.when(t + (XS_DEPTH - 1) < n_used)
    def _():
        fetch(t + (XS_DEPTH - 1)).start()

    @pl.when((t == 0) | (e != texp_ref[jnp.maximum(t - 1, 0)]))
    def _():
        wg_b[...] = wg_ref[0].astype(BF16)
        wu_b[...] = wu_ref[0].astype(BF16)
        wd_b[...] = wd_ref[0].astype(BF16)

    @pl.when(t < n_used)
    def _():
        fetch(t).wait()
        xa = xbuf[lax.rem(t, XS_DEPTH)]
        xrow = xa[:, 0:D]
        cw = xa[:, D:D + LANE].astype(F32)
        lane = lax.broadcasted_iota(jnp.int32, cw.shape, 1)
        c = jnp.sum(jnp.where((lane == e) | (lane == e + N_EXPERTS), cw, 0.0), axis=1, keepdims=True)
        a = _dot(xrow, wg_b[...])
        b = _dot(xrow, wu_b[...])
        hid = (a * jax.nn.sigmoid(a)) * b * c
        ys_ref[...] = _dot(hid.astype(BF16), wd_b[...]).astype(ys_ref.dtype)


def _combine_kernel(nch_ref, go_ref, route_ref, h2_ref, g_ref, b_ref, ys_ref, out_ref,
                    ybuf, sem, *, n_steps, rows):
    i = pl.program_id(0)
    par = lax.rem(i, 2)
    n_u = MOE_SUBS_PER_STEP
    tm = TOK_TILE

    def copies(step, par_, act):
        for u in range(n_u):
            slot_ = par_ * n_u + u

            def mk(lo, go, slot_=slot_):
                return pltpu.make_async_copy(ys_ref.at[pl.ds(go, SEG_ALIGN), :],
                                             ybuf.at[slot_, pl.ds(lo, SEG_ALIGN), :], sem.at[slot_])
            _chunk_copies(step * n_u + u, nch_ref, go_ref, mk, act)

    @pl.when(i == 0)
    def _():
        ybuf[...] = jnp.zeros_like(ybuf)
        copies(0, 0, lambda c: c.start())

    @pl.when(i + 1 < n_steps)
    def _():
        copies(i + 1, 1 - par, lambda c: c.start())

    copies(i, par, lambda c: c.wait())

    def chain(u):
        rs = slice(u * tm, (u + 1) * tm)
        r_io = lax.broadcasted_iota(jnp.int32, (tm, rows), 1).astype(F32)
        d1 = route_ref[rs, 0:1]
        d2 = route_ref[rs, 1:2]
        p_t = jnp.where((r_io == d1) | (r_io == d2), 1.0, 0.0).astype(BF16)
        yield
        ff = _dot(p_t, ybuf[par * n_u + u])
        yield
        out_ref[rs, :] = _ln(DN_ALPHA * h2_ref[rs, :] + ff, g_ref[...], b_ref[...])

    _interleave([chain(u) for u in range(n_u)])


def _bias_kernel(tbl_ref, bkt_ref, out_ref):
    h = pl.program_id(0)
    far = tbl_ref[h, REL_BUCKETS - 1]
    for which in range(2):
        bk = bkt_ref[which]
        acc = jnp.where(bk < 0, NEG, 0.0)
        for kk in range(REL_BUCKETS):
            acc = jnp.where(bk == kk, tbl_ref[h, kk] - far, acc)
        out_ref[which, 0] = acc


def _rel_bucket_table(dist):
    max_exact = REL_BUCKETS // 2
    d = jnp.maximum(dist, 0)
    large = max_exact + (jnp.log(jnp.maximum(d, 1).astype(F32) / max_exact)
                         / math.log(REL_MAX_DIST / max_exact) * (REL_BUCKETS - max_exact)).astype(jnp.int32)
    large = jnp.minimum(large, REL_BUCKETS - 1)
    return jnp.where(d < max_exact, d, large)


def _const_spec(shape):
    nd = len(shape)
    return pl.BlockSpec(shape, lambda *_: (0,) * nd)


def kernel(x, mem, ln_in_g, ln_in_b, rel_bias, w_in, b_gate, w_pool_grp, pool_scale, w_pool_up, w_attn_up,
           w_mix_out, ln1_g, ln1_b, w_mq, w_mk, w_mv, w_mo, ln2_g, ln2_b, w_coarse, b_coarse, w_fine, b_fine,
           w_gate, w_up, w_down, ln3_g, ln3_b):
    B, S, _ = x.shape
    assert S == N_BLK * BLK and w_in.shape[0] == 1
    M = mem.shape[1]
    T = B * S
    tm = 512

    wi = w_in[0]
    w_u = wi[:, 0:POOL_W]
    w_q = wi[:, POOL_W:POOL_W + ATTN_W] * (HEAD_DIM ** -0.5)
    w_k = wi[:, POOL_W + ATTN_W:POOL_W + 2 * ATTN_W]
    w_v = wi[:, POOL_W + 2 * ATTN_W:POOL_W + 3 * ATTN_W]
    w_gl = wi[:, POOL_W + 3 * ATTN_W:]

    w1 = jnp.concatenate([w_u, w_q, w_k, w_v], axis=1).astype(BF16)
    row2 = lambda a: a.reshape(1, -1)

    iq = jnp.arange(BLK, dtype=jnp.int32)[:, None]
    ik = jnp.arange(BLK, dtype=jnp.int32)[None, :]
    d_own = iq - ik
    bkt = jnp.stack([jnp.where(d_own >= 0, _rel_bucket_table(d_own), -1), _rel_bucket_table(d_own + BLK)])
    t_bias = pl.pallas_call(
        _bias_kernel,
        grid=(N_HEADS,),
        in_specs=[pl.BlockSpec(memory_space=pltpu.SMEM), _const_spec((2, BLK, BLK))],
        out_specs=pl.BlockSpec((2, 1, BLK, BLK), lambda h: (0, h, 0, 0)),
        out_shape=jax.ShapeDtypeStruct((2, N_HEADS, BLK, BLK), F32),
        name="relbias_tiles",
    )(rel_bias.T, bkt)

    n_w1 = w1.shape[1]
    tmp = MERGE_TILE
    ypool, q_aug, k_aug, v_p = pl.pallas_call(
        functools.partial(_proj_kernel, tm=tmp),
        grid=(B, S // tmp),
        in_specs=[
            pl.BlockSpec((1, tmp, D), lambda b, s: (b, s, 0)),
            _const_spec((1, D)), _const_spec((1, D)),
            _const_spec((D, n_w1)),
            _const_spec((len(POOL_WINDOWS), LANE, LANE)),
            _const_spec((1, POOL_W)),
        ],
        out_specs=[
            pl.BlockSpec((1, tmp, POOL_W), lambda b, s: (b, s, 0)),
            pl.BlockSpec((1, N_HEADS, tmp, LANE), lambda b, s: (b, 0, s, 0)),
            pl.BlockSpec((1, N_HEADS, tmp, LANE), lambda b, s: (b, 0, s, 0)),
            pl.BlockSpec((1, ATTN_W // LANE, tmp, LANE), lambda b, s: (b, 0, s, 0)),
        ],
        out_shape=[
            jax.ShapeDtypeStruct((B, S, POOL_W), BF16),
            jax.ShapeDtypeStruct((B, N_HEADS, S, LANE), BF16),
            jax.ShapeDtypeStruct((B, N_HEADS, S, LANE), BF16),
            jax.ShapeDtypeStruct((B, ATTN_W // LANE, S, LANE), BF16),
        ],
        scratch_shapes=[pltpu.VMEM((HALO + tmp, POOL_W), F32), pltpu.VMEM((LANE, ATTN_W), F32)],
        compiler_params=pltpu.CompilerParams(dimension_semantics=("arbitrary", "arbitrary"),
                                             vmem_limit_bytes=VMEM_LIMIT),
        name="proj_pool_gate",
    )(x, row2(ln_in_g), row2(ln_in_b), w1, w_pool_grp[0].astype(BF16), row2(pool_scale[0]))

    o_attn = pl.pallas_call(
        _attn_kernel,
        grid=(B, N_BLK),
        in_specs=[
            pl.BlockSpec((1, N_HEADS, BLK, LANE), lambda b, j: (b, 0, j, 0)),
            pl.BlockSpec((1, N_HEADS, S, LANE), lambda b, j: (b, 0, 0, 0)),
            pl.BlockSpec((1, ATTN_W // LANE, S, LANE), lambda b, j: (b, 0, 0, 0)),
            _const_spec((2, N_HEADS, BLK, BLK)),
        ],
        out_specs=pl.BlockSpec((1, ATTN_W // LANE, BLK, LANE), lambda b, j: (b, 0, j, 0)),
        out_shape=jax.ShapeDtypeStruct((B, ATTN_W // LANE, S, LANE), BF16),
        compiler_params=pltpu.CompilerParams(dimension_semantics=("arbitrary", "arbitrary"),
                                             vmem_limit_bytes=VMEM_LIMIT),
        name="moba_attn",
    )(q_aug, k_aug, v_p, t_bias)

    kmem, vmem = pl.pallas_call(
        _memkv_kernel,
        grid=(B,),
        in_specs=[pl.BlockSpec((1, M, D), lambda b: (b, 0, 0)),
                  _const_spec((D, MEM_W)), _const_spec((D, MEM_W))],
        out_specs=[pl.BlockSpec((1, M, MEM_W), lambda b: (b, 0, 0)),
                   pl.BlockSpec((1, M, MEM_W), lambda b: (b, 0, 0))],
        out_shape=[jax.ShapeDtypeStruct((B, M, MEM_W), BF16)] * 2,
        compiler_params=pltpu.CompilerParams(dimension_semantics=("arbitrary",)),
        name="mem_kv",
    )(mem, w_mk[0].astype(BF16), w_mv[0].astype(BF16))

    w_r = jnp.concatenate([
        w_fine[0].reshape(D, N_EXPERTS),
        jnp.repeat(w_coarse[0], EPG, axis=1),
        jnp.zeros((D, LANE - 2 * N_EXPERTS), F32)], axis=1)
    b_r = jnp.concatenate([
        b_fine[0].reshape(N_EXPERTS), jnp.repeat(b_coarse[0], EPG),
        jnp.zeros((LANE - 2 * N_EXPERTS,), F32)]).reshape(1, LANE)
    w_r_hi = w_r.astype(BF16)
    w_r_lo = (w_r - w_r_hi.astype(F32)).astype(BF16)
    w_r_cat = jnp.concatenate([w_r_hi, w_r_lo], axis=1)

    n_sub = T // tm
    tmm = MERGE_TILE
    per = tmm // tm
    sub_idx = lambda b, s: b * (S // tmm) + s
    h2, h2b, comb, route, route_t, seg_cnt = pl.pallas_call(
        functools.partial(_merge_kernel, tm=tmm),
        grid=(B, S // tmm),
        in_specs=[
            pl.BlockSpec((1, tmm, D), lambda b, s: (b, s, 0)),
            pl.BlockSpec((1, tmm, POOL_W), lambda b, s: (b, s, 0)),
            pl.BlockSpec((1, ATTN_W // LANE, tmm, LANE), lambda b, s: (b, 0, s, 0)),
            pl.BlockSpec((1, M, MEM_W), lambda b, s: (b, 0, 0)),
            pl.BlockSpec((1, M, MEM_W), lambda b, s: (b, 0, 0)),
            _const_spec((1, D)), _const_spec((1, D)),
            _const_spec((D, 2 * D)), _const_spec((1, 2 * D)),
            _const_spec((POOL_W, D)), _const_spec((ATTN_W, D)), _const_spec((D, D)),
            _const_spec((1, D)), _const_spec((1, D)),
            _const_spec((D, MEM_W)), _const_spec((MEM_W, D)),
            _const_spec((1, D)), _const_spec((1, D)),
            _const_spec((D, 2 * LANE)), _const_spec((1, LANE)),
        ],
        out_specs=[
            pl.BlockSpec((1, tmm, D), lambda b, s: (b, s, 0)),
            pl.BlockSpec((1, tmm, D), lambda b, s: (b, s, 0)),
            pl.BlockSpec((1, tmm, LANE), lambda b, s: (b, s, 0)),
            pl.BlockSpec((1, tmm, LANE), lambda b, s: (b, s, 0)),
            pl.BlockSpec((per, 8, tm), lambda b, s: (sub_idx(b, s), 0, 0)),
            pl.BlockSpec((per, 8, LANE), lambda b, s: (sub_idx(b, s), 0, 0)),
        ],
        out_shape=[
            jax.ShapeDtypeStruct((B, S, D), F32),
            jax.ShapeDtypeStruct((B, S, D), BF16),
            jax.ShapeDtypeStruct((B, S, LANE), F32),
            jax.ShapeDtypeStruct((B, S, LANE), F32),
            jax.ShapeDtypeStruct((n_sub, 8, tm), F32),
            jax.ShapeDtypeStruct((n_sub, 8, LANE), F32),
        ],
        compiler_params=pltpu.CompilerParams(dimension_semantics=("arbitrary", "arbitrary"),
                                             vmem_limit_bytes=VMEM_LIMIT),
        name="merge_memattn_router",
    )(x, ypool, o_attn, kmem, vmem,
      row2(ln_in_g), row2(ln_in_b), w_gl.astype(BF16), row2(b_gate[0]),
      w_pool_up[0].astype(BF16), w_attn_up[0].astype(BF16), w_mix_out[0].astype(BF16),
      row2(ln1_g[0]), row2(ln1_b[0]),
      (w_mq[0] * (MEM_HD ** -0.5)).astype(BF16), w_mo[0].astype(BF16),
      row2(ln2_g[0]), row2(ln2_b[0]),
      w_r_cat, b_r)

    pcs = seg_cnt[:, 0, :N_EXPERTS].astype(jnp.int32)
    tot = jnp.sum(pcs, axis=0)
    cap = ((tot + FFN_TILE - 1) // FFN_TILE) * FFN_TILE
    ends = jnp.cumsum(cap)
    base = ends - cap
    gs = base[None, :] + jnp.cumsum(pcs, axis=0) - pcs
    ls = jnp.cumsum(pcs, axis=1) - pcs
    n_sorted = n_sub * COMPACT_ROWS + N_EXPERTS * FFN_TILE
    n_ffn_tiles = n_sorted // FFN_TILE
    n_used = (ends[-1] // FFN_TILE).astype(jnp.int32)
    tile_row = jnp.arange(n_ffn_tiles, dtype=jnp.int32) * FFN_TILE
    tile_exp = jnp.sum(jnp.minimum(tile_row, ends[-1] - 1)[:, None] >= ends[None, :], axis=1).astype(jnp.int32)
    tails = jnp.concatenate([base + tot, cap - tot]).astype(jnp.int32)
    chunk_row = jnp.arange(MAX_CHUNKS, dtype=jnp.int32) * SEG_ALIGN
    chunk_exp = jnp.sum(chunk_row[None, :, None] >= (ls + pcs)[:, None, :], axis=2)
    shift = jnp.sum(jnp.where(chunk_exp[:, :, None] == jnp.arange(N_EXPERTS)[None, None, :],
                              (gs - ls)[:, None, :], 0), axis=2)
    chunk_go = (chunk_row[None, :] + shift).astype(jnp.int32).reshape(-1)
    n_chunks = (jnp.sum(pcs, axis=1) // SEG_ALIGN).astype(jnp.int32)

    aug_w = D + LANE
    n_u = MOE_SUBS_PER_STEP
    n_steps = n_sub // n_u
    x_sorted = pl.pallas_call(
        functools.partial(_dispatch_kernel, n_steps=n_steps, rows=COMPACT_ROWS),
        grid_spec=pltpu.PrefetchScalarGridSpec(
            num_scalar_prefetch=3,
            grid=(n_steps,),
            in_specs=[
                pl.BlockSpec((n_u * tm, D), lambda i, *_: (i, 0)),
                pl.BlockSpec((n_u * tm, LANE), lambda i, *_: (i, 0)),
                pl.BlockSpec((n_u, 8, tm), lambda i, *_: (i, 0, 0)),
            ],
            out_specs=pl.BlockSpec(memory_space=pl.ANY),
            scratch_shapes=[
                pltpu.VMEM((2 * n_u, COMPACT_ROWS, aug_w), BF16),
                pltpu.VMEM((FFN_TILE // 2, aug_w), BF16),
                pltpu.SemaphoreType.DMA((2 * n_u,)),
                pltpu.SemaphoreType.DMA(()),
            ],
        ),
        out_shape=jax.ShapeDtypeStruct((n_sorted, aug_w), BF16),
        compiler_params=pltpu.CompilerParams(dimension_semantics=("arbitrary",), vmem_limit_bytes=VMEM_LIMIT),
        name="moe_dispatch",
    )(n_chunks, chunk_go, tails, h2b.reshape(T, D), comb.reshape(T, LANE), route_t)

    used_tile = lambda t, texp, nused: (jnp.minimum(t, nused[0] - 1), 0)
    y_sorted = pl.pallas_call(
        _ffn_kernel,
        grid_spec=pltpu.PrefetchScalarGridSpec(
            num_scalar_prefetch=2,
            grid=(n_ffn_tiles,),
            in_specs=[
                pl.BlockSpec(memory_space=pl.ANY),
                pl.BlockSpec((1, D, FF), lambda t, texp, nused: (texp[t], 0, 0)),
                pl.BlockSpec((1, D, FF), lambda t, texp, nused: (texp[t], 0, 0)),
                pl.BlockSpec((1, FF, D), lambda t, texp, nused: (texp[t], 0, 0)),
            ],
            out_specs=pl.BlockSpec((FFN_TILE, D), used_tile),
            scratch_shapes=[pltpu.VMEM((D, FF), BF16), pltpu.VMEM((D, FF), BF16), pltpu.VMEM((FF, D), BF16),
                            pltpu.VMEM((XS_DEPTH, FFN_TILE, aug_w), BF16), pltpu.SemaphoreType.DMA((XS_DEPTH,))],
        ),
        out_shape=jax.ShapeDtypeStruct((n_sorted, D), BF16),
        compiler_params=pltpu.CompilerParams(dimension_semantics=("arbitrary",), vmem_limit_bytes=VMEM_LIMIT),
        name="moe_expert_ffn",
    )(tile_exp, n_used.reshape(1), x_sorted, w_gate[0], w_up[0], w_down[0])

    out = pl.pallas_call(
        functools.partial(_combine_kernel, n_steps=n_steps, rows=COMPACT_ROWS),
        grid_spec=pltpu.PrefetchScalarGridSpec(
            num_scalar_prefetch=2,
            grid=(n_steps,),
            in_specs=[
                pl.BlockSpec((n_u * tm, LANE), lambda i, *_: (i, 0)),
                pl.BlockSpec((n_u * tm, D), lambda i, *_: (i, 0)),
                pl.BlockSpec((1, D), lambda i, *_: (0, 0)),
                pl.BlockSpec((1, D), lambda i, *_: (0, 0)),
                pl.BlockSpec(memory_space=pl.ANY),
            ],
            out_specs=pl.BlockSpec((n_u * tm, D), lambda i, *_: (i, 0)),
            scratch_shapes=[
                pltpu.VMEM((2 * n_u, COMPACT_ROWS, D), BF16),
                pltpu.SemaphoreType.DMA((2 * n_u,)),
            ],
        ),
        out_shape=jax.ShapeDtypeStruct((T, D), F32),
        compiler_params=pltpu.CompilerParams(dimension_semantics=("arbitrary",), vmem_limit_bytes=VMEM_LIMIT),
        name="moe_combine_ln3",
    )(n_chunks, chunk_go, route.reshape(T, LANE), h2.reshape(T, D), row2(ln3_g[0]), row2(ln3_b[0]), y_sorted)
    return out.reshape(B, S, D)
```

```python
import functools
import math

import jax
import jax.numpy as jnp
from jax import lax
from jax.experimental import pallas as pl
from jax.experimental.pallas import tpu as pltpu

D = 1024
POOL_WINDOWS = (2, 4, 8, 16)
POOL_W = 512
N_HEADS = 8
HEAD_DIM = 64
ATTN_W = 512
BLK = 256
N_BLK = 8
TOPK = 3
REL_BUCKETS = 32
REL_MAX_DIST = 128
MEM_HEADS = 4
MEM_HD = 128
MEM_W = 512
N_GROUPS = 4
EPG = 8
N_EXPERTS = 32
FF = 256
DN_ALPHA = 2.0 ** 0.25
LN_EPS = 1e-5

LANE = 128
TOK_TILE = 512
SEG_ALIGN = 16
COMPACT_ROWS = 2 * TOK_TILE + N_EXPERTS * SEG_ALIGN
BIG_CHUNK = 2 * SEG_ALIGN
MAX_BIG = COMPACT_ROWS // BIG_CHUNK
CHUNK_TAB_W = 2 * MAX_BIG + 2 * N_EXPERTS
FFN_TILE = 512
ATTN_PAIRS_PER_TRIP = 4
MOE_SUBS_PER_STEP = 2
FFN_CHAIN = 512
XS_DEPTH = 3
MERGE_TILE = 1024
HALO = 16
NEG = -1e30
VMEM_LIMIT = 56 * 1024 * 1024

F32 = jnp.float32
BF16 = jnp.bfloat16

_NT = (((1,), (1,)), ((), ()))


def _dot(a, b):
    return jnp.dot(a, b, preferred_element_type=F32)


def _dot_nt(a, b):
    return lax.dot_general(a, b, _NT, preferred_element_type=F32)


def _split(a):
    hi = a.astype(BF16)
    lo = (a - hi.astype(F32)).astype(BF16)
    return hi, lo


def _interleave(chains):
    results = [None] * len(chains)
    live = list(range(len(chains)))
    while live:
        for ci in list(live):
            try:
                next(chains[ci])
            except StopIteration as done:
                results[ci] = done.value
                live.remove(ci)
    return results


def _ln(x, g, b):
    mu = jnp.mean(x, axis=-1, keepdims=True)
    xc = x - mu
    var = jnp.mean(xc * xc, axis=-1, keepdims=True)
    return xc * lax.rsqrt(var + LN_EPS) * g + b


def _proj_kernel(x_ref, g_ref, b_ref, w_ref, wgrp_ref, pscale_ref,
                 ypool_ref, q_ref, k_ref, v_ref, ubuf, kbt, *, tm):
    s = pl.program_id(1)

    @pl.when(s == 0)
    def _():
        ubuf[0:HALO, :] = jnp.zeros((HALO, POOL_W), F32)
        kbt[...] = jnp.zeros_like(kbt)

    _interleave([_proj_chain(ci, s, tm, x_ref, g_ref, b_ref, w_ref, wgrp_ref, pscale_ref,
                             ypool_ref, q_ref, k_ref, v_ref, ubuf, kbt)
                 for ci in range(tm // TOK_TILE)])
    ubuf[0:HALO, :] = ubuf[tm:tm + HALO, :]


def _proj_chain(ci, s, tm, x_ref, g_ref, b_ref, w_ref, wgrp_ref, pscale_ref,
                ypool_ref, q_ref, k_ref, v_ref, ubuf, kbt):
    n = TOK_TILE
    r0 = ci * n
    rows = slice(r0, r0 + n)
    blk0 = s * (tm // BLK) + ci * (n // BLK)

    h = _ln(x_ref[0, rows, :], g_ref[...], b_ref[...])
    hb = h.astype(BF16)
    zu = _dot(hb, w_ref[:, 0:POOL_W])
    zq = _dot(hb, w_ref[:, POOL_W:POOL_W + ATTN_W])
    zk = _dot(hb, w_ref[:, POOL_W + ATTN_W:POOL_W + 2 * ATTN_W])
    zv = _dot(hb, w_ref[:, POOL_W + 2 * ATTN_W:])
    ubuf[HALO + r0:HALO + r0 + n, :] = zu

    r_io = lax.broadcasted_iota(jnp.int32, kbt.shape, 0)
    c_io = lax.broadcasted_iota(jnp.int32, kbt.shape, 1)
    head_match = (r_io >> 3) == (c_io >> 6)
    for bi in range(n // BLK):
        kmean = jnp.mean(zk[bi * BLK:(bi + 1) * BLK], axis=0, keepdims=True)
        kbt[...] = jnp.where(head_match & ((r_io & 7) == blk0 + bi), kmean, kbt[...])
    yield

    t_pos = s * tm + r0 + lax.broadcasted_iota(jnp.int32, (n, LANE), 0)
    for g, w in enumerate(POOL_WINDOWS):
        cols = slice(g * LANE, (g + 1) * LANE)
        ws = ubuf[HALO + r0:HALO + r0 + n, cols]
        for kk in range(1, w):
            ws = ws + ubuf[HALO + r0 - kk:HALO + r0 - kk + n, cols]
        cnt = jnp.minimum(t_pos + 1, w).astype(F32)
        y = ws / cnt - ubuf[HALO + r0:HALO + r0 + n, cols]
        yg = _dot(y.astype(BF16), wgrp_ref[g]) * pscale_ref[:, cols]
        ypool_ref[0, rows, cols] = yg.astype(ypool_ref.dtype)
    yield

    q_hi, q_lo = _split(zq)
    kb_hi, kb_lo = _split(kbt[...])
    g2 = _dot_nt(q_hi, jnp.concatenate([kb_hi, kb_lo], axis=0))
    gate = g2[:, 0:LANE] + g2[:, LANE:2 * LANE] + _dot_nt(q_lo, kb_hi)
    yield

    lane = lax.broadcasted_iota(jnp.int32, (n, LANE), 1)
    row = lax.broadcasted_iota(jnp.int32, (n, LANE), 0)
    n_l = lane & 7
    jrow = blk0 + (row >> 8)
    past = n_l < jrow
    gt = jnp.where(past, gate, -jnp.inf)
    cnt = jnp.zeros((n, LANE), F32)
    for sh in range(1, N_BLK):
        wrap = (n_l + sh) >= N_BLK
        gm = jnp.where(wrap, pltpu.roll(gt, N_BLK - sh, 1), pltpu.roll(gt, LANE - sh, 1))
        cnt = cnt + jnp.where(wrap, jnp.where(gm >= gt, 1.0, 0.0), jnp.where(gm > gt, 1.0, 0.0))
    keep = (past & (cnt < TOPK)) | (n_l == jrow)
    negmask = jnp.where(keep, 0.0, NEG)

    aug_lane = (lane >= HEAD_DIM) & (lane < HEAD_DIM + N_BLK)
    k_onehot = jnp.where(lane == HEAD_DIM + jrow, 1.0, 0.0)
    head_lane = lane < HEAD_DIM
    for hh in range(N_HEADS):
        cols = slice((hh // 2) * LANE, (hh // 2 + 1) * LANE)
        q_h, k_h = zq[:, cols], zk[:, cols]
        if hh % 2:
            q_h, k_h = pltpu.roll(q_h, HEAD_DIM, 1), pltpu.roll(k_h, HEAD_DIM, 1)
        m_h = jnp.where(aug_lane, pltpu.roll(negmask, HEAD_DIM - N_BLK * hh, 1), 0.0)
        q_ref[0, hh, rows, :] = jnp.where(head_lane, q_h, m_h).astype(q_ref.dtype)
        k_ref[0, hh, rows, :] = jnp.where(head_lane, k_h, k_onehot).astype(k_ref.dtype)
    for p in range(ATTN_W // LANE):
        v_ref[0, p, rows, :] = zv[:, p * LANE:(p + 1) * LANE].astype(v_ref.dtype)


def _attn_kernel(q_ref, k_ref, v_ref, tb_ref, o_ref):
    j = pl.program_id(1)
    lane = lax.broadcasted_iota(jnp.int32, (BLK, LANE), 1)

    def one_head(h, p, jj):
        q = q_ref[0, h]
        own0 = jj * BLK
        pieces = []
        s_own = _dot_nt(q, k_ref[0, h, own0:own0 + BLK, :]) + tb_ref[0, h]
        pieces.append((s_own, own0, BLK))
        if jj >= 1:
            s_adj = _dot_nt(q, k_ref[0, h, own0 - BLK:own0, :]) + tb_ref[1, h]
            pieces.append((s_adj, own0 - BLK, BLK))
        if jj >= 2:
            s_far = _dot_nt(q, k_ref[0, h, 0:own0 - BLK, :])
            pieces.append((s_far, 0, own0 - BLK))
        yield
        m = None
        for sc, _, _ in pieces:
            mm = jnp.max(sc, axis=1, keepdims=True)
            m = mm if m is None else jnp.maximum(m, mm)
        l = None
        probs = []
        for sc, start, size in pieces:
            e = jnp.exp(sc - m)
            ls = jnp.sum(e, axis=1, keepdims=True)
            l = ls if l is None else l + ls
            probs.append((e.astype(BF16), start, size))
        yield
        acc = None
        for pb, start, size in probs:
            pv = _dot(pb, v_ref[0, p, start:start + size, :])
            acc = pv if acc is None else acc + pv
        return acc / l

    for jj in range(N_BLK):
        @pl.when(j == jj)
        def _(jj=jj):
            def group(gi, carry):
                pairs = [gi * ATTN_PAIRS_PER_TRIP + pi for pi in range(ATTN_PAIRS_PER_TRIP)]
                outs = _interleave([one_head(2 * p + hh, p, jj) for p in pairs for hh in range(2)])
                for pi, p in enumerate(pairs):
                    o_ref[0, p] = jnp.where(lane < HEAD_DIM, outs[2 * pi], outs[2 * pi + 1]).astype(o_ref.dtype)
                return carry
            lax.fori_loop(0, N_HEADS // 2 // ATTN_PAIRS_PER_TRIP, group, 0)


def _memkv_kernel(mem_ref, wk_ref, wv_ref, k_ref, v_ref):
    mb = mem_ref[0].astype(BF16)
    k_ref[0] = _dot(mb, wk_ref[...]).astype(k_ref.dtype)
    v_ref[0] = _dot(mb, wv_ref[...]).astype(v_ref.dtype)


def _merge_kernel(x_ref, ypool_ref, o_ref, kmem_ref, vmem_ref,
                  lng_ref, lnb_ref, wgl_ref, bgate_ref, wpu_ref, wau_ref, wout_ref,
                  ln1g_ref, ln1b_ref, wmq_ref, wmo_ref, ln2g_ref, ln2b_ref,
                  wrh_ref, br_ref,
                  h2_ref, h2b_ref, comb_ref, route_ref, route_t_ref, cnt_ref, *, tm):
    _interleave([_merge_chain(ci, x_ref, ypool_ref, o_ref, kmem_ref, vmem_ref,
                              lng_ref, lnb_ref, wgl_ref, bgate_ref, wpu_ref, wau_ref, wout_ref,
                              ln1g_ref, ln1b_ref, wmq_ref, wmo_ref, ln2g_ref, ln2b_ref,
                              wrh_ref, br_ref, h2_ref, h2b_ref, comb_ref, route_ref, route_t_ref, cnt_ref)
                 for ci in range(tm // TOK_TILE)])


def _merge_chain(ci, x_ref, ypool_ref, o_ref, kmem_ref, vmem_ref,
                 lng_ref, lnb_ref, wgl_ref, bgate_ref, wpu_ref, wau_ref, wout_ref,
                 ln1g_ref, ln1b_ref, wmq_ref, wmo_ref, ln2g_ref, ln2b_ref,
                 wrh_ref, br_ref, h2_ref, h2b_ref, comb_ref, route_ref, route_t_ref, cnt_ref):
    rows = slice(ci * TOK_TILE, (ci + 1) * TOK_TILE)
    n_rows = TOK_TILE
    h = _ln(x_ref[0, rows, :], lng_ref[...], lnb_ref[...])
    hb = h.astype(BF16)
    gl = _dot(hb, wgl_ref[...]) + bgate_ref[...]
    yield
    gates = 0.5 * jnp.tanh(0.5 * gl) + 0.5
    y_pool = _dot(ypool_ref[0, rows, :], wpu_ref[...])
    o_cat = jnp.concatenate([o_ref[0, p, rows, :] for p in range(ATTN_W // LANE)], axis=1)
    y_attn = _dot(o_cat, wau_ref[...])
    yield
    merged = gates[:, 0:D] * y_pool + gates[:, D:2 * D] * y_attn
    mix = _dot(merged.astype(BF16), wout_ref[...])
    yield
    h1 = _ln(DN_ALPHA * h + mix, ln1g_ref[...], ln1b_ref[...])

    qm = _dot(h1.astype(BF16), wmq_ref[...]).astype(BF16)
    yield
    outs = []
    for hd in range(MEM_HEADS):
        cols = slice(hd * MEM_HD, (hd + 1) * MEM_HD)
        sc = _dot_nt(qm[:, cols], kmem_ref[0, :, cols])
        m = jnp.max(sc, axis=1, keepdims=True)
        e = jnp.exp(sc - m)
        l = jnp.sum(e, axis=1, keepdims=True)
        outs.append(_dot(e.astype(BF16), vmem_ref[0, :, cols]) / l)
    om = jnp.concatenate(outs, axis=1).astype(BF16)
    xa = _dot(om, wmo_ref[...])
    yield
    h2 = _ln(DN_ALPHA * h1 + xa, ln2g_ref[...], ln2b_ref[...])
    h2_ref[0, rows, :] = h2
    h2b_ref[0, rows, :] = h2.astype(BF16)

    x_hi, x_lo = _split(h2)
    r2 = _dot(x_hi, wrh_ref[...])
    r = r2[:, 0:LANE] + r2[:, LANE:2 * LANE] + _dot(x_lo, wrh_ref[:, 0:LANE]) + br_ref[...]
    yield
    lane = lax.broadcasted_iota(jnp.int32, (n_rows, LANE), 1)
    lane_f = lane.astype(F32)
    cmask = (lane >= N_EXPERTS) & (lane < 2 * N_EXPERTS)
    c = jnp.where(cmask, r, -jnp.inf)
    cmax = jnp.max(c, axis=1, keepdims=True)
    ce = jnp.exp(c - cmax)
    csum = jnp.sum(ce, axis=1, keepdims=True) * (1.0 / EPG)
    g_prob = 1.0 / csum
    grp_lane = ((lane & (N_EXPERTS - 1)) >> 3).astype(F32)
    gidx = jnp.min(jnp.where(cmask & (c == cmax), grp_lane, 99.0), axis=1, keepdims=True)
    fmask = (lane < N_EXPERTS) & (grp_lane == gidx)
    f = jnp.where(fmask, r, -jnp.inf)
    fmax = jnp.max(f, axis=1, keepdims=True)
    fe = jnp.exp(f - fmax)
    fsum = jnp.sum(fe, axis=1, keepdims=True)
    prob = fe / fsum
    p1 = jnp.max(prob, axis=1, keepdims=True)
    i1 = jnp.min(jnp.where(fmask & (prob == p1), lane_f, 999.0), axis=1, keepdims=True)
    rest = fmask & (lane_f != i1)
    prob2 = jnp.where(rest, prob, -1.0)
    p2 = jnp.max(prob2, axis=1, keepdims=True)
    i2 = jnp.min(jnp.where(rest & (prob2 == p2), lane_f, 999.0), axis=1, keepdims=True)
    den = p1 + p2
    comb = jnp.where(lane_f == i1, g_prob * (p1 / den),
                     jnp.where(lane_f == i2, g_prob * (p2 / den), 0.0))
    comb_ref[0, rows, :] = comb
    yield

    sel = jnp.where((lane_f == i1) | (lane_f == i2), 1.0, 0.0)
    cnt = jnp.sum(sel, axis=0, keepdims=True)
    pc = jnp.floor((cnt + (SEG_ALIGN - 1)) * (1.0 / SEG_ALIGN)) * SEG_ALIGN
    lane8 = lax.broadcasted_iota(jnp.int32, (8, LANE), 1)
    inc = jnp.broadcast_to(pc, (8, LANE))
    for sh in (1, 2, 4, 8, 16, 32, 64):
        inc = inc + jnp.where(lane8 >= sh, pltpu.roll(inc, sh, 1), 0.0)
    seg_start = inc[0:1] - pc
    t_row = lax.broadcasted_iota(jnp.int32, (n_rows, n_rows), 0)
    t_col = lax.broadcasted_iota(jnp.int32, (n_rows, n_rows), 1)
    earlier = jnp.where(t_row > t_col, 1.0, 0.0).astype(BF16)
    rank = _dot(earlier, sel.astype(BF16))
    pos = seg_start + rank
    d1 = jnp.sum(jnp.where(lane_f == i1, pos, 0.0), axis=1, keepdims=True)
    d2 = jnp.sum(jnp.where(lane_f == i2, pos, 0.0), axis=1, keepdims=True)
    route = jnp.where(lane == 0, d1, jnp.where(lane == 1, d2, 0.0))
    route_ref[0, rows, :] = route
    r_hi, r_lo = _split(route)
    eye = jnp.where(lax.broadcasted_iota(jnp.int32, (8, LANE), 0) == lane8, 1.0, 0.0).astype(BF16)
    route_t_ref[ci] = _dot_nt(eye, r_hi) + _dot_nt(eye, r_lo)
    cnt_ref[ci] = jnp.broadcast_to(pc, (8, LANE))


def _chunk_copies(sub, nch_ref, tab_ref, make_copy, act):
    base = sub * CHUNK_TAB_W
    for size, cnt_i, lo_off, go_off in ((BIG_CHUNK, 0, 0, MAX_BIG), (SEG_ALIGN, 1, 2 * MAX_BIG, 2 * MAX_BIG + N_EXPERTS)):
        def body(k, carry, size=size, lo_off=lo_off, go_off=go_off):
            lo = pl.multiple_of(tab_ref[base + lo_off + k], SEG_ALIGN)
            go = pl.multiple_of(tab_ref[base + go_off + k], SEG_ALIGN)
            act(make_copy(lo, go, size))
            return carry
        lax.fori_loop(0, nch_ref[2 * sub + cnt_i], body, 0)


def _dispatch_kernel(nch_ref, tab_ref, tail_ref, xb_ref, comb_ref, route_t_ref, xs_ref,
                     cbuf, zbuf, sem, zsem, *, n_steps, rows):
    i = pl.program_id(0)
    par = lax.rem(i, 2)
    n_u = MOE_SUBS_PER_STEP
    tm = TOK_TILE

    def copies(step, par_, act):
        for u in range(n_u):
            slot_ = par_ * n_u + u

            def mk(lo, go, size, slot_=slot_):
                return pltpu.make_async_copy(cbuf.at[slot_, pl.ds(lo, size), :],
                                             xs_ref.at[pl.ds(go, size), :], sem.at[slot_])
            _chunk_copies(step * n_u + u, nch_ref, tab_ref, mk, act)

    @pl.when(i >= 2)
    def _():
        copies(i - 2, par, lambda c: c.wait())

    def chain(u):
        rs = slice(u * tm, (u + 1) * tm)
        r_io = lax.broadcasted_iota(jnp.int32, (rows, tm), 0).astype(F32)
        d1 = route_t_ref[u, 0:1, :]
        d2 = route_t_ref[u, 1:2, :]
        p_mat = jnp.where((r_io == d1) | (r_io == d2), 1.0, 0.0).astype(BF16)
        comb = comb_ref[rs, :]
        c_hi = comb.astype(BF16).astype(F32)
        c_pack = (c_hi + pltpu.roll(comb - c_hi, N_EXPERTS, 1)).astype(BF16)
        x_aug = jnp.concatenate([xb_ref[rs, :], c_pack], axis=1)
        yield
        cbuf[par * n_u + u] = _dot(p_mat, x_aug).astype(BF16)

    _interleave([chain(u) for u in range(n_u)])
    copies(i, par, lambda c: c.start())

    @pl.when(i == n_steps - 1)
    def _():
        if n_steps >= 2:
            copies(i - 1, 1 - par, lambda c: c.wait())
        copies(i, par, lambda c: c.wait())
        zbuf[...] = jnp.zeros_like(zbuf)

        def tails(act):
            def body(e, carry):
                st = tail_ref[e]
                n = tail_ref[N_EXPERTS + e]
                off = jnp.int32(0)
                size = zbuf.shape[0]
                while size >= SEG_ALIGN:
                    bit = (n & size) != 0

                    @pl.when(bit)
                    def _(size=size, off=off):
                        act(pltpu.make_async_copy(
                            zbuf.at[pl.ds(0, size), :],
                            xs_ref.at[pl.ds(pl.multiple_of(st + off, SEG_ALIGN), size), :], zsem))
                    off = off + jnp.where(bit, size, 0)
                    size //= 2
                return carry
            lax.fori_loop(0, N_EXPERTS, body, 0)
        tails(lambda c: c.start())
        tails(lambda c: c.wait())


def _ffn_kernel(texp_ref, nused_ref, xs_ref, wg_ref, wu_ref, wd_ref, ys_ref, wg_b, wu_b, wd_b, xbuf, xsem):
    t = pl.program_id(0)
    e = texp_ref[t]
    n_used = nused_ref[0]

    def fetch(u):
        slot = lax.rem(u, XS_DEPTH)
        return pltpu.make_async_copy(xs_ref.at[pl.ds(pl.multiple_of(u * FFN_TILE, FFN_TILE), FFN_TILE), :],
                                     xbuf.at[slot], xsem.at[slot])

    @pl.when(t == 0)
    def _():
        for u in range(XS_DEPTH - 1):
            @pl.when(u < n_used)
            def _(u=u):
                fetch(jnp.int32(u)).start()

    @pl.when(t + (XS_DEPTH - 1) < n_used)
    def _():
        fetch(t + (XS_DEPTH - 1)).start()

    @pl.when((t == 0) | (e != texp_ref[jnp.maximum(t - 1, 0)]))
    def _():
        wg_b[...] = wg_ref[0].astype(BF16)
        wu_b[...] = wu_ref[0].astype(BF16)
        wd_b[...] = wd_ref[0].astype(BF16)

    @pl.when(t < n_used)
    def _():
        fetch(t).wait()
        slot = lax.rem(t, XS_DEPTH)

        def chain(ci):
            rs = slice(ci * FFN_CHAIN, (ci + 1) * FFN_CHAIN)
            xrow = xbuf[slot, rs, 0:D]
            cw = xbuf[slot, rs, D:D + LANE].astype(F32)
            lane = lax.broadcasted_iota(jnp.int32, cw.shape, 1)
            c = jnp.sum(jnp.where((lane == e) | (lane == e + N_EXPERTS), cw, 0.0), axis=1, keepdims=True)
            a = _dot(xrow, wg_b[...])
            b = _dot(xrow, wu_b[...])
            yield
            hid = ((a * jax.nn.sigmoid(a)) * b * c).astype(BF16)
            yield
            ys_ref[rs, :] = _dot(hid, wd_b[...]).astype(ys_ref.dtype)

        _interleave([chain(ci) for ci in range(FFN_TILE // FFN_CHAIN)])


def _combine_kernel(nch_ref, tab_ref, route_ref, h2_ref, g_ref, b_ref, ys_ref, out_ref,
                    ybuf, sem, *, n_steps, rows):
    i = pl.program_id(0)
    par = lax.rem(i, 2)
    n_u = MOE_SUBS_PER_STEP
    tm = TOK_TILE

    def copies(step, par_, act):
        for u in range(n_u):
            slot_ = par_ * n_u + u

            def mk(lo, go, size, slot_=slot_):
                return pltpu.make_async_copy(ys_ref.at[pl.ds(go, size), :],
                                             ybuf.at[slot_, pl.ds(lo, size), :], sem.at[slot_])
            _chunk_copies(step * n_u + u, nch_ref, tab_ref, mk, act)

    @pl.when(i == 0)
    def _():
        ybuf[...] = jnp.zeros_like(ybuf)
        copies(0, 0, lambda c: c.start())

    @pl.when(i + 1 < n_steps)
    def _():
        copies(i + 1, 1 - par, lambda c: c.start())

    copies(i, par, lambda c: c.wait())

    def chain(u):
        rs = slice(u * tm, (u + 1) * tm)
        r_io = lax.broadcasted_iota(jnp.int32, (tm, rows), 1).astype(F32)
        d1 = route_ref[rs, 0:1]
        d2 = route_ref[rs, 1:2]
        p_t = jnp.where((r_io == d1) | (r_io == d2), 1.0, 0.0).astype(BF16)
        yield
        ff = _dot(p_t, ybuf[par * n_u + u])
        yield
        out_ref[rs, :] = _ln(DN_ALPHA * h2_ref[rs, :] + ff, g_ref[...], b_ref[...])

    _interleave([chain(u) for u in range(n_u)])


def _bias_kernel(tbl_ref, bkt_ref, out_ref):
    h = pl.program_id(0)
    far = tbl_ref[h, REL_BUCKETS - 1]
    for which in range(2):
        bk = bkt_ref[which]
        acc = jnp.where(bk < 0, NEG, 0.0)
        for kk in range(REL_BUCKETS):
            acc = jnp.where(bk == kk, tbl_ref[h, kk] - far, acc)
        out_ref[which, 0] = acc


def _rel_bucket_table(dist):
    max_exact = REL_BUCKETS // 2
    d = jnp.maximum(dist, 0)
    large = max_exact + (jnp.log(jnp.maximum(d, 1).astype(F32) / max_exact)
                         / math.log(REL_MAX_DIST / max_exact) * (REL_BUCKETS - max_exact)).astype(jnp.int32)
    large = jnp.minimum(large, REL_BUCKETS - 1)
    return jnp.where(d < max_exact, d, large)


def _const_spec(shape):
    nd = len(shape)
    return pl.BlockSpec(shape, lambda *_: (0,) * nd)


def kernel(x, mem, ln_in_g, ln_in_b, rel_bias, w_in, b_gate, w_pool_grp, pool_scale, w_pool_up, w_attn_up,
           w_mix_out, ln1_g, ln1_b, w_mq, w_mk, w_mv, w_mo, ln2_g, ln2_b, w_coarse, b_coarse, w_fine, b_fine,
           w_gate, w_up, w_down, ln3_g, ln3_b):
    B, S, _ = x.shape
    assert S == N_BLK * BLK and w_in.shape[0] == 1
    M = mem.shape[1]
    T = B * S
    tm = 512

    wi = w_in[0]
    w_u = wi[:, 0:POOL_W]
    w_q = wi[:, POOL_W:POOL_W + ATTN_W] * (HEAD_DIM ** -0.5)
    w_k = wi[:, POOL_W + ATTN_W:POOL_W + 2 * ATTN_W]
    w_v = wi[:, POOL_W + 2 * ATTN_W:POOL_W + 3 * ATTN_W]
    w_gl = wi[:, POOL_W + 3 * ATTN_W:]

    w1 = jnp.concatenate([w_u, w_q, w_k, w_v], axis=1).astype(BF16)
    row2 = lambda a: a.reshape(1, -1)

    iq = jnp.arange(BLK, dtype=jnp.int32)[:, None]
    ik = jnp.arange(BLK, dtype=jnp.int32)[None, :]
    d_own = iq - ik
    bkt = jnp.stack([jnp.where(d_own >= 0, _rel_bucket_table(d_own), -1), _rel_bucket_table(d_own + BLK)])
    t_bias = pl.pallas_call(
        _bias_kernel,
        grid=(N_HEADS,),
        in_specs=[pl.BlockSpec(memory_space=pltpu.SMEM), _const_spec((2, BLK, BLK))],
        out_specs=pl.BlockSpec((2, 1, BLK, BLK), lambda h: (0, h, 0, 0)),
        out_shape=jax.ShapeDtypeStruct((2, N_HEADS, BLK, BLK), F32),
        name="relbias_tiles",
    )(rel_bias.T, bkt)

    n_w1 = w1.shape[1]
    tmp = MERGE_TILE
    ypool, q_aug, k_aug, v_p = pl.pallas_call(
        functools.partial(_proj_kernel, tm=tmp),
        grid=(B, S // tmp),
        in_specs=[
            pl.BlockSpec((1, tmp, D), lambda b, s: (b, s, 0)),
            _const_spec((1, D)), _const_spec((1, D)),
            _const_spec((D, n_w1)),
            _const_spec((len(POOL_WINDOWS), LANE, LANE)),
            _const_spec((1, POOL_W)),
        ],
        out_specs=[
            pl.BlockSpec((1, tmp, POOL_W), lambda b, s: (b, s, 0)),
            pl.BlockSpec((1, N_HEADS, tmp, LANE), lambda b, s: (b, 0, s, 0)),
            pl.BlockSpec((1, N_HEADS, tmp, LANE), lambda b, s: (b, 0, s, 0)),
            pl.BlockSpec((1, ATTN_W // LANE, tmp, LANE), lambda b, s: (b, 0, s, 0)),
        ],
        out_shape=[
            jax.ShapeDtypeStruct((B, S, POOL_W), BF16),
            jax.ShapeDtypeStruct((B, N_HEADS, S, LANE), BF16),
            jax.ShapeDtypeStruct((B, N_HEADS, S, LANE), BF16),
            jax.ShapeDtypeStruct((B, ATTN_W // LANE, S, LANE), BF16),
        ],
        scratch_shapes=[pltpu.VMEM((HALO + tmp, POOL_W), F32), pltpu.VMEM((LANE, ATTN_W), F32)],
        compiler_params=pltpu.CompilerParams(dimension_semantics=("arbitrary", "arbitrary"),
                                             vmem_limit_bytes=VMEM_LIMIT),
        name="proj_pool_gate",
    )(x, row2(ln_in_g), row2(ln_in_b), w1, w_pool_grp[0].astype(BF16), row2(pool_scale[0]))

    o_attn = pl.pallas_call(
        _attn_kernel,
        grid=(B, N_BLK),
        in_specs=[
            pl.BlockSpec((1, N_HEADS, BLK, LANE), lambda b, j: (b, 0, j, 0)),
            pl.BlockSpec((1, N_HEADS, S, LANE), lambda b, j: (b, 0, 0, 0)),
            pl.BlockSpec((1, ATTN_W // LANE, S, LANE), lambda b, j: (b, 0, 0, 0)),
            _const_spec((2, N_HEADS, BLK, BLK)),
        ],
        out_specs=pl.BlockSpec((1, ATTN_W // LANE, BLK, LANE), lambda b, j: (b, 0, j, 0)),
        out_shape=jax.ShapeDtypeStruct((B, ATTN_W // LANE, S, LANE), BF16),
        compiler_params=pltpu.CompilerParams(dimension_semantics=("arbitrary", "arbitrary"),
                                             vmem_limit_bytes=VMEM_LIMIT),
        name="moba_attn",
    )(q_aug, k_aug, v_p, t_bias)

    kmem, vmem = pl.pallas_call(
        _memkv_kernel,
        grid=(B,),
        in_specs=[pl.BlockSpec((1, M, D), lambda b: (b, 0, 0)),
                  _const_spec((D, MEM_W)), _const_spec((D, MEM_W))],
        out_specs=[pl.BlockSpec((1, M, MEM_W), lambda b: (b, 0, 0)),
                   pl.BlockSpec((1, M, MEM_W), lambda b: (b, 0, 0))],
        out_shape=[jax.ShapeDtypeStruct((B, M, MEM_W), BF16)] * 2,
        compiler_params=pltpu.CompilerParams(dimension_semantics=("arbitrary",)),
        name="mem_kv",
    )(mem, w_mk[0].astype(BF16), w_mv[0].astype(BF16))

    w_r = jnp.concatenate([
        w_fine[0].reshape(D, N_EXPERTS),
        jnp.repeat(w_coarse[0], EPG, axis=1),
        jnp.zeros((D, LANE - 2 * N_EXPERTS), F32)], axis=1)
    b_r = jnp.concatenate([
        b_fine[0].reshape(N_EXPERTS), jnp.repeat(b_coarse[0], EPG),
        jnp.zeros((LANE - 2 * N_EXPERTS,), F32)]).reshape(1, LANE)
    w_r_hi = w_r.astype(BF16)
    w_r_lo = (w_r - w_r_hi.astype(F32)).astype(BF16)
    w_r_cat = jnp.concatenate([w_r_hi, w_r_lo], axis=1)

    n_sub = T // tm
    tmm = MERGE_TILE
    per = tmm // tm
    sub_idx = lambda b, s: b * (S // tmm) + s
    h2, h2b, comb, route, route_t, seg_cnt = pl.pallas_call(
        functools.partial(_merge_kernel, tm=tmm),
        grid=(B, S // tmm),
        in_specs=[
            pl.BlockSpec((1, tmm, D), lambda b, s: (b, s, 0)),
            pl.BlockSpec((1, tmm, POOL_W), lambda b, s: (b, s, 0)),
            pl.BlockSpec((1, ATTN_W // LANE, tmm, LANE), lambda b, s: (b, 0, s, 0)),
            pl.BlockSpec((1, M, MEM_W), lambda b, s: (b, 0, 0)),
            pl.BlockSpec((1, M, MEM_W), lambda b, s: (b, 0, 0)),
            _const_spec((1, D)), _const_spec((1, D)),
            _const_spec((D, 2 * D)), _const_spec((1, 2 * D)),
            _const_spec((POOL_W, D)), _const_spec((ATTN_W, D)), _const_spec((D, D)),
            _const_spec((1, D)), _const_spec((1, D)),
            _const_spec((D, MEM_W)), _const_spec((MEM_W, D)),
            _const_spec((1, D)), _const_spec((1, D)),
            _const_spec((D, 2 * LANE)), _const_spec((1, LANE)),
        ],
        out_specs=[
            pl.BlockSpec((1, tmm, D), lambda b, s: (b, s, 0)),
            pl.BlockSpec((1, tmm, D), lambda b, s: (b, s, 0)),
            pl.BlockSpec((1, tmm, LANE), lambda b, s: (b, s, 0)),
            pl.BlockSpec((1, tmm, LANE), lambda b, s: (b, s, 0)),
            pl.BlockSpec((per, 8, tm), lambda b, s: (sub_idx(b, s), 0, 0)),
            pl.BlockSpec((per, 8, LANE), lambda b, s: (sub_idx(b, s), 0, 0)),
        ],
        out_shape=[
            jax.ShapeDtypeStruct((B, S, D), F32),
            jax.ShapeDtypeStruct((B, S, D), BF16),
            jax.ShapeDtypeStruct((B, S, LANE), F32),
            jax.ShapeDtypeStruct((B, S, LANE), F32),
            jax.ShapeDtypeStruct((n_sub, 8, tm), F32),
            jax.ShapeDtypeStruct((n_sub, 8, LANE), F32),
        ],
        compiler_params=pltpu.CompilerParams(dimension_semantics=("arbitrary", "arbitrary"),
                                             vmem_limit_bytes=VMEM_LIMIT),
        name="merge_memattn_router",
    )(x, ypool, o_attn, kmem, vmem,
      row2(ln_in_g), row2(ln_in_b), w_gl.astype(BF16), row2(b_gate[0]),
      w_pool_up[0].astype(BF16), w_attn_up[0].astype(BF16), w_mix_out[0].astype(BF16),
      row2(ln1_g[0]), row2(ln1_b[0]),
      (w_mq[0] * (MEM_HD ** -0.5)).astype(BF16), w_mo[0].astype(BF16),
      row2(ln2_g[0]), row2(ln2_b[0]),
      w_r_cat, b_r)

    pcs = seg_cnt[:, 0, :N_EXPERTS].astype(jnp.int32)
    tot = jnp.sum(pcs, axis=0)
    cap = ((tot + FFN_TILE - 1) // FFN_TILE) * FFN_TILE
    ends = jnp.cumsum(cap)
    base = ends - cap
    gs = base[None, :] + jnp.cumsum(pcs, axis=0) - pcs
    ls = jnp.cumsum(pcs, axis=1) - pcs
    n_sorted = n_sub * COMPACT_ROWS + N_EXPERTS * FFN_TILE
    n_ffn_tiles = n_sorted // FFN_TILE
    n_used = (ends[-1] // FFN_TILE).astype(jnp.int32)
    tile_row = jnp.arange(n_ffn_tiles, dtype=jnp.int32) * FFN_TILE
    tile_exp = jnp.sum(jnp.minimum(tile_row, ends[-1] - 1)[:, None] >= ends[None, :], axis=1).astype(jnp.int32)
    tails = jnp.concatenate([base + tot, cap - tot]).astype(jnp.int32)
    def kth_of(counts, n_slots, seg_off):
        end = jnp.cumsum(counts, axis=1)
        k = jnp.arange(n_slots, dtype=jnp.int32)
        exp_k = jnp.sum(k[None, :, None] >= end[:, None, :], axis=2)
        hot = exp_k[:, :, None] == jnp.arange(N_EXPERTS)[None, None, :]
        pick = lambda tbl: jnp.sum(jnp.where(hot, tbl[:, None, :], 0), axis=2)
        within = k[None, :] - pick(end - counts)
        return pick(ls + seg_off), pick(gs + seg_off), within

    n_big = pcs // BIG_CHUNK
    n_end = (pcs // SEG_ALIGN) % 2
    lo_b, go_b, m_b = kth_of(n_big, MAX_BIG, 0)
    lo_e, go_e, _ = kth_of(n_end, N_EXPERTS, pcs - SEG_ALIGN)
    chunk_tab = jnp.concatenate([lo_b + BIG_CHUNK * m_b, go_b + BIG_CHUNK * m_b, lo_e, go_e],
                                axis=1).astype(jnp.int32).reshape(-1)
    n_chunks = jnp.stack([jnp.sum(n_big, axis=1), jnp.sum(n_end, axis=1)], axis=1).astype(jnp.int32).reshape(-1)

    aug_w = D + LANE
    n_u = MOE_SUBS_PER_STEP
    n_steps = n_sub // n_u
    x_sorted = pl.pallas_call(
        functools.partial(_dispatch_kernel, n_steps=n_steps, rows=COMPACT_ROWS),
        grid_spec=pltpu.PrefetchScalarGridSpec(
            num_scalar_prefetch=3,
            grid=(n_steps,),
            in_specs=[
                pl.BlockSpec((n_u * tm, D), lambda i, *_: (i, 0)),
                pl.BlockSpec((n_u * tm, LANE), lambda i, *_: (i, 0)),
                pl.BlockSpec((n_u, 8, tm), lambda i, *_: (i, 0, 0)),
            ],
            out_specs=pl.BlockSpec(memory_space=pl.ANY),
            scratch_shapes=[
                pltpu.VMEM((2 * n_u, COMPACT_ROWS, aug_w), BF16),
                pltpu.VMEM((FFN_TILE // 2, aug_w), BF16),
                pltpu.SemaphoreType.DMA((2 * n_u,)),
                pltpu.SemaphoreType.DMA(()),
            ],
        ),
        out_shape=jax.ShapeDtypeStruct((n_sorted, aug_w), BF16),
        compiler_params=pltpu.CompilerParams(dimension_semantics=("arbitrary",), vmem_limit_bytes=VMEM_LIMIT),
        name="moe_dispatch",
    )(n_chunks, chunk_tab, tails, h2b.reshape(T, D), comb.reshape(T, LANE), route_t)

    used_tile = lambda t, texp, nused: (jnp.minimum(t, nused[0] - 1), 0)
    y_sorted = pl.pallas_call(
        _ffn_kernel,
        grid_spec=pltpu.PrefetchScalarGridSpec(
            num_scalar_prefetch=2,
            grid=(n_ffn_tiles,),
            in_specs=[
                pl.BlockSpec(memory_space=pl.ANY),
                pl.BlockSpec((1, D, FF), lambda t, texp, nused: (texp[t], 0, 0)),
                pl.BlockSpec((1, D, FF), lambda t, texp, nused: (texp[t], 0, 0)),
                pl.BlockSpec((1, FF, D), lambda t, texp, nused: (texp[t], 0, 0)),
            ],
            out_specs=pl.BlockSpec((FFN_TILE, D), used_tile),
            scratch_shapes=[pltpu.VMEM((D, FF), BF16), pltpu.VMEM((D, FF), BF16), pltpu.VMEM((FF, D), BF16),
                            pltpu.VMEM((XS_DEPTH, FFN_TILE, aug_w), BF16), pltpu.SemaphoreType.DMA((XS_DEPTH,))],
        ),
        out_shape=jax.ShapeDtypeStruct((n_sorted, D), BF16),
        compiler_params=pltpu.CompilerParams(dimension_semantics=("arbitrary",), vmem_limit_bytes=VMEM_LIMIT),
        name="moe_expert_ffn",
    )(tile_exp, n_used.reshape(1), x_sorted, w_gate[0], w_up[0], w_down[0])

    out = pl.pallas_call(
        functools.partial(_combine_kernel, n_steps=n_steps, rows=COMPACT_ROWS),
        grid_spec=pltpu.PrefetchScalarGridSpec(
            num_scalar_prefetch=2,
            grid=(n_steps,),
            in_specs=[
                pl.BlockSpec((n_u * tm, LANE), lambda i, *_: (i, 0)),
                pl.BlockSpec((n_u * tm, D), lambda i, *_: (i, 0)),
                pl.BlockSpec((1, D), lambda i, *_: (0, 0)),
                pl.BlockSpec((1, D), lambda i, *_: (0, 0)),
                pl.BlockSpec(memory_space=pl.ANY),
            ],
            out_specs=pl.BlockSpec((n_u * tm, D), lambda i, *_: (i, 0)),
            scratch_shapes=[
                pltpu.VMEM((2 * n_u, COMPACT_ROWS, D), BF16),
                pltpu.SemaphoreType.DMA((2 * n_u,)),
            ],
        ),
        out_shape=jax.ShapeDtypeStruct((T, D), F32),
        compiler_params=pltpu.CompilerParams(dimension_semantics=("arbitrary",), vmem_limit_bytes=VMEM_LIMIT),
        name="moe_combine_ln3",
    )(n_chunks, chunk_tab, route.reshape(T, LANE), h2.reshape(T, D), row2(ln3_g[0]), row2(ln3_b[0]), y_sorted)
    return out.reshape(B, S, D)
```

```python
import functools
import math

import jax
import jax.numpy as jnp
from jax import lax
from jax.experimental import pallas as pl
from jax.experimental.pallas import tpu as pltpu

D = 1024
POOL_WINDOWS = (2, 4, 8, 16)
POOL_W = 512
N_HEADS = 8
HEAD_DIM = 64
ATTN_W = 512
BLK = 256
N_BLK = 8
TOPK = 3
REL_BUCKETS = 32
REL_MAX_DIST = 128
MEM_HEADS = 4
MEM_HD = 128
MEM_W = 512
N_GROUPS = 4
EPG = 8
N_EXPERTS = 32
FF = 256
DN_ALPHA = 2.0 ** 0.25
LN_EPS = 1e-5

LANE = 128
TOK_TILE = 512
SEG_ALIGN = 16
COMPACT_ROWS = 2 * TOK_TILE + N_EXPERTS * SEG_ALIGN
BIG_CHUNK = 2 * SEG_ALIGN
MAX_BIG = COMPACT_ROWS // BIG_CHUNK
CHUNK_TAB_W = 2 * MAX_BIG + 2 * N_EXPERTS
FFN_TILE = 512
PROJ_CHAIN = 256
ATTN_PAIRS_PER_TRIP = 2
MOE_SUBS_PER_STEP = 2
FFN_CHAIN = 512
XS_DEPTH = 3
MERGE_TILE = 1024
HALO = 16
NEG = -1e30
VMEM_LIMIT = 56 * 1024 * 1024

F32 = jnp.float32
BF16 = jnp.bfloat16

_NT = (((1,), (1,)), ((), ()))


def _dot(a, b):
    return jnp.dot(a, b, preferred_element_type=F32)


def _dot_nt(a, b):
    return lax.dot_general(a, b, _NT, preferred_element_type=F32)


def _split(a):
    hi = a.astype(BF16)
    lo = (a - hi.astype(F32)).astype(BF16)
    return hi, lo


def _interleave(chains):
    results = [None] * len(chains)
    live = list(range(len(chains)))
    while live:
        for ci in list(live):
            try:
                next(chains[ci])
            except StopIteration as done:
                results[ci] = done.value
                live.remove(ci)
    return results


def _ln(x, g, b):
    mu = jnp.mean(x, axis=-1, keepdims=True)
    xc = x - mu
    var = jnp.mean(xc * xc, axis=-1, keepdims=True)
    return xc * lax.rsqrt(var + LN_EPS) * g + b


def _proj_kernel(x_ref, g_ref, b_ref, w_ref, wgrp_ref, pscale_ref,
                 ypool_ref, q_ref, k_ref, v_ref, ubuf, kbt, *, tm):
    s = pl.program_id(1)

    @pl.when(s == 0)
    def _():
        ubuf[0:HALO, :] = jnp.zeros((HALO, POOL_W), F32)
        kbt[...] = jnp.zeros_like(kbt)

    _interleave([_proj_chain(ci, s, tm, x_ref, g_ref, b_ref, w_ref, wgrp_ref, pscale_ref,
                             ypool_ref, q_ref, k_ref, v_ref, ubuf, kbt)
                 for ci in range(tm // PROJ_CHAIN)])
    ubuf[0:HALO, :] = ubuf[tm:tm + HALO, :]


def _proj_chain(ci, s, tm, x_ref, g_ref, b_ref, w_ref, wgrp_ref, pscale_ref,
                ypool_ref, q_ref, k_ref, v_ref, ubuf, kbt):
    n = PROJ_CHAIN
    r0 = ci * n
    rows = slice(r0, r0 + n)
    blk0 = s * (tm // BLK) + ci * (n // BLK)

    h = _ln(x_ref[0, rows, :], g_ref[...], b_ref[...])
    hb = h.astype(BF16)
    zu = _dot(hb, w_ref[:, 0:POOL_W])
    zq = _dot(hb, w_ref[:, POOL_W:POOL_W + ATTN_W])
    zk = _dot(hb, w_ref[:, POOL_W + ATTN_W:POOL_W + 2 * ATTN_W])
    zv = _dot(hb, w_ref[:, POOL_W + 2 * ATTN_W:])
    ubuf[HALO + r0:HALO + r0 + n, :] = zu

    r_io = lax.broadcasted_iota(jnp.int32, kbt.shape, 0)
    c_io = lax.broadcasted_iota(jnp.int32, kbt.shape, 1)
    head_match = (r_io >> 3) == (c_io >> 6)
    for bi in range(n // BLK):
        kmean = jnp.mean(zk[bi * BLK:(bi + 1) * BLK], axis=0, keepdims=True)
        kbt[...] = jnp.where(head_match & ((r_io & 7) == blk0 + bi), kmean, kbt[...])
    yield

    t_pos = s * tm + r0 + lax.broadcasted_iota(jnp.int32, (n, LANE), 0)
    for g, w in enumerate(POOL_WINDOWS):
        cols = slice(g * LANE, (g + 1) * LANE)
        ws = ubuf[HALO + r0:HALO + r0 + n, cols]
        for kk in range(1, w):
            ws = ws + ubuf[HALO + r0 - kk:HALO + r0 - kk + n, cols]
        cnt = jnp.minimum(t_pos + 1, w).astype(F32)
        y = ws / cnt - ubuf[HALO + r0:HALO + r0 + n, cols]
        yg = _dot(y.astype(BF16), wgrp_ref[g]) * pscale_ref[:, cols]
        ypool_ref[0, rows, cols] = yg.astype(ypool_ref.dtype)
    yield

    q_hi, q_lo = _split(zq)
    kb_hi, kb_lo = _split(kbt[...])
    g2 = _dot_nt(q_hi, jnp.concatenate([kb_hi, kb_lo], axis=0))
    gate = g2[:, 0:LANE] + g2[:, LANE:2 * LANE] + _dot_nt(q_lo, kb_hi)
    yield

    lane = lax.broadcasted_iota(jnp.int32, (n, LANE), 1)
    row = lax.broadcasted_iota(jnp.int32, (n, LANE), 0)
    n_l = lane & 7
    jrow = blk0 + (row >> 8)
    past = n_l < jrow
    gt = jnp.where(past, gate, -jnp.inf)
    cnt = jnp.zeros((n, LANE), F32)
    for sh in range(1, N_BLK):
        wrap = (n_l + sh) >= N_BLK
        gm = jnp.where(wrap, pltpu.roll(gt, N_BLK - sh, 1), pltpu.roll(gt, LANE - sh, 1))
        cnt = cnt + jnp.where(wrap, jnp.where(gm >= gt, 1.0, 0.0), jnp.where(gm > gt, 1.0, 0.0))
    keep = (past & (cnt < TOPK)) | (n_l == jrow)
    negmask = jnp.where(keep, 0.0, NEG)

    aug_lane = (lane >= HEAD_DIM) & (lane < HEAD_DIM + N_BLK)
    k_onehot = jnp.where(lane == HEAD_DIM + jrow, 1.0, 0.0)
    head_lane = lane < HEAD_DIM
    for hh in range(N_HEADS):
        cols = slice((hh // 2) * LANE, (hh // 2 + 1) * LANE)
        q_h, k_h = zq[:, cols], zk[:, cols]
        if hh % 2:
            q_h, k_h = pltpu.roll(q_h, HEAD_DIM, 1), pltpu.roll(k_h, HEAD_DIM, 1)
        m_h = jnp.where(aug_lane, pltpu.roll(negmask, HEAD_DIM - N_BLK * hh, 1), 0.0)
        q_ref[0, hh, rows, :] = jnp.where(head_lane, q_h, m_h).astype(q_ref.dtype)
        k_ref[0, hh, rows, :] = jnp.where(head_lane, k_h, k_onehot).astype(k_ref.dtype)
    for p in range(ATTN_W // LANE):
        v_ref[0, p, rows, :] = zv[:, p * LANE:(p + 1) * LANE].astype(v_ref.dtype)


def _attn_kernel(q_ref, k_ref, v_ref, tb_ref, o_ref):
    j = pl.program_id(1)
    lane = lax.broadcasted_iota(jnp.int32, (BLK, LANE), 1)

    def one_head(h, p, jj):
        q = q_ref[0, h]
        own0 = jj * BLK
        pieces = []
        s_own = _dot_nt(q, k_ref[0, h, own0:own0 + BLK, :]) + tb_ref[0, h]
        pieces.append((s_own, own0, BLK))
        if jj >= 1:
            s_adj = _dot_nt(q, k_ref[0, h, own0 - BLK:own0, :]) + tb_ref[1, h]
            pieces.append((s_adj, own0 - BLK, BLK))
        if jj >= 2:
            s_far = _dot_nt(q, k_ref[0, h, 0:own0 - BLK, :])
            pieces.append((s_far, 0, own0 - BLK))
        yield
        m = None
        for sc, _, _ in pieces:
            mm = jnp.max(sc, axis=1, keepdims=True)
            m = mm if m is None else jnp.maximum(m, mm)
        l = None
        probs = []
        for sc, start, size in pieces:
            e = jnp.exp(sc - m)
            ls = jnp.sum(e, axis=1, keepdims=True)
            l = ls if l is None else l + ls
            probs.append((e.astype(BF16), start, size))
        yield
        acc = None
        for pb, start, size in probs:
            pv = _dot(pb, v_ref[0, p, start:start + size, :])
            acc = pv if acc is None else acc + pv
        return acc / l

    for jj in range(N_BLK):
        @pl.when(j == jj)
        def _(jj=jj):
            def group(gi, carry):
                pairs = [gi * ATTN_PAIRS_PER_TRIP + pi for pi in range(ATTN_PAIRS_PER_TRIP)]
                outs = _interleave([one_head(2 * p + hh, p, jj) for p in pairs for hh in range(2)])
                for pi, p in enumerate(pairs):
                    o_ref[0, p] = jnp.where(lane < HEAD_DIM, outs[2 * pi], outs[2 * pi + 1]).astype(o_ref.dtype)
                return carry
            lax.fori_loop(0, N_HEADS // 2 // ATTN_PAIRS_PER_TRIP, group, 0)


def _memkv_kernel(mem_ref, wk_ref, wv_ref, k_ref, v_ref):
    mb = mem_ref[0].astype(BF16)
    k_ref[0] = _dot(mb, wk_ref[...]).astype(k_ref.dtype)
    v_ref[0] = _dot(mb, wv_ref[...]).astype(v_ref.dtype)


def _merge_kernel(x_ref, ypool_ref, o_ref, kmem_ref, vmem_ref,
                  lng_ref, lnb_ref, wgl_ref, bgate_ref, wpu_ref, wau_ref, wout_ref,
                  ln1g_ref, ln1b_ref, wmq_ref, wmo_ref, ln2g_ref, ln2b_ref,
                  wrh_ref, br_ref,
                  h2_ref, h2b_ref, comb_ref, route_ref, route_t_ref, cnt_ref, *, tm):
    _interleave([_merge_chain(ci, x_ref, ypool_ref, o_ref, kmem_ref, vmem_ref,
                              lng_ref, lnb_ref, wgl_ref, bgate_ref, wpu_ref, wau_ref, wout_ref,
                              ln1g_ref, ln1b_ref, wmq_ref, wmo_ref, ln2g_ref, ln2b_ref,
                              wrh_ref, br_ref, h2_ref, h2b_ref, comb_ref, route_ref, route_t_ref, cnt_ref)
                 for ci in range(tm // TOK_TILE)])


def _merge_chain(ci, x_ref, ypool_ref, o_ref, kmem_ref, vmem_ref,
                 lng_ref, lnb_ref, wgl_ref, bgate_ref, wpu_ref, wau_ref, wout_ref,
                 ln1g_ref, ln1b_ref, wmq_ref, wmo_ref, ln2g_ref, ln2b_ref,
                 wrh_ref, br_ref, h2_ref, h2b_ref, comb_ref, route_ref, route_t_ref, cnt_ref):
    rows = slice(ci * TOK_TILE, (ci + 1) * TOK_TILE)
    n_rows = TOK_TILE
    h = _ln(x_ref[0, rows, :], lng_ref[...], lnb_ref[...])
    hb = h.astype(BF16)
    gl = _dot(hb, wgl_ref[...]) + bgate_ref[...]
    yield
    gates = 0.5 * jnp.tanh(0.5 * gl) + 0.5
    y_pool = _dot(ypool_ref[0, rows, :], wpu_ref[...])
    o_cat = jnp.concatenate([o_ref[0, p, rows, :] for p in range(ATTN_W // LANE)], axis=1)
    y_attn = _dot(o_cat, wau_ref[...])
    yield
    merged = gates[:, 0:D] * y_pool + gates[:, D:2 * D] * y_attn
    mix = _dot(merged.astype(BF16), wout_ref[...])
    yield
    h1 = _ln(DN_ALPHA * h + mix, ln1g_ref[...], ln1b_ref[...])

    qm = _dot(h1.astype(BF16), wmq_ref[...]).astype(BF16)
    yield
    outs = []
    for hd in range(MEM_HEADS):
        cols = slice(hd * MEM_HD, (hd + 1) * MEM_HD)
        sc = _dot_nt(qm[:, cols], kmem_ref[0, :, cols])
        m = jnp.max(sc, axis=1, keepdims=True)
        e = jnp.exp(sc - m)
        l = jnp.sum(e, axis=1, keepdims=True)
        outs.append(_dot(e.astype(BF16), vmem_ref[0, :, cols]) / l)
    om = jnp.concatenate(outs, axis=1).astype(BF16)
    xa = _dot(om, wmo_ref[...])
    yield
    h2 = _ln(DN_ALPHA * h1 + xa, ln2g_ref[...], ln2b_ref[...])
    h2_ref[0, rows, :] = h2
    h2b_ref[0, rows, :] = h2.astype(BF16)

    x_hi, x_lo = _split(h2)
    r2 = _dot(x_hi, wrh_ref[...])
    r = r2[:, 0:LANE] + r2[:, LANE:2 * LANE] + _dot(x_lo, wrh_ref[:, 0:LANE]) + br_ref[...]
    yield
    lane = lax.broadcasted_iota(jnp.int32, (n_rows, LANE), 1)
    lane_f = lane.astype(F32)
    cmask = (lane >= N_EXPERTS) & (lane < 2 * N_EXPERTS)
    c = jnp.where(cmask, r, -jnp.inf)
    cmax = jnp.max(c, axis=1, keepdims=True)
    ce = jnp.exp(c - cmax)
    csum = jnp.sum(ce, axis=1, keepdims=True) * (1.0 / EPG)
    g_prob = 1.0 / csum
    grp_lane = ((lane & (N_EXPERTS - 1)) >> 3).astype(F32)
    gidx = jnp.min(jnp.where(cmask & (c == cmax), grp_lane, 99.0), axis=1, keepdims=True)
    fmask = (lane < N_EXPERTS) & (grp_lane == gidx)
    f = jnp.where(fmask, r, -jnp.inf)
    fmax = jnp.max(f, axis=1, keepdims=True)
    fe = jnp.exp(f - fmax)
    fsum = jnp.sum(fe, axis=1, keepdims=True)
    prob = fe / fsum
    p1 = jnp.max(prob, axis=1, keepdims=True)
    i1 = jnp.min(jnp.where(fmask & (prob == p1), lane_f, 999.0), axis=1, keepdims=True)
    rest = fmask & (lane_f != i1)
    prob2 = jnp.where(rest, prob, -1.0)
    p2 = jnp.max(prob2, axis=1, keepdims=True)
    i2 = jnp.min(jnp.where(rest & (prob2 == p2), lane_f, 999.0), axis=1, keepdims=True)
    den = p1 + p2
    comb = jnp.where(lane_f == i1, g_prob * (p1 / den),
                     jnp.where(lane_f == i2, g_prob * (p2 / den), 0.0))
    comb_ref[0, rows, :] = comb
    yield

    sel = jnp.where((lane_f == i1) | (lane_f == i2), 1.0, 0.0)
    cnt = jnp.sum(sel, axis=0, keepdims=True)
    pc = jnp.floor((cnt + (SEG_ALIGN - 1)) * (1.0 / SEG_ALIGN)) * SEG_ALIGN
    lane8 = lax.broadcasted_iota(jnp.int32, (8, LANE), 1)
    inc = jnp.broadcast_to(pc, (8, LANE))
    for sh in (1, 2, 4, 8, 16, 32, 64):
        inc = inc + jnp.where(lane8 >= sh, pltpu.roll(inc, sh, 1), 0.0)
    seg_start = inc[0:1] - pc
    t_row = lax.broadcasted_iota(jnp.int32, (n_rows, n_rows), 0)
    t_col = lax.broadcasted_iota(jnp.int32, (n_rows, n_rows), 1)
    earlier = jnp.where(t_row > t_col, 1.0, 0.0).astype(BF16)
    rank = _dot(earlier, sel.astype(BF16))
    pos = seg_start + rank
    d1 = jnp.sum(jnp.where(lane_f == i1, pos, 0.0), axis=1, keepdims=True)
    d2 = jnp.sum(jnp.where(lane_f == i2, pos, 0.0), axis=1, keepdims=True)
    route = jnp.where(lane == 0, d1, jnp.where(lane == 1, d2, 0.0))
    route_ref[0, rows, :] = route
    r_hi, r_lo = _split(route)
    eye = jnp.where(lax.broadcasted_iota(jnp.int32, (8, LANE), 0) == lane8, 1.0, 0.0).astype(BF16)
    route_t_ref[ci] = _dot_nt(eye, r_hi) + _dot_nt(eye, r_lo)
    cnt_ref[ci] = jnp.broadcast_to(pc, (8, LANE))


def _chunk_copies(sub, nch_ref, tab_ref, make_copy, act):
    base = sub * CHUNK_TAB_W
    for size, cnt_i, lo_off, go_off in ((BIG_CHUNK, 0, 0, MAX_BIG), (SEG_ALIGN, 1, 2 * MAX_BIG, 2 * MAX_BIG + N_EXPERTS)):
        def body(k, carry, size=size, lo_off=lo_off, go_off=go_off):
            lo = pl.multiple_of(tab_ref[base + lo_off + k], SEG_ALIGN)
            go = pl.multiple_of(tab_ref[base + go_off + k], SEG_ALIGN)
            act(make_copy(lo, go, size))
            return carry
        lax.fori_loop(0, nch_ref[2 * sub + cnt_i], body, 0)


def _dispatch_kernel(nch_ref, tab_ref, tail_ref, xb_ref, comb_ref, route_t_ref, xs_ref,
                     cbuf, zbuf, sem, zsem, *, n_steps, rows):
    i = pl.program_id(0)
    par = lax.rem(i, 2)
    n_u = MOE_SUBS_PER_STEP
    tm = TOK_TILE

    def copies(step, par_, act):
        for u in range(n_u):
            slot_ = par_ * n_u + u

            def mk(lo, go, size, slot_=slot_):
                return pltpu.make_async_copy(cbuf.at[slot_, pl.ds(lo, size), :],
                                             xs_ref.at[pl.ds(go, size), :], sem.at[slot_])
            _chunk_copies(step * n_u + u, nch_ref, tab_ref, mk, act)

    @pl.when(i >= 2)
    def _():
        copies(i - 2, par, lambda c: c.wait())

    def chain(u):
        rs = slice(u * tm, (u + 1) * tm)
        r_io = lax.broadcasted_iota(jnp.int32, (rows, tm), 0).astype(F32)
        d1 = route_t_ref[u, 0:1, :]
        d2 = route_t_ref[u, 1:2, :]
        p_mat = jnp.where((r_io == d1) | (r_io == d2), 1.0, 0.0).astype(BF16)
        comb = comb_ref[rs, :]
        c_hi = comb.astype(BF16).astype(F32)
        c_pack = (c_hi + pltpu.roll(comb - c_hi, N_EXPERTS, 1)).astype(BF16)
        x_aug = jnp.concatenate([xb_ref[rs, :], c_pack], axis=1)
        yield
        cbuf[par * n_u + u] = _dot(p_mat, x_aug).astype(BF16)

    _interleave([chain(u) for u in range(n_u)])
    copies(i, par, lambda c: c.start())

    @pl.when(i == n_steps - 1)
    def _():
        if n_steps >= 2:
            copies(i - 1, 1 - par, lambda c: c.wait())
        copies(i, par, lambda c: c.wait())
        zbuf[...] = jnp.zeros_like(zbuf)

        def tails(act):
            def body(e, carry):
                st = tail_ref[e]
                n = tail_ref[N_EXPERTS + e]
                off = jnp.int32(0)
                size = zbuf.shape[0]
                while size >= SEG_ALIGN:
                    bit = (n & size) != 0

                    @pl.when(bit)
                    def _(size=size, off=off):
                        act(pltpu.make_async_copy(
                            zbuf.at[pl.ds(0, size), :],
                            xs_ref.at[pl.ds(pl.multiple_of(st + off, SEG_ALIGN), size), :], zsem))
                    off = off + jnp.where(bit, size, 0)
                    size //= 2
                return carry
            lax.fori_loop(0, N_EXPERTS, body, 0)
        tails(lambda c: c.start())
        tails(lambda c: c.wait())


def _ffn_kernel(texp_ref, nused_ref, xs_ref, wg_ref, wu_ref, wd_ref, ys_ref, wg_b, wu_b, wd_b, xbuf, xsem):
    t = pl.program_id(0)
    e = texp_ref[t]
    n_used = nused_ref[0]

    def fetch(u):
        slot = lax.rem(u, XS_DEPTH)
        return pltpu.make_async_copy(xs_ref.at[pl.ds(pl.multiple_of(u * FFN_TILE, FFN_TILE), FFN_TILE), :],
                                     xbuf.at[slot], xsem.at[slot])

    @pl.when(t == 0)
    def _():
        for u in range(XS_DEPTH - 1):
            @pl.when(u < n_used)
            def _(u=u):
                fetch(jnp.int32(u)).start()

    @pl.when(t + (XS_DEPTH - 1) < n_used)
    def _():
        fetch(t + (XS_DEPTH - 1)).start()

    @pl.when((t == 0) | (e != texp_ref[jnp.maximum(t - 1, 0)]))
    def _():
        wg_b[...] = wg_ref[0].astype(BF16)
        wu_b[...] = wu_ref[0].astype(BF16)
        wd_b[...] = wd_ref[0].astype(BF16)

    @pl.when(t < n_used)
    def _():
        fetch(t).wait()
        slot = lax.rem(t, XS_DEPTH)

        def chain(ci):
            rs = slice(ci * FFN_CHAIN, (ci + 1) * FFN_CHAIN)
            xrow = xbuf[slot, rs, 0:D]
            cw = xbuf[slot, rs, D:D + LANE].astype(F32)
            lane = lax.broadcasted_iota(jnp.int32, cw.shape, 1)
            c = jnp.sum(jnp.where((lane == e) | (lane == e + N_EXPERTS), cw, 0.0), axis=1, keepdims=True)
            a = _dot(xrow, wg_b[...])
            b = _dot(xrow, wu_b[...])
            yield
            hid = ((a * jax.nn.sigmoid(a)) * b * c).astype(BF16)
            yield
            ys_ref[rs, :] = _dot(hid, wd_b[...]).astype(ys_ref.dtype)

        _interleave([chain(ci) for ci in range(FFN_TILE // FFN_CHAIN)])


def _combine_kernel(nch_ref, tab_ref, route_ref, h2_ref, g_ref, b_ref, ys_ref, out_ref,
                    ybuf, sem, *, n_steps, rows):
    i = pl.program_id(0)
    par = lax.rem(i, 2)
    n_u = MOE_SUBS_PER_STEP
    tm = TOK_TILE

    def copies(step, par_, act):
        for u in range(n_u):
            slot_ = par_ * n_u + u

            def mk(lo, go, size, slot_=slot_):
                return pltpu.make_async_copy(ys_ref.at[pl.ds(go, size), :],
                                             ybuf.at[slot_, pl.ds(lo, size), :], sem.at[slot_])
            _chunk_copies(step * n_u + u, nch_ref, tab_ref, mk, act)

    @pl.when(i == 0)
    def _():
        ybuf[...] = jnp.zeros_like(ybuf)
        copies(0, 0, lambda c: c.start())

    @pl.when(i + 1 < n_steps)
    def _():
        copies(i + 1, 1 - par, lambda c: c.start())

    copies(i, par, lambda c: c.wait())

    def chain(u):
        rs = slice(u * tm, (u + 1) * tm)
        r_io = lax.broadcasted_iota(jnp.int32, (tm, rows), 1).astype(F32)
        d1 = route_ref[rs, 0:1]
        d2 = route_ref[rs, 1:2]
        p_t = jnp.where((r_io == d1) | (r_io == d2), 1.0, 0.0).astype(BF16)
        yield
        ff = _dot(p_t, ybuf[par * n_u + u])
        yield
        out_ref[rs, :] = _ln(DN_ALPHA * h2_ref[rs, :] + ff, g_ref[...], b_ref[...])

    _interleave([chain(u) for u in range(n_u)])


def _bias_kernel(tbl_ref, bkt_ref, out_ref):
    h = pl.program_id(0)
    far = tbl_ref[h, REL_BUCKETS - 1]
    for which in range(2):
        bk = bkt_ref[which]
        acc = jnp.where(bk < 0, NEG, 0.0)
        for kk in range(REL_BUCKETS):
            acc = jnp.where(bk == kk, tbl_ref[h, kk] - far, acc)
        out_ref[which, 0] = acc


def _rel_bucket_table(dist):
    max_exact = REL_BUCKETS // 2
    d = jnp.maximum(dist, 0)
    large = max_exact + (jnp.log(jnp.maximum(d, 1).astype(F32) / max_exact)
                         / math.log(REL_MAX_DIST / max_exact) * (REL_BUCKETS - max_exact)).astype(jnp.int32)
    large = jnp.minimum(large, REL_BUCKETS - 1)
    return jnp.where(d < max_exact, d, large)


def _const_spec(shape):
    nd = len(shape)
    return pl.BlockSpec(shape, lambda *_: (0,) * nd)


def kernel(x, mem, ln_in_g, ln_in_b, rel_bias, w_in, b_gate, w_pool_grp, pool_scale, w_pool_up, w_attn_up,
           w_mix_out, ln1_g, ln1_b, w_mq, w_mk, w_mv, w_mo, ln2_g, ln2_b, w_coarse, b_coarse, w_fine, b_fine,
           w_gate, w_up, w_down, ln3_g, ln3_b):
    B, S, _ = x.shape
    assert S == N_BLK * BLK and w_in.shape[0] == 1
    M = mem.shape[1]
    T = B * S
    tm = 512

    wi = w_in[0]
    w_u = wi[:, 0:POOL_W]
    w_q = wi[:, POOL_W:POOL_W + ATTN_W] * (HEAD_DIM ** -0.5)
    w_k = wi[:, POOL_W + ATTN_W:POOL_W + 2 * ATTN_W]
    w_v = wi[:, POOL_W + 2 * ATTN_W:POOL_W + 3 * ATTN_W]
    w_gl = wi[:, POOL_W + 3 * ATTN_W:]

    w1 = jnp.concatenate([w_u, w_q, w_k, w_v], axis=1).astype(BF16)
    row2 = lambda a: a.reshape(1, -1)

    iq = jnp.arange(BLK, dtype=jnp.int32)[:, None]
    ik = jnp.arange(BLK, dtype=jnp.int32)[None, :]
    d_own = iq - ik
    bkt = jnp.stack([jnp.where(d_own >= 0, _rel_bucket_table(d_own), -1), _rel_bucket_table(d_own + BLK)])
    t_bias = pl.pallas_call(
        _bias_kernel,
        grid=(N_HEADS,),
        in_specs=[pl.BlockSpec(memory_space=pltpu.SMEM), _const_spec((2, BLK, BLK))],
        out_specs=pl.BlockSpec((2, 1, BLK, BLK), lambda h: (0, h, 0, 0)),
        out_shape=jax.ShapeDtypeStruct((2, N_HEADS, BLK, BLK), F32),
        name="relbias_tiles",
    )(rel_bias.T, bkt)

    n_w1 = w1.shape[1]
    tmp = MERGE_TILE
    ypool, q_aug, k_aug, v_p = pl.pallas_call(
        functools.partial(_proj_kernel, tm=tmp),
        grid=(B, S // tmp),
        in_specs=[
            pl.BlockSpec((1, tmp, D), lambda b, s: (b, s, 0)),
            _const_spec((1, D)), _const_spec((1, D)),
            _const_spec((D, n_w1)),
            _const_spec((len(POOL_WINDOWS), LANE, LANE)),
            _const_spec((1, POOL_W)),
        ],
        out_specs=[
            pl.BlockSpec((1, tmp, POOL_W), lambda b, s: (b, s, 0)),
            pl.BlockSpec((1, N_HEADS, tmp, LANE), lambda b, s: (b, 0, s, 0)),
            pl.BlockSpec((1, N_HEADS, tmp, LANE), lambda b, s: (b, 0, s, 0)),
            pl.BlockSpec((1, ATTN_W // LANE, tmp, LANE), lambda b, s: (b, 0, s, 0)),
        ],
        out_shape=[
            jax.ShapeDtypeStruct((B, S, POOL_W), BF16),
            jax.ShapeDtypeStruct((B, N_HEADS, S, LANE), BF16),
            jax.ShapeDtypeStruct((B, N_HEADS, S, LANE), BF16),
            jax.ShapeDtypeStruct((B, ATTN_W // LANE, S, LANE), BF16),
        ],
        scratch_shapes=[pltpu.VMEM((HALO + tmp, POOL_W), F32), pltpu.VMEM((LANE, ATTN_W), F32)],
        compiler_params=pltpu.CompilerParams(dimension_semantics=("arbitrary", "arbitrary"),
                                             vmem_limit_bytes=VMEM_LIMIT),
        name="proj_pool_gate",
    )(x, row2(ln_in_g), row2(ln_in_b), w1, w_pool_grp[0].astype(BF16), row2(pool_scale[0]))

    o_attn = pl.pallas_call(
        _attn_kernel,
        grid=(B, N_BLK),
        in_specs=[
            pl.BlockSpec((1, N_HEADS, BLK, LANE), lambda b, j: (b, 0, j, 0)),
            pl.BlockSpec((1, N_HEADS, S, LANE), lambda b, j: (b, 0, 0, 0)),
            pl.BlockSpec((1, ATTN_W // LANE, S, LANE), lambda b, j: (b, 0, 0, 0)),
            _const_spec((2, N_HEADS, BLK, BLK)),
        ],
        out_specs=pl.BlockSpec((1, ATTN_W // LANE, BLK, LANE), lambda b, j: (b, 0, j, 0)),
        out_shape=jax.ShapeDtypeStruct((B, ATTN_W // LANE, S, LANE), BF16),
        compiler_params=pltpu.CompilerParams(dimension_semantics=("arbitrary", "arbitrary"),
                                             vmem_limit_bytes=VMEM_LIMIT),
        name="moba_attn",
    )(q_aug, k_aug, v_p, t_bias)

    kmem, vmem = pl.pallas_call(
        _memkv_kernel,
        grid=(B,),
        in_specs=[pl.BlockSpec((1, M, D), lambda b: (b, 0, 0)),
                  _const_spec((D, MEM_W)), _const_spec((D, MEM_W))],
        out_specs=[pl.BlockSpec((1, M, MEM_W), lambda b: (b, 0, 0)),
                   pl.BlockSpec((1, M, MEM_W), lambda b: (b, 0, 0))],
        out_shape=[jax.ShapeDtypeStruct((B, M, MEM_W), BF16)] * 2,
        compiler_params=pltpu.CompilerParams(dimension_semantics=("arbitrary",)),
        name="mem_kv",
    )(mem, w_mk[0].astype(BF16), w_mv[0].astype(BF16))

    w_r = jnp.concatenate([
        w_fine[0].reshape(D, N_EXPERTS),
        jnp.repeat(w_coarse[0], EPG, axis=1),
        jnp.zeros((D, LANE - 2 * N_EXPERTS), F32)], axis=1)
    b_r = jnp.concatenate([
        b_fine[0].reshape(N_EXPERTS), jnp.repeat(b_coarse[0], EPG),
        jnp.zeros((LANE - 2 * N_EXPERTS,), F32)]).reshape(1, LANE)
    w_r_hi = w_r.astype(BF16)
    w_r_lo = (w_r - w_r_hi.astype(F32)).astype(BF16)
    w_r_cat = jnp.concatenate([w_r_hi, w_r_lo], axis=1)

    n_sub = T // tm
    tmm = MERGE_TILE
    per = tmm // tm
    sub_idx = lambda b, s: b * (S // tmm) + s
    h2, h2b, comb, route, route_t, seg_cnt = pl.pallas_call(
        functools.partial(_merge_kernel, tm=tmm),
        grid=(B, S // tmm),
        in_specs=[
            pl.BlockSpec((1, tmm, D), lambda b, s: (b, s, 0)),
            pl.BlockSpec((1, tmm, POOL_W), lambda b, s: (b, s, 0)),
            pl.BlockSpec((1, ATTN_W // LANE, tmm, LANE), lambda b, s: (b, 0, s, 0)),
            pl.BlockSpec((1, M, MEM_W), lambda b, s: (b, 0, 0)),
            pl.BlockSpec((1, M, MEM_W), lambda b, s: (b, 0, 0)),
            _const_spec((1, D)), _const_spec((1, D)),
            _const_spec((D, 2 * D)), _const_spec((1, 2 * D)),
            _const_spec((POOL_W, D)), _const_spec((ATTN_W, D)), _const_spec((D, D)),
            _const_spec((1, D)), _const_spec((1, D)),
            _const_spec((D, MEM_W)), _const_spec((MEM_W, D)),
            _const_spec((1, D)), _const_spec((1, D)),
            _const_spec((D, 2 * LANE)), _const_spec((1, LANE)),
        ],
        out_specs=[
            pl.BlockSpec((1, tmm, D), lambda b, s: (b, s, 0)),
            pl.BlockSpec((1, tmm, D), lambda b, s: (b, s, 0)),
            pl.BlockSpec((1, tmm, LANE), lambda b, s: (b, s, 0)),
            pl.BlockSpec((1, tmm, LANE), lambda b, s: (b, s, 0)),
            pl.BlockSpec((per, 8, tm), lambda b, s: (sub_idx(b, s), 0, 0)),
            pl.BlockSpec((per, 8, LANE), lambda b, s: (sub_idx(b, s), 0, 0)),
        ],
        out_shape=[
            jax.ShapeDtypeStruct((B, S, D), F32),
            jax.ShapeDtypeStruct((B, S, D), BF16),
            jax.ShapeDtypeStruct((B, S, LANE), F32),
            jax.ShapeDtypeStruct((B, S, LANE), F32),
            jax.ShapeDtypeStruct((n_sub, 8, tm), F32),
            jax.ShapeDtypeStruct((n_sub, 8, LANE), F32),
        ],
        compiler_params=pltpu.CompilerParams(dimension_semantics=("arbitrary", "arbitrary"),
                                             vmem_limit_bytes=VMEM_LIMIT),
        name="merge_memattn_router",
    )(x, ypool, o_attn, kmem, vmem,
      row2(ln_in_g), row2(ln_in_b), w_gl.astype(BF16), row2(b_gate[0]),
      w_pool_up[0].astype(BF16), w_attn_up[0].astype(BF16), w_mix_out[0].astype(BF16),
      row2(ln1_g[0]), row2(ln1_b[0]),
      (w_mq[0] * (MEM_HD ** -0.5)).astype(BF16), w_mo[0].astype(BF16),
      row2(ln2_g[0]), row2(ln2_b[0]),
      w_r_cat, b_r)

    pcs = seg_cnt[:, 0, :N_EXPERTS].astype(jnp.int32)
    tot = jnp.sum(pcs, axis=0)
    cap = ((tot + FFN_TILE - 1) // FFN_TILE) * FFN_TILE
    ends = jnp.cumsum(cap)
    base = ends - cap
    gs = base[None, :] + jnp.cumsum(pcs, axis=0) - pcs
    ls = jnp.cumsum(pcs, axis=1) - pcs
    n_sorted = n_sub * COMPACT_ROWS + N_EXPERTS * FFN_TILE
    n_ffn_tiles = n_sorted // FFN_TILE
    n_used = (ends[-1] // FFN_TILE).astype(jnp.int32)
    tile_row = jnp.arange(n_ffn_tiles, dtype=jnp.int32) * FFN_TILE
    tile_exp = jnp.sum(jnp.minimum(tile_row, ends[-1] - 1)[:, None] >= ends[None, :], axis=1).astype(jnp.int32)
    tails = jnp.concatenate([base + tot, cap - tot]).astype(jnp.int32)
    def kth_of(counts, n_slots, seg_off):
        end = jnp.cumsum(counts, axis=1)
        k = jnp.arange(n_slots, dtype=jnp.int32)
        exp_k = jnp.sum(k[None, :, None] >= end[:, None, :], axis=2)
        hot = exp_k[:, :, None] == jnp.arange(N_EXPERTS)[None, None, :]
        pick = lambda tbl: jnp.sum(jnp.where(hot, tbl[:, None, :], 0), axis=2)
        within = k[None, :] - pick(end - counts)
        return pick(ls + seg_off), pick(gs + seg_off), within

    n_big = pcs // BIG_CHUNK
    n_end = (pcs // SEG_ALIGN) % 2
    lo_b, go_b, m_b = kth_of(n_big, MAX_BIG, 0)
    lo_e, go_e, _ = kth_of(n_end, N_EXPERTS, pcs - SEG_ALIGN)
    chunk_tab = jnp.concatenate([lo_b + BIG_CHUNK * m_b, go_b + BIG_CHUNK * m_b, lo_e, go_e],
                                axis=1).astype(jnp.int32).reshape(-1)
    n_chunks = jnp.stack([jnp.sum(n_big, axis=1), jnp.sum(n_end, axis=1)], axis=1).astype(jnp.int32).reshape(-1)

    aug_w = D + LANE
    n_u = MOE_SUBS_PER_STEP
    n_steps = n_sub // n_u
    x_sorted = pl.pallas_call(
        functools.partial(_dispatch_kernel, n_steps=n_steps, rows=COMPACT_ROWS),
        grid_spec=pltpu.PrefetchScalarGridSpec(
            num_scalar_prefetch=3,
            grid=(n_steps,),
            in_specs=[
                pl.BlockSpec((n_u * tm, D), lambda i, *_: (i, 0)),
                pl.BlockSpec((n_u * tm, LANE), lambda i, *_: (i, 0)),
                pl.BlockSpec((n_u, 8, tm), lambda i, *_: (i, 0, 0)),
            ],
            out_specs=pl.BlockSpec(memory_space=pl.ANY),
            scratch_shapes=[
                pltpu.VMEM((2 * n_u, COMPACT_ROWS, aug_w), BF16),
                pltpu.VMEM((FFN_TILE // 2, aug_w), BF16),
                pltpu.SemaphoreType.DMA((2 * n_u,)),
                pltpu.SemaphoreType.DMA(()),
            ],
        ),
        out_shape=jax.ShapeDtypeStruct((n_sorted, aug_w), BF16),
        compiler_params=pltpu.CompilerParams(dimension_semantics=("arbitrary",), vmem_limit_bytes=VMEM_LIMIT),
        name="moe_dispatch",
    )(n_chunks, chunk_tab, tails, h2b.reshape(T, D), comb.reshape(T, LANE), route_t)

    used_tile = lambda t, texp, nused: (jnp.minimum(t, nused[0] - 1), 0)
    y_sorted = pl.pallas_call(
        _ffn_kernel,
        grid_spec=pltpu.PrefetchScalarGridSpec(
            num_scalar_prefetch=2,
            grid=(n_ffn_tiles,),
            in_specs=[
                pl.BlockSpec(memory_space=pl.ANY),
                pl.BlockSpec((1, D, FF), lambda t, texp, nused: (texp[t], 0, 0)),
                pl.BlockSpec((1, D, FF), lambda t, texp, nused: (texp[t], 0, 0)),
                pl.BlockSpec((1, FF, D), lambda t, texp, nused: (texp[t], 0, 0)),
            ],
            out_specs=pl.BlockSpec((FFN_TILE, D), used_tile),
            scratch_shapes=[pltpu.VMEM((D, FF), BF16), pltpu.VMEM((D, FF), BF16), pltpu.VMEM((FF, D), BF16),
                            pltpu.VMEM((XS_DEPTH, FFN_TILE, aug_w), BF16), pltpu.SemaphoreType.DMA((XS_DEPTH,))],
        ),
        out_shape=jax.ShapeDtypeStruct((n_sorted, D), BF16),
        compiler_params=pltpu.CompilerParams(dimension_semantics=("arbitrary",), vmem_limit_bytes=VMEM_LIMIT),
        name="moe_expert_ffn",
    )(tile_exp, n_used.reshape(1), x_sorted, w_gate[0], w_up[0], w_down[0])

    out = pl.pallas_call(
        functools.partial(_combine_kernel, n_steps=n_steps, rows=COMPACT_ROWS),
        grid_spec=pltpu.PrefetchScalarGridSpec(
            num_scalar_prefetch=2,
            grid=(n_steps,),
            in_specs=[
                pl.BlockSpec((n_u * tm, LANE), lambda i, *_: (i, 0)),
                pl.BlockSpec((n_u * tm, D), lambda i, *_: (i, 0)),
                pl.BlockSpec((1, D), lambda i, *_: (0, 0)),
                pl.BlockSpec((1, D), lambda i, *_: (0, 0)),
                pl.BlockSpec(memory_space=pl.ANY),
            ],
            out_specs=pl.BlockSpec((n_u * tm, D), lambda i, *_: (i, 0)),
            scratch_shapes=[
                pltpu.VMEM((2 * n_u, COMPACT_ROWS, D), BF16),
                pltpu.SemaphoreType.DMA((2 * n_u,)),
            ],
        ),
        out_shape=jax.ShapeDtypeStruct((T, D), F32),
        compiler_params=pltpu.CompilerParams(dimension_semantics=("arbitrary",), vmem_limit_bytes=VMEM_LIMIT),
        name="moe_combine_ln3",
    )(n_chunks, chunk_tab, route.reshape(T, LANE), h2.reshape(T, D), row2(ln3_g[0]), row2(ln3_b[0]), y_sorted)
    return out.reshape(B, S, D)
```

```python
import functools
import math

import jax
import jax.numpy as jnp
from jax import lax
from jax.experimental import pallas as pl
from jax.experimental.pallas import tpu as pltpu

D = 1024
POOL_WINDOWS = (2, 4, 8, 16)
POOL_W = 512
N_HEADS = 8
HEAD_DIM = 64
ATTN_W = 512
BLK = 256
N_BLK = 8
TOPK = 3
REL_BUCKETS = 32
REL_MAX_DIST = 128
MEM_HEADS = 4
MEM_HD = 128
MEM_W = 512
N_GROUPS = 4
EPG = 8
N_EXPERTS = 32
FF = 256
DN_ALPHA = 2.0 ** 0.25
LN_EPS = 1e-5

LANE = 128
TOK_TILE = 512
SEG_ALIGN = 16
COMPACT_ROWS = 2 * TOK_TILE + N_EXPERTS * SEG_ALIGN
BIG_CHUNK = 2 * SEG_ALIGN
MAX_BIG = COMPACT_ROWS // BIG_CHUNK
CHUNK_TAB_W = 2 * MAX_BIG + 2 * N_EXPERTS
FFN_TILE = 512
PROJ_CHAIN = 256
ATTN_PAIRS_PER_TRIP = 2
MOE_SUBS_PER_STEP = 2
FFN_CHAIN = 512
XS_DEPTH = 3
MERGE_TILE = 1024
HALO = 16
NEG = -1e30
VMEM_LIMIT = 56 * 1024 * 1024

F32 = jnp.float32
BF16 = jnp.bfloat16

_NT = (((1,), (1,)), ((), ()))
_TN = (((0,), (0,)), ((), ()))


def _dot(a, b):
    return jnp.dot(a, b, preferred_element_type=F32)


def _dot_nt(a, b):
    return lax.dot_general(a, b, _NT, preferred_element_type=F32)


def _split(a):
    hi = a.astype(BF16)
    lo = (a - hi.astype(F32)).astype(BF16)
    return hi, lo


def _interleave(chains):
    results = [None] * len(chains)
    live = list(range(len(chains)))
    while live:
        for ci in list(live):
            try:
                next(chains[ci])
            except StopIteration as done:
                results[ci] = done.value
                live.remove(ci)
    return results


def _ln(x, g, b):
    mu = jnp.mean(x, axis=-1, keepdims=True)
    xc = x - mu
    var = jnp.mean(xc * xc, axis=-1, keepdims=True)
    return xc * lax.rsqrt(var + LN_EPS) * g + b


def _proj_kernel(x_ref, g_ref, b_ref, w_ref, wgrp_ref, pscale_ref,
                 ypool_ref, q_ref, k_ref, v_ref, ubuf, kbt, *, tm):
    s = pl.program_id(1)

    @pl.when(s == 0)
    def _():
        ubuf[0:HALO, :] = jnp.zeros((HALO, POOL_W), F32)
        kbt[...] = jnp.zeros_like(kbt)

    _interleave([_proj_chain(ci, s, tm, x_ref, g_ref, b_ref, w_ref, wgrp_ref, pscale_ref,
                             ypool_ref, q_ref, k_ref, v_ref, ubuf, kbt)
                 for ci in range(tm // PROJ_CHAIN)])
    ubuf[0:HALO, :] = ubuf[tm:tm + HALO, :]


def _proj_chain(ci, s, tm, x_ref, g_ref, b_ref, w_ref, wgrp_ref, pscale_ref,
                ypool_ref, q_ref, k_ref, v_ref, ubuf, kbt):
    n = PROJ_CHAIN
    r0 = ci * n
    rows = slice(r0, r0 + n)
    blk0 = s * (tm // BLK) + ci * (n // BLK)

    h = _ln(x_ref[0, rows, :], g_ref[...], b_ref[...])
    hb = h.astype(BF16)
    zu = _dot(hb, w_ref[:, 0:POOL_W])
    zq = _dot(hb, w_ref[:, POOL_W:POOL_W + ATTN_W])
    zk = _dot(hb, w_ref[:, POOL_W + ATTN_W:POOL_W + 2 * ATTN_W])
    zv = _dot(hb, w_ref[:, POOL_W + 2 * ATTN_W:])
    ubuf[HALO + r0:HALO + r0 + n, :] = zu

    r_io = lax.broadcasted_iota(jnp.int32, kbt.shape, 0)
    c_io = lax.broadcasted_iota(jnp.int32, kbt.shape, 1)
    head_match = (r_io >> 3) == (c_io >> 6)
    for bi in range(n // BLK):
        kmean = jnp.mean(zk[bi * BLK:(bi + 1) * BLK], axis=0, keepdims=True)
        kbt[...] = jnp.where(head_match & ((r_io & 7) == blk0 + bi), kmean, kbt[...])
    yield

    t_pos = s * tm + r0 + lax.broadcasted_iota(jnp.int32, (n, LANE), 0)
    for g, w in enumerate(POOL_WINDOWS):
        cols = slice(g * LANE, (g + 1) * LANE)
        ws = ubuf[HALO + r0:HALO + r0 + n, cols]
        for kk in range(1, w):
            ws = ws + ubuf[HALO + r0 - kk:HALO + r0 - kk + n, cols]
        cnt = jnp.minimum(t_pos + 1, w).astype(F32)
        y = ws / cnt - ubuf[HALO + r0:HALO + r0 + n, cols]
        yg = _dot(y.astype(BF16), wgrp_ref[g]) * pscale_ref[:, cols]
        ypool_ref[0, rows, cols] = yg.astype(ypool_ref.dtype)
    yield

    q_hi, q_lo = _split(zq)
    kb_hi, kb_lo = _split(kbt[...])
    g2 = _dot_nt(q_hi, jnp.concatenate([kb_hi, kb_lo], axis=0))
    gate = g2[:, 0:LANE] + g2[:, LANE:2 * LANE] + _dot_nt(q_lo, kb_hi)
    yield

    lane = lax.broadcasted_iota(jnp.int32, (n, LANE), 1)
    row = lax.broadcasted_iota(jnp.int32, (n, LANE), 0)
    n_l = lane & 7
    jrow = blk0 + (row >> 8)
    past = n_l < jrow
    gt = jnp.where(past, gate, -jnp.inf)
    cnt = jnp.zeros((n, LANE), F32)
    for sh in range(1, N_BLK):
        wrap = (n_l + sh) >= N_BLK
        gm = jnp.where(wrap, pltpu.roll(gt, N_BLK - sh, 1), pltpu.roll(gt, LANE - sh, 1))
        cnt = cnt + jnp.where(wrap, jnp.where(gm >= gt, 1.0, 0.0), jnp.where(gm > gt, 1.0, 0.0))
    keep = (past & (cnt < TOPK)) | (n_l == jrow)
    negmask = jnp.where(keep, 0.0, NEG)

    aug_lane = (lane >= HEAD_DIM) & (lane < HEAD_DIM + N_BLK)
    k_onehot = jnp.where(lane == HEAD_DIM + jrow, 1.0, 0.0)
    head_lane = lane < HEAD_DIM
    for hh in range(N_HEADS):
        cols = slice((hh // 2) * LANE, (hh // 2 + 1) * LANE)
        q_h, k_h = zq[:, cols], zk[:, cols]
        if hh % 2:
            q_h, k_h = pltpu.roll(q_h, HEAD_DIM, 1), pltpu.roll(k_h, HEAD_DIM, 1)
        m_h = jnp.where(aug_lane, pltpu.roll(negmask, HEAD_DIM - N_BLK * hh, 1), 0.0)
        q_ref[0, hh, rows, :] = jnp.where(head_lane, q_h, m_h).astype(q_ref.dtype)
        k_ref[0, hh, rows, :] = jnp.where(head_lane, k_h, k_onehot).astype(k_ref.dtype)
    for p in range(ATTN_W // LANE):
        v_ref[0, p, rows, :] = zv[:, p * LANE:(p + 1) * LANE].astype(v_ref.dtype)


def _attn_kernel(q_ref, k_ref, v_ref, tb_ref, o_ref):
    j = pl.program_id(1)
    vrow = lax.broadcasted_iota(jnp.int32, (LANE, BLK), 0)

    def one_head(h, p, jj):
        q = q_ref[0, h]
        own0 = jj * BLK
        pieces = []
        s_own = _dot_nt(k_ref[0, h, own0:own0 + BLK, :], q) + tb_ref[0, h]
        pieces.append((s_own, own0, BLK))
        if jj >= 1:
            s_adj = _dot_nt(k_ref[0, h, own0 - BLK:own0, :], q) + tb_ref[1, h]
            pieces.append((s_adj, own0 - BLK, BLK))
        if jj >= 2:
            s_far = _dot_nt(k_ref[0, h, 0:own0 - BLK, :], q)
            pieces.append((s_far, 0, own0 - BLK))
        yield
        m = None
        for sc, _, _ in pieces:
            mm = jnp.max(sc, axis=0, keepdims=True)
            m = mm if m is None else jnp.maximum(m, mm)
        l = None
        probs = []
        for sc, start, size in pieces:
            e = jnp.exp(sc - m)
            ls = jnp.sum(e, axis=0, keepdims=True)
            l = ls if l is None else l + ls
            probs.append((e.astype(BF16), start, size))
        yield
        acc = None
        for pb, start, size in probs:
            pv = lax.dot_general(v_ref[0, p, start:start + size, :], pb, _TN, preferred_element_type=F32)
            acc = pv if acc is None else acc + pv
        return acc / l

    for jj in range(N_BLK):
        @pl.when(j == jj)
        def _(jj=jj):
            def group(gi, carry):
                pairs = [gi * ATTN_PAIRS_PER_TRIP + pi for pi in range(ATTN_PAIRS_PER_TRIP)]
                outs = _interleave([one_head(2 * p + hh, p, jj) for p in pairs for hh in range(2)])
                for pi, p in enumerate(pairs):
                    o_t = jnp.where(vrow < HEAD_DIM, outs[2 * pi], outs[2 * pi + 1])
                    o_ref[0, p] = o_t.T.astype(o_ref.dtype)
                return carry
            lax.fori_loop(0, N_HEADS // 2 // ATTN_PAIRS_PER_TRIP, group, 0)


def _memkv_kernel(mem_ref, wk_ref, wv_ref, k_ref, v_ref):
    mb = mem_ref[0].astype(BF16)
    k_ref[0] = _dot(mb, wk_ref[...]).astype(k_ref.dtype)
    v_ref[0] = _dot(mb, wv_ref[...]).astype(v_ref.dtype)


def _merge_kernel(x_ref, ypool_ref, o_ref, kmem_ref, vmem_ref,
                  lng_ref, lnb_ref, wgl_ref, bgate_ref, wpu_ref, wau_ref, wout_ref,
                  ln1g_ref, ln1b_ref, wmq_ref, wmo_ref, ln2g_ref, ln2b_ref,
                  wrh_ref, br_ref,
                  h2_ref, h2b_ref, comb_ref, route_ref, route_t_ref, cnt_ref, *, tm):
    _interleave([_merge_chain(ci, x_ref, ypool_ref, o_ref, kmem_ref, vmem_ref,
                              lng_ref, lnb_ref, wgl_ref, bgate_ref, wpu_ref, wau_ref, wout_ref,
                              ln1g_ref, ln1b_ref, wmq_ref, wmo_ref, ln2g_ref, ln2b_ref,
                              wrh_ref, br_ref, h2_ref, h2b_ref, comb_ref, route_ref, route_t_ref, cnt_ref)
                 for ci in range(tm // TOK_TILE)])


def _merge_chain(ci, x_ref, ypool_ref, o_ref, kmem_ref, vmem_ref,
                 lng_ref, lnb_ref, wgl_ref, bgate_ref, wpu_ref, wau_ref, wout_ref,
                 ln1g_ref, ln1b_ref, wmq_ref, wmo_ref, ln2g_ref, ln2b_ref,
                 wrh_ref, br_ref, h2_ref, h2b_ref, comb_ref, route_ref, route_t_ref, cnt_ref):
    rows = slice(ci * TOK_TILE, (ci + 1) * TOK_TILE)
    n_rows = TOK_TILE
    h = _ln(x_ref[0, rows, :], lng_ref[...], lnb_ref[...])
    hb = h.astype(BF16)
    gl = _dot(hb, wgl_ref[...]) + bgate_ref[...]
    yield
    gates = 0.5 * jnp.tanh(0.5 * gl) + 0.5
    y_pool = _dot(ypool_ref[0, rows, :], wpu_ref[...])
    o_cat = jnp.concatenate([o_ref[0, p, rows, :] for p in range(ATTN_W // LANE)], axis=1)
    y_attn = _dot(o_cat, wau_ref[...])
    yield
    merged = gates[:, 0:D] * y_pool + gates[:, D:2 * D] * y_attn
    mix = _dot(merged.astype(BF16), wout_ref[...])
    yield
    h1 = _ln(DN_ALPHA * h + mix, ln1g_ref[...], ln1b_ref[...])

    qm = _dot(h1.astype(BF16), wmq_ref[...]).astype(BF16)
    yield
    outs = []
    for hd in range(MEM_HEADS):
        cols = slice(hd * MEM_HD, (hd + 1) * MEM_HD)
        sc = _dot_nt(qm[:, cols], kmem_ref[0, :, cols])
        m = jnp.max(sc, axis=1, keepdims=True)
        e = jnp.exp(sc - m)
        l = jnp.sum(e, axis=1, keepdims=True)
        outs.append(_dot(e.astype(BF16), vmem_ref[0, :, cols]) / l)
    om = jnp.concatenate(outs, axis=1).astype(BF16)
    xa = _dot(om, wmo_ref[...])
    yield
    h2 = _ln(DN_ALPHA * h1 + xa, ln2g_ref[...], ln2b_ref[...])
    h2_ref[0, rows, :] = h2
    h2b_ref[0, rows, :] = h2.astype(BF16)

    x_hi, x_lo = _split(h2)
    r2 = _dot(x_hi, wrh_ref[...])
    r = r2[:, 0:LANE] + r2[:, LANE:2 * LANE] + _dot(x_lo, wrh_ref[:, 0:LANE]) + br_ref[...]
    yield
    lane = lax.broadcasted_iota(jnp.int32, (n_rows, LANE), 1)
    lane_f = lane.astype(F32)
    cmask = (lane >= N_EXPERTS) & (lane < 2 * N_EXPERTS)
    c = jnp.where(cmask, r, -jnp.inf)
    cmax = jnp.max(c, axis=1, keepdims=True)
    ce = jnp.exp(c - cmax)
    csum = jnp.sum(ce, axis=1, keepdims=True) * (1.0 / EPG)
    g_prob = 1.0 / csum
    grp_lane = ((lane & (N_EXPERTS - 1)) >> 3).astype(F32)
    gidx = jnp.min(jnp.where(cmask & (c == cmax), grp_lane, 99.0), axis=1, keepdims=True)
    fmask = (lane < N_EXPERTS) & (grp_lane == gidx)
    f = jnp.where(fmask, r, -jnp.inf)
    fmax = jnp.max(f, axis=1, keepdims=True)
    fe = jnp.exp(f - fmax)
    fsum = jnp.sum(fe, axis=1, keepdims=True)
    prob = fe / fsum
    p1 = jnp.max(prob, axis=1, keepdims=True)
    i1 = jnp.min(jnp.where(fmask & (prob == p1), lane_f, 999.0), axis=1, keepdims=True)
    rest = fmask & (lane_f != i1)
    prob2 = jnp.where(rest, prob, -1.0)
    p2 = jnp.max(prob2, axis=1, keepdims=True)
    i2 = jnp.min(jnp.where(rest & (prob2 == p2), lane_f, 999.0), axis=1, keepdims=True)
    den = p1 + p2
    comb = jnp.where(lane_f == i1, g_prob * (p1 / den),
                     jnp.where(lane_f == i2, g_prob * (p2 / den), 0.0))
    comb_ref[0, rows, :] = comb
    yield

    sel = jnp.where((lane_f == i1) | (lane_f == i2), 1.0, 0.0)
    cnt = jnp.sum(sel, axis=0, keepdims=True)
    pc = jnp.floor((cnt + (SEG_ALIGN - 1)) * (1.0 / SEG_ALIGN)) * SEG_ALIGN
    lane8 = lax.broadcasted_iota(jnp.int32, (8, LANE), 1)
    inc = jnp.broadcast_to(pc, (8, LANE))
    for sh in (1, 2, 4, 8, 16, 32, 64):
        inc = inc + jnp.where(lane8 >= sh, pltpu.roll(inc, sh, 1), 0.0)
    seg_start = inc[0:1] - pc
    t_row = lax.broadcasted_iota(jnp.int32, (n_rows, n_rows), 0)
    t_col = lax.broadcasted_iota(jnp.int32, (n_rows, n_rows), 1)
    earlier = jnp.where(t_row > t_col, 1.0, 0.0).astype(BF16)
    rank = _dot(earlier, sel.astype(BF16))
    pos = seg_start + rank
    d1 = jnp.sum(jnp.where(lane_f == i1, pos, 0.0), axis=1, keepdims=True)
    d2 = jnp.sum(jnp.where(lane_f == i2, pos, 0.0), axis=1, keepdims=True)
    route = jnp.where(lane == 0, d1, jnp.where(lane == 1, d2, 0.0))
    route_ref[0, rows, :] = route
    r_hi, r_lo = _split(route)
    eye = jnp.where(lax.broadcasted_iota(jnp.int32, (8, LANE), 0) == lane8, 1.0, 0.0).astype(BF16)
    route_t_ref[ci] = _dot_nt(eye, r_hi) + _dot_nt(eye, r_lo)
    cnt_ref[ci] = jnp.broadcast_to(pc, (8, LANE))


def _chunk_copies(sub, nch_ref, tab_ref, make_copy, act):
    base = sub * CHUNK_TAB_W
    for size, cnt_i, lo_off, go_off in ((BIG_CHUNK, 0, 0, MAX_BIG), (SEG_ALIGN, 1, 2 * MAX_BIG, 2 * MAX_BIG + N_EXPERTS)):
        def body(k, carry, size=size, lo_off=lo_off, go_off=go_off):
            lo = pl.multiple_of(tab_ref[base + lo_off + k], SEG_ALIGN)
            go = pl.multiple_of(tab_ref[base + go_off + k], SEG_ALIGN)
            act(make_copy(lo, go, size))
            return carry
        lax.fori_loop(0, nch_ref[2 * sub + cnt_i], body, 0)


def _dispatch_kernel(nch_ref, tab_ref, tail_ref, xb_ref, comb_ref, route_t_ref, xs_ref,
                     cbuf, zbuf, sem, zsem, *, n_steps, rows):
    i = pl.program_id(0)
    par = lax.rem(i, 2)
    n_u = MOE_SUBS_PER_STEP
    tm = TOK_TILE

    def copies(step, par_, act):
        for u in range(n_u):
            slot_ = par_ * n_u + u

            def mk(lo, go, size, slot_=slot_):
                return pltpu.make_async_copy(cbuf.at[slot_, pl.ds(lo, size), :],
                                             xs_ref.at[pl.ds(go, size), :], sem.at[slot_])
            _chunk_copies(step * n_u + u, nch_ref, tab_ref, mk, act)

    @pl.when(i >= 2)
    def _():
        copies(i - 2, par, lambda c: c.wait())

    def chain(u):
        rs = slice(u * tm, (u + 1) * tm)
        r_io = lax.broadcasted_iota(jnp.int32, (rows, tm), 0).astype(F32)
        d1 = route_t_ref[u, 0:1, :]
        d2 = route_t_ref[u, 1:2, :]
        p_mat = jnp.where((r_io == d1) | (r_io == d2), 1.0, 0.0).astype(BF16)
        comb = comb_ref[rs, :]
        c_hi = comb.astype(BF16).astype(F32)
        c_pack = (c_hi + pltpu.roll(comb - c_hi, N_EXPERTS, 1)).astype(BF16)
        x_aug = jnp.concatenate([xb_ref[rs, :], c_pack], axis=1)
        yield
        cbuf[par * n_u + u] = _dot(p_mat, x_aug).astype(BF16)

    _interleave([chain(u) for u in range(n_u)])
    copies(i, par, lambda c: c.start())

    @pl.when(i == n_steps - 1)
    def _():
        if n_steps >= 2:
            copies(i - 1, 1 - par, lambda c: c.wait())
        copies(i, par, lambda c: c.wait())
        zbuf[...] = jnp.zeros_like(zbuf)

        def tails(act):
            def body(e, carry):
                st = tail_ref[e]
                n = tail_ref[N_EXPERTS + e]
                off = jnp.int32(0)
                size = zbuf.shape[0]
                while size >= SEG_ALIGN:
                    bit = (n & size) != 0

                    @pl.when(bit)
                    def _(size=size, off=off):
                        act(pltpu.make_async_copy(
                            zbuf.at[pl.ds(0, size), :],
                            xs_ref.at[pl.ds(pl.multiple_of(st + off, SEG_ALIGN), size), :], zsem))
                    off = off + jnp.where(bit, size, 0)
                    size //= 2
                return carry
            lax.fori_loop(0, N_EXPERTS, body, 0)
        tails(lambda c: c.start())
        tails(lambda c: c.wait())


def _ffn_kernel(texp_ref, nused_ref, xs_ref, wg_ref, wu_ref, wd_ref, ys_ref, wg_b, wu_b, wd_b, xbuf, xsem):
    t = pl.program_id(0)
    e = texp_ref[t]
    n_used = nused_ref[0]

    def fetch(u):
        slot = lax.rem(u, XS_DEPTH)
        return pltpu.make_async_copy(xs_ref.at[pl.ds(pl.multiple_of(u * FFN_TILE, FFN_TILE), FFN_TILE), :],
                                     xbuf.at[slot], xsem.at[slot])

    @pl.when(t == 0)
    def _():
        for u in range(XS_DEPTH - 1):
            @pl.when(u < n_used)
            def _(u=u):
                fetch(jnp.int32(u)).start()

    @pl.when(t + (XS_DEPTH - 1) < n_used)
    def _():
        fetch(t + (XS_DEPTH - 1)).start()

    @pl.when((t == 0) | (e != texp_ref[jnp.maximum(t - 1, 0)]))
    def _():
        wg_b[...] = wg_ref[0].astype(BF16)
        wu_b[...] = wu_ref[0].astype(BF16)
        wd_b[...] = wd_ref[0].astype(BF16)

    @pl.when(t < n_used)
    def _():
        fetch(t).wait()
        slot = lax.rem(t, XS_DEPTH)

        def chain(ci):
            rs = slice(ci * FFN_CHAIN, (ci + 1) * FFN_CHAIN)
            xrow = xbuf[slot, rs, 0:D]
            cw = xbuf[slot, rs, D:D + LANE].astype(F32)
            lane = lax.broadcasted_iota(jnp.int32, cw.shape, 1)
            c = jnp.sum(jnp.where((lane == e) | (lane == e + N_EXPERTS), cw, 0.0), axis=1, keepdims=True)
            a = _dot(xrow, wg_b[...])
            b = _dot(xrow, wu_b[...])
            yield
            hid = ((a * jax.nn.sigmoid(a)) * b * c).astype(BF16)
            yield
            ys_ref[rs, :] = _dot(hid, wd_b[...]).astype(ys_ref.dtype)

        _interleave([chain(ci) for ci in range(FFN_TILE // FFN_CHAIN)])


def _combine_kernel(nch_ref, tab_ref, route_ref, h2_ref, g_ref, b_ref, ys_ref, out_ref,
                    ybuf, sem, *, n_steps, rows):
    i = pl.program_id(0)
    par = lax.rem(i, 2)
    n_u = MOE_SUBS_PER_STEP
    tm = TOK_TILE

    def copies(step, par_, act):
        for u in range(n_u):
            slot_ = par_ * n_u + u

            def mk(lo, go, size, slot_=slot_):
                return pltpu.make_async_copy(ys_ref.at[pl.ds(go, size), :],
                                             ybuf.at[slot_, pl.ds(lo, size), :], sem.at[slot_])
            _chunk_copies(step * n_u + u, nch_ref, tab_ref, mk, act)

    @pl.when(i == 0)
    def _():
        ybuf[...] = jnp.zeros_like(ybuf)
        copies(0, 0, lambda c: c.start())

    @pl.when(i + 1 < n_steps)
    def _():
        copies(i + 1, 1 - par, lambda c: c.start())

    copies(i, par, lambda c: c.wait())

    def chain(u):
        rs = slice(u * tm, (u + 1) * tm)
        r_io = lax.broadcasted_iota(jnp.int32, (tm, rows), 1).astype(F32)
        d1 = route_ref[rs, 0:1]
        d2 = route_ref[rs, 1:2]
        p_t = jnp.where((r_io == d1) | (r_io == d2), 1.0, 0.0).astype(BF16)
        yield
        ff = _dot(p_t, ybuf[par * n_u + u])
        yield
        out_ref[rs, :] = _ln(DN_ALPHA * h2_ref[rs, :] + ff, g_ref[...], b_ref[...])

    _interleave([chain(u) for u in range(n_u)])


def _bias_kernel(tbl_ref, bkt_ref, out_ref):
    h = pl.program_id(0)
    far = tbl_ref[h, REL_BUCKETS - 1]
    for which in range(2):
        bk = bkt_ref[which]
        acc = jnp.where(bk < 0, NEG, 0.0)
        for kk in range(REL_BUCKETS):
            acc = jnp.where(bk == kk, tbl_ref[h, kk] - far, acc)
        out_ref[which, 0] = acc


def _rel_bucket_table(dist):
    max_exact = REL_BUCKETS // 2
    d = jnp.maximum(dist, 0)
    large = max_exact + (jnp.log(jnp.maximum(d, 1).astype(F32) / max_exact)
                         / math.log(REL_MAX_DIST / max_exact) * (REL_BUCKETS - max_exact)).astype(jnp.int32)
    large = jnp.minimum(large, REL_BUCKETS - 1)
    return jnp.where(d < max_exact, d, large)


def _const_spec(shape):
    nd = len(shape)
    return pl.BlockSpec(shape, lambda *_: (0,) * nd)


def kernel(x, mem, ln_in_g, ln_in_b, rel_bias, w_in, b_gate, w_pool_grp, pool_scale, w_pool_up, w_attn_up,
           w_mix_out, ln1_g, ln1_b, w_mq, w_mk, w_mv, w_mo, ln2_g, ln2_b, w_coarse, b_coarse, w_fine, b_fine,
           w_gate, w_up, w_down, ln3_g, ln3_b):
    B, S, _ = x.shape
    assert S == N_BLK * BLK and w_in.shape[0] == 1
    M = mem.shape[1]
    T = B * S
    tm = 512

    wi = w_in[0]
    w_u = wi[:, 0:POOL_W]
    w_q = wi[:, POOL_W:POOL_W + ATTN_W] * (HEAD_DIM ** -0.5)
    w_k = wi[:, POOL_W + ATTN_W:POOL_W + 2 * ATTN_W]
    w_v = wi[:, POOL_W + 2 * ATTN_W:POOL_W + 3 * ATTN_W]
    w_gl = wi[:, POOL_W + 3 * ATTN_W:]

    w1 = jnp.concatenate([w_u, w_q, w_k, w_v], axis=1).astype(BF16)
    row2 = lambda a: a.reshape(1, -1)

    iq = jnp.arange(BLK, dtype=jnp.int32)[:, None]
    ik = jnp.arange(BLK, dtype=jnp.int32)[None, :]
    d_own = iq - ik
    bkt = jnp.stack([jnp.where(d_own >= 0, _rel_bucket_table(d_own), -1),
                     _rel_bucket_table(d_own + BLK)]).transpose(0, 2, 1)
    t_bias = pl.pallas_call(
        _bias_kernel,
        grid=(N_HEADS,),
        in_specs=[pl.BlockSpec(memory_space=pltpu.SMEM), _const_spec((2, BLK, BLK))],
        out_specs=pl.BlockSpec((2, 1, BLK, BLK), lambda h: (0, h, 0, 0)),
        out_shape=jax.ShapeDtypeStruct((2, N_HEADS, BLK, BLK), F32),
        name="relbias_tiles",
    )(rel_bias.T, bkt)

    n_w1 = w1.shape[1]
    tmp = MERGE_TILE
    ypool, q_aug, k_aug, v_p = pl.pallas_call(
        functools.partial(_proj_kernel, tm=tmp),
        grid=(B, S // tmp),
        in_specs=[
            pl.BlockSpec((1, tmp, D), lambda b, s: (b, s, 0)),
            _const_spec((1, D)), _const_spec((1, D)),
            _const_spec((D, n_w1)),
            _const_spec((len(POOL_WINDOWS), LANE, LANE)),
            _const_spec((1, POOL_W)),
        ],
        out_specs=[
            pl.BlockSpec((1, tmp, POOL_W), lambda b, s: (b, s, 0)),
            pl.BlockSpec((1, N_HEADS, tmp, LANE), lambda b, s: (b, 0, s, 0)),
            pl.BlockSpec((1, N_HEADS, tmp, LANE), lambda b, s: (b, 0, s, 0)),
            pl.BlockSpec((1, ATTN_W // LANE, tmp, LANE), lambda b, s: (b, 0, s, 0)),
        ],
        out_shape=[
            jax.ShapeDtypeStruct((B, S, POOL_W), BF16),
            jax.ShapeDtypeStruct((B, N_HEADS, S, LANE), BF16),
            jax.ShapeDtypeStruct((B, N_HEADS, S, LANE), BF16),
            jax.ShapeDtypeStruct((B, ATTN_W // LANE, S, LANE), BF16),
        ],
        scratch_shapes=[pltpu.VMEM((HALO + tmp, POOL_W), F32), pltpu.VMEM((LANE, ATTN_W), F32)],
        compiler_params=pltpu.CompilerParams(dimension_semantics=("arbitrary", "arbitrary"),
                                             vmem_limit_bytes=VMEM_LIMIT),
        name="proj_pool_gate",
    )(x, row2(ln_in_g), row2(ln_in_b), w1, w_pool_grp[0].astype(BF16), row2(pool_scale[0]))

    o_attn = pl.pallas_call(
        _attn_kernel,
        grid=(B, N_BLK),
        in_specs=[
            pl.BlockSpec((1, N_HEADS, BLK, LANE), lambda b, j: (b, 0, j, 0)),
            pl.BlockSpec((1, N_HEADS, S, LANE), lambda b, j: (b, 0, 0, 0)),
            pl.BlockSpec((1, ATTN_W // LANE, S, LANE), lambda b, j: (b, 0, 0, 0)),
            _const_spec((2, N_HEADS, BLK, BLK)),
        ],
        out_specs=pl.BlockSpec((1, ATTN_W // LANE, BLK, LANE), lambda b, j: (b, 0, j, 0)),
        out_shape=jax.ShapeDtypeStruct((B, ATTN_W // LANE, S, LANE), BF16),
        compiler_params=pltpu.CompilerParams(dimension_semantics=("arbitrary", "arbitrary"),
                                             vmem_limit_bytes=VMEM_LIMIT),
        name="moba_attn",
    )(q_aug, k_aug, v_p, t_bias)

    kmem, vmem = pl.pallas_call(
        _memkv_kernel,
        grid=(B,),
        in_specs=[pl.BlockSpec((1, M, D), lambda b: (b, 0, 0)),
                  _const_spec((D, MEM_W)), _const_spec((D, MEM_W))],
        out_specs=[pl.BlockSpec((1, M, MEM_W), lambda b: (b, 0, 0)),
                   pl.BlockSpec((1, M, MEM_W), lambda b: (b, 0, 0))],
        out_shape=[jax.ShapeDtypeStruct((B, M, MEM_W), BF16)] * 2,
        compiler_params=pltpu.CompilerParams(dimension_semantics=("arbitrary",)),
        name="mem_kv",
    )(mem, w_mk[0].astype(BF16), w_mv[0].astype(BF16))

    w_r = jnp.concatenate([
        w_fine[0].reshape(D, N_EXPERTS),
        jnp.repeat(w_coarse[0], EPG, axis=1),
        jnp.zeros((D, LANE - 2 * N_EXPERTS), F32)], axis=1)
    b_r = jnp.concatenate([
        b_fine[0].reshape(N_EXPERTS), jnp.repeat(b_coarse[0], EPG),
        jnp.zeros((LANE - 2 * N_EXPERTS,), F32)]).reshape(1, LANE)
    w_r_hi = w_r.astype(BF16)
    w_r_lo = (w_r - w_r_hi.astype(F32)).astype(BF16)
    w_r_cat = jnp.concatenate([w_r_hi, w_r_lo], axis=1)

    n_sub = T // tm
    tmm = MERGE_TILE
    per = tmm // tm
    sub_idx = lambda b, s: b * (S // tmm) + s
    h2, h2b, comb, route, route_t, seg_cnt = pl.pallas_call(
        functools.partial(_merge_kernel, tm=tmm),
        grid=(B, S // tmm),
        in_specs=[
            pl.BlockSpec((1, tmm, D), lambda b, s: (b, s, 0)),
            pl.BlockSpec((1, tmm, POOL_W), lambda b, s: (b, s, 0)),
            pl.BlockSpec((1, ATTN_W // LANE, tmm, LANE), lambda b, s: (b, 0, s, 0)),
            pl.BlockSpec((1, M, MEM_W), lambda b, s: (b, 0, 0)),
            pl.BlockSpec((1, M, MEM_W), lambda b, s: (b, 0, 0)),
            _const_spec((1, D)), _const_spec((1, D)),
            _const_spec((D, 2 * D)), _const_spec((1, 2 * D)),
            _const_spec((POOL_W, D)), _const_spec((ATTN_W, D)), _const_spec((D, D)),
            _const_spec((1, D)), _const_spec((1, D)),
            _const_spec((D, MEM_W)), _const_spec((MEM_W, D)),
            _const_spec((1, D)), _const_spec((1, D)),
            _const_spec((D, 2 * LANE)), _const_spec((1, LANE)),
        ],
        out_specs=[
            pl.BlockSpec((1, tmm, D), lambda b, s: (b, s, 0)),
            pl.BlockSpec((1, tmm, D), lambda b, s: (b, s, 0)),
            pl.BlockSpec((1, tmm, LANE), lambda b, s: (b, s, 0)),
            pl.BlockSpec((1, tmm, LANE), lambda b, s: (b, s, 0)),
            pl.BlockSpec((per, 8, tm), lambda b, s: (sub_idx(b, s), 0, 0)),
            pl.BlockSpec((per, 8, LANE), lambda b, s: (sub_idx(b, s), 0, 0)),
        ],
        out_shape=[
            jax.ShapeDtypeStruct((B, S, D), F32),
            jax.ShapeDtypeStruct((B, S, D), BF16),
            jax.ShapeDtypeStruct((B, S, LANE), F32),
            jax.ShapeDtypeStruct((B, S, LANE), F32),
            jax.ShapeDtypeStruct((n_sub, 8, tm), F32),
            jax.ShapeDtypeStruct((n_sub, 8, LANE), F32),
        ],
        compiler_params=pltpu.CompilerParams(dimension_semantics=("arbitrary", "arbitrary"),
                                             vmem_limit_bytes=VMEM_LIMIT),
        name="merge_memattn_router",
    )(x, ypool, o_attn, kmem, vmem,
      row2(ln_in_g), row2(ln_in_b), w_gl.astype(BF16), row2(b_gate[0]),
      w_pool_up[0].astype(BF16), w_attn_up[0].astype(BF16), w_mix_out[0].astype(BF16),
      row2(ln1_g[0]), row2(ln1_b[0]),
      (w_mq[0] * (MEM_HD ** -0.5)).astype(BF16), w_mo[0].astype(BF16),
      row2(ln2_g[0]), row2(ln2_b[0]),
      w_r_cat, b_r)

    pcs = seg_cnt[:, 0, :N_EXPERTS].astype(jnp.int32)
    tot = jnp.sum(pcs, axis=0)
    cap = ((tot + FFN_TILE - 1) // FFN_TILE) * FFN_TILE
    ends = jnp.cumsum(cap)
    base = ends - cap
    gs = base[None, :] + jnp.cumsum(pcs, axis=0) - pcs
    ls = jnp.cumsum(pcs, axis=1) - pcs
    n_sorted = n_sub * COMPACT_ROWS + N_EXPERTS * FFN_TILE
    n_ffn_tiles = n_sorted // FFN_TILE
    n_used = (ends[-1] // FFN_TILE).astype(jnp.int32)
    tile_row = jnp.arange(n_ffn_tiles, dtype=jnp.int32) * FFN_TILE
    tile_exp = jnp.sum(jnp.minimum(tile_row, ends[-1] - 1)[:, None] >= ends[None, :], axis=1).astype(jnp.int32)
    tails = jnp.concatenate([base + tot, cap - tot]).astype(jnp.int32)
    def kth_of(counts, n_slots, seg_off):
        end = jnp.cumsum(counts, axis=1)
        k = jnp.arange(n_slots, dtype=jnp.int32)
        exp_k = jnp.sum(k[None, :, None] >= end[:, None, :], axis=2)
        hot = exp_k[:, :, None] == jnp.arange(N_EXPERTS)[None, None, :]
        pick = lambda tbl: jnp.sum(jnp.where(hot, tbl[:, None, :], 0), axis=2)
        within = k[None, :] - pick(end - counts)
        return pick(ls + seg_off), pick(gs + seg_off), within

    n_big = pcs // BIG_CHUNK
    n_end = (pcs // SEG_ALIGN) % 2
    lo_b, go_b, m_b = kth_of(n_big, MAX_BIG, 0)
    lo_e, go_e, _ = kth_of(n_end, N_EXPERTS, pcs - SEG_ALIGN)
    chunk_tab = jnp.concatenate([lo_b + BIG_CHUNK * m_b, go_b + BIG_CHUNK * m_b, lo_e, go_e],
                                axis=1).astype(jnp.int32).reshape(-1)
    n_chunks = jnp.stack([jnp.sum(n_big, axis=1), jnp.sum(n_end, axis=1)], axis=1).astype(jnp.int32).reshape(-1)

    aug_w = D + LANE
    n_u = MOE_SUBS_PER_STEP
    n_steps = n_sub // n_u
    x_sorted = pl.pallas_call(
        functools.partial(_dispatch_kernel, n_steps=n_steps, rows=COMPACT_ROWS),
        grid_spec=pltpu.PrefetchScalarGridSpec(
            num_scalar_prefetch=3,
            grid=(n_steps,),
            in_specs=[
                pl.BlockSpec((n_u * tm, D), lambda i, *_: (i, 0)),
                pl.BlockSpec((n_u * tm, LANE), lambda i, *_: (i, 0)),
                pl.BlockSpec((n_u, 8, tm), lambda i, *_: (i, 0, 0)),
            ],
            out_specs=pl.BlockSpec(memory_space=pl.ANY),
            scratch_shapes=[
                pltpu.VMEM((2 * n_u, COMPACT_ROWS, aug_w), BF16),
                pltpu.VMEM((FFN_TILE // 2, aug_w), BF16),
                pltpu.SemaphoreType.DMA((2 * n_u,)),
                pltpu.SemaphoreType.DMA(()),
            ],
        ),
        out_shape=jax.ShapeDtypeStruct((n_sorted, aug_w), BF16),
        compiler_params=pltpu.CompilerParams(dimension_semantics=("arbitrary",), vmem_limit_bytes=VMEM_LIMIT),
        name="moe_dispatch",
    )(n_chunks, chunk_tab, tails, h2b.reshape(T, D), comb.reshape(T, LANE), route_t)

    used_tile = lambda t, texp, nused: (jnp.minimum(t, nused[0] - 1), 0)
    y_sorted = pl.pallas_call(
        _ffn_kernel,
        grid_spec=pltpu.PrefetchScalarGridSpec(
            num_scalar_prefetch=2,
            grid=(n_ffn_tiles,),
            in_specs=[
                pl.BlockSpec(memory_space=pl.ANY),
                pl.BlockSpec((1, D, FF), lambda t, texp, nused: (texp[t], 0, 0)),
                pl.BlockSpec((1, D, FF), lambda t, texp, nused: (texp[t], 0, 0)),
                pl.BlockSpec((1, FF, D), lambda t, texp, nused: (texp[t], 0, 0)),
            ],
            out_specs=pl.BlockSpec((FFN_TILE, D), used_tile),
            scratch_shapes=[pltpu.VMEM((D, FF), BF16), pltpu.VMEM((D, FF), BF16), pltpu.VMEM((FF, D), BF16),
                            pltpu.VMEM((XS_DEPTH, FFN_TILE, aug_w), BF16), pltpu.SemaphoreType.DMA((XS_DEPTH,))],
        ),
        out_shape=jax.ShapeDtypeStruct((n_sorted, D), BF16),
        compiler_params=pltpu.CompilerParams(dimension_semantics=("arbitrary",), vmem_limit_bytes=VMEM_LIMIT),
        name="moe_expert_ffn",
    )(tile_exp, n_used.reshape(1), x_sorted, w_gate[0], w_up[0], w_down[0])

    out = pl.pallas_call(
        functools.partial(_combine_kernel, n_steps=n_steps, rows=COMPACT_ROWS),
        grid_spec=pltpu.PrefetchScalarGridSpec(
            num_scalar_prefetch=2,
            grid=(n_steps,),
            in_specs=[
                pl.BlockSpec((n_u * tm, LANE), lambda i, *_: (i, 0)),
                pl.BlockSpec((n_u * tm, D), lambda i, *_: (i, 0)),
                pl.BlockSpec((1, D), lambda i, *_: (0, 0)),
                pl.BlockSpec((1, D), lambda i, *_: (0, 0)),
                pl.BlockSpec(memory_space=pl.ANY),
            ],
            out_specs=pl.BlockSpec((n_u * tm, D), lambda i, *_: (i, 0)),
            scratch_shapes=[
                pltpu.VMEM((2 * n_u, COMPACT_ROWS, D), BF16),
                pltpu.SemaphoreType.DMA((2 * n_u,)),
            ],
        ),
        out_shape=jax.ShapeDtypeStruct((T, D), F32),
        compiler_params=pltpu.CompilerParams(dimension_semantics=("arbitrary",), vmem_limit_bytes=VMEM_LIMIT),
        name="moe_combine_ln3",
    )(n_chunks, chunk_tab, route.reshape(T, LANE), h2.reshape(T, D), row2(ln3_g[0]), row2(ln3_b[0]), y_sorted)
    return out.reshape(B, S, D)
```

```python
import functools
import math

import jax
import jax.numpy as jnp
from jax import lax
from jax.experimental import pallas as pl
from jax.experimental.pallas import tpu as pltpu

D = 1024
POOL_WINDOWS = (2, 4, 8, 16)
POOL_W = 512
N_HEADS = 8
HEAD_DIM = 64
ATTN_W = 512
BLK = 256
N_BLK = 8
TOPK = 3
REL_BUCKETS = 32
REL_MAX_DIST = 128
MEM_HEADS = 4
MEM_HD = 128
MEM_W = 512
N_GROUPS = 4
EPG = 8
N_EXPERTS = 32
FF = 256
DN_ALPHA = 2.0 ** 0.25
LN_EPS = 1e-5

LANE = 128
TOK_TILE = 512
SEG_ALIGN = 16
COMPACT_ROWS = 2 * TOK_TILE + N_EXPERTS * SEG_ALIGN
BIG_CHUNK = 2 * SEG_ALIGN
MAX_BIG = COMPACT_ROWS // BIG_CHUNK
CHUNK_TAB_W = 2 * MAX_BIG + 2 * N_EXPERTS
FFN_TILE = 512
PROJ_CHAIN = 256
ATTN_PAIRS_PER_TRIP = 2
MOE_SUBS_PER_STEP = 2
FFN_CHAIN = 512
XS_DEPTH = 3
MERGE_TILE = 1024
HALO = 16
NEG = -1e30
VMEM_LIMIT = 56 * 1024 * 1024

F32 = jnp.float32
BF16 = jnp.bfloat16

_NT = (((1,), (1,)), ((), ()))


def _dot(a, b):
    return jnp.dot(a, b, preferred_element_type=F32)


def _dot_nt(a, b):
    return lax.dot_general(a, b, _NT, preferred_element_type=F32)


def _split(a):
    hi = a.astype(BF16)
    lo = (a - hi.astype(F32)).astype(BF16)
    return hi, lo


def _interleave(chains):
    results = [None] * len(chains)
    live = list(range(len(chains)))
    while live:
        for ci in list(live):
            try:
                next(chains[ci])
            except StopIteration as done:
                results[ci] = done.value
                live.remove(ci)
    return results


def _ln(x, g, b):
    mu = jnp.mean(x, axis=-1, keepdims=True)
    xc = x - mu
    var = jnp.mean(xc * xc, axis=-1, keepdims=True)
    return xc * lax.rsqrt(var + LN_EPS) * g + b


def _proj_kernel(x_ref, g_ref, b_ref, w_ref, wgl_ref, bgate_ref, wgrp_ref, pscale_ref,
                 ypool_ref, q_ref, k_ref, v_ref, gates_ref, ubuf, kbt, *, tm):
    s = pl.program_id(1)

    @pl.when(s == 0)
    def _():
        ubuf[0:HALO, :] = jnp.zeros((HALO, POOL_W), F32)
        kbt[...] = jnp.zeros_like(kbt)

    _interleave([_proj_chain(ci, s, tm, x_ref, g_ref, b_ref, w_ref, wgl_ref, bgate_ref, wgrp_ref, pscale_ref,
                             ypool_ref, q_ref, k_ref, v_ref, gates_ref, ubuf, kbt)
                 for ci in range(tm // PROJ_CHAIN)])
    ubuf[0:HALO, :] = ubuf[tm:tm + HALO, :]


def _proj_chain(ci, s, tm, x_ref, g_ref, b_ref, w_ref, wgl_ref, bgate_ref, wgrp_ref, pscale_ref,
                ypool_ref, q_ref, k_ref, v_ref, gates_ref, ubuf, kbt):
    n = PROJ_CHAIN
    r0 = ci * n
    rows = slice(r0, r0 + n)
    blk0 = s * (tm // BLK) + ci * (n // BLK)

    h = _ln(x_ref[0, rows, :], g_ref[...], b_ref[...])
    hb = h.astype(BF16)
    zu = _dot(hb, w_ref[:, 0:POOL_W])
    zq = _dot(hb, w_ref[:, POOL_W:POOL_W + ATTN_W])
    zk = _dot(hb, w_ref[:, POOL_W + ATTN_W:POOL_W + 2 * ATTN_W])
    zv = _dot(hb, w_ref[:, POOL_W + 2 * ATTN_W:])
    gl = _dot(hb, wgl_ref[...]) + bgate_ref[...]
    ubuf[HALO + r0:HALO + r0 + n, :] = zu

    r_io = lax.broadcasted_iota(jnp.int32, kbt.shape, 0)
    c_io = lax.broadcasted_iota(jnp.int32, kbt.shape, 1)
    head_match = (r_io >> 3) == (c_io >> 6)
    for bi in range(n // BLK):
        kmean = jnp.mean(zk[bi * BLK:(bi + 1) * BLK], axis=0, keepdims=True)
        kbt[...] = jnp.where(head_match & ((r_io & 7) == blk0 + bi), kmean, kbt[...])
    yield

    gates_ref[0, rows, :] = (0.5 * jnp.tanh(0.5 * gl) + 0.5).astype(gates_ref.dtype)

    t_pos = s * tm + r0 + lax.broadcasted_iota(jnp.int32, (n, LANE), 0)
    for g, w in enumerate(POOL_WINDOWS):
        cols = slice(g * LANE, (g + 1) * LANE)
        ws = ubuf[HALO + r0:HALO + r0 + n, cols]
        for kk in range(1, w):
            ws = ws + ubuf[HALO + r0 - kk:HALO + r0 - kk + n, cols]
        cnt = jnp.minimum(t_pos + 1, w).astype(F32)
        y = ws / cnt - ubuf[HALO + r0:HALO + r0 + n, cols]
        yg = _dot(y.astype(BF16), wgrp_ref[g]) * pscale_ref[:, cols]
        ypool_ref[0, rows, cols] = yg.astype(ypool_ref.dtype)
    yield

    q_hi, q_lo = _split(zq)
    kb_hi, kb_lo = _split(kbt[...])
    g2 = _dot_nt(q_hi, jnp.concatenate([kb_hi, kb_lo], axis=0))
    gate = g2[:, 0:LANE] + g2[:, LANE:2 * LANE] + _dot_nt(q_lo, kb_hi)
    yield

    lane = lax.broadcasted_iota(jnp.int32, (n, LANE), 1)
    row = lax.broadcasted_iota(jnp.int32, (n, LANE), 0)
    n_l = lane & 7
    jrow = blk0 + (row >> 8)
    past = n_l < jrow
    gt = jnp.where(past, gate, -jnp.inf)
    cnt = jnp.zeros((n, LANE), F32)
    for sh in range(1, N_BLK):
        wrap = (n_l + sh) >= N_BLK
        gm = jnp.where(wrap, pltpu.roll(gt, N_BLK - sh, 1), pltpu.roll(gt, LANE - sh, 1))
        cnt = cnt + jnp.where(wrap, jnp.where(gm >= gt, 1.0, 0.0), jnp.where(gm > gt, 1.0, 0.0))
    keep = (past & (cnt < TOPK)) | (n_l == jrow)
    negmask = jnp.where(keep, 0.0, NEG)

    aug_lane = (lane >= HEAD_DIM) & (lane < HEAD_DIM + N_BLK)
    k_onehot = jnp.where(lane == HEAD_DIM + jrow, 1.0, 0.0)
    head_lane = lane < HEAD_DIM
    for hh in range(N_HEADS):
        cols = slice((hh // 2) * LANE, (hh // 2 + 1) * LANE)
        q_h, k_h = zq[:, cols], zk[:, cols]
        if hh % 2:
            q_h, k_h = pltpu.roll(q_h, HEAD_DIM, 1), pltpu.roll(k_h, HEAD_DIM, 1)
        m_h = jnp.where(aug_lane, pltpu.roll(negmask, HEAD_DIM - N_BLK * hh, 1), 0.0)
        q_ref[0, hh, rows, :] = jnp.where(head_lane, q_h, m_h).astype(q_ref.dtype)
        k_ref[0, hh, rows, :] = jnp.where(head_lane, k_h, k_onehot).astype(k_ref.dtype)
    for p in range(ATTN_W // LANE):
        v_ref[0, p, rows, :] = zv[:, p * LANE:(p + 1) * LANE].astype(v_ref.dtype)


def _attn_kernel(q_ref, k_ref, v_ref, tb_ref, o_ref):
    j = pl.program_id(1)
    lane = lax.broadcasted_iota(jnp.int32, (BLK, LANE), 1)

    def one_head(h, p, jj):
        q = q_ref[0, h]
        own0 = jj * BLK
        pieces = []
        s_own = _dot_nt(q, k_ref[0, h, own0:own0 + BLK, :]) + tb_ref[0, h]
        pieces.append((s_own, own0, BLK))
        if jj >= 1:
            s_adj = _dot_nt(q, k_ref[0, h, own0 - BLK:own0, :]) + tb_ref[1, h]
            pieces.append((s_adj, own0 - BLK, BLK))
        if jj >= 2:
            s_far = _dot_nt(q, k_ref[0, h, 0:own0 - BLK, :])
            pieces.append((s_far, 0, own0 - BLK))
        yield
        m = None
        for sc, _, _ in pieces:
            mm = jnp.max(sc, axis=1, keepdims=True)
            m = mm if m is None else jnp.maximum(m, mm)
        l = None
        probs = []
        for sc, start, size in pieces:
            e = jnp.exp(sc - m)
            ls = jnp.sum(e, axis=1, keepdims=True)
            l = ls if l is None else l + ls
            probs.append((e.astype(BF16), start, size))
        yield
        acc = None
        for pb, start, size in probs:
            pv = _dot(pb, v_ref[0, p, start:start + size, :])
            acc = pv if acc is None else acc + pv
        return acc / l

    for jj in range(N_BLK):
        @pl.when(j == jj)
        def _(jj=jj):
            def group(gi, carry):
                pairs = [gi * ATTN_PAIRS_PER_TRIP + pi for pi in range(ATTN_PAIRS_PER_TRIP)]
                outs = _interleave([one_head(2 * p + hh, p, jj) for p in pairs for hh in range(2)])
                for pi, p in enumerate(pairs):
                    o_ref[0, p] = jnp.where(lane < HEAD_DIM, outs[2 * pi], outs[2 * pi + 1]).astype(o_ref.dtype)
                return carry
            lax.fori_loop(0, N_HEADS // 2 // ATTN_PAIRS_PER_TRIP, group, 0)


def _memkv_kernel(mem_ref, wk_ref, wv_ref, k_ref, v_ref):
    mb = mem_ref[0].astype(BF16)
    k_ref[0] = _dot(mb, wk_ref[...]).astype(k_ref.dtype)
    v_ref[0] = _dot(mb, wv_ref[...]).astype(v_ref.dtype)


def _merge_kernel(x_ref, ypool_ref, o_ref, kmem_ref, vmem_ref,
                  lng_ref, lnb_ref, gates_ref, wpu_ref, wau_ref, wout_ref,
                  ln1g_ref, ln1b_ref, wmq_ref, wmo_ref, ln2g_ref, ln2b_ref,
                  wrh_ref, br_ref,
                  h2_ref, h2b_ref, comb_ref, route_ref, route_t_ref, cnt_ref, *, tm):
    _interleave([_merge_chain(ci, x_ref, ypool_ref, o_ref, kmem_ref, vmem_ref,
                              lng_ref, lnb_ref, gates_ref, wpu_ref, wau_ref, wout_ref,
                              ln1g_ref, ln1b_ref, wmq_ref, wmo_ref, ln2g_ref, ln2b_ref,
                              wrh_ref, br_ref, h2_ref, h2b_ref, comb_ref, route_ref, route_t_ref, cnt_ref)
                 for ci in range(tm // TOK_TILE)])


def _merge_chain(ci, x_ref, ypool_ref, o_ref, kmem_ref, vmem_ref,
                 lng_ref, lnb_ref, gates_ref, wpu_ref, wau_ref, wout_ref,
                 ln1g_ref, ln1b_ref, wmq_ref, wmo_ref, ln2g_ref, ln2b_ref,
                 wrh_ref, br_ref, h2_ref, h2b_ref, comb_ref, route_ref, route_t_ref, cnt_ref):
    rows = slice(ci * TOK_TILE, (ci + 1) * TOK_TILE)
    n_rows = TOK_TILE
    h = _ln(x_ref[0, rows, :], lng_ref[...], lnb_ref[...])
    gates = gates_ref[0, rows, :]
    y_pool = _dot(ypool_ref[0, rows, :], wpu_ref[...])
    o_cat = jnp.concatenate([o_ref[0, p, rows, :] for p in range(ATTN_W // LANE)], axis=1)
    y_attn = _dot(o_cat, wau_ref[...])
    yield
    merged = gates[:, 0:D] * y_pool + gates[:, D:2 * D] * y_attn
    mix = _dot(merged.astype(BF16), wout_ref[...])
    yield
    h1 = _ln(DN_ALPHA * h + mix, ln1g_ref[...], ln1b_ref[...])

    qm = _dot(h1.astype(BF16), wmq_ref[...]).astype(BF16)
    yield
    outs = []
    for hd in range(MEM_HEADS):
        cols = slice(hd * MEM_HD, (hd + 1) * MEM_HD)
        sc = _dot_nt(qm[:, cols], kmem_ref[0, :, cols])
        m = jnp.max(sc, axis=1, keepdims=True)
        e = jnp.exp(sc - m)
        l = jnp.sum(e, axis=1, keepdims=True)
        outs.append(_dot(e.astype(BF16), vmem_ref[0, :, cols]) / l)
    om = jnp.concatenate(outs, axis=1).astype(BF16)
    xa = _dot(om, wmo_ref[...])
    yield
    h2 = _ln(DN_ALPHA * h1 + xa, ln2g_ref[...], ln2b_ref[...])
    h2_ref[0, rows, :] = h2
    h2b_ref[0, rows, :] = h2.astype(BF16)

    x_hi, x_lo = _split(h2)
    r2 = _dot(x_hi, wrh_ref[...])
    r = r2[:, 0:LANE] + r2[:, LANE:2 * LANE] + _dot(x_lo, wrh_ref[:, 0:LANE]) + br_ref[...]
    yield
    lane = lax.broadcasted_iota(jnp.int32, (n_rows, LANE), 1)
    lane_f = lane.astype(F32)
    cmask = (lane >= N_EXPERTS) & (lane < 2 * N_EXPERTS)
    c = jnp.where(cmask, r, -jnp.inf)
    cmax = jnp.max(c, axis=1, keepdims=True)
    ce = jnp.exp(c - cmax)
    csum = jnp.sum(ce, axis=1, keepdims=True) * (1.0 / EPG)
    g_prob = 1.0 / csum
    grp_lane = ((lane & (N_EXPERTS - 1)) >> 3).astype(F32)
    gidx = jnp.min(jnp.where(cmask & (c == cmax), grp_lane, 99.0), axis=1, keepdims=True)
    fmask = (lane < N_EXPERTS) & (grp_lane == gidx)
    f = jnp.where(fmask, r, -jnp.inf)
    fmax = jnp.max(f, axis=1, keepdims=True)
    fe = jnp.exp(f - fmax)
    fsum = jnp.sum(fe, axis=1, keepdims=True)
    prob = fe / fsum
    p1 = jnp.max(prob, axis=1, keepdims=True)
    i1 = jnp.min(jnp.where(fmask & (prob == p1), lane_f, 999.0), axis=1, keepdims=True)
    rest = fmask & (lane_f != i1)
    prob2 = jnp.where(rest, prob, -1.0)
    p2 = jnp.max(prob2, axis=1, keepdims=True)
    i2 = jnp.min(jnp.where(rest & (prob2 == p2), lane_f, 999.0), axis=1, keepdims=True)
    den = p1 + p2
    comb = jnp.where(lane_f == i1, g_prob * (p1 / den),
                     jnp.where(lane_f == i2, g_prob * (p2 / den), 0.0))
    comb_ref[0, rows, :] = comb
    yield

    sel = jnp.where((lane_f == i1) | (lane_f == i2), 1.0, 0.0)
    cnt = jnp.sum(sel, axis=0, keepdims=True)
    pc = jnp.floor((cnt + (SEG_ALIGN - 1)) * (1.0 / SEG_ALIGN)) * SEG_ALIGN
    lane8 = lax.broadcasted_iota(jnp.int32, (8, LANE), 1)
    inc = jnp.broadcast_to(pc, (8, LANE))
    for sh in (1, 2, 4, 8, 16, 32, 64):
        inc = inc + jnp.where(lane8 >= sh, pltpu.roll(inc, sh, 1), 0.0)
    seg_start = inc[0:1] - pc
    t_row = lax.broadcasted_iota(jnp.int32, (n_rows, n_rows), 0)
    t_col = lax.broadcasted_iota(jnp.int32, (n_rows, n_rows), 1)
    earlier = jnp.where(t_row > t_col, 1.0, 0.0).astype(BF16)
    rank = _dot(earlier, sel.astype(BF16))
    pos = seg_start + rank
    d1 = jnp.sum(jnp.where(lane_f == i1, pos, 0.0), axis=1, keepdims=True)
    d2 = jnp.sum(jnp.where(lane_f == i2, pos, 0.0), axis=1, keepdims=True)
    route = jnp.where(lane == 0, d1, jnp.where(lane == 1, d2, 0.0))
    route_ref[0, rows, :] = route
    r_hi, r_lo = _split(route)
    eye = jnp.where(lax.broadcasted_iota(jnp.int32, (8, LANE), 0) == lane8, 1.0, 0.0).astype(BF16)
    route_t_ref[ci] = _dot_nt(eye, r_hi) + _dot_nt(eye, r_lo)
    cnt_ref[ci] = jnp.broadcast_to(pc, (8, LANE))


def _chunk_copies(sub, nch_ref, tab_ref, make_copy, act):
    base = sub * CHUNK_TAB_W
    for size, cnt_i, lo_off, go_off in ((BIG_CHUNK, 0, 0, MAX_BIG), (SEG_ALIGN, 1, 2 * MAX_BIG, 2 * MAX_BIG + N_EXPERTS)):
        def body(k, carry, size=size, lo_off=lo_off, go_off=go_off):
            lo = pl.multiple_of(tab_ref[base + lo_off + k], SEG_ALIGN)
            go = pl.multiple_of(tab_ref[base + go_off + k], SEG_ALIGN)
            act(make_copy(lo, go, size))
            return carry
        lax.fori_loop(0, nch_ref[2 * sub + cnt_i], body, 0)


def _dispatch_kernel(nch_ref, tab_ref, tail_ref, xb_ref, comb_ref, route_t_ref, xs_ref,
                     cbuf, zbuf, sem, zsem, *, n_steps, rows):
    i = pl.program_id(0)
    par = lax.rem(i, 2)
    n_u = MOE_SUBS_PER_STEP
    tm = TOK_TILE

    def copies(step, par_, act):
        for u in range(n_u):
            slot_ = par_ * n_u + u

            def mk(lo, go, size, slot_=slot_):
                return pltpu.make_async_copy(cbuf.at[slot_, pl.ds(lo, size), :],
                                             xs_ref.at[pl.ds(go, size), :], sem.at[slot_])
            _chunk_copies(step * n_u + u, nch_ref, tab_ref, mk, act)

    @pl.when(i >= 2)
    def _():
        copies(i - 2, par, lambda c: c.wait())

    def chain(u):
        rs = slice(u * tm, (u + 1) * tm)
        r_io = lax.broadcasted_iota(jnp.int32, (rows, tm), 0).astype(F32)
        d1 = route_t_ref[u, 0:1, :]
        d2 = route_t_ref[u, 1:2, :]
        p_mat = jnp.where((r_io == d1) | (r_io == d2), 1.0, 0.0).astype(BF16)
        comb = comb_ref[rs, :]
        c_hi = comb.astype(BF16).astype(F32)
        c_pack = (c_hi + pltpu.roll(comb - c_hi, N_EXPERTS, 1)).astype(BF16)
        x_aug = jnp.concatenate([xb_ref[rs, :], c_pack], axis=1)
        yield
        cbuf[par * n_u + u] = _dot(p_mat, x_aug).astype(BF16)

    _interleave([chain(u) for u in range(n_u)])
    copies(i, par, lambda c: c.start())

    @pl.when(i == n_steps - 1)
    def _():
        if n_steps >= 2:
            copies(i - 1, 1 - par, lambda c: c.wait())
        copies(i, par, lambda c: c.wait())
        zbuf[...] = jnp.zeros_like(zbuf)

        def tails(act):
            def body(e, carry):
                st = tail_ref[e]
                n = tail_ref[N_EXPERTS + e]
                off = jnp.int32(0)
                size = zbuf.shape[0]
                while size >= SEG_ALIGN:
                    bit = (n & size) != 0

                    @pl.when(bit)
                    def _(size=size, off=off):
                        act(pltpu.make_async_copy(
                            zbuf.at[pl.ds(0, size), :],
                            xs_ref.at[pl.ds(pl.multiple_of(st + off, SEG_ALIGN), size), :], zsem))
                    off = off + jnp.where(bit, size, 0)
                    size //= 2
                return carry
            lax.fori_loop(0, N_EXPERTS, body, 0)
        tails(lambda c: c.start())
        tails(lambda c: c.wait())


def _ffn_kernel(texp_ref, nused_ref, xs_ref, wg_ref, wu_ref, wd_ref, ys_ref, wg_b, wu_b, wd_b, xbuf, xsem):
    t = pl.program_id(0)
    e = texp_ref[t]
    n_used = nused_ref[0]

    def fetch(u):
        slot = lax.rem(u, XS_DEPTH)
        return pltpu.make_async_copy(xs_ref.at[pl.ds(pl.multiple_of(u * FFN_TILE, FFN_TILE), FFN_TILE), :],
                                     xbuf.at[slot], xsem.at[slot])

    @pl.when(t == 0)
    def _():
        for u in range(XS_DEPTH - 1):
            @pl.when(u < n_used)
            def _(u=u):
                fetch(jnp.int32(u)).start()

    @pl.when(t + (XS_DEPTH - 1) < n_used)
    def _():
        fetch(t + (XS_DEPTH - 1)).start()

    @pl.when((t == 0) | (e != texp_ref[jnp.maximum(t - 1, 0)]))
    def _():
        wg_b[...] = wg_ref[0].astype(BF16)
        wu_b[...] = wu_ref[0].astype(BF16)
        wd_b[...] = wd_ref[0].astype(BF16)

    @pl.when(t < n_used)
    def _():
        fetch(t).wait()
        slot = lax.rem(t, XS_DEPTH)

        def chain(ci):
            rs = slice(ci * FFN_CHAIN, (ci + 1) * FFN_CHAIN)
            xrow = xbuf[slot, rs, 0:D]
            cw = xbuf[slot, rs, D:D + LANE].astype(F32)
            lane = lax.broadcasted_iota(jnp.int32, cw.shape, 1)
            c = jnp.sum(jnp.where((lane == e) | (lane == e + N_EXPERTS), cw, 0.0), axis=1, keepdims=True)
            a = _dot(xrow, wg_b[...])
            b = _dot(xrow, wu_b[...])
            yield
            hid = ((a * jax.nn.sigmoid(a)) * b * c).astype(BF16)
            yield
            ys_ref[rs, :] = _dot(hid, wd_b[...]).astype(ys_ref.dtype)

        _interleave([chain(ci) for ci in range(FFN_TILE // FFN_CHAIN)])


def _combine_kernel(nch_ref, tab_ref, route_ref, h2_ref, g_ref, b_ref, ys_ref, out_ref,
                    ybuf, sem, *, n_steps, rows):
    i = pl.program_id(0)
    par = lax.rem(i, 2)
    n_u = MOE_SUBS_PER_STEP
    tm = TOK_TILE

    def copies(step, par_, act):
        for u in range(n_u):
            slot_ = par_ * n_u + u

            def mk(lo, go, size, slot_=slot_):
                return pltpu.make_async_copy(ys_ref.at[pl.ds(go, size), :],
                                             ybuf.at[slot_, pl.ds(lo, size), :], sem.at[slot_])
            _chunk_copies(step * n_u + u, nch_ref, tab_ref, mk, act)

    @pl.when(i == 0)
    def _():
        ybuf[...] = jnp.zeros_like(ybuf)
        copies(0, 0, lambda c: c.start())

    @pl.when(i + 1 < n_steps)
    def _():
        copies(i + 1, 1 - par, lambda c: c.start())

    copies(i, par, lambda c: c.wait())

    def chain(u):
        rs = slice(u * tm, (u + 1) * tm)
        r_io = lax.broadcasted_iota(jnp.int32, (tm, rows), 1).astype(F32)
        d1 = route_ref[rs, 0:1]
        d2 = route_ref[rs, 1:2]
        p_t = jnp.where((r_io == d1) | (r_io == d2), 1.0, 0.0).astype(BF16)
        yield
        ff = _dot(p_t, ybuf[par * n_u + u])
        yield
        out_ref[rs, :] = _ln(DN_ALPHA * h2_ref[rs, :] + ff, g_ref[...], b_ref[...])

    _interleave([chain(u) for u in range(n_u)])


def _bias_kernel(tbl_ref, bkt_ref, out_ref):
    h = pl.program_id(0)
    far = tbl_ref[h, REL_BUCKETS - 1]
    for which in range(2):
        bk = bkt_ref[which]
        acc = jnp.where(bk < 0, NEG, 0.0)
        for kk in range(REL_BUCKETS):
            acc = jnp.where(bk == kk, tbl_ref[h, kk] - far, acc)
        out_ref[which, 0] = acc


def _rel_bucket_table(dist):
    max_exact = REL_BUCKETS // 2
    d = jnp.maximum(dist, 0)
    large = max_exact + (jnp.log(jnp.maximum(d, 1).astype(F32) / max_exact)
                         / math.log(REL_MAX_DIST / max_exact) * (REL_BUCKETS - max_exact)).astype(jnp.int32)
    large = jnp.minimum(large, REL_BUCKETS - 1)
    return jnp.where(d < max_exact, d, large)


def _const_spec(shape):
    nd = len(shape)
    return pl.BlockSpec(shape, lambda *_: (0,) * nd)


def kernel(x, mem, ln_in_g, ln_in_b, rel_bias, w_in, b_gate, w_pool_grp, pool_scale, w_pool_up, w_attn_up,
           w_mix_out, ln1_g, ln1_b, w_mq, w_mk, w_mv, w_mo, ln2_g, ln2_b, w_coarse, b_coarse, w_fine, b_fine,
           w_gate, w_up, w_down, ln3_g, ln3_b):
    B, S, _ = x.shape
    assert S == N_BLK * BLK and w_in.shape[0] == 1
    M = mem.shape[1]
    T = B * S
    tm = 512

    wi = w_in[0]
    w_u = wi[:, 0:POOL_W]
    w_q = wi[:, POOL_W:POOL_W + ATTN_W] * (HEAD_DIM ** -0.5)
    w_k = wi[:, POOL_W + ATTN_W:POOL_W + 2 * ATTN_W]
    w_v = wi[:, POOL_W + 2 * ATTN_W:POOL_W + 3 * ATTN_W]
    w_gl = wi[:, POOL_W + 3 * ATTN_W:]

    w1 = jnp.concatenate([w_u, w_q, w_k, w_v], axis=1).astype(BF16)
    row2 = lambda a: a.reshape(1, -1)

    iq = jnp.arange(BLK, dtype=jnp.int32)[:, None]
    ik = jnp.arange(BLK, dtype=jnp.int32)[None, :]
    d_own = iq - ik
    bkt = jnp.stack([jnp.where(d_own >= 0, _rel_bucket_table(d_own), -1), _rel_bucket_table(d_own + BLK)])
    t_bias = pl.pallas_call(
        _bias_kernel,
        grid=(N_HEADS,),
        in_specs=[pl.BlockSpec(memory_space=pltpu.SMEM), _const_spec((2, BLK, BLK))],
        out_specs=pl.BlockSpec((2, 1, BLK, BLK), lambda h: (0, h, 0, 0)),
        out_shape=jax.ShapeDtypeStruct((2, N_HEADS, BLK, BLK), F32),
        name="relbias_tiles",
    )(rel_bias.T, bkt)

    n_w1 = w1.shape[1]
    tmp = MERGE_TILE
    ypool, q_aug, k_aug, v_p, gates = pl.pallas_call(
        functools.partial(_proj_kernel, tm=tmp),
        grid=(B, S // tmp),
        in_specs=[
            pl.BlockSpec((1, tmp, D), lambda b, s: (b, s, 0)),
            _const_spec((1, D)), _const_spec((1, D)),
            _const_spec((D, n_w1)),
            _const_spec((D, 2 * D)), _const_spec((1, 2 * D)),
            _const_spec((len(POOL_WINDOWS), LANE, LANE)),
            _const_spec((1, POOL_W)),
        ],
        out_specs=[
            pl.BlockSpec((1, tmp, POOL_W), lambda b, s: (b, s, 0)),
            pl.BlockSpec((1, N_HEADS, tmp, LANE), lambda b, s: (b, 0, s, 0)),
            pl.BlockSpec((1, N_HEADS, tmp, LANE), lambda b, s: (b, 0, s, 0)),
            pl.BlockSpec((1, ATTN_W // LANE, tmp, LANE), lambda b, s: (b, 0, s, 0)),
            pl.BlockSpec((1, tmp, 2 * D), lambda b, s: (b, s, 0)),
        ],
        out_shape=[
            jax.ShapeDtypeStruct((B, S, POOL_W), BF16),
            jax.ShapeDtypeStruct((B, N_HEADS, S, LANE), BF16),
            jax.ShapeDtypeStruct((B, N_HEADS, S, LANE), BF16),
            jax.ShapeDtypeStruct((B, ATTN_W // LANE, S, LANE), BF16),
            jax.ShapeDtypeStruct((B, S, 2 * D), BF16),
        ],
        scratch_shapes=[pltpu.VMEM((HALO + tmp, POOL_W), F32), pltpu.VMEM((LANE, ATTN_W), F32)],
        compiler_params=pltpu.CompilerParams(dimension_semantics=("arbitrary", "arbitrary"),
                                             vmem_limit_bytes=VMEM_LIMIT),
        name="proj_pool_gate",
    )(x, row2(ln_in_g), row2(ln_in_b), w1, w_gl.astype(BF16), row2(b_gate[0]),
      w_pool_grp[0].astype(BF16), row2(pool_scale[0]))

    o_attn = pl.pallas_call(
        _attn_kernel,
        grid=(B, N_BLK),
        in_specs=[
            pl.BlockSpec((1, N_HEADS, BLK, LANE), lambda b, j: (b, 0, j, 0)),
            pl.BlockSpec((1, N_HEADS, S, LANE), lambda b, j: (b, 0, 0, 0)),
            pl.BlockSpec((1, ATTN_W // LANE, S, LANE), lambda b, j: (b, 0, 0, 0)),
            _const_spec((2, N_HEADS, BLK, BLK)),
        ],
        out_specs=pl.BlockSpec((1, ATTN_W // LANE, BLK, LANE), lambda b, j: (b, 0, j, 0)),
        out_shape=jax.ShapeDtypeStruct((B, ATTN_W // LANE, S, LANE), BF16),
        compiler_params=pltpu.CompilerParams(dimension_semantics=("arbitrary", "arbitrary"),
                                             vmem_limit_bytes=VMEM_LIMIT),
        name="moba_attn",
    )(q_aug, k_aug, v_p, t_bias)

    kmem, vmem = pl.pallas_call(
        _memkv_kernel,
        grid=(B,),
        in_specs=[pl.BlockSpec((1, M, D), lambda b: (b, 0, 0)),
                  _const_spec((D, MEM_W)), _const_spec((D, MEM_W))],
        out_specs=[pl.BlockSpec((1, M, MEM_W), lambda b: (b, 0, 0)),
                   pl.BlockSpec((1, M, MEM_W), lambda b: (b, 0, 0))],
        out_shape=[jax.ShapeDtypeStruct((B, M, MEM_W), BF16)] * 2,
        compiler_params=pltpu.CompilerParams(dimension_semantics=("arbitrary",)),
        name="mem_kv",
    )(mem, w_mk[0].astype(BF16), w_mv[0].astype(BF16))

    w_r = jnp.concatenate([
        w_fine[0].reshape(D, N_EXPERTS),
        jnp.repeat(w_coarse[0], EPG, axis=1),
        jnp.zeros((D, LANE - 2 * N_EXPERTS), F32)], axis=1)
    b_r = jnp.concatenate([
        b_fine[0].reshape(N_EXPERTS), jnp.repeat(b_coarse[0], EPG),
        jnp.zeros((LANE - 2 * N_EXPERTS,), F32)]).reshape(1, LANE)
    w_r_hi = w_r.astype(BF16)
    w_r_lo = (w_r - w_r_hi.astype(F32)).astype(BF16)
    w_r_cat = jnp.concatenate([w_r_hi, w_r_lo], axis=1)

    n_sub = T // tm
    tmm = MERGE_TILE
    per = tmm // tm
    sub_idx = lambda b, s: b * (S // tmm) + s
    h2, h2b, comb, route, route_t, seg_cnt = pl.pallas_call(
        functools.partial(_merge_kernel, tm=tmm),
        grid=(B, S // tmm),
        in_specs=[
            pl.BlockSpec((1, tmm, D), lambda b, s: (b, s, 0)),
            pl.BlockSpec((1, tmm, POOL_W), lambda b, s: (b, s, 0)),
            pl.BlockSpec((1, ATTN_W // LANE, tmm, LANE), lambda b, s: (b, 0, s, 0)),
            pl.BlockSpec((1, M, MEM_W), lambda b, s: (b, 0, 0)),
            pl.BlockSpec((1, M, MEM_W), lambda b, s: (b, 0, 0)),
            _const_spec((1, D)), _const_spec((1, D)),
            pl.BlockSpec((1, tmm, 2 * D), lambda b, s: (b, s, 0)),
            _const_spec((POOL_W, D)), _const_spec((ATTN_W, D)), _const_spec((D, D)),
            _const_spec((1, D)), _const_spec((1, D)),
            _const_spec((D, MEM_W)), _const_spec((MEM_W, D)),
            _const_spec((1, D)), _const_spec((1, D)),
            _const_spec((D, 2 * LANE)), _const_spec((1, LANE)),
        ],
        out_specs=[
            pl.BlockSpec((1, tmm, D), lambda b, s: (b, s, 0)),
            pl.BlockSpec((1, tmm, D), lambda b, s: (b, s, 0)),
            pl.BlockSpec((1, tmm, LANE), lambda b, s: (b, s, 0)),
            pl.BlockSpec((1, tmm, LANE), lambda b, s: (b, s, 0)),
            pl.BlockSpec((per, 8, tm), lambda b, s: (sub_idx(b, s), 0, 0)),
            pl.BlockSpec((per, 8, LANE), lambda b, s: (sub_idx(b, s), 0, 0)),
        ],
        out_shape=[
            jax.ShapeDtypeStruct((B, S, D), F32),
            jax.ShapeDtypeStruct((B, S, D), BF16),
            jax.ShapeDtypeStruct((B, S, LANE), F32),
            jax.ShapeDtypeStruct((B, S, LANE), F32),
            jax.ShapeDtypeStruct((n_sub, 8, tm), F32),
            jax.ShapeDtypeStruct((n_sub, 8, LANE), F32),
        ],
        compiler_params=pltpu.CompilerParams(dimension_semantics=("arbitrary", "arbitrary"),
                                             vmem_limit_bytes=VMEM_LIMIT),
        name="merge_memattn_router",
    )(x, ypool, o_attn, kmem, vmem,
      row2(ln_in_g), row2(ln_in_b), gates,
      w_pool_up[0].astype(BF16), w_attn_up[0].astype(BF16), w_mix_out[0].astype(BF16),
      row2(ln1_g[0]), row2(ln1_b[0]),
      (w_mq[0] * (MEM_HD ** -0.5)).astype(BF16), w_mo[0].astype(BF16),
      row2(ln2_g[0]), row2(ln2_b[0]),
      w_r_cat, b_r)

    pcs = seg_cnt[:, 0, :N_EXPERTS].astype(jnp.int32)
    tot = jnp.sum(pcs, axis=0)
    cap = ((tot + FFN_TILE - 1) // FFN_TILE) * FFN_TILE
    ends = jnp.cumsum(cap)
    base = ends - cap
    gs = base[None, :] + jnp.cumsum(pcs, axis=0) - pcs
    ls = jnp.cumsum(pcs, axis=1) - pcs
    n_sorted = n_sub * COMPACT_ROWS + N_EXPERTS * FFN_TILE
    n_ffn_tiles = n_sorted // FFN_TILE
    n_used = (ends[-1] // FFN_TILE).astype(jnp.int32)
    tile_row = jnp.arange(n_ffn_tiles, dtype=jnp.int32) * FFN_TILE
    tile_exp = jnp.sum(jnp.minimum(tile_row, ends[-1] - 1)[:, None] >= ends[None, :], axis=1).astype(jnp.int32)
    tails = jnp.concatenate([base + tot, cap - tot]).astype(jnp.int32)
    def kth_of(counts, n_slots, seg_off):
        end = jnp.cumsum(counts, axis=1)
        k = jnp.arange(n_slots, dtype=jnp.int32)
        exp_k = jnp.sum(k[None, :, None] >= end[:, None, :], axis=2)
        hot = exp_k[:, :, None] == jnp.arange(N_EXPERTS)[None, None, :]
        pick = lambda tbl: jnp.sum(jnp.where(hot, tbl[:, None, :], 0), axis=2)
        within = k[None, :] - pick(end - counts)
        return pick(ls + seg_off), pick(gs + seg_off), within

    n_big = pcs // BIG_CHUNK
    n_end = (pcs // SEG_ALIGN) % 2
    lo_b, go_b, m_b = kth_of(n_big, MAX_BIG, 0)
    lo_e, go_e, _ = kth_of(n_end, N_EXPERTS, pcs - SEG_ALIGN)
    chunk_tab = jnp.concatenate([lo_b + BIG_CHUNK * m_b, go_b + BIG_CHUNK * m_b, lo_e, go_e],
                                axis=1).astype(jnp.int32).reshape(-1)
    n_chunks = jnp.stack([jnp.sum(n_big, axis=1), jnp.sum(n_end, axis=1)], axis=1).astype(jnp.int32).reshape(-1)

    aug_w = D + LANE
    n_u = MOE_SUBS_PER_STEP
    n_steps = n_sub // n_u
    x_sorted = pl.pallas_call(
        functools.partial(_dispatch_kernel, n_steps=n_steps, rows=COMPACT_ROWS),
        grid_spec=pltpu.PrefetchScalarGridSpec(
            num_scalar_prefetch=3,
            grid=(n_steps,),
            in_specs=[
                pl.BlockSpec((n_u * tm, D), lambda i, *_: (i, 0)),
                pl.BlockSpec((n_u * tm, LANE), lambda i, *_: (i, 0)),
                pl.BlockSpec((n_u, 8, tm), lambda i, *_: (i, 0, 0)),
            ],
            out_specs=pl.BlockSpec(memory_space=pl.ANY),
            scratch_shapes=[
                pltpu.VMEM((2 * n_u, COMPACT_ROWS, aug_w), BF16),
                pltpu.VMEM((FFN_TILE // 2, aug_w), BF16),
                pltpu.SemaphoreType.DMA((2 * n_u,)),
                pltpu.SemaphoreType.DMA(()),
            ],
        ),
        out_shape=jax.ShapeDtypeStruct((n_sorted, aug_w), BF16),
        compiler_params=pltpu.CompilerParams(dimension_semantics=("arbitrary",), vmem_limit_bytes=VMEM_LIMIT),
        name="moe_dispatch",
    )(n_chunks, chunk_tab, tails, h2b.reshape(T, D), comb.reshape(T, LANE), route_t)

    used_tile = lambda t, texp, nused: (jnp.minimum(t, nused[0] - 1), 0)
    y_sorted = pl.pallas_call(
        _ffn_kernel,
        grid_spec=pltpu.PrefetchScalarGridSpec(
            num_scalar_prefetch=2,
            grid=(n_ffn_tiles,),
            in_specs=[
                pl.BlockSpec(memory_space=pl.ANY),
                pl.BlockSpec((1, D, FF), lambda t, texp, nused: (texp[t], 0, 0)),
                pl.BlockSpec((1, D, FF), lambda t, texp, nused: (texp[t], 0, 0)),
                pl.BlockSpec((1, FF, D), lambda t, texp, nused: (texp[t], 0, 0)),
            ],
            out_specs=pl.BlockSpec((FFN_TILE, D), used_tile),
            scratch_shapes=[pltpu.VMEM((D, FF), BF16), pltpu.VMEM((D, FF), BF16), pltpu.VMEM((FF, D), BF16),
                            pltpu.VMEM((XS_DEPTH, FFN_TILE, aug_w), BF16), pltpu.SemaphoreType.DMA((XS_DEPTH,))],
        ),
        out_shape=jax.ShapeDtypeStruct((n_sorted, D), BF16),
        compiler_params=pltpu.CompilerParams(dimension_semantics=("arbitrary",), vmem_limit_bytes=VMEM_LIMIT),
        name="moe_expert_ffn",
    )(tile_exp, n_used.reshape(1), x_sorted, w_gate[0], w_up[0], w_down[0])

    out = pl.pallas_call(
        functools.partial(_combine_kernel, n_steps=n_steps, rows=COMPACT_ROWS),
        grid_spec=pltpu.PrefetchScalarGridSpec(
            num_scalar_prefetch=2,
            grid=(n_steps,),
            in_specs=[
                pl.BlockSpec((n_u * tm, LANE), lambda i, *_: (i, 0)),
                pl.BlockSpec((n_u * tm, D), lambda i, *_: (i, 0)),
                pl.BlockSpec((1, D), lambda i, *_: (0, 0)),
                pl.BlockSpec((1, D), lambda i, *_: (0, 0)),
                pl.BlockSpec(memory_space=pl.ANY),
            ],
            out_specs=pl.BlockSpec((n_u * tm, D), lambda i, *_: (i, 0)),
            scratch_shapes=[
                pltpu.VMEM((2 * n_u, COMPACT_ROWS, D), BF16),
                pltpu.SemaphoreType.DMA((2 * n_u,)),
            ],
        ),
        out_shape=jax.ShapeDtypeStruct((T, D), F32),
        compiler_params=pltpu.CompilerParams(dimension_semantics=("arbitrary",), vmem_limit_bytes=VMEM_LIMIT),
        name="moe_combine_ln3",
    )(n_chunks, chunk_tab, route.reshape(T, LANE), h2.reshape(T, D), row2(ln3_g[0]), row2(ln3_b[0]), y_sorted)
    return out.reshape(B, S, D)
```

```python
import functools
import math

import jax
import jax.numpy as jnp
from jax import lax
from jax.experimental import pallas as pl
from jax.experimental.pallas import tpu as pltpu

D = 1024
POOL_WINDOWS = (2, 4, 8, 16)
POOL_W = 512
N_HEADS = 8
HEAD_DIM = 64
ATTN_W = 512
BLK = 256
N_BLK = 8
TOPK = 3
REL_BUCKETS = 32
REL_MAX_DIST = 128
MEM_HEADS = 4
MEM_HD = 128
MEM_W = 512
N_GROUPS = 4
EPG = 8
N_EXPERTS = 32
FF = 256
DN_ALPHA = 2.0 ** 0.25
LN_EPS = 1e-5

LANE = 128
TOK_TILE = 512
SEG_ALIGN = 16
COMPACT_ROWS = 2 * TOK_TILE + N_EXPERTS * SEG_ALIGN
BIG_CHUNK = 2 * SEG_ALIGN
MAX_BIG = COMPACT_ROWS // BIG_CHUNK
CHUNK_TAB_W = 2 * MAX_BIG + 2 * N_EXPERTS
FFN_TILE = 512
PROJ_CHAIN = 256
ATTN_PAIRS_PER_TRIP = 2
MOE_SUBS_PER_STEP = 2
FFN_CHAIN = 512
XS_DEPTH = 3
MERGE_TILE = 1024
HALO = 16
NEG = -1e30
LOG2E = math.log2(math.e)
VMEM_LIMIT = 56 * 1024 * 1024

F32 = jnp.float32
BF16 = jnp.bfloat16

_NT = (((1,), (1,)), ((), ()))


def _dot(a, b):
    return jnp.dot(a, b, preferred_element_type=F32)


def _dot_nt(a, b):
    return lax.dot_general(a, b, _NT, preferred_element_type=F32)


def _split(a):
    hi = a.astype(BF16)
    lo = (a - hi.astype(F32)).astype(BF16)
    return hi, lo


def _interleave(chains):
    results = [None] * len(chains)
    live = list(range(len(chains)))
    while live:
        for ci in list(live):
            try:
                next(chains[ci])
            except StopIteration as done:
                results[ci] = done.value
                live.remove(ci)
    return results


def _ln(x, g, b):
    mu = jnp.mean(x, axis=-1, keepdims=True)
    xc = x - mu
    var = jnp.mean(xc * xc, axis=-1, keepdims=True)
    return xc * lax.rsqrt(var + LN_EPS) * g + b


def _proj_kernel(x_ref, g_ref, b_ref, w_ref, wgrp_ref, pscale_ref,
                 ypool_ref, q_ref, k_ref, v_ref, ubuf, kbt, *, tm):
    s = pl.program_id(1)

    @pl.when(s == 0)
    def _():
        ubuf[0:HALO, :] = jnp.zeros((HALO, POOL_W), F32)
        kbt[...] = jnp.zeros_like(kbt)

    _interleave([_proj_chain(ci, s, tm, x_ref, g_ref, b_ref, w_ref, wgrp_ref, pscale_ref,
                             ypool_ref, q_ref, k_ref, v_ref, ubuf, kbt)
                 for ci in range(tm // PROJ_CHAIN)])
    ubuf[0:HALO, :] = ubuf[tm:tm + HALO, :]


def _proj_chain(ci, s, tm, x_ref, g_ref, b_ref, w_ref, wgrp_ref, pscale_ref,
                ypool_ref, q_ref, k_ref, v_ref, ubuf, kbt):
    n = PROJ_CHAIN
    r0 = ci * n
    rows = slice(r0, r0 + n)
    blk0 = s * (tm // BLK) + ci * (n // BLK)

    h = _ln(x_ref[0, rows, :], g_ref[...], b_ref[...])
    hb = h.astype(BF16)
    zu = _dot(hb, w_ref[:, 0:POOL_W])
    zq = _dot(hb, w_ref[:, POOL_W:POOL_W + ATTN_W])
    zk = _dot(hb, w_ref[:, POOL_W + ATTN_W:POOL_W + 2 * ATTN_W])
    zv = _dot(hb, w_ref[:, POOL_W + 2 * ATTN_W:])
    ubuf[HALO + r0:HALO + r0 + n, :] = zu

    r_io = lax.broadcasted_iota(jnp.int32, kbt.shape, 0)
    c_io = lax.broadcasted_iota(jnp.int32, kbt.shape, 1)
    head_match = (r_io >> 3) == (c_io >> 6)
    for bi in range(n // BLK):
        kmean = jnp.mean(zk[bi * BLK:(bi + 1) * BLK], axis=0, keepdims=True)
        kbt[...] = jnp.where(head_match & ((r_io & 7) == blk0 + bi), kmean, kbt[...])
    yield

    t_pos = s * tm + r0 + lax.broadcasted_iota(jnp.int32, (n, LANE), 0)
    for g, w in enumerate(POOL_WINDOWS):
        cols = slice(g * LANE, (g + 1) * LANE)
        ws = ubuf[HALO + r0:HALO + r0 + n, cols]
        for kk in range(1, w):
            ws = ws + ubuf[HALO + r0 - kk:HALO + r0 - kk + n, cols]
        cnt = jnp.minimum(t_pos + 1, w).astype(F32)
        y = ws / cnt - ubuf[HALO + r0:HALO + r0 + n, cols]
        yg = _dot(y.astype(BF16), wgrp_ref[g]) * pscale_ref[:, cols]
        ypool_ref[0, rows, cols] = yg.astype(ypool_ref.dtype)
    yield

    q_hi, q_lo = _split(zq)
    kb_hi, kb_lo = _split(kbt[...])
    g2 = _dot_nt(q_hi, jnp.concatenate([kb_hi, kb_lo], axis=0))
    gate = g2[:, 0:LANE] + g2[:, LANE:2 * LANE] + _dot_nt(q_lo, kb_hi)
    yield

    lane = lax.broadcasted_iota(jnp.int32, (n, LANE), 1)
    row = lax.broadcasted_iota(jnp.int32, (n, LANE), 0)
    n_l = lane & 7
    jrow = blk0 + (row >> 8)
    past = n_l < jrow
    gt = jnp.where(past, gate, -jnp.inf)
    cnt = jnp.zeros((n, LANE), F32)
    for sh in range(1, N_BLK):
        wrap = (n_l + sh) >= N_BLK
        gm = jnp.where(wrap, pltpu.roll(gt, N_BLK - sh, 1), pltpu.roll(gt, LANE - sh, 1))
        cnt = cnt + jnp.where(wrap, jnp.where(gm >= gt, 1.0, 0.0), jnp.where(gm > gt, 1.0, 0.0))
    keep = (past & (cnt < TOPK)) | (n_l == jrow)
    negmask = jnp.where(keep, 0.0, NEG)

    aug_lane = (lane >= HEAD_DIM) & (lane < HEAD_DIM + N_BLK)
    k_onehot = jnp.where(lane == HEAD_DIM + jrow, 1.0, 0.0)
    head_lane = lane < HEAD_DIM
    for hh in range(N_HEADS):
        cols = slice((hh // 2) * LANE, (hh // 2 + 1) * LANE)
        q_h, k_h = zq[:, cols], zk[:, cols]
        if hh % 2:
            q_h, k_h = pltpu.roll(q_h, HEAD_DIM, 1), pltpu.roll(k_h, HEAD_DIM, 1)
        m_h = jnp.where(aug_lane, pltpu.roll(negmask, HEAD_DIM - N_BLK * hh, 1), 0.0)
        q_ref[0, hh, rows, :] = jnp.where(head_lane, q_h, m_h).astype(q_ref.dtype)
        k_ref[0, hh, rows, :] = jnp.where(head_lane, k_h, k_onehot).astype(k_ref.dtype)
    for p in range(ATTN_W // LANE):
        v_ref[0, p, rows, :] = zv[:, p * LANE:(p + 1) * LANE].astype(v_ref.dtype)


def _attn_kernel(q_ref, k_ref, v_ref, tb_ref, o_ref):
    j = pl.program_id(1)
    lane = lax.broadcasted_iota(jnp.int32, (BLK, LANE), 1)

    def one_head(h, p, jj):
        q = q_ref[0, h]
        own0 = jj * BLK
        pieces = []
        s_own = _dot_nt(q, k_ref[0, h, own0:own0 + BLK, :]) + tb_ref[0, h]
        pieces.append((s_own, own0, BLK))
        if jj >= 1:
            s_adj = _dot_nt(q, k_ref[0, h, own0 - BLK:own0, :]) + tb_ref[1, h]
            pieces.append((s_adj, own0 - BLK, BLK))
        if jj >= 2:
            s_far = _dot_nt(q, k_ref[0, h, 0:own0 - BLK, :])
            pieces.append((s_far, 0, own0 - BLK))
        yield
        m = None
        for sc, _, _ in pieces:
            mm = jnp.max(sc, axis=1, keepdims=True)
            m = mm if m is None else jnp.maximum(m, mm)
        l = None
        probs = []
        for sc, start, size in pieces:
            e = jnp.exp2(sc - m)
            ls = jnp.sum(e, axis=1, keepdims=True)
            l = ls if l is None else l + ls
            probs.append((e.astype(BF16), start, size))
        yield
        acc = None
        for pb, start, size in probs:
            pv = _dot(pb, v_ref[0, p, start:start + size, :])
            acc = pv if acc is None else acc + pv
        return acc / l

    for jj in range(N_BLK):
        @pl.when(j == jj)
        def _(jj=jj):
            def group(gi, carry):
                pairs = [gi * ATTN_PAIRS_PER_TRIP + pi for pi in range(ATTN_PAIRS_PER_TRIP)]
                outs = _interleave([one_head(2 * p + hh, p, jj) for p in pairs for hh in range(2)])
                for pi, p in enumerate(pairs):
                    o_ref[0, p] = jnp.where(lane < HEAD_DIM, outs[2 * pi], outs[2 * pi + 1]).astype(o_ref.dtype)
                return carry
            lax.fori_loop(0, N_HEADS // 2 // ATTN_PAIRS_PER_TRIP, group, 0)


def _memkv_kernel(mem_ref, wk_ref, wv_ref, k_ref, v_ref):
    mb = mem_ref[0].astype(BF16)
    k_ref[0] = _dot(mb, wk_ref[...]).astype(k_ref.dtype)
    v_ref[0] = _dot(mb, wv_ref[...]).astype(v_ref.dtype)


def _merge_kernel(x_ref, ypool_ref, o_ref, kmem_ref, vmem_ref,
                  lng_ref, lnb_ref, wgl_ref, bgate_ref, wpu_ref, wau_ref, wout_ref,
                  ln1g_ref, ln1b_ref, wmq_ref, wmo_ref, ln2g_ref, ln2b_ref,
                  wrh_ref, br_ref,
                  h2_ref, h2b_ref, comb_ref, route_ref, route_t_ref, cnt_ref, *, tm):
    _interleave([_merge_chain(ci, x_ref, ypool_ref, o_ref, kmem_ref, vmem_ref,
                              lng_ref, lnb_ref, wgl_ref, bgate_ref, wpu_ref, wau_ref, wout_ref,
                              ln1g_ref, ln1b_ref, wmq_ref, wmo_ref, ln2g_ref, ln2b_ref,
                              wrh_ref, br_ref, h2_ref, h2b_ref, comb_ref, route_ref, route_t_ref, cnt_ref)
                 for ci in range(tm // TOK_TILE)])


def _merge_chain(ci, x_ref, ypool_ref, o_ref, kmem_ref, vmem_ref,
                 lng_ref, lnb_ref, wgl_ref, bgate_ref, wpu_ref, wau_ref, wout_ref,
                 ln1g_ref, ln1b_ref, wmq_ref, wmo_ref, ln2g_ref, ln2b_ref,
                 wrh_ref, br_ref, h2_ref, h2b_ref, comb_ref, route_ref, route_t_ref, cnt_ref):
    rows = slice(ci * TOK_TILE, (ci + 1) * TOK_TILE)
    n_rows = TOK_TILE
    h = _ln(x_ref[0, rows, :], lng_ref[...], lnb_ref[...])
    hb = h.astype(BF16)
    gl = _dot(hb, wgl_ref[...]) + bgate_ref[...]
    yield
    gates = 0.5 * jnp.tanh(0.5 * gl) + 0.5
    y_pool = _dot(ypool_ref[0, rows, :], wpu_ref[...])
    o_cat = jnp.concatenate([o_ref[0, p, rows, :] for p in range(ATTN_W // LANE)], axis=1)
    y_attn = _dot(o_cat, wau_ref[...])
    yield
    merged = gates[:, 0:D] * y_pool + gates[:, D:2 * D] * y_attn
    mix = _dot(merged.astype(BF16), wout_ref[...])
    yield
    h1 = _ln(DN_ALPHA * h + mix, ln1g_ref[...], ln1b_ref[...])

    qm = _dot(h1.astype(BF16), wmq_ref[...]).astype(BF16)
    yield
    outs = []
    for hd in range(MEM_HEADS):
        cols = slice(hd * MEM_HD, (hd + 1) * MEM_HD)
        sc = _dot_nt(qm[:, cols], kmem_ref[0, :, cols])
        m = jnp.max(sc, axis=1, keepdims=True)
        e = jnp.exp2(sc - m)
        l = jnp.sum(e, axis=1, keepdims=True)
        outs.append(_dot(e.astype(BF16), vmem_ref[0, :, cols]) / l)
    om = jnp.concatenate(outs, axis=1).astype(BF16)
    xa = _dot(om, wmo_ref[...])
    yield
    h2 = _ln(DN_ALPHA * h1 + xa, ln2g_ref[...], ln2b_ref[...])
    h2_ref[0, rows, :] = h2
    h2b_ref[0, rows, :] = h2.astype(BF16)

    x_hi, x_lo = _split(h2)
    r2 = _dot(x_hi, wrh_ref[...])
    r = r2[:, 0:LANE] + r2[:, LANE:2 * LANE] + _dot(x_lo, wrh_ref[:, 0:LANE]) + br_ref[...]
    yield
    lane = lax.broadcasted_iota(jnp.int32, (n_rows, LANE), 1)
    lane_f = lane.astype(F32)
    cmask = (lane >= N_EXPERTS) & (lane < 2 * N_EXPERTS)
    c = jnp.where(cmask, r, -jnp.inf)
    cmax = jnp.max(c, axis=1, keepdims=True)
    ce = jnp.exp(c - cmax)
    csum = jnp.sum(ce, axis=1, keepdims=True) * (1.0 / EPG)
    g_prob = 1.0 / csum
    grp_lane = ((lane & (N_EXPERTS - 1)) >> 3).astype(F32)
    gidx = jnp.min(jnp.where(cmask & (c == cmax), grp_lane, 99.0), axis=1, keepdims=True)
    fmask = (lane < N_EXPERTS) & (grp_lane == gidx)
    f = jnp.where(fmask, r, -jnp.inf)
    fmax = jnp.max(f, axis=1, keepdims=True)
    fe = jnp.exp(f - fmax)
    fsum = jnp.sum(fe, axis=1, keepdims=True)
    prob = fe / fsum
    p1 = jnp.max(prob, axis=1, keepdims=True)
    i1 = jnp.min(jnp.where(fmask & (prob == p1), lane_f, 999.0), axis=1, keepdims=True)
    rest = fmask & (lane_f != i1)
    prob2 = jnp.where(rest, prob, -1.0)
    p2 = jnp.max(prob2, axis=1, keepdims=True)
    i2 = jnp.min(jnp.where(rest & (prob2 == p2), lane_f, 999.0), axis=1, keepdims=True)
    den = p1 + p2
    comb = jnp.where(lane_f == i1, g_prob * (p1 / den),
                     jnp.where(lane_f == i2, g_prob * (p2 / den), 0.0))
    comb_ref[0, rows, :] = comb
    yield

    sel = jnp.where((lane_f == i1) | (lane_f == i2), 1.0, 0.0)
    cnt = jnp.sum(sel, axis=0, keepdims=True)
    pc = jnp.floor((cnt + (SEG_ALIGN - 1)) * (1.0 / SEG_ALIGN)) * SEG_ALIGN
    lane8 = lax.broadcasted_iota(jnp.int32, (8, LANE), 1)
    inc = jnp.broadcast_to(pc, (8, LANE))
    for sh in (1, 2, 4, 8, 16, 32, 64):
        inc = inc + jnp.where(lane8 >= sh, pltpu.roll(inc, sh, 1), 0.0)
    seg_start = inc[0:1] - pc
    t_row = lax.broadcasted_iota(jnp.int32, (n_rows, n_rows), 0)
    t_col = lax.broadcasted_iota(jnp.int32, (n_rows, n_rows), 1)
    earlier = jnp.where(t_row > t_col, 1.0, 0.0).astype(BF16)
    rank = _dot(earlier, sel.astype(BF16))
    pos = seg_start + rank
    d1 = jnp.sum(jnp.where(lane_f == i1, pos, 0.0), axis=1, keepdims=True)
    d2 = jnp.sum(jnp.where(lane_f == i2, pos, 0.0), axis=1, keepdims=True)
    route = jnp.where(lane == 0, d1, jnp.where(lane == 1, d2, 0.0))
    route_ref[0, rows, :] = route
    r_hi, r_lo = _split(route)
    eye = jnp.where(lax.broadcasted_iota(jnp.int32, (8, LANE), 0) == lane8, 1.0, 0.0).astype(BF16)
    route_t_ref[ci] = _dot_nt(eye, r_hi) + _dot_nt(eye, r_lo)
    cnt_ref[ci] = jnp.broadcast_to(pc, (8, LANE))


def _chunk_copies(sub, nch_ref, tab_ref, make_copy, act):
    base = sub * CHUNK_TAB_W
    for size, cnt_i, lo_off, go_off in ((BIG_CHUNK, 0, 0, MAX_BIG), (SEG_ALIGN, 1, 2 * MAX_BIG, 2 * MAX_BIG + N_EXPERTS)):
        def body(k, carry, size=size, lo_off=lo_off, go_off=go_off):
            lo = pl.multiple_of(tab_ref[base + lo_off + k], SEG_ALIGN)
            go = pl.multiple_of(tab_ref[base + go_off + k], SEG_ALIGN)
            act(make_copy(lo, go, size))
            return carry
        lax.fori_loop(0, nch_ref[2 * sub + cnt_i], body, 0)


def _dispatch_kernel(nch_ref, tab_ref, tail_ref, xb_ref, comb_ref, route_t_ref, xs_ref,
                     cbuf, zbuf, sem, zsem, *, n_steps, rows):
    i = pl.program_id(0)
    par = lax.rem(i, 2)
    n_u = MOE_SUBS_PER_STEP
    tm = TOK_TILE

    def copies(step, par_, act):
        for u in range(n_u):
            slot_ = par_ * n_u + u

            def mk(lo, go, size, slot_=slot_):
                return pltpu.make_async_copy(cbuf.at[slot_, pl.ds(lo, size), :],
                                             xs_ref.at[pl.ds(go, size), :], sem.at[slot_])
            _chunk_copies(step * n_u + u, nch_ref, tab_ref, mk, act)

    @pl.when(i >= 2)
    def _():
        copies(i - 2, par, lambda c: c.wait())

    def chain(u):
        rs = slice(u * tm, (u + 1) * tm)
        r_io = lax.broadcasted_iota(jnp.int32, (rows, tm), 0).astype(F32)
        d1 = route_t_ref[u, 0:1, :]
        d2 = route_t_ref[u, 1:2, :]
        p_mat = jnp.where((r_io == d1) | (r_io == d2), 1.0, 0.0).astype(BF16)
        comb = comb_ref[rs, :]
        c_hi = comb.astype(BF16).astype(F32)
        c_pack = (c_hi + pltpu.roll(comb - c_hi, N_EXPERTS, 1)).astype(BF16)
        x_aug = jnp.concatenate([xb_ref[rs, :], c_pack], axis=1)
        yield
        cbuf[par * n_u + u] = _dot(p_mat, x_aug).astype(BF16)

    _interleave([chain(u) for u in range(n_u)])
    copies(i, par, lambda c: c.start())

    @pl.when(i == n_steps - 1)
    def _():
        if n_steps >= 2:
            copies(i - 1, 1 - par, lambda c: c.wait())
        copies(i, par, lambda c: c.wait())
        zbuf[...] = jnp.zeros_like(zbuf)

        def tails(act):
            def body(e, carry):
                st = tail_ref[e]
                n = tail_ref[N_EXPERTS + e]
                off = jnp.int32(0)
                size = zbuf.shape[0]
                while size >= SEG_ALIGN:
                    bit = (n & size) != 0

                    @pl.when(bit)
                    def _(size=size, off=off):
                        act(pltpu.make_async_copy(
                            zbuf.at[pl.ds(0, size), :],
                            xs_ref.at[pl.ds(pl.multiple_of(st + off, SEG_ALIGN), size), :], zsem))
                    off = off + jnp.where(bit, size, 0)
                    size //= 2
                return carry
            lax.fori_loop(0, N_EXPERTS, body, 0)
        tails(lambda c: c.start())
        tails(lambda c: c.wait())


def _ffn_kernel(texp_ref, nused_ref, xs_ref, wg_ref, wu_ref, wd_ref, ys_ref, wg_b, wu_b, wd_b, xbuf, xsem):
    t = pl.program_id(0)
    e = texp_ref[t]
    n_used = nused_ref[0]

    def fetch(u):
        slot = lax.rem(u, XS_DEPTH)
        return pltpu.make_async_copy(xs_ref.at[pl.ds(pl.multiple_of(u * FFN_TILE, FFN_TILE), FFN_TILE), :],
                                     xbuf.at[slot], xsem.at[slot])

    @pl.when(t == 0)
    def _():
        for u in range(XS_DEPTH - 1):
            @pl.when(u < n_used)
            def _(u=u):
                fetch(jnp.int32(u)).start()

    @pl.when(t + (XS_DEPTH - 1) < n_used)
    def _():
        fetch(t + (XS_DEPTH - 1)).start()

    @pl.when((t == 0) | (e != texp_ref[jnp.maximum(t - 1, 0)]))
    def _():
        wg_b[...] = wg_ref[0].astype(BF16)
        wu_b[...] = wu_ref[0].astype(BF16)
        wd_b[...] = wd_ref[0].astype(BF16)

    @pl.when(t < n_used)
    def _():
        fetch(t).wait()
        slot = lax.rem(t, XS_DEPTH)

        def chain(ci):
            rs = slice(ci * FFN_CHAIN, (ci + 1) * FFN_CHAIN)
            xrow = xbuf[slot, rs, 0:D]
            cw = xbuf[slot, rs, D:D + LANE].astype(F32)
            lane = lax.broadcasted_iota(jnp.int32, cw.shape, 1)
            c = jnp.sum(jnp.where((lane == e) | (lane == e + N_EXPERTS), cw, 0.0), axis=1, keepdims=True)
            a = _dot(xrow, wg_b[...])
            b = _dot(xrow, wu_b[...])
            yield
            hid = ((a * jax.nn.sigmoid(a)) * b * c).astype(BF16)
            yield
            ys_ref[rs, :] = _dot(hid, wd_b[...]).astype(ys_ref.dtype)

        _interleave([chain(ci) for ci in range(FFN_TILE // FFN_CHAIN)])


def _combine_kernel(nch_ref, tab_ref, route_ref, h2_ref, g_ref, b_ref, ys_ref, out_ref,
                    ybuf, sem, *, n_steps, rows):
    i = pl.program_id(0)
    par = lax.rem(i, 2)
    n_u = MOE_SUBS_PER_STEP
    tm = TOK_TILE

    def copies(step, par_, act):
        for u in range(n_u):
            slot_ = par_ * n_u + u

            def mk(lo, go, size, slot_=slot_):
                return pltpu.make_async_copy(ys_ref.at[pl.ds(go, size), :],
                                             ybuf.at[slot_, pl.ds(lo, size), :], sem.at[slot_])
            _chunk_copies(step * n_u + u, nch_ref, tab_ref, mk, act)

    @pl.when(i == 0)
    def _():
        ybuf[...] = jnp.zeros_like(ybuf)
        copies(0, 0, lambda c: c.start())

    @pl.when(i + 1 < n_steps)
    def _():
        copies(i + 1, 1 - par, lambda c: c.start())

    copies(i, par, lambda c: c.wait())

    def chain(u):
        rs = slice(u * tm, (u + 1) * tm)
        r_io = lax.broadcasted_iota(jnp.int32, (tm, rows), 1).astype(F32)
        d1 = route_ref[rs, 0:1]
        d2 = route_ref[rs, 1:2]
        p_t = jnp.where((r_io == d1) | (r_io == d2), 1.0, 0.0).astype(BF16)
        yield
        ff = _dot(p_t, ybuf[par * n_u + u])
        yield
        out_ref[rs, :] = _ln(DN_ALPHA * h2_ref[rs, :] + ff, g_ref[...], b_ref[...])

    _interleave([chain(u) for u in range(n_u)])


def _bias_kernel(tbl_ref, bkt_ref, out_ref):
    h = pl.program_id(0)
    far = tbl_ref[h, REL_BUCKETS - 1]
    for which in range(2):
        bk = bkt_ref[which]
        acc = jnp.where(bk < 0, NEG, 0.0)
        for kk in range(REL_BUCKETS):
            acc = jnp.where(bk == kk, (tbl_ref[h, kk] - far) * LOG2E, acc)
        out_ref[which, 0] = acc


def _rel_bucket_table(dist):
    max_exact = REL_BUCKETS // 2
    d = jnp.maximum(dist, 0)
    large = max_exact + (jnp.log(jnp.maximum(d, 1).astype(F32) / max_exact)
                         / math.log(REL_MAX_DIST / max_exact) * (REL_BUCKETS - max_exact)).astype(jnp.int32)
    large = jnp.minimum(large, REL_BUCKETS - 1)
    return jnp.where(d < max_exact, d, large)


def _const_spec(shape):
    nd = len(shape)
    return pl.BlockSpec(shape, lambda *_: (0,) * nd)


def kernel(x, mem, ln_in_g, ln_in_b, rel_bias, w_in, b_gate, w_pool_grp, pool_scale, w_pool_up, w_attn_up,
           w_mix_out, ln1_g, ln1_b, w_mq, w_mk, w_mv, w_mo, ln2_g, ln2_b, w_coarse, b_coarse, w_fine, b_fine,
           w_gate, w_up, w_down, ln3_g, ln3_b):
    B, S, _ = x.shape
    assert S == N_BLK * BLK and w_in.shape[0] == 1
    M = mem.shape[1]
    T = B * S
    tm = 512

    wi = w_in[0]
    w_u = wi[:, 0:POOL_W]
    w_q = wi[:, POOL_W:POOL_W + ATTN_W] * (HEAD_DIM ** -0.5 * LOG2E)
    w_k = wi[:, POOL_W + ATTN_W:POOL_W + 2 * ATTN_W]
    w_v = wi[:, POOL_W + 2 * ATTN_W:POOL_W + 3 * ATTN_W]
    w_gl = wi[:, POOL_W + 3 * ATTN_W:]

    w1 = jnp.concatenate([w_u, w_q, w_k, w_v], axis=1).astype(BF16)
    row2 = lambda a: a.reshape(1, -1)

    iq = jnp.arange(BLK, dtype=jnp.int32)[:, None]
    ik = jnp.arange(BLK, dtype=jnp.int32)[None, :]
    d_own = iq - ik
    bkt = jnp.stack([jnp.where(d_own >= 0, _rel_bucket_table(d_own), -1), _rel_bucket_table(d_own + BLK)])
    t_bias = pl.pallas_call(
        _bias_kernel,
        grid=(N_HEADS,),
        in_specs=[pl.BlockSpec(memory_space=pltpu.SMEM), _const_spec((2, BLK, BLK))],
        out_specs=pl.BlockSpec((2, 1, BLK, BLK), lambda h: (0, h, 0, 0)),
        out_shape=jax.ShapeDtypeStruct((2, N_HEADS, BLK, BLK), F32),
        name="relbias_tiles",
    )(rel_bias.T, bkt)

    n_w1 = w1.shape[1]
    tmp = MERGE_TILE
    ypool, q_aug, k_aug, v_p = pl.pallas_call(
        functools.partial(_proj_kernel, tm=tmp),
        grid=(B, S // tmp),
        in_specs=[
            pl.BlockSpec((1, tmp, D), lambda b, s: (b, s, 0)),
            _const_spec((1, D)), _const_spec((1, D)),
            _const_spec((D, n_w1)),
            _const_spec((len(POOL_WINDOWS), LANE, LANE)),
            _const_spec((1, POOL_W)),
        ],
        out_specs=[
            pl.BlockSpec((1, tmp, POOL_W), lambda b, s: (b, s, 0)),
            pl.BlockSpec((1, N_HEADS, tmp, LANE), lambda b, s: (b, 0, s, 0)),
            pl.BlockSpec((1, N_HEADS, tmp, LANE), lambda b, s: (b, 0, s, 0)),
            pl.BlockSpec((1, ATTN_W // LANE, tmp, LANE), lambda b, s: (b, 0, s, 0)),
        ],
        out_shape=[
            jax.ShapeDtypeStruct((B, S, POOL_W), BF16),
            jax.ShapeDtypeStruct((B, N_HEADS, S, LANE), BF16),
            jax.ShapeDtypeStruct((B, N_HEADS, S, LANE), BF16),
            jax.ShapeDtypeStruct((B, ATTN_W // LANE, S, LANE), BF16),
        ],
        scratch_shapes=[pltpu.VMEM((HALO + tmp, POOL_W), F32), pltpu.VMEM((LANE, ATTN_W), F32)],
        compiler_params=pltpu.CompilerParams(dimension_semantics=("arbitrary", "arbitrary"),
                                             vmem_limit_bytes=VMEM_LIMIT),
        name="proj_pool_gate",
    )(x, row2(ln_in_g), row2(ln_in_b), w1, w_pool_grp[0].astype(BF16), row2(pool_scale[0]))

    o_attn = pl.pallas_call(
        _attn_kernel,
        grid=(B, N_BLK),
        in_specs=[
            pl.BlockSpec((1, N_HEADS, BLK, LANE), lambda b, j: (b, 0, j, 0)),
            pl.BlockSpec((1, N_HEADS, S, LANE), lambda b, j: (b, 0, 0, 0)),
            pl.BlockSpec((1, ATTN_W // LANE, S, LANE), lambda b, j: (b, 0, 0, 0)),
            _const_spec((2, N_HEADS, BLK, BLK)),
        ],
        out_specs=pl.BlockSpec((1, ATTN_W // LANE, BLK, LANE), lambda b, j: (b, 0, j, 0)),
        out_shape=jax.ShapeDtypeStruct((B, ATTN_W // LANE, S, LANE), BF16),
        compiler_params=pltpu.CompilerParams(dimension_semantics=("arbitrary", "arbitrary"),
                                             vmem_limit_bytes=VMEM_LIMIT),
        name="moba_attn",
    )(q_aug, k_aug, v_p, t_bias)

    kmem, vmem = pl.pallas_call(
        _memkv_kernel,
        grid=(B,),
        in_specs=[pl.BlockSpec((1, M, D), lambda b: (b, 0, 0)),
                  _const_spec((D, MEM_W)), _const_spec((D, MEM_W))],
        out_specs=[pl.BlockSpec((1, M, MEM_W), lambda b: (b, 0, 0)),
                   pl.BlockSpec((1, M, MEM_W), lambda b: (b, 0, 0))],
        out_shape=[jax.ShapeDtypeStruct((B, M, MEM_W), BF16)] * 2,
        compiler_params=pltpu.CompilerParams(dimension_semantics=("arbitrary",)),
        name="mem_kv",
    )(mem, w_mk[0].astype(BF16), w_mv[0].astype(BF16))

    w_r = jnp.concatenate([
        w_fine[0].reshape(D, N_EXPERTS),
        jnp.repeat(w_coarse[0], EPG, axis=1),
        jnp.zeros((D, LANE - 2 * N_EXPERTS), F32)], axis=1)
    b_r = jnp.concatenate([
        b_fine[0].reshape(N_EXPERTS), jnp.repeat(b_coarse[0], EPG),
        jnp.zeros((LANE - 2 * N_EXPERTS,), F32)]).reshape(1, LANE)
    w_r_hi = w_r.astype(BF16)
    w_r_lo = (w_r - w_r_hi.astype(F32)).astype(BF16)
    w_r_cat = jnp.concatenate([w_r_hi, w_r_lo], axis=1)

    n_sub = T // tm
    tmm = MERGE_TILE
    per = tmm // tm
    sub_idx = lambda b, s: b * (S // tmm) + s
    h2, h2b, comb, route, route_t, seg_cnt = pl.pallas_call(
        functools.partial(_merge_kernel, tm=tmm),
        grid=(B, S // tmm),
        in_specs=[
            pl.BlockSpec((1, tmm, D), lambda b, s: (b, s, 0)),
            pl.BlockSpec((1, tmm, POOL_W), lambda b, s: (b, s, 0)),
            pl.BlockSpec((1, ATTN_W // LANE, tmm, LANE), lambda b, s: (b, 0, s, 0)),
            pl.BlockSpec((1, M, MEM_W), lambda b, s: (b, 0, 0)),
            pl.BlockSpec((1, M, MEM_W), lambda b, s: (b, 0, 0)),
            _const_spec((1, D)), _const_spec((1, D)),
            _const_spec((D, 2 * D)), _const_spec((1, 2 * D)),
            _const_spec((POOL_W, D)), _const_spec((ATTN_W, D)), _const_spec((D, D)),
            _const_spec((1, D)), _const_spec((1, D)),
            _const_spec((D, MEM_W)), _const_spec((MEM_W, D)),
            _const_spec((1, D)), _const_spec((1, D)),
            _const_spec((D, 2 * LANE)), _const_spec((1, LANE)),
        ],
        out_specs=[
            pl.BlockSpec((1, tmm, D), lambda b, s: (b, s, 0)),
            pl.BlockSpec((1, tmm, D), lambda b, s: (b, s, 0)),
            pl.BlockSpec((1, tmm, LANE), lambda b, s: (b, s, 0)),
            pl.BlockSpec((1, tmm, LANE), lambda b, s: (b, s, 0)),
            pl.BlockSpec((per, 8, tm), lambda b, s: (sub_idx(b, s), 0, 0)),
            pl.BlockSpec((per, 8, LANE), lambda b, s: (sub_idx(b, s), 0, 0)),
        ],
        out_shape=[
            jax.ShapeDtypeStruct((B, S, D), F32),
            jax.ShapeDtypeStruct((B, S, D), BF16),
            jax.ShapeDtypeStruct((B, S, LANE), F32),
            jax.ShapeDtypeStruct((B, S, LANE), F32),
            jax.ShapeDtypeStruct((n_sub, 8, tm), F32),
            jax.ShapeDtypeStruct((n_sub, 8, LANE), F32),
        ],
        compiler_params=pltpu.CompilerParams(dimension_semantics=("arbitrary", "arbitrary"),
                                             vmem_limit_bytes=VMEM_LIMIT),
        name="merge_memattn_router",
    )(x, ypool, o_attn, kmem, vmem,
      row2(ln_in_g), row2(ln_in_b), w_gl.astype(BF16), row2(b_gate[0]),
      w_pool_up[0].astype(BF16), w_attn_up[0].astype(BF16), w_mix_out[0].astype(BF16),
      row2(ln1_g[0]), row2(ln1_b[0]),
      (w_mq[0] * (MEM_HD ** -0.5 * LOG2E)).astype(BF16), w_mo[0].astype(BF16),
      row2(ln2_g[0]), row2(ln2_b[0]),
      w_r_cat, b_r)

    pcs = seg_cnt[:, 0, :N_EXPERTS].astype(jnp.int32)
    tot = jnp.sum(pcs, axis=0)
    cap = ((tot + FFN_TILE - 1) // FFN_TILE) * FFN_TILE
    ends = jnp.cumsum(cap)
    base = ends - cap
    gs = base[None, :] + jnp.cumsum(pcs, axis=0) - pcs
    ls = jnp.cumsum(pcs, axis=1) - pcs
    n_sorted = n_sub * COMPACT_ROWS + N_EXPERTS * FFN_TILE
    n_ffn_tiles = n_sorted // FFN_TILE
    n_used = (ends[-1] // FFN_TILE).astype(jnp.int32)
    tile_row = jnp.arange(n_ffn_tiles, dtype=jnp.int32) * FFN_TILE
    tile_exp = jnp.sum(jnp.minimum(tile_row, ends[-1] - 1)[:, None] >= ends[None, :], axis=1).astype(jnp.int32)
    tails = jnp.concatenate([base + tot, cap - tot]).astype(jnp.int32)
    def kth_of(counts, n_slots, seg_off):
        end = jnp.cumsum(counts, axis=1)
        k = jnp.arange(n_slots, dtype=jnp.int32)
        exp_k = jnp.sum(k[None, :, None] >= end[:, None, :], axis=2)
        hot = exp_k[:, :, None] == jnp.arange(N_EXPERTS)[None, None, :]
        pick = lambda tbl: jnp.sum(jnp.where(hot, tbl[:, None, :], 0), axis=2)
        within = k[None, :] - pick(end - counts)
        return pick(ls + seg_off), pick(gs + seg_off), within

    n_big = pcs // BIG_CHUNK
    n_end = (pcs // SEG_ALIGN) % 2
    lo_b, go_b, m_b = kth_of(n_big, MAX_BIG, 0)
    lo_e, go_e, _ = kth_of(n_end, N_EXPERTS, pcs - SEG_ALIGN)
    chunk_tab = jnp.concatenate([lo_b + BIG_CHUNK * m_b, go_b + BIG_CHUNK * m_b, lo_e, go_e],
                                axis=1).astype(jnp.int32).reshape(-1)
    n_chunks = jnp.stack([jnp.sum(n_big, axis=1), jnp.sum(n_end, axis=1)], axis=1).astype(jnp.int32).reshape(-1)

    aug_w = D + LANE
    n_u = MOE_SUBS_PER_STEP
    n_steps = n_sub // n_u
    x_sorted = pl.pallas_call(
        functools.partial(_dispatch_kernel, n_steps=n_steps, rows=COMPACT_ROWS),
        grid_spec=pltpu.PrefetchScalarGridSpec(
            num_scalar_prefetch=3,
            grid=(n_steps,),
            in_specs=[
                pl.BlockSpec((n_u * tm, D), lambda i, *_: (i, 0)),
                pl.BlockSpec((n_u * tm, LANE), lambda i, *_: (i, 0)),
                pl.BlockSpec((n_u, 8, tm), lambda i, *_: (i, 0, 0)),
            ],
            out_specs=pl.BlockSpec(memory_space=pl.ANY),
            scratch_shapes=[
                pltpu.VMEM((2 * n_u, COMPACT_ROWS, aug_w), BF16),
                pltpu.VMEM((FFN_TILE // 2, aug_w), BF16),
                pltpu.SemaphoreType.DMA((2 * n_u,)),
                pltpu.SemaphoreType.DMA(()),
            ],
        ),
        out_shape=jax.ShapeDtypeStruct((n_sorted, aug_w), BF16),
        compiler_params=pltpu.CompilerParams(dimension_semantics=("arbitrary",), vmem_limit_bytes=VMEM_LIMIT),
        name="moe_dispatch",
    )(n_chunks, chunk_tab, tails, h2b.reshape(T, D), comb.reshape(T, LANE), route_t)

    used_tile = lambda t, texp, nused: (jnp.minimum(t, nused[0] - 1), 0)
    y_sorted = pl.pallas_call(
        _ffn_kernel,
        grid_spec=pltpu.PrefetchScalarGridSpec(
            num_scalar_prefetch=2,
            grid=(n_ffn_tiles,),
            in_specs=[
                pl.BlockSpec(memory_space=pl.ANY),
                pl.BlockSpec((1, D, FF), lambda t, texp, nused: (texp[t], 0, 0)),
                pl.BlockSpec((1, D, FF), lambda t, texp, nused: (texp[t], 0, 0)),
                pl.BlockSpec((1, FF, D), lambda t, texp, nused: (texp[t], 0, 0)),
            ],
            out_specs=pl.BlockSpec((FFN_TILE, D), used_tile),
            scratch_shapes=[pltpu.VMEM((D, FF), BF16), pltpu.VMEM((D, FF), BF16), pltpu.VMEM((FF, D), BF16),
                            pltpu.VMEM((XS_DEPTH, FFN_TILE, aug_w), BF16), pltpu.SemaphoreType.DMA((XS_DEPTH,))],
        ),
        out_shape=jax.ShapeDtypeStruct((n_sorted, D), BF16),
        compiler_params=pltpu.CompilerParams(dimension_semantics=("arbitrary",), vmem_limit_bytes=VMEM_LIMIT),
        name="moe_expert_ffn",
    )(tile_exp, n_used.reshape(1), x_sorted, w_gate[0], w_up[0], w_down[0])

    out = pl.pallas_call(
        functools.partial(_combine_kernel, n_steps=n_steps, rows=COMPACT_ROWS),
        grid_spec=pltpu.PrefetchScalarGridSpec(
            num_scalar_prefetch=2,
            grid=(n_steps,),
            in_specs=[
                pl.BlockSpec((n_u * tm, LANE), lambda i, *_: (i, 0)),
                pl.BlockSpec((n_u * tm, D), lambda i, *_: (i, 0)),
                pl.BlockSpec((1, D), lambda i, *_: (0, 0)),
                pl.BlockSpec((1, D), lambda i, *_: (0, 0)),
                pl.BlockSpec(memory_space=pl.ANY),
            ],
            out_specs=pl.BlockSpec((n_u * tm, D), lambda i, *_: (i, 0)),
            scratch_shapes=[
                pltpu.VMEM((2 * n_u, COMPACT_ROWS, D), BF16),
                pltpu.SemaphoreType.DMA((2 * n_u,)),
            ],
        ),
        out_shape=jax.ShapeDtypeStruct((T, D), F32),
        compiler_params=pltpu.CompilerParams(dimension_semantics=("arbitrary",), vmem_limit_bytes=VMEM_LIMIT),
        name="moe_combine_ln3",
    )(n_chunks, chunk_tab, route.reshape(T, LANE), h2.reshape(T, D), row2(ln3_g[0]), row2(ln3_b[0]), y_sorted)
    return out.reshape(B, S, D)
```

```python
import functools
import math

import jax
import jax.numpy as jnp
from jax import lax
from jax.experimental import pallas as pl
from jax.experimental.pallas import tpu as pltpu

D = 1024
POOL_WINDOWS = (2, 4, 8, 16)
POOL_W = 512
N_HEADS = 8
HEAD_DIM = 64
ATTN_W = 512
BLK = 256
N_BLK = 8
TOPK = 3
REL_BUCKETS = 32
REL_MAX_DIST = 128
MEM_HEADS = 4
MEM_HD = 128
MEM_W = 512
N_GROUPS = 4
EPG = 8
N_EXPERTS = 32
FF = 256
DN_ALPHA = 2.0 ** 0.25
LN_EPS = 1e-5

LANE = 128
TOK_TILE = 512
SEG_ALIGN = 16
COMPACT_ROWS = 2 * TOK_TILE + N_EXPERTS * SEG_ALIGN
BIG_CHUNK = 2 * SEG_ALIGN
MAX_BIG = COMPACT_ROWS // BIG_CHUNK
CHUNK_TAB_W = 2 * MAX_BIG + 2 * N_EXPERTS
DISPATCH_MAIN_ROWS = 2 * TOK_TILE + 18 * SEG_ALIGN
FFN_TILE = 512
PROJ_CHAIN = 256
ATTN_PAIRS_PER_TRIP = 2
MOE_SUBS_PER_STEP = 2
FFN_CHAIN = 512
XS_DEPTH = 3
MERGE_TILE = 1024
HALO = 16
NEG = -1e30
LOG2E = math.log2(math.e)
VMEM_LIMIT = 56 * 1024 * 1024

F32 = jnp.float32
BF16 = jnp.bfloat16

_NT = (((1,), (1,)), ((), ()))


def _dot(a, b):
    return jnp.dot(a, b, preferred_element_type=F32)


def _dot_nt(a, b):
    return lax.dot_general(a, b, _NT, preferred_element_type=F32)


def _split(a):
    hi = a.astype(BF16)
    lo = (a - hi.astype(F32)).astype(BF16)
    return hi, lo


def _interleave(chains):
    results = [None] * len(chains)
    live = list(range(len(chains)))
    while live:
        for ci in list(live):
            try:
                next(chains[ci])
            except StopIteration as done:
                results[ci] = done.value
                live.remove(ci)
    return results


def _ln(x, g, b):
    mu = jnp.mean(x, axis=-1, keepdims=True)
    xc = x - mu
    var = jnp.mean(xc * xc, axis=-1, keepdims=True)
    return xc * lax.rsqrt(var + LN_EPS) * g + b


def _proj_kernel(x_ref, g_ref, b_ref, w_ref, wgrp_ref, pscale_ref,
                 ypool_ref, q_ref, k_ref, v_ref, ubuf, kbt, *, tm):
    s = pl.program_id(1)

    @pl.when(s == 0)
    def _():
        ubuf[0:HALO, :] = jnp.zeros((HALO, POOL_W), F32)
        kbt[...] = jnp.zeros_like(kbt)

    _interleave([_proj_chain(ci, s, tm, x_ref, g_ref, b_ref, w_ref, wgrp_ref, pscale_ref,
                             ypool_ref, q_ref, k_ref, v_ref, ubuf, kbt)
                 for ci in range(tm // PROJ_CHAIN)])
    ubuf[0:HALO, :] = ubuf[tm:tm + HALO, :]


def _proj_chain(ci, s, tm, x_ref, g_ref, b_ref, w_ref, wgrp_ref, pscale_ref,
                ypool_ref, q_ref, k_ref, v_ref, ubuf, kbt):
    n = PROJ_CHAIN
    r0 = ci * n
    rows = slice(r0, r0 + n)
    blk0 = s * (tm // BLK) + ci * (n // BLK)

    h = _ln(x_ref[0, rows, :], g_ref[...], b_ref[...])
    hb = h.astype(BF16)
    zu = _dot(hb, w_ref[:, 0:POOL_W])
    zq = _dot(hb, w_ref[:, POOL_W:POOL_W + ATTN_W])
    zk = _dot(hb, w_ref[:, POOL_W + ATTN_W:POOL_W + 2 * ATTN_W])
    zv = _dot(hb, w_ref[:, POOL_W + 2 * ATTN_W:])
    ubuf[HALO + r0:HALO + r0 + n, :] = zu

    r_io = lax.broadcasted_iota(jnp.int32, kbt.shape, 0)
    c_io = lax.broadcasted_iota(jnp.int32, kbt.shape, 1)
    head_match = (r_io >> 3) == (c_io >> 6)
    for bi in range(n // BLK):
        kmean = jnp.mean(zk[bi * BLK:(bi + 1) * BLK], axis=0, keepdims=True)
        kbt[...] = jnp.where(head_match & ((r_io & 7) == blk0 + bi), kmean, kbt[...])
    yield

    t_pos = s * tm + r0 + lax.broadcasted_iota(jnp.int32, (n, LANE), 0)
    for g, w in enumerate(POOL_WINDOWS):
        cols = slice(g * LANE, (g + 1) * LANE)
        ws = ubuf[HALO + r0:HALO + r0 + n, cols]
        for kk in range(1, w):
            ws = ws + ubuf[HALO + r0 - kk:HALO + r0 - kk + n, cols]
        cnt = jnp.minimum(t_pos + 1, w).astype(F32)
        y = ws / cnt - ubuf[HALO + r0:HALO + r0 + n, cols]
        yg = _dot(y.astype(BF16), wgrp_ref[g]) * pscale_ref[:, cols]
        ypool_ref[0, rows, cols] = yg.astype(ypool_ref.dtype)
    yield

    q_hi, q_lo = _split(zq)
    kb_hi, kb_lo = _split(kbt[...])
    g2 = _dot_nt(q_hi, jnp.concatenate([kb_hi, kb_lo], axis=0))
    gate = g2[:, 0:LANE] + g2[:, LANE:2 * LANE] + _dot_nt(q_lo, kb_hi)
    yield

    lane = lax.broadcasted_iota(jnp.int32, (n, LANE), 1)
    row = lax.broadcasted_iota(jnp.int32, (n, LANE), 0)
    n_l = lane & 7
    jrow = blk0 + (row >> 8)
    past = n_l < jrow
    gt = jnp.where(past, gate, -jnp.inf)
    cnt = jnp.zeros((n, LANE), F32)
    for sh in range(1, N_BLK):
        wrap = (n_l + sh) >= N_BLK
        gm = jnp.where(wrap, pltpu.roll(gt, N_BLK - sh, 1), pltpu.roll(gt, LANE - sh, 1))
        cnt = cnt + jnp.where(wrap, jnp.where(gm >= gt, 1.0, 0.0), jnp.where(gm > gt, 1.0, 0.0))
    keep = (past & (cnt < TOPK)) | (n_l == jrow)
    negmask = jnp.where(keep, 0.0, NEG)

    aug_lane = (lane >= HEAD_DIM) & (lane < HEAD_DIM + N_BLK)
    k_onehot = jnp.where(lane == HEAD_DIM + jrow, 1.0, 0.0)
    head_lane = lane < HEAD_DIM
    for hh in range(N_HEADS):
        cols = slice((hh // 2) * LANE, (hh // 2 + 1) * LANE)
        q_h, k_h = zq[:, cols], zk[:, cols]
        if hh % 2:
            q_h, k_h = pltpu.roll(q_h, HEAD_DIM, 1), pltpu.roll(k_h, HEAD_DIM, 1)
        m_h = jnp.where(aug_lane, pltpu.roll(negmask, HEAD_DIM - N_BLK * hh, 1), 0.0)
        q_ref[0, hh, rows, :] = jnp.where(head_lane, q_h, m_h).astype(q_ref.dtype)
        k_ref[0, hh, rows, :] = jnp.where(head_lane, k_h, k_onehot).astype(k_ref.dtype)
    for p in range(ATTN_W // LANE):
        v_ref[0, p, rows, :] = zv[:, p * LANE:(p + 1) * LANE].astype(v_ref.dtype)


def _attn_kernel(q_ref, k_ref, v_ref, tb_ref, o_ref):
    j = pl.program_id(1)
    lane = lax.broadcasted_iota(jnp.int32, (BLK, LANE), 1)

    def one_head(h, p, jj):
        q = q_ref[0, h]
        own0 = jj * BLK
        pieces = []
        s_own = _dot_nt(q, k_ref[0, h, own0:own0 + BLK, :]) + tb_ref[0, h]
        pieces.append((s_own, own0, BLK))
        if jj >= 1:
            s_adj = _dot_nt(q, k_ref[0, h, own0 - BLK:own0, :]) + tb_ref[1, h]
            pieces.append((s_adj, own0 - BLK, BLK))
        if jj >= 2:
            s_far = _dot_nt(q, k_ref[0, h, 0:own0 - BLK, :])
            pieces.append((s_far, 0, own0 - BLK))
        yield
        m = None
        for sc, _, _ in pieces:
            mm = jnp.max(sc, axis=1, keepdims=True)
            m = mm if m is None else jnp.maximum(m, mm)
        l = None
        probs = []
        for sc, start, size in pieces:
            e = jnp.exp2(sc - m)
            ls = jnp.sum(e, axis=1, keepdims=True)
            l = ls if l is None else l + ls
            probs.append((e.astype(BF16), start, size))
        yield
        acc = None
        for pb, start, size in probs:
            pv = _dot(pb, v_ref[0, p, start:start + size, :])
            acc = pv if acc is None else acc + pv
        return acc / l

    for jj in range(N_BLK):
        @pl.when(j == jj)
        def _(jj=jj):
            def group(gi, carry):
                pairs = [gi * ATTN_PAIRS_PER_TRIP + pi for pi in range(ATTN_PAIRS_PER_TRIP)]
                outs = _interleave([one_head(2 * p + hh, p, jj) for p in pairs for hh in range(2)])
                for pi, p in enumerate(pairs):
                    o_ref[0, p] = jnp.where(lane < HEAD_DIM, outs[2 * pi], outs[2 * pi + 1]).astype(o_ref.dtype)
                return carry
            lax.fori_loop(0, N_HEADS // 2 // ATTN_PAIRS_PER_TRIP, group, 0)


def _memkv_kernel(mem_ref, wk_ref, wv_ref, k_ref, v_ref):
    mb = mem_ref[0].astype(BF16)
    k_ref[0] = _dot(mb, wk_ref[...]).astype(k_ref.dtype)
    v_ref[0] = _dot(mb, wv_ref[...]).astype(v_ref.dtype)


def _merge_kernel(x_ref, ypool_ref, o_ref, kmem_ref, vmem_ref,
                  lng_ref, lnb_ref, wgl_ref, bgate_ref, wpu_ref, wau_ref, wout_ref,
                  ln1g_ref, ln1b_ref, wmq_ref, wmo_ref, ln2g_ref, ln2b_ref,
                  wrh_ref, br_ref,
                  h2_ref, h2b_ref, comb_ref, route_ref, route_t_ref, cnt_ref, *, tm):
    _interleave([_merge_chain(ci, x_ref, ypool_ref, o_ref, kmem_ref, vmem_ref,
                              lng_ref, lnb_ref, wgl_ref, bgate_ref, wpu_ref, wau_ref, wout_ref,
                              ln1g_ref, ln1b_ref, wmq_ref, wmo_ref, ln2g_ref, ln2b_ref,
                              wrh_ref, br_ref, h2_ref, h2b_ref, comb_ref, route_ref, route_t_ref, cnt_ref)
                 for ci in range(tm // TOK_TILE)])


def _merge_chain(ci, x_ref, ypool_ref, o_ref, kmem_ref, vmem_ref,
                 lng_ref, lnb_ref, wgl_ref, bgate_ref, wpu_ref, wau_ref, wout_ref,
                 ln1g_ref, ln1b_ref, wmq_ref, wmo_ref, ln2g_ref, ln2b_ref,
                 wrh_ref, br_ref, h2_ref, h2b_ref, comb_ref, route_ref, route_t_ref, cnt_ref):
    rows = slice(ci * TOK_TILE, (ci + 1) * TOK_TILE)
    n_rows = TOK_TILE
    h = _ln(x_ref[0, rows, :], lng_ref[...], lnb_ref[...])
    hb = h.astype(BF16)
    gl = _dot(hb, wgl_ref[...]) + bgate_ref[...]
    yield
    gates = 0.5 * jnp.tanh(0.5 * gl) + 0.5
    y_pool = _dot(ypool_ref[0, rows, :], wpu_ref[...])
    o_cat = jnp.concatenate([o_ref[0, p, rows, :] for p in range(ATTN_W // LANE)], axis=1)
    y_attn = _dot(o_cat, wau_ref[...])
    yield
    merged = gates[:, 0:D] * y_pool + gates[:, D:2 * D] * y_attn
    mix = _dot(merged.astype(BF16), wout_ref[...])
    yield
    h1 = _ln(DN_ALPHA * h + mix, ln1g_ref[...], ln1b_ref[...])

    qm = _dot(h1.astype(BF16), wmq_ref[...]).astype(BF16)
    yield
    outs = []
    for hd in range(MEM_HEADS):
        cols = slice(hd * MEM_HD, (hd + 1) * MEM_HD)
        sc = _dot_nt(qm[:, cols], kmem_ref[0, :, cols])
        m = jnp.max(sc, axis=1, keepdims=True)
        e = jnp.exp2(sc - m)
        l = jnp.sum(e, axis=1, keepdims=True)
        outs.append(_dot(e.astype(BF16), vmem_ref[0, :, cols]) / l)
    om = jnp.concatenate(outs, axis=1).astype(BF16)
    xa = _dot(om, wmo_ref[...])
    yield
    h2 = _ln(DN_ALPHA * h1 + xa, ln2g_ref[...], ln2b_ref[...])
    h2_ref[0, rows, :] = h2
    h2b_ref[0, rows, :] = h2.astype(BF16)

    x_hi, x_lo = _split(h2)
    r2 = _dot(x_hi, wrh_ref[...])
    r = r2[:, 0:LANE] + r2[:, LANE:2 * LANE] + _dot(x_lo, wrh_ref[:, 0:LANE]) + br_ref[...]
    yield
    lane = lax.broadcasted_iota(jnp.int32, (n_rows, LANE), 1)
    lane_f = lane.astype(F32)
    cmask = (lane >= N_EXPERTS) & (lane < 2 * N_EXPERTS)
    c = jnp.where(cmask, r, -jnp.inf)
    cmax = jnp.max(c, axis=1, keepdims=True)
    ce = jnp.exp(c - cmax)
    csum = jnp.sum(ce, axis=1, keepdims=True) * (1.0 / EPG)
    g_prob = 1.0 / csum
    grp_lane = ((lane & (N_EXPERTS - 1)) >> 3).astype(F32)
    gidx = jnp.min(jnp.where(cmask & (c == cmax), grp_lane, 99.0), axis=1, keepdims=True)
    fmask = (lane < N_EXPERTS) & (grp_lane == gidx)
    f = jnp.where(fmask, r, -jnp.inf)
    fmax = jnp.max(f, axis=1, keepdims=True)
    fe = jnp.exp(f - fmax)
    fsum = jnp.sum(fe, axis=1, keepdims=True)
    prob = fe / fsum
    p1 = jnp.max(prob, axis=1, keepdims=True)
    i1 = jnp.min(jnp.where(fmask & (prob == p1), lane_f, 999.0), axis=1, keepdims=True)
    rest = fmask & (lane_f != i1)
    prob2 = jnp.where(rest, prob, -1.0)
    p2 = jnp.max(prob2, axis=1, keepdims=True)
    i2 = jnp.min(jnp.where(rest & (prob2 == p2), lane_f, 999.0), axis=1, keepdims=True)
    den = p1 + p2
    comb = jnp.where(lane_f == i1, g_prob * (p1 / den),
                     jnp.where(lane_f == i2, g_prob * (p2 / den), 0.0))
    comb_ref[0, rows, :] = comb
    yield

    sel = jnp.where((lane_f == i1) | (lane_f == i2), 1.0, 0.0)
    cnt = jnp.sum(sel, axis=0, keepdims=True)
    pc = jnp.floor((cnt + (SEG_ALIGN - 1)) * (1.0 / SEG_ALIGN)) * SEG_ALIGN
    lane8 = lax.broadcasted_iota(jnp.int32, (8, LANE), 1)
    inc = jnp.broadcast_to(pc, (8, LANE))
    for sh in (1, 2, 4, 8, 16, 32, 64):
        inc = inc + jnp.where(lane8 >= sh, pltpu.roll(inc, sh, 1), 0.0)
    seg_start = inc[0:1] - pc
    t_row = lax.broadcasted_iota(jnp.int32, (n_rows, n_rows), 0)
    t_col = lax.broadcasted_iota(jnp.int32, (n_rows, n_rows), 1)
    earlier = jnp.where(t_row > t_col, 1.0, 0.0).astype(BF16)
    rank = _dot(earlier, sel.astype(BF16))
    pos = seg_start + rank
    d1 = jnp.sum(jnp.where(lane_f == i1, pos, 0.0), axis=1, keepdims=True)
    d2 = jnp.sum(jnp.where(lane_f == i2, pos, 0.0), axis=1, keepdims=True)
    route = jnp.where(lane == 0, d1, jnp.where(lane == 1, d2, 0.0))
    route_ref[0, rows, :] = route
    r_hi, r_lo = _split(route)
    eye = jnp.where(lax.broadcasted_iota(jnp.int32, (8, LANE), 0) == lane8, 1.0, 0.0).astype(BF16)
    route_t_ref[ci] = _dot_nt(eye, r_hi) + _dot_nt(eye, r_lo)
    cnt_ref[ci] = jnp.broadcast_to(pc, (8, LANE))


def _chunk_copies(sub, nch_ref, tab_ref, make_copy, act):
    base = sub * CHUNK_TAB_W
    for size, cnt_i, lo_off, go_off in ((BIG_CHUNK, 0, 0, MAX_BIG), (SEG_ALIGN, 1, 2 * MAX_BIG, 2 * MAX_BIG + N_EXPERTS)):
        def body(k, carry, size=size, lo_off=lo_off, go_off=go_off):
            lo = pl.multiple_of(tab_ref[base + lo_off + k], SEG_ALIGN)
            go = pl.multiple_of(tab_ref[base + go_off + k], SEG_ALIGN)
            act(make_copy(lo, go, size))
            return carry
        lax.fori_loop(0, nch_ref[2 * sub + cnt_i], body, 0)


def _dispatch_kernel(nch_ref, tab_ref, tail_ref, xb_ref, comb_ref, route_t_ref, xs_ref,
                     cbuf, zbuf, sem, zsem, *, n_steps, rows):
    i = pl.program_id(0)
    par = lax.rem(i, 2)
    n_u = MOE_SUBS_PER_STEP
    tm = TOK_TILE

    def copies(step, par_, act):
        for u in range(n_u):
            slot_ = par_ * n_u + u

            def mk(lo, go, size, slot_=slot_):
                return pltpu.make_async_copy(cbuf.at[slot_, pl.ds(lo, size), :],
                                             xs_ref.at[pl.ds(go, size), :], sem.at[slot_])
            _chunk_copies(step * n_u + u, nch_ref, tab_ref, mk, act)

    @pl.when(i >= 2)
    def _():
        copies(i - 2, par, lambda c: c.wait())

    def one_hot_rows(u, r_lo, r_hi):
        r_io = (r_lo + lax.broadcasted_iota(jnp.int32, (r_hi - r_lo, tm), 0)).astype(F32)
        d1 = route_t_ref[u, 0:1, :]
        d2 = route_t_ref[u, 1:2, :]
        return jnp.where((r_io == d1) | (r_io == d2), 1.0, 0.0).astype(BF16)

    def augmented(u):
        rs = slice(u * tm, (u + 1) * tm)
        comb = comb_ref[rs, :]
        c_hi = comb.astype(BF16).astype(F32)
        c_pack = (c_hi + pltpu.roll(comb - c_hi, N_EXPERTS, 1)).astype(BF16)
        return jnp.concatenate([xb_ref[rs, :], c_pack], axis=1)

    def chain(u):
        p_mat = one_hot_rows(u, 0, DISPATCH_MAIN_ROWS)
        x_aug = augmented(u)
        yield
        cbuf[par * n_u + u, 0:DISPATCH_MAIN_ROWS, :] = _dot(p_mat, x_aug).astype(BF16)

    _interleave([chain(u) for u in range(n_u)])

    for u in range(n_u):
        sub = i * n_u + u
        used_rows = nch_ref[2 * sub] * BIG_CHUNK + nch_ref[2 * sub + 1] * SEG_ALIGN

        @pl.when(used_rows > DISPATCH_MAIN_ROWS)
        def _(u=u):
            cbuf[par * n_u + u, DISPATCH_MAIN_ROWS:rows, :] = _dot(
                one_hot_rows(u, DISPATCH_MAIN_ROWS, rows), augmented(u)).astype(BF16)

    copies(i, par, lambda c: c.start())

    @pl.when(i == n_steps - 1)
    def _():
        if n_steps >= 2:
            copies(i - 1, 1 - par, lambda c: c.wait())
        copies(i, par, lambda c: c.wait())
        zbuf[...] = jnp.zeros_like(zbuf)

        def tails(act):
            def body(e, carry):
                st = tail_ref[e]
                n = tail_ref[N_EXPERTS + e]
                off = jnp.int32(0)
                size = zbuf.shape[0]
                while size >= SEG_ALIGN:
                    bit = (n & size) != 0

                    @pl.when(bit)
                    def _(size=size, off=off):
                        act(pltpu.make_async_copy(
                            zbuf.at[pl.ds(0, size), :],
                            xs_ref.at[pl.ds(pl.multiple_of(st + off, SEG_ALIGN), size), :], zsem))
                    off = off + jnp.where(bit, size, 0)
                    size //= 2
                return carry
            lax.fori_loop(0, N_EXPERTS, body, 0)
        tails(lambda c: c.start())
        tails(lambda c: c.wait())


def _ffn_kernel(texp_ref, nused_ref, xs_ref, wg_ref, wu_ref, wd_ref, ys_ref, wg_b, wu_b, wd_b, xbuf, xsem):
    t = pl.program_id(0)
    e = texp_ref[t]
    n_used = nused_ref[0]

    def fetch(u):
        slot = lax.rem(u, XS_DEPTH)
        return pltpu.make_async_copy(xs_ref.at[pl.ds(pl.multiple_of(u * FFN_TILE, FFN_TILE), FFN_TILE), :],
                                     xbuf.at[slot], xsem.at[slot])

    @pl.when(t == 0)
    def _():
        for u in range(XS_DEPTH - 1):
            @pl.when(u < n_used)
            def _(u=u):
                fetch(jnp.int32(u)).start()

    @pl.when(t + (XS_DEPTH - 1) < n_used)
    def _():
        fetch(t + (XS_DEPTH - 1)).start()

    @pl.when((t == 0) | (e != texp_ref[jnp.maximum(t - 1, 0)]))
    def _():
        wg_b[...] = wg_ref[0].astype(BF16)
        wu_b[...] = wu_ref[0].astype(BF16)
        wd_b[...] = wd_ref[0].astype(BF16)

    @pl.when(t < n_used)
    def _():
        fetch(t).wait()
        slot = lax.rem(t, XS_DEPTH)

        def chain(ci):
            rs = slice(ci * FFN_CHAIN, (ci + 1) * FFN_CHAIN)
            xrow = xbuf[slot, rs, 0:D]
            cw = xbuf[slot, rs, D:D + LANE].astype(F32)
            lane = lax.broadcasted_iota(jnp.int32, cw.shape, 1)
            c = jnp.sum(jnp.where((lane == e) | (lane == e + N_EXPERTS), cw, 0.0), axis=1, keepdims=True)
            a = _dot(xrow, wg_b[...])
            b = _dot(xrow, wu_b[...])
            yield
            hid = ((a * jax.nn.sigmoid(a)) * b * c).astype(BF16)
            yield
            ys_ref[rs, :] = _dot(hid, wd_b[...]).astype(ys_ref.dtype)

        _interleave([chain(ci) for ci in range(FFN_TILE // FFN_CHAIN)])


def _combine_kernel(nch_ref, tab_ref, route_ref, h2_ref, g_ref, b_ref, ys_ref, out_ref,
                    ybuf, sem, *, n_steps, rows):
    i = pl.program_id(0)
    par = lax.rem(i, 2)
    n_u = MOE_SUBS_PER_STEP
    tm = TOK_TILE

    def copies(step, par_, act):
        for u in range(n_u):
            slot_ = par_ * n_u + u

            def mk(lo, go, size, slot_=slot_):
                return pltpu.make_async_copy(ys_ref.at[pl.ds(go, size), :],
                                             ybuf.at[slot_, pl.ds(lo, size), :], sem.at[slot_])
            _chunk_copies(step * n_u + u, nch_ref, tab_ref, mk, act)

    @pl.when(i == 0)
    def _():
        ybuf[...] = jnp.zeros_like(ybuf)
        copies(0, 0, lambda c: c.start())

    @pl.when(i + 1 < n_steps)
    def _():
        copies(i + 1, 1 - par, lambda c: c.start())

    copies(i, par, lambda c: c.wait())

    def chain(u):
        rs = slice(u * tm, (u + 1) * tm)
        r_io = lax.broadcasted_iota(jnp.int32, (tm, rows), 1).astype(F32)
        d1 = route_ref[rs, 0:1]
        d2 = route_ref[rs, 1:2]
        p_t = jnp.where((r_io == d1) | (r_io == d2), 1.0, 0.0).astype(BF16)
        yield
        ff = _dot(p_t, ybuf[par * n_u + u])
        yield
        out_ref[rs, :] = _ln(DN_ALPHA * h2_ref[rs, :] + ff, g_ref[...], b_ref[...])

    _interleave([chain(u) for u in range(n_u)])


def _bias_kernel(tbl_ref, bkt_ref, out_ref):
    h = pl.program_id(0)
    far = tbl_ref[h, REL_BUCKETS - 1]
    for which in range(2):
        bk = bkt_ref[which]
        acc = jnp.where(bk < 0, NEG, 0.0)
        for kk in range(REL_BUCKETS):
            acc = jnp.where(bk == kk, (tbl_ref[h, kk] - far) * LOG2E, acc)
        out_ref[which, 0] = acc


def _rel_bucket_table(dist):
    max_exact = REL_BUCKETS // 2
    d = jnp.maximum(dist, 0)
    large = max_exact + (jnp.log(jnp.maximum(d, 1).astype(F32) / max_exact)
                         / math.log(REL_MAX_DIST / max_exact) * (REL_BUCKETS - max_exact)).astype(jnp.int32)
    large = jnp.minimum(large, REL_BUCKETS - 1)
    return jnp.where(d < max_exact, d, large)


def _const_spec(shape):
    nd = len(shape)
    return pl.BlockSpec(shape, lambda *_: (0,) * nd)


def kernel(x, mem, ln_in_g, ln_in_b, rel_bias, w_in, b_gate, w_pool_grp, pool_scale, w_pool_up, w_attn_up,
           w_mix_out, ln1_g, ln1_b, w_mq, w_mk, w_mv, w_mo, ln2_g, ln2_b, w_coarse, b_coarse, w_fine, b_fine,
           w_gate, w_up, w_down, ln3_g, ln3_b):
    B, S, _ = x.shape
    assert S == N_BLK * BLK and w_in.shape[0] == 1
    M = mem.shape[1]
    T = B * S
    tm = 512

    wi = w_in[0]
    w_u = wi[:, 0:POOL_W]
    w_q = wi[:, POOL_W:POOL_W + ATTN_W] * (HEAD_DIM ** -0.5 * LOG2E)
    w_k = wi[:, POOL_W + ATTN_W:POOL_W + 2 * ATTN_W]
    w_v = wi[:, POOL_W + 2 * ATTN_W:POOL_W + 3 * ATTN_W]
    w_gl = wi[:, POOL_W + 3 * ATTN_W:]

    w1 = jnp.concatenate([w_u, w_q, w_k, w_v], axis=1).astype(BF16)
    row2 = lambda a: a.reshape(1, -1)

    iq = jnp.arange(BLK, dtype=jnp.int32)[:, None]
    ik = jnp.arange(BLK, dtype=jnp.int32)[None, :]
    d_own = iq - ik
    bkt = jnp.stack([jnp.where(d_own >= 0, _rel_bucket_table(d_own), -1), _rel_bucket_table(d_own + BLK)])
    t_bias = pl.pallas_call(
        _bias_kernel,
        grid=(N_HEADS,),
        in_specs=[pl.BlockSpec(memory_space=pltpu.SMEM), _const_spec((2, BLK, BLK))],
        out_specs=pl.BlockSpec((2, 1, BLK, BLK), lambda h: (0, h, 0, 0)),
        out_shape=jax.ShapeDtypeStruct((2, N_HEADS, BLK, BLK), F32),
        name="relbias_tiles",
    )(rel_bias.T, bkt)

    n_w1 = w1.shape[1]
    tmp = MERGE_TILE
    ypool, q_aug, k_aug, v_p = pl.pallas_call(
        functools.partial(_proj_kernel, tm=tmp),
        grid=(B, S // tmp),
        in_specs=[
            pl.BlockSpec((1, tmp, D), lambda b, s: (b, s, 0)),
            _const_spec((1, D)), _const_spec((1, D)),
            _const_spec((D, n_w1)),
            _const_spec((len(POOL_WINDOWS), LANE, LANE)),
            _const_spec((1, POOL_W)),
        ],
        out_specs=[
            pl.BlockSpec((1, tmp, POOL_W), lambda b, s: (b, s, 0)),
            pl.BlockSpec((1, N_HEADS, tmp, LANE), lambda b, s: (b, 0, s, 0)),
            pl.BlockSpec((1, N_HEADS, tmp, LANE), lambda b, s: (b, 0, s, 0)),
            pl.BlockSpec((1, ATTN_W // LANE, tmp, LANE), lambda b, s: (b, 0, s, 0)),
        ],
        out_shape=[
            jax.ShapeDtypeStruct((B, S, POOL_W), BF16),
            jax.ShapeDtypeStruct((B, N_HEADS, S, LANE), BF16),
            jax.ShapeDtypeStruct((B, N_HEADS, S, LANE), BF16),
            jax.ShapeDtypeStruct((B, ATTN_W // LANE, S, LANE), BF16),
        ],
        scratch_shapes=[pltpu.VMEM((HALO + tmp, POOL_W), F32), pltpu.VMEM((LANE, ATTN_W), F32)],
        compiler_params=pltpu.CompilerParams(dimension_semantics=("arbitrary", "arbitrary"),
                                             vmem_limit_bytes=VMEM_LIMIT),
        name="proj_pool_gate",
    )(x, row2(ln_in_g), row2(ln_in_b), w1, w_pool_grp[0].astype(BF16), row2(pool_scale[0]))

    o_attn = pl.pallas_call(
        _attn_kernel,
        grid=(B, N_BLK),
        in_specs=[
            pl.BlockSpec((1, N_HEADS, BLK, LANE), lambda b, j: (b, 0, j, 0)),
            pl.BlockSpec((1, N_HEADS, S, LANE), lambda b, j: (b, 0, 0, 0)),
            pl.BlockSpec((1, ATTN_W // LANE, S, LANE), lambda b, j: (b, 0, 0, 0)),
            _const_spec((2, N_HEADS, BLK, BLK)),
        ],
        out_specs=pl.BlockSpec((1, ATTN_W // LANE, BLK, LANE), lambda b, j: (b, 0, j, 0)),
        out_shape=jax.ShapeDtypeStruct((B, ATTN_W // LANE, S, LANE), BF16),
        compiler_params=pltpu.CompilerParams(dimension_semantics=("arbitrary", "arbitrary"),
                                             vmem_limit_bytes=VMEM_LIMIT),
        name="moba_attn",
    )(q_aug, k_aug, v_p, t_bias)

    kmem, vmem = pl.pallas_call(
        _memkv_kernel,
        grid=(B,),
        in_specs=[pl.BlockSpec((1, M, D), lambda b: (b, 0, 0)),
                  _const_spec((D, MEM_W)), _const_spec((D, MEM_W))],
        out_specs=[pl.BlockSpec((1, M, MEM_W), lambda b: (b, 0, 0)),
                   pl.BlockSpec((1, M, MEM_W), lambda b: (b, 0, 0))],
        out_shape=[jax.ShapeDtypeStruct((B, M, MEM_W), BF16)] * 2,
        compiler_params=pltpu.CompilerParams(dimension_semantics=("arbitrary",)),
        name="mem_kv",
    )(mem, w_mk[0].astype(BF16), w_mv[0].astype(BF16))

    w_r = jnp.concatenate([
        w_fine[0].reshape(D, N_EXPERTS),
        jnp.repeat(w_coarse[0], EPG, axis=1),
        jnp.zeros((D, LANE - 2 * N_EXPERTS), F32)], axis=1)
    b_r = jnp.concatenate([
        b_fine[0].reshape(N_EXPERTS), jnp.repeat(b_coarse[0], EPG),
        jnp.zeros((LANE - 2 * N_EXPERTS,), F32)]).reshape(1, LANE)
    w_r_hi = w_r.astype(BF16)
    w_r_lo = (w_r - w_r_hi.astype(F32)).astype(BF16)
    w_r_cat = jnp.concatenate([w_r_hi, w_r_lo], axis=1)

    n_sub = T // tm
    tmm = MERGE_TILE
    per = tmm // tm
    sub_idx = lambda b, s: b * (S // tmm) + s
    h2, h2b, comb, route, route_t, seg_cnt = pl.pallas_call(
        functools.partial(_merge_kernel, tm=tmm),
        grid=(B, S // tmm),
        in_specs=[
            pl.BlockSpec((1, tmm, D), lambda b, s: (b, s, 0)),
            pl.BlockSpec((1, tmm, POOL_W), lambda b, s: (b, s, 0)),
            pl.BlockSpec((1, ATTN_W // LANE, tmm, LANE), lambda b, s: (b, 0, s, 0)),
            pl.BlockSpec((1, M, MEM_W), lambda b, s: (b, 0, 0)),
            pl.BlockSpec((1, M, MEM_W), lambda b, s: (b, 0, 0)),
            _const_spec((1, D)), _const_spec((1, D)),
            _const_spec((D, 2 * D)), _const_spec((1, 2 * D)),
            _const_spec((POOL_W, D)), _const_spec((ATTN_W, D)), _const_spec((D, D)),
            _const_spec((1, D)), _const_spec((1, D)),
            _const_spec((D, MEM_W)), _const_spec((MEM_W, D)),
            _const_spec((1, D)), _const_spec((1, D)),
            _const_spec((D, 2 * LANE)), _const_spec((1, LANE)),
        ],
        out_specs=[
            pl.BlockSpec((1, tmm, D), lambda b, s: (b, s, 0)),
            pl.BlockSpec((1, tmm, D), lambda b, s: (b, s, 0)),
            pl.BlockSpec((1, tmm, LANE), lambda b, s: (b, s, 0)),
            pl.BlockSpec((1, tmm, LANE), lambda b, s: (b, s, 0)),
            pl.BlockSpec((per, 8, tm), lambda b, s: (sub_idx(b, s), 0, 0)),
            pl.BlockSpec((per, 8, LANE), lambda b, s: (sub_idx(b, s), 0, 0)),
        ],
        out_shape=[
            jax.ShapeDtypeStruct((B, S, D), F32),
            jax.ShapeDtypeStruct((B, S, D), BF16),
            jax.ShapeDtypeStruct((B, S, LANE), F32),
            jax.ShapeDtypeStruct((B, S, LANE), F32),
            jax.ShapeDtypeStruct((n_sub, 8, tm), F32),
            jax.ShapeDtypeStruct((n_sub, 8, LANE), F32),
        ],
        compiler_params=pltpu.CompilerParams(dimension_semantics=("arbitrary", "arbitrary"),
                                             vmem_limit_bytes=VMEM_LIMIT),
        name="merge_memattn_router",
    )(x, ypool, o_attn, kmem, vmem,
      row2(ln_in_g), row2(ln_in_b), w_gl.astype(BF16), row2(b_gate[0]),
      w_pool_up[0].astype(BF16), w_attn_up[0].astype(BF16), w_mix_out[0].astype(BF16),
      row2(ln1_g[0]), row2(ln1_b[0]),
      (w_mq[0] * (MEM_HD ** -0.5 * LOG2E)).astype(BF16), w_mo[0].astype(BF16),
      row2(ln2_g[0]), row2(ln2_b[0]),
      w_r_cat, b_r)

    pcs = seg_cnt[:, 0, :N_EXPERTS].astype(jnp.int32)
    tot = jnp.sum(pcs, axis=0)
    cap = ((tot + FFN_TILE - 1) // FFN_TILE) * FFN_TILE
    ends = jnp.cumsum(cap)
    base = ends - cap
    gs = base[None, :] + jnp.cumsum(pcs, axis=0) - pcs
    ls = jnp.cumsum(pcs, axis=1) - pcs
    n_sorted = n_sub * COMPACT_ROWS + N_EXPERTS * FFN_TILE
    n_ffn_tiles = n_sorted // FFN_TILE
    n_used = (ends[-1] // FFN_TILE).astype(jnp.int32)
    tile_row = jnp.arange(n_ffn_tiles, dtype=jnp.int32) * FFN_TILE
    tile_exp = jnp.sum(jnp.minimum(tile_row, ends[-1] - 1)[:, None] >= ends[None, :], axis=1).astype(jnp.int32)
    tails = jnp.concatenate([base + tot, cap - tot]).astype(jnp.int32)
    def kth_of(counts, n_slots, seg_off):
        end = jnp.cumsum(counts, axis=1)
        k = jnp.arange(n_slots, dtype=jnp.int32)
        exp_k = jnp.sum(k[None, :, None] >= end[:, None, :], axis=2)
        hot = exp_k[:, :, None] == jnp.arange(N_EXPERTS)[None, None, :]
        pick = lambda tbl: jnp.sum(jnp.where(hot, tbl[:, None, :], 0), axis=2)
        within = k[None, :] - pick(end - counts)
        return pick(ls + seg_off), pick(gs + seg_off), within

    n_big = pcs // BIG_CHUNK
    n_end = (pcs // SEG_ALIGN) % 2
    lo_b, go_b, m_b = kth_of(n_big, MAX_BIG, 0)
    lo_e, go_e, _ = kth_of(n_end, N_EXPERTS, pcs - SEG_ALIGN)
    chunk_tab = jnp.concatenate([lo_b + BIG_CHUNK * m_b, go_b + BIG_CHUNK * m_b, lo_e, go_e],
                                axis=1).astype(jnp.int32).reshape(-1)
    n_chunks = jnp.stack([jnp.sum(n_big, axis=1), jnp.sum(n_end, axis=1)], axis=1).astype(jnp.int32).reshape(-1)

    aug_w = D + LANE
    n_u = MOE_SUBS_PER_STEP
    n_steps = n_sub // n_u
    x_sorted = pl.pallas_call(
        functools.partial(_dispatch_kernel, n_steps=n_steps, rows=COMPACT_ROWS),
        grid_spec=pltpu.PrefetchScalarGridSpec(
            num_scalar_prefetch=3,
            grid=(n_steps,),
            in_specs=[
                pl.BlockSpec((n_u * tm, D), lambda i, *_: (i, 0)),
                pl.BlockSpec((n_u * tm, LANE), lambda i, *_: (i, 0)),
                pl.BlockSpec((n_u, 8, tm), lambda i, *_: (i, 0, 0)),
            ],
            out_specs=pl.BlockSpec(memory_space=pl.ANY),
            scratch_shapes=[
                pltpu.VMEM((2 * n_u, COMPACT_ROWS, aug_w), BF16),
                pltpu.VMEM((FFN_TILE // 2, aug_w), BF16),
                pltpu.SemaphoreType.DMA((2 * n_u,)),
                pltpu.SemaphoreType.DMA(()),
            ],
        ),
        out_shape=jax.ShapeDtypeStruct((n_sorted, aug_w), BF16),
        compiler_params=pltpu.CompilerParams(dimension_semantics=("arbitrary",), vmem_limit_bytes=VMEM_LIMIT),
        name="moe_dispatch",
    )(n_chunks, chunk_tab, tails, h2b.reshape(T, D), comb.reshape(T, LANE), route_t)

    used_tile = lambda t, texp, nused: (jnp.minimum(t, nused[0] - 1), 0)
    y_sorted = pl.pallas_call(
        _ffn_kernel,
        grid_spec=pltpu.PrefetchScalarGridSpec(
            num_scalar_prefetch=2,
            grid=(n_ffn_tiles,),
            in_specs=[
                pl.BlockSpec(memory_space=pl.ANY),
                pl.BlockSpec((1, D, FF), lambda t, texp, nused: (texp[t], 0, 0)),
                pl.BlockSpec((1, D, FF), lambda t, texp, nused: (texp[t], 0, 0)),
                pl.BlockSpec((1, FF, D), lambda t, texp, nused: (texp[t], 0, 0)),
            ],
            out_specs=pl.BlockSpec((FFN_TILE, D), used_tile),
            scratch_shapes=[pltpu.VMEM((D, FF), BF16), pltpu.VMEM((D, FF), BF16), pltpu.VMEM((FF, D), BF16),
                            pltpu.VMEM((XS_DEPTH, FFN_TILE, aug_w), BF16), pltpu.SemaphoreType.DMA((XS_DEPTH,))],
        ),
        out_shape=jax.ShapeDtypeStruct((n_sorted, D), BF16),
        compiler_params=pltpu.CompilerParams(dimension_semantics=("arbitrary",), vmem_limit_bytes=VMEM_LIMIT),
        name="moe_expert_ffn",
    )(tile_exp, n_used.reshape(1), x_sorted, w_gate[0], w_up[0], w_down[0])

    out = pl.pallas_call(
        functools.partial(_combine_kernel, n_steps=n_steps, rows=COMPACT_ROWS),
        grid_spec=pltpu.PrefetchScalarGridSpec(
            num_scalar_prefetch=2,
            grid=(n_steps,),
            in_specs=[
                pl.BlockSpec((n_u * tm, LANE), lambda i, *_: (i, 0)),
                pl.BlockSpec((n_u * tm, D), lambda i, *_: (i, 0)),
                pl.BlockSpec((1, D), lambda i, *_: (0, 0)),
                pl.BlockSpec((1, D), lambda i, *_: (0, 0)),
                pl.BlockSpec(memory_space=pl.ANY),
            ],
            out_specs=pl.BlockSpec((n_u * tm, D), lambda i, *_: (i, 0)),
            scratch_shapes=[
                pltpu.VMEM((2 * n_u, COMPACT_ROWS, D), BF16),
                pltpu.SemaphoreType.DMA((2 * n_u,)),
            ],
        ),
        out_shape=jax.ShapeDtypeStruct((T, D), F32),
        compiler_params=pltpu.CompilerParams(dimension_semantics=("arbitrary",), vmem_limit_bytes=VMEM_LIMIT),
        name="moe_combine_ln3",
    )(n_chunks, chunk_tab, route.reshape(T, LANE), h2.reshape(T, D), row2(ln3_g[0]), row2(ln3_b[0]), y_sorted)
    return out.reshape(B, S, D)
```

```python
import functools
import math

import jax
import jax.numpy as jnp
from jax import lax
from jax.experimental import pallas as pl
from jax.experimental.pallas import tpu as pltpu

D = 1024
POOL_WINDOWS = (2, 4, 8, 16)
POOL_W = 512
N_HEADS = 8
HEAD_DIM = 64
ATTN_W = 512
BLK = 256
N_BLK = 8
TOPK = 3
REL_BUCKETS = 32
REL_MAX_DIST = 128
MEM_HEADS = 4
MEM_HD = 128
MEM_W = 512
EPG = 8
N_EXPERTS = 32
FF = 256
DN_ALPHA = 2.0 ** 0.25
LN_EPS = 1e-5

LANE = 128
TOK_TILE = 512
SEG_ALIGN = 16
COMPACT_ROWS = 2 * TOK_TILE + N_EXPERTS * SEG_ALIGN
BIG_CHUNK = 2 * SEG_ALIGN
MAX_BIG = COMPACT_ROWS // BIG_CHUNK
CHUNK_TAB_W = 2 * MAX_BIG + 2 * N_EXPERTS
DISPATCH_MAIN_ROWS = 2 * TOK_TILE + 18 * SEG_ALIGN
FFN_TILE = 512
PROJ_CHAIN = 256
ATTN_PAIRS_PER_TRIP = 2
MOE_SUBS_PER_STEP = 2
FFN_CHAIN = 512
XS_DEPTH = 3
MERGE_TILE = 1024
HALO = 16
NEG = -1e30
LOG2E = math.log2(math.e)
VMEM_LIMIT = 56 * 1024 * 1024

F32 = jnp.float32
BF16 = jnp.bfloat16

_NT = (((1,), (1,)), ((), ()))


def _dot(a, b):
    return jnp.dot(a, b, preferred_element_type=F32)


def _dot_nt(a, b):
    return lax.dot_general(a, b, _NT, preferred_element_type=F32)


def _split(a):
    hi = a.astype(BF16)
    lo = (a - hi.astype(F32)).astype(BF16)
    return hi, lo


def _interleave(chains):
    results = [None] * len(chains)
    live = list(range(len(chains)))
    while live:
        for ci in list(live):
            try:
                next(chains[ci])
            except StopIteration as done:
                results[ci] = done.value
                live.remove(ci)
    return results


def _ln(x, g, b):
    mu = jnp.mean(x, axis=-1, keepdims=True)
    xc = x - mu
    var = jnp.mean(xc * xc, axis=-1, keepdims=True)
    return xc * lax.rsqrt(var + LN_EPS) * g + b


def _proj_kernel(x_ref, g_ref, b_ref, w_ref, wgrp_ref, pscale_ref,
                 ypool_ref, q_ref, k_ref, v_ref, ubuf, kbt, *, tm):
    s = pl.program_id(1)

    @pl.when(s == 0)
    def _():
        ubuf[0:HALO, :] = jnp.zeros((HALO, POOL_W), F32)
        kbt[...] = jnp.zeros_like(kbt)

    _interleave([_proj_chain(ci, s, tm, x_ref, g_ref, b_ref, w_ref, wgrp_ref, pscale_ref,
                             ypool_ref, q_ref, k_ref, v_ref, ubuf, kbt)
                 for ci in range(tm // PROJ_CHAIN)])
    ubuf[0:HALO, :] = ubuf[tm:tm + HALO, :]


def _proj_chain(ci, s, tm, x_ref, g_ref, b_ref, w_ref, wgrp_ref, pscale_ref,
                ypool_ref, q_ref, k_ref, v_ref, ubuf, kbt):
    n = PROJ_CHAIN
    r0 = ci * n
    rows = slice(r0, r0 + n)
    blk0 = s * (tm // BLK) + ci * (n // BLK)

    h = _ln(x_ref[0, rows, :], g_ref[...], b_ref[...])
    hb = h.astype(BF16)
    zu = _dot(hb, w_ref[:, 0:POOL_W])
    zq = _dot(hb, w_ref[:, POOL_W:POOL_W + ATTN_W])
    zk = _dot(hb, w_ref[:, POOL_W + ATTN_W:POOL_W + 2 * ATTN_W])
    zv = _dot(hb, w_ref[:, POOL_W + 2 * ATTN_W:])
    ubuf[HALO + r0:HALO + r0 + n, :] = zu

    r_io = lax.broadcasted_iota(jnp.int32, kbt.shape, 0)
    c_io = lax.broadcasted_iota(jnp.int32, kbt.shape, 1)
    head_match = (r_io >> 3) == (c_io >> 6)
    for bi in range(n // BLK):
        kmean = jnp.mean(zk[bi * BLK:(bi + 1) * BLK], axis=0, keepdims=True)
        kbt[...] = jnp.where(head_match & ((r_io & 7) == blk0 + bi), kmean, kbt[...])
    yield

    t_pos = s * tm + r0 + lax.broadcasted_iota(jnp.int32, (n, LANE), 0)
    for g, w in enumerate(POOL_WINDOWS):
        cols = slice(g * LANE, (g + 1) * LANE)
        ws = ubuf[HALO + r0:HALO + r0 + n, cols]
        for kk in range(1, w):
            ws = ws + ubuf[HALO + r0 - kk:HALO + r0 - kk + n, cols]
        cnt = jnp.minimum(t_pos + 1, w).astype(F32)
        y = ws / cnt - ubuf[HALO + r0:HALO + r0 + n, cols]
        yg = _dot(y.astype(BF16), wgrp_ref[g]) * pscale_ref[:, cols]
        ypool_ref[0, rows, cols] = yg.astype(ypool_ref.dtype)
    yield

    q_hi, q_lo = _split(zq)
    kb_hi, kb_lo = _split(kbt[...])
    g2 = _dot_nt(q_hi, jnp.concatenate([kb_hi, kb_lo], axis=0))
    gate = g2[:, 0:LANE] + g2[:, LANE:2 * LANE] + _dot_nt(q_lo, kb_hi)
    yield

    lane = lax.broadcasted_iota(jnp.int32, (n, LANE), 1)
    row = lax.broadcasted_iota(jnp.int32, (n, LANE), 0)
    n_l = lane & 7
    jrow = blk0 + (row >> 8)
    past = n_l < jrow
    gt = jnp.where(past, gate, -jnp.inf)
    cnt = jnp.zeros((n, LANE), F32)
    for sh in range(1, N_BLK):
        wrap = (n_l + sh) >= N_BLK
        gm = jnp.where(wrap, pltpu.roll(gt, N_BLK - sh, 1), pltpu.roll(gt, LANE - sh, 1))
        cnt = cnt + jnp.where(wrap, jnp.where(gm >= gt, 1.0, 0.0), jnp.where(gm > gt, 1.0, 0.0))
    keep = (past & (cnt < TOPK)) | (n_l == jrow)
    negmask = jnp.where(keep, 0.0, NEG)

    aug_lane = (lane >= HEAD_DIM) & (lane < HEAD_DIM + N_BLK)
    k_onehot = jnp.where(lane == HEAD_DIM + jrow, 1.0, 0.0)
    head_lane = lane < HEAD_DIM
    for hh in range(N_HEADS):
        cols = slice((hh // 2) * LANE, (hh // 2 + 1) * LANE)
        q_h, k_h = zq[:, cols], zk[:, cols]
        if hh % 2:
            q_h, k_h = pltpu.roll(q_h, HEAD_DIM, 1), pltpu.roll(k_h, HEAD_DIM, 1)
        m_h = jnp.where(aug_lane, pltpu.roll(negmask, HEAD_DIM - N_BLK * hh, 1), 0.0)
        q_ref[0, hh, rows, :] = jnp.where(head_lane, q_h, m_h).astype(q_ref.dtype)
        k_ref[0, hh, rows, :] = jnp.where(head_lane, k_h, k_onehot).astype(k_ref.dtype)
    for p in range(ATTN_W // LANE):
        v_ref[0, p, rows, :] = zv[:, p * LANE:(p + 1) * LANE].astype(v_ref.dtype)


def _attn_kernel(q_ref, k_ref, v_ref, tb_ref, o_ref):
    j = pl.program_id(1)
    lane = lax.broadcasted_iota(jnp.int32, (BLK, LANE), 1)

    def one_head(h, p, jj):
        q = q_ref[0, h]
        own0 = jj * BLK
        pieces = []
        if jj >= 1:
            bias = jnp.concatenate([tb_ref[1, h], tb_ref[0, h]], axis=1)
            s_near = _dot_nt(q, k_ref[0, h, own0 - BLK:own0 + BLK, :]) + bias
            pieces.append((s_near, own0 - BLK, 2 * BLK))
        else:
            s_own = _dot_nt(q, k_ref[0, h, own0:own0 + BLK, :]) + tb_ref[0, h]
            pieces.append((s_own, own0, BLK))
        if jj >= 2:
            s_far = _dot_nt(q, k_ref[0, h, 0:own0 - BLK, :])
            pieces.append((s_far, 0, own0 - BLK))
        yield
        m = None
        for sc, _, _ in pieces:
            mm = jnp.max(sc, axis=1, keepdims=True)
            m = mm if m is None else jnp.maximum(m, mm)
        l = None
        probs = []
        for sc, start, size in pieces:
            e = jnp.exp2(sc - m)
            ls = jnp.sum(e, axis=1, keepdims=True)
            l = ls if l is None else l + ls
            probs.append((e.astype(BF16), start, size))
        yield
        acc = None
        for pb, start, size in probs:
            pv = _dot(pb, v_ref[0, p, start:start + size, :])
            acc = pv if acc is None else acc + pv
        return acc / l

    for jj in range(N_BLK):
        @pl.when(j == jj)
        def _(jj=jj):
            def group(gi, carry):
                pairs = [gi * ATTN_PAIRS_PER_TRIP + pi for pi in range(ATTN_PAIRS_PER_TRIP)]
                outs = _interleave([one_head(2 * p + hh, p, jj) for p in pairs for hh in range(2)])
                for pi, p in enumerate(pairs):
                    o_ref[0, p] = jnp.where(lane < HEAD_DIM, outs[2 * pi], outs[2 * pi + 1]).astype(o_ref.dtype)
                return carry
            lax.fori_loop(0, N_HEADS // 2 // ATTN_PAIRS_PER_TRIP, group, 0)


def _memkv_kernel(mem_ref, wk_ref, wv_ref, k_ref, v_ref):
    mb = mem_ref[0].astype(BF16)
    k_ref[0] = _dot(mb, wk_ref[...]).astype(k_ref.dtype)
    v_ref[0] = _dot(mb, wv_ref[...]).astype(v_ref.dtype)


def _merge_kernel(x_ref, ypool_ref, o_ref, kmem_ref, vmem_ref,
                  lng_ref, lnb_ref, wgl_ref, bgate_ref, wpu_ref, wau_ref, wout_ref,
                  ln1g_ref, ln1b_ref, wmq_ref, wmo_ref, ln2g_ref, ln2b_ref,
                  wrh_ref, br_ref,
                  h2_ref, h2b_ref, comb_ref, route_ref, route_t_ref, cnt_ref, *, tm):
    _interleave([_merge_chain(ci, x_ref, ypool_ref, o_ref, kmem_ref, vmem_ref,
                              lng_ref, lnb_ref, wgl_ref, bgate_ref, wpu_ref, wau_ref, wout_ref,
                              ln1g_ref, ln1b_ref, wmq_ref, wmo_ref, ln2g_ref, ln2b_ref,
                              wrh_ref, br_ref, h2_ref, h2b_ref, comb_ref, route_ref, route_t_ref, cnt_ref)
                 for ci in range(tm // TOK_TILE)])


def _merge_chain(ci, x_ref, ypool_ref, o_ref, kmem_ref, vmem_ref,
                 lng_ref, lnb_ref, wgl_ref, bgate_ref, wpu_ref, wau_ref, wout_ref,
                 ln1g_ref, ln1b_ref, wmq_ref, wmo_ref, ln2g_ref, ln2b_ref,
                 wrh_ref, br_ref, h2_ref, h2b_ref, comb_ref, route_ref, route_t_ref, cnt_ref):
    rows = slice(ci * TOK_TILE, (ci + 1) * TOK_TILE)
    n_rows = TOK_TILE
    h = _ln(x_ref[0, rows, :], lng_ref[...], lnb_ref[...])
    hb = h.astype(BF16)
    gl = _dot(hb, wgl_ref[...]) + bgate_ref[...]
    yield
    gates = 0.5 * jnp.tanh(0.5 * gl) + 0.5
    y_pool = _dot(ypool_ref[0, rows, :], wpu_ref[...])
    o_cat = jnp.concatenate([o_ref[0, p, rows, :] for p in range(ATTN_W // LANE)], axis=1)
    y_attn = _dot(o_cat, wau_ref[...])
    yield
    merged = gates[:, 0:D] * y_pool + gates[:, D:2 * D] * y_attn
    mix = _dot(merged.astype(BF16), wout_ref[...])
    yield
    h1 = _ln(DN_ALPHA * h + mix, ln1g_ref[...], ln1b_ref[...])

    qm = _dot(h1.astype(BF16), wmq_ref[...]).astype(BF16)
    yield
    outs = []
    for hd in range(MEM_HEADS):
        cols = slice(hd * MEM_HD, (hd + 1) * MEM_HD)
        sc = _dot_nt(qm[:, cols], kmem_ref[0, :, cols])
        m = jnp.max(sc, axis=1, keepdims=True)
        e = jnp.exp2(sc - m)
        l = jnp.sum(e, axis=1, keepdims=True)
        outs.append(_dot(e.astype(BF16), vmem_ref[0, :, cols]) / l)
    om = jnp.concatenate(outs, axis=1).astype(BF16)
    xa = _dot(om, wmo_ref[...])
    yield
    h2 = _ln(DN_ALPHA * h1 + xa, ln2g_ref[...], ln2b_ref[...])
    h2_ref[0, rows, :] = h2
    h2b_ref[0, rows, :] = h2.astype(BF16)

    x_hi, x_lo = _split(h2)
    r2 = _dot(x_hi, wrh_ref[...])
    r = r2[:, 0:LANE] + r2[:, LANE:2 * LANE] + _dot(x_lo, wrh_ref[:, 0:LANE]) + br_ref[...]
    yield
    lane = lax.broadcasted_iota(jnp.int32, (n_rows, LANE), 1)
    lane_f = lane.astype(F32)
    cmask = (lane >= N_EXPERTS) & (lane < 2 * N_EXPERTS)
    c = jnp.where(cmask, r, -jnp.inf)
    cmax = jnp.max(c, axis=1, keepdims=True)
    ce = jnp.exp(c - cmax)
    csum = jnp.sum(ce, axis=1, keepdims=True) * (1.0 / EPG)
    g_prob = 1.0 / csum
    grp_lane = ((lane & (N_EXPERTS - 1)) >> 3).astype(F32)
    gidx = jnp.min(jnp.where(cmask & (c == cmax), grp_lane, 99.0), axis=1, keepdims=True)
    fmask = (lane < N_EXPERTS) & (grp_lane == gidx)
    f = jnp.where(fmask, r, -jnp.inf)
    fmax = jnp.max(f, axis=1, keepdims=True)
    fe = jnp.exp(f - fmax)
    fsum = jnp.sum(fe, axis=1, keepdims=True)
    prob = fe / fsum
    p1 = jnp.max(prob, axis=1, keepdims=True)
    i1 = jnp.min(jnp.where(fmask & (prob == p1), lane_f, 999.0), axis=1, keepdims=True)
    rest = fmask & (lane_f != i1)
    prob2 = jnp.where(rest, prob, -1.0)
    p2 = jnp.max(prob2, axis=1, keepdims=True)
    i2 = jnp.min(jnp.where(rest & (prob2 == p2), lane_f, 999.0), axis=1, keepdims=True)
    den = p1 + p2
    comb = jnp.where(lane_f == i1, g_prob * (p1 / den),
                     jnp.where(lane_f == i2, g_prob * (p2 / den), 0.0))
    comb_ref[0, rows, :] = comb
    yield

    sel = jnp.where((lane_f == i1) | (lane_f == i2), 1.0, 0.0)
    cnt = jnp.sum(sel, axis=0, keepdims=True)
    pc = jnp.floor((cnt + (SEG_ALIGN - 1)) * (1.0 / SEG_ALIGN)) * SEG_ALIGN
    lane8 = lax.broadcasted_iota(jnp.int32, (8, LANE), 1)
    inc = jnp.broadcast_to(pc, (8, LANE))
    for sh in (1, 2, 4, 8, 16, 32, 64):
        inc = inc + jnp.where(lane8 >= sh, pltpu.roll(inc, sh, 1), 0.0)
    seg_start = inc[0:1] - pc
    t_row = lax.broadcasted_iota(jnp.int32, (n_rows, n_rows), 0)
    t_col = lax.broadcasted_iota(jnp.int32, (n_rows, n_rows), 1)
    earlier = jnp.where(t_row > t_col, 1.0, 0.0).astype(BF16)
    rank = _dot(earlier, sel.astype(BF16))
    pos = seg_start + rank
    d1 = jnp.sum(jnp.where(lane_f == i1, pos, 0.0), axis=1, keepdims=True)
    d2 = jnp.sum(jnp.where(lane_f == i2, pos, 0.0), axis=1, keepdims=True)
    route = jnp.where(lane == 0, d1, jnp.where(lane == 1, d2, 0.0))
    route_ref[0, rows, :] = route
    r_hi, r_lo = _split(route)
    eye = jnp.where(lax.broadcasted_iota(jnp.int32, (8, LANE), 0) == lane8, 1.0, 0.0).astype(BF16)
    route_t_ref[ci] = _dot_nt(eye, r_hi) + _dot_nt(eye, r_lo)
    cnt_ref[ci] = jnp.broadcast_to(pc, (8, LANE))


def _chunk_copies(sub, nch_ref, tab_ref, make_copy, act):
    base = sub * CHUNK_TAB_W
    for size, cnt_i, lo_off, go_off in ((BIG_CHUNK, 0, 0, MAX_BIG), (SEG_ALIGN, 1, 2 * MAX_BIG, 2 * MAX_BIG + N_EXPERTS)):
        def body(k, carry, size=size, lo_off=lo_off, go_off=go_off):
            lo = pl.multiple_of(tab_ref[base + lo_off + k], SEG_ALIGN)
            go = pl.multiple_of(tab_ref[base + go_off + k], SEG_ALIGN)
            act(make_copy(lo, go, size))
            return carry
        lax.fori_loop(0, nch_ref[2 * sub + cnt_i], body, 0)


def _dispatch_kernel(nch_ref, tab_ref, tail_ref, xb_ref, comb_ref, route_t_ref, xs_ref,
                     cbuf, zbuf, sem, zsem, *, n_steps, rows):
    i = pl.program_id(0)
    par = lax.rem(i, 2)
    n_u = MOE_SUBS_PER_STEP
    tm = TOK_TILE

    def copies(step, par_, act):
        for u in range(n_u):
            slot_ = par_ * n_u + u

            def mk(lo, go, size, slot_=slot_):
                return pltpu.make_async_copy(cbuf.at[slot_, pl.ds(lo, size), :],
                                             xs_ref.at[pl.ds(go, size), :], sem.at[slot_])
            _chunk_copies(step * n_u + u, nch_ref, tab_ref, mk, act)

    @pl.when(i >= 2)
    def _():
        copies(i - 2, par, lambda c: c.wait())

    def one_hot_rows(u, r_lo, r_hi):
        r_io = (r_lo + lax.broadcasted_iota(jnp.int32, (r_hi - r_lo, tm), 0)).astype(F32)
        d1 = route_t_ref[u, 0:1, :]
        d2 = route_t_ref[u, 1:2, :]
        return jnp.where((r_io == d1) | (r_io == d2), 1.0, 0.0).astype(BF16)

    def augmented(u):
        rs = slice(u * tm, (u + 1) * tm)
        comb = comb_ref[rs, :]
        c_hi = comb.astype(BF16).astype(F32)
        c_pack = (c_hi + pltpu.roll(comb - c_hi, N_EXPERTS, 1)).astype(BF16)
        return jnp.concatenate([xb_ref[rs, :], c_pack], axis=1)

    def chain(u):
        p_mat = one_hot_rows(u, 0, DISPATCH_MAIN_ROWS)
        x_aug = augmented(u)
        yield
        cbuf[par * n_u + u, 0:DISPATCH_MAIN_ROWS, :] = _dot(p_mat, x_aug).astype(BF16)

    _interleave([chain(u) for u in range(n_u)])

    for u in range(n_u):
        sub = i * n_u + u
        used_rows = nch_ref[2 * sub] * BIG_CHUNK + nch_ref[2 * sub + 1] * SEG_ALIGN

        @pl.when(used_rows > DISPATCH_MAIN_ROWS)
        def _(u=u):
            cbuf[par * n_u + u, DISPATCH_MAIN_ROWS:rows, :] = _dot(
                one_hot_rows(u, DISPATCH_MAIN_ROWS, rows), augmented(u)).astype(BF16)

    copies(i, par, lambda c: c.start())

    @pl.when(i == n_steps - 1)
    def _():
        if n_steps >= 2:
            copies(i - 1, 1 - par, lambda c: c.wait())
        copies(i, par, lambda c: c.wait())
        zbuf[...] = jnp.zeros_like(zbuf)

        def tails(act):
            def body(e, carry):
                st = tail_ref[e]
                n = tail_ref[N_EXPERTS + e]
                off = jnp.int32(0)
                size = zbuf.shape[0]
                while size >= SEG_ALIGN:
                    bit = (n & size) != 0

                    @pl.when(bit)
                    def _(size=size, off=off):
                        act(pltpu.make_async_copy(
                            zbuf.at[pl.ds(0, size), :],
                            xs_ref.at[pl.ds(pl.multiple_of(st + off, SEG_ALIGN), size), :], zsem))
                    off = off + jnp.where(bit, size, 0)
                    size //= 2
                return carry
            lax.fori_loop(0, N_EXPERTS, body, 0)
        tails(lambda c: c.start())
        tails(lambda c: c.wait())


def _ffn_kernel(texp_ref, nused_ref, xs_ref, wg_ref, wu_ref, wd_ref, ys_ref, wg_b, wu_b, wd_b, xbuf, xsem):
    t = pl.program_id(0)
    e = texp_ref[t]
    n_used = nused_ref[0]

    def fetch(u):
        slot = lax.rem(u, XS_DEPTH)
        return pltpu.make_async_copy(xs_ref.at[pl.ds(pl.multiple_of(u * FFN_TILE, FFN_TILE), FFN_TILE), :],
                                     xbuf.at[slot], xsem.at[slot])

    @pl.when(t == 0)
    def _():
        for u in range(XS_DEPTH - 1):
            @pl.when(u < n_used)
            def _(u=u):
                fetch(jnp.int32(u)).start()

    @pl.when(t + (XS_DEPTH - 1) < n_used)
    def _():
        fetch(t + (XS_DEPTH - 1)).start()

    @pl.when((t == 0) | (e != texp_ref[jnp.maximum(t - 1, 0)]))
    def _():
        wg_b[...] = wg_ref[0].astype(BF16)
        wu_b[...] = wu_ref[0].astype(BF16)
        wd_b[...] = wd_ref[0].astype(BF16)

    @pl.when(t < n_used)
    def _():
        fetch(t).wait()
        slot = lax.rem(t, XS_DEPTH)

        def chain(ci):
            rs = slice(ci * FFN_CHAIN, (ci + 1) * FFN_CHAIN)
            xrow = xbuf[slot, rs, 0:D]
            cw = xbuf[slot, rs, D:D + LANE].astype(F32)
            lane = lax.broadcasted_iota(jnp.int32, cw.shape, 1)
            c = jnp.sum(jnp.where((lane == e) | (lane == e + N_EXPERTS), cw, 0.0), axis=1, keepdims=True)
            a = _dot(xrow, wg_b[...])
            b = _dot(xrow, wu_b[...])
            yield
            hid = ((a * jax.nn.sigmoid(a)) * b * c).astype(BF16)
            yield
            ys_ref[rs, :] = _dot(hid, wd_b[...]).astype(ys_ref.dtype)

        _interleave([chain(ci) for ci in range(FFN_TILE // FFN_CHAIN)])


def _combine_kernel(nch_ref, tab_ref, route_ref, h2_ref, g_ref, b_ref, ys_ref, out_ref,
                    ybuf, sem, *, n_steps, rows):
    i = pl.program_id(0)
    par = lax.rem(i, 2)
    n_u = MOE_SUBS_PER_STEP
    tm = TOK_TILE

    def copies(step, par_, act):
        for u in range(n_u):
            slot_ = par_ * n_u + u

            def mk(lo, go, size, slot_=slot_):
                return pltpu.make_async_copy(ys_ref.at[pl.ds(go, size), :],
                                             ybuf.at[slot_, pl.ds(lo, size), :], sem.at[slot_])
            _chunk_copies(step * n_u + u, nch_ref, tab_ref, mk, act)

    @pl.when(i == 0)
    def _():
        ybuf[...] = jnp.zeros_like(ybuf)
        copies(0, 0, lambda c: c.start())

    @pl.when(i + 1 < n_steps)
    def _():
        copies(i + 1, 1 - par, lambda c: c.start())

    copies(i, par, lambda c: c.wait())

    def chain(u):
        rs = slice(u * tm, (u + 1) * tm)
        r_io = lax.broadcasted_iota(jnp.int32, (tm, rows), 1).astype(F32)
        d1 = route_ref[rs, 0:1]
        d2 = route_ref[rs, 1:2]
        p_t = jnp.where((r_io == d1) | (r_io == d2), 1.0, 0.0).astype(BF16)
        yield
        ff = _dot(p_t, ybuf[par * n_u + u])
        yield
        out_ref[rs, :] = _ln(DN_ALPHA * h2_ref[rs, :] + ff, g_ref[...], b_ref[...])

    _interleave([chain(u) for u in range(n_u)])


def _bias_kernel(tbl_ref, bkt_ref, out_ref):
    h = pl.program_id(0)
    far = tbl_ref[h, REL_BUCKETS - 1]
    for which in range(2):
        bk = bkt_ref[which]
        acc = jnp.where(bk < 0, NEG, 0.0)
        for kk in range(REL_BUCKETS):
            acc = jnp.where(bk == kk, (tbl_ref[h, kk] - far) * LOG2E, acc)
        out_ref[which, 0] = acc


def _rel_bucket_table(dist):
    max_exact = REL_BUCKETS // 2
    d = jnp.maximum(dist, 0)
    large = max_exact + (jnp.log(jnp.maximum(d, 1).astype(F32) / max_exact)
                         / math.log(REL_MAX_DIST / max_exact) * (REL_BUCKETS - max_exact)).astype(jnp.int32)
    large = jnp.minimum(large, REL_BUCKETS - 1)
    return jnp.where(d < max_exact, d, large)


def _const_spec(shape):
    nd = len(shape)
    return pl.BlockSpec(shape, lambda *_: (0,) * nd)


def kernel(x, mem, ln_in_g, ln_in_b, rel_bias, w_in, b_gate, w_pool_grp, pool_scale, w_pool_up, w_attn_up,
           w_mix_out, ln1_g, ln1_b, w_mq, w_mk, w_mv, w_mo, ln2_g, ln2_b, w_coarse, b_coarse, w_fine, b_fine,
           w_gate, w_up, w_down, ln3_g, ln3_b):
    B, S, _ = x.shape
    assert S == N_BLK * BLK and w_in.shape[0] == 1
    M = mem.shape[1]
    T = B * S
    tm = TOK_TILE

    wi = w_in[0]
    w_u = wi[:, 0:POOL_W]
    w_q = wi[:, POOL_W:POOL_W + ATTN_W] * (HEAD_DIM ** -0.5 * LOG2E)
    w_k = wi[:, POOL_W + ATTN_W:POOL_W + 2 * ATTN_W]
    w_v = wi[:, POOL_W + 2 * ATTN_W:POOL_W + 3 * ATTN_W]
    w_gl = wi[:, POOL_W + 3 * ATTN_W:]

    w1 = jnp.concatenate([w_u, w_q, w_k, w_v], axis=1).astype(BF16)
    row2 = lambda a: a.reshape(1, -1)

    iq = jnp.arange(BLK, dtype=jnp.int32)[:, None]
    ik = jnp.arange(BLK, dtype=jnp.int32)[None, :]
    d_own = iq - ik
    bkt = jnp.stack([jnp.where(d_own >= 0, _rel_bucket_table(d_own), -1), _rel_bucket_table(d_own + BLK)])
    t_bias = pl.pallas_call(
        _bias_kernel,
        grid=(N_HEADS,),
        in_specs=[pl.BlockSpec(memory_space=pltpu.SMEM), _const_spec((2, BLK, BLK))],
        out_specs=pl.BlockSpec((2, 1, BLK, BLK), lambda h: (0, h, 0, 0)),
        out_shape=jax.ShapeDtypeStruct((2, N_HEADS, BLK, BLK), F32),
        name="relbias_tiles",
    )(rel_bias.T, bkt)

    n_w1 = w1.shape[1]
    tmp = MERGE_TILE
    ypool, q_aug, k_aug, v_p = pl.pallas_call(
        functools.partial(_proj_kernel, tm=tmp),
        grid=(B, S // tmp),
        in_specs=[
            pl.BlockSpec((1, tmp, D), lambda b, s: (b, s, 0)),
            _const_spec((1, D)), _const_spec((1, D)),
            _const_spec((D, n_w1)),
            _const_spec((len(POOL_WINDOWS), LANE, LANE)),
            _const_spec((1, POOL_W)),
        ],
        out_specs=[
            pl.BlockSpec((1, tmp, POOL_W), lambda b, s: (b, s, 0)),
            pl.BlockSpec((1, N_HEADS, tmp, LANE), lambda b, s: (b, 0, s, 0)),
            pl.BlockSpec((1, N_HEADS, tmp, LANE), lambda b, s: (b, 0, s, 0)),
            pl.BlockSpec((1, ATTN_W // LANE, tmp, LANE), lambda b, s: (b, 0, s, 0)),
        ],
        out_shape=[
            jax.ShapeDtypeStruct((B, S, POOL_W), BF16),
            jax.ShapeDtypeStruct((B, N_HEADS, S, LANE), BF16),
            jax.ShapeDtypeStruct((B, N_HEADS, S, LANE), BF16),
            jax.ShapeDtypeStruct((B, ATTN_W // LANE, S, LANE), BF16),
        ],
        scratch_shapes=[pltpu.VMEM((HALO + tmp, POOL_W), F32), pltpu.VMEM((LANE, ATTN_W), F32)],
        compiler_params=pltpu.CompilerParams(dimension_semantics=("arbitrary", "arbitrary"),
                                             vmem_limit_bytes=VMEM_LIMIT),
        name="proj_pool_gate",
    )(x, row2(ln_in_g), row2(ln_in_b), w1, w_pool_grp[0].astype(BF16), row2(pool_scale[0]))

    o_attn = pl.pallas_call(
        _attn_kernel,
        grid=(B, N_BLK),
        in_specs=[
            pl.BlockSpec((1, N_HEADS, BLK, LANE), lambda b, j: (b, 0, j, 0)),
            pl.BlockSpec((1, N_HEADS, S, LANE), lambda b, j: (b, 0, 0, 0)),
            pl.BlockSpec((1, ATTN_W // LANE, S, LANE), lambda b, j: (b, 0, 0, 0)),
            _const_spec((2, N_HEADS, BLK, BLK)),
        ],
        out_specs=pl.BlockSpec((1, ATTN_W // LANE, BLK, LANE), lambda b, j: (b, 0, j, 0)),
        out_shape=jax.ShapeDtypeStruct((B, ATTN_W // LANE, S, LANE), BF16),
        compiler_params=pltpu.CompilerParams(dimension_semantics=("arbitrary", "arbitrary"),
                                             vmem_limit_bytes=VMEM_LIMIT),
        name="moba_attn",
    )(q_aug, k_aug, v_p, t_bias)

    kmem, vmem = pl.pallas_call(
        _memkv_kernel,
        grid=(B,),
        in_specs=[pl.BlockSpec((1, M, D), lambda b: (b, 0, 0)),
                  _const_spec((D, MEM_W)), _const_spec((D, MEM_W))],
        out_specs=[pl.BlockSpec((1, M, MEM_W), lambda b: (b, 0, 0)),
                   pl.BlockSpec((1, M, MEM_W), lambda b: (b, 0, 0))],
        out_shape=[jax.ShapeDtypeStruct((B, M, MEM_W), BF16)] * 2,
        compiler_params=pltpu.CompilerParams(dimension_semantics=("arbitrary",)),
        name="mem_kv",
    )(mem, w_mk[0].astype(BF16), w_mv[0].astype(BF16))

    w_r = jnp.concatenate([
        w_fine[0].reshape(D, N_EXPERTS),
        jnp.repeat(w_coarse[0], EPG, axis=1),
        jnp.zeros((D, LANE - 2 * N_EXPERTS), F32)], axis=1)
    b_r = jnp.concatenate([
        b_fine[0].reshape(N_EXPERTS), jnp.repeat(b_coarse[0], EPG),
        jnp.zeros((LANE - 2 * N_EXPERTS,), F32)]).reshape(1, LANE)
    w_r_hi = w_r.astype(BF16)
    w_r_lo = (w_r - w_r_hi.astype(F32)).astype(BF16)
    w_r_cat = jnp.concatenate([w_r_hi, w_r_lo], axis=1)

    n_sub = T // tm
    tmm = MERGE_TILE
    per = tmm // tm
    sub_idx = lambda b, s: b * (S // tmm) + s
    h2, h2b, comb, route, route_t, seg_cnt = pl.pallas_call(
        functools.partial(_merge_kernel, tm=tmm),
        grid=(B, S // tmm),
        in_specs=[
            pl.BlockSpec((1, tmm, D), lambda b, s: (b, s, 0)),
            pl.BlockSpec((1, tmm, POOL_W), lambda b, s: (b, s, 0)),
            pl.BlockSpec((1, ATTN_W // LANE, tmm, LANE), lambda b, s: (b, 0, s, 0)),
            pl.BlockSpec((1, M, MEM_W), lambda b, s: (b, 0, 0)),
            pl.BlockSpec((1, M, MEM_W), lambda b, s: (b, 0, 0)),
            _const_spec((1, D)), _const_spec((1, D)),
            _const_spec((D, 2 * D)), _const_spec((1, 2 * D)),
            _const_spec((POOL_W, D)), _const_spec((ATTN_W, D)), _const_spec((D, D)),
            _const_spec((1, D)), _const_spec((1, D)),
            _const_spec((D, MEM_W)), _const_spec((MEM_W, D)),
            _const_spec((1, D)), _const_spec((1, D)),
            _const_spec((D, 2 * LANE)), _const_spec((1, LANE)),
        ],
        out_specs=[
            pl.BlockSpec((1, tmm, D), lambda b, s: (b, s, 0)),
            pl.BlockSpec((1, tmm, D), lambda b, s: (b, s, 0)),
            pl.BlockSpec((1, tmm, LANE), lambda b, s: (b, s, 0)),
            pl.BlockSpec((1, tmm, LANE), lambda b, s: (b, s, 0)),
            pl.BlockSpec((per, 8, tm), lambda b, s: (sub_idx(b, s), 0, 0)),
            pl.BlockSpec((per, 8, LANE), lambda b, s: (sub_idx(b, s), 0, 0)),
        ],
        out_shape=[
            jax.ShapeDtypeStruct((B, S, D), F32),
            jax.ShapeDtypeStruct((B, S, D), BF16),
            jax.ShapeDtypeStruct((B, S, LANE), F32),
            jax.ShapeDtypeStruct((B, S, LANE), F32),
            jax.ShapeDtypeStruct((n_sub, 8, tm), F32),
            jax.ShapeDtypeStruct((n_sub, 8, LANE), F32),
        ],
        compiler_params=pltpu.CompilerParams(dimension_semantics=("arbitrary", "arbitrary"),
                                             vmem_limit_bytes=VMEM_LIMIT),
        name="merge_memattn_router",
    )(x, ypool, o_attn, kmem, vmem,
      row2(ln_in_g), row2(ln_in_b), w_gl.astype(BF16), row2(b_gate[0]),
      w_pool_up[0].astype(BF16), w_attn_up[0].astype(BF16), w_mix_out[0].astype(BF16),
      row2(ln1_g[0]), row2(ln1_b[0]),
      (w_mq[0] * (MEM_HD ** -0.5 * LOG2E)).astype(BF16), w_mo[0].astype(BF16),
      row2(ln2_g[0]), row2(ln2_b[0]),
      w_r_cat, b_r)

    pcs = seg_cnt[:, 0, :N_EXPERTS].astype(jnp.int32)
    tot = jnp.sum(pcs, axis=0)
    cap = ((tot + FFN_TILE - 1) // FFN_TILE) * FFN_TILE
    ends = jnp.cumsum(cap)
    base = ends - cap
    gs = base[None, :] + jnp.cumsum(pcs, axis=0) - pcs
    ls = jnp.cumsum(pcs, axis=1) - pcs
    n_sorted = n_sub * COMPACT_ROWS + N_EXPERTS * FFN_TILE
    n_ffn_tiles = n_sorted // FFN_TILE
    n_used = (ends[-1] // FFN_TILE).astype(jnp.int32)
    tile_row = jnp.arange(n_ffn_tiles, dtype=jnp.int32) * FFN_TILE
    tile_exp = jnp.sum(jnp.minimum(tile_row, ends[-1] - 1)[:, None] >= ends[None, :], axis=1).astype(jnp.int32)
    tails = jnp.concatenate([base + tot, cap - tot]).astype(jnp.int32)
    def kth_of(counts, n_slots, seg_off):
        end = jnp.cumsum(counts, axis=1)
        k = jnp.arange(n_slots, dtype=jnp.int32)
        exp_k = jnp.sum(k[None, :, None] >= end[:, None, :], axis=2)
        hot = exp_k[:, :, None] == jnp.arange(N_EXPERTS)[None, None, :]
        pick = lambda tbl: jnp.sum(jnp.where(hot, tbl[:, None, :], 0), axis=2)
        within = k[None, :] - pick(end - counts)
        return pick(ls + seg_off), pick(gs + seg_off), within

    n_big = pcs // BIG_CHUNK
    n_end = (pcs // SEG_ALIGN) % 2
    lo_b, go_b, m_b = kth_of(n_big, MAX_BIG, 0)
    lo_e, go_e, _ = kth_of(n_end, N_EXPERTS, pcs - SEG_ALIGN)
    chunk_tab = jnp.concatenate([lo_b + BIG_CHUNK * m_b, go_b + BIG_CHUNK * m_b, lo_e, go_e],
                                axis=1).astype(jnp.int32).reshape(-1)
    n_chunks = jnp.stack([jnp.sum(n_big, axis=1), jnp.sum(n_end, axis=1)], axis=1).astype(jnp.int32).reshape(-1)

    aug_w = D + LANE
    n_u = MOE_SUBS_PER_STEP
    n_steps = n_sub // n_u
    x_sorted = pl.pallas_call(
        functools.partial(_dispatch_kernel, n_steps=n_steps, rows=COMPACT_ROWS),
        grid_spec=pltpu.PrefetchScalarGridSpec(
            num_scalar_prefetch=3,
            grid=(n_steps,),
            in_specs=[
                pl.BlockSpec((n_u * tm, D), lambda i, *_: (i, 0)),
                pl.BlockSpec((n_u * tm, LANE), lambda i, *_: (i, 0)),
                pl.BlockSpec((n_u, 8, tm), lambda i, *_: (i, 0, 0)),
            ],
            out_specs=pl.BlockSpec(memory_space=pl.ANY),
            scratch_shapes=[
                pltpu.VMEM((2 * n_u, COMPACT_ROWS, aug_w), BF16),
                pltpu.VMEM((FFN_TILE // 2, aug_w), BF16),
                pltpu.SemaphoreType.DMA((2 * n_u,)),
                pltpu.SemaphoreType.DMA(()),
            ],
        ),
        out_shape=jax.ShapeDtypeStruct((n_sorted, aug_w), BF16),
        compiler_params=pltpu.CompilerParams(dimension_semantics=("arbitrary",), vmem_limit_bytes=VMEM_LIMIT),
        name="moe_dispatch",
    )(n_chunks, chunk_tab, tails, h2b.reshape(T, D), comb.reshape(T, LANE), route_t)

    used_tile = lambda t, texp, nused: (jnp.minimum(t, nused[0] - 1), 0)
    y_sorted = pl.pallas_call(
        _ffn_kernel,
        grid_spec=pltpu.PrefetchScalarGridSpec(
            num_scalar_prefetch=2,
            grid=(n_ffn_tiles,),
            in_specs=[
                pl.BlockSpec(memory_space=pl.ANY),
                pl.BlockSpec((1, D, FF), lambda t, texp, nused: (texp[t], 0, 0)),
                pl.BlockSpec((1, D, FF), lambda t, texp, nused: (texp[t], 0, 0)),
                pl.BlockSpec((1, FF, D), lambda t, texp, nused: (texp[t], 0, 0)),
            ],
            out_specs=pl.BlockSpec((FFN_TILE, D), used_tile),
            scratch_shapes=[pltpu.VMEM((D, FF), BF16), pltpu.VMEM((D, FF), BF16), pltpu.VMEM((FF, D), BF16),
                            pltpu.VMEM((XS_DEPTH, FFN_TILE, aug_w), BF16), pltpu.SemaphoreType.DMA((XS_DEPTH,))],
        ),
        out_shape=jax.ShapeDtypeStruct((n_sorted, D), BF16),
        compiler_params=pltpu.CompilerParams(dimension_semantics=("arbitrary",), vmem_limit_bytes=VMEM_LIMIT),
        name="moe_expert_ffn",
    )(tile_exp, n_used.reshape(1), x_sorted, w_gate[0], w_up[0], w_down[0])

    out = pl.pallas_call(
        functools.partial(_combine_kernel, n_steps=n_steps, rows=COMPACT_ROWS),
        grid_spec=pltpu.PrefetchScalarGridSpec(
            num_scalar_prefetch=2,
            grid=(n_steps,),
            in_specs=[
                pl.BlockSpec((n_u * tm, LANE), lambda i, *_: (i, 0)),
                pl.BlockSpec((n_u * tm, D), lambda i, *_: (i, 0)),
                pl.BlockSpec((1, D), lambda i, *_: (0, 0)),
                pl.BlockSpec((1, D), lambda i, *_: (0, 0)),
                pl.BlockSpec(memory_space=pl.ANY),
            ],
            out_specs=pl.BlockSpec((n_u * tm, D), lambda i, *_: (i, 0)),
            scratch_shapes=[
                pltpu.VMEM((2 * n_u, COMPACT_ROWS, D), BF16),
                pltpu.SemaphoreType.DMA((2 * n_u,)),
            ],
        ),
        out_shape=jax.ShapeDtypeStruct((T, D), F32),
        compiler_params=pltpu.CompilerParams(dimension_semantics=("arbitrary",), vmem_limit_bytes=VMEM_LIMIT),
        name="moe_combine_ln3",
    )(n_chunks, chunk_tab, route.reshape(T, LANE), h2.reshape(T, D), row2(ln3_g[0]), row2(ln3_b[0]), y_sorted)
    return out.reshape(B, S, D)
```

```python
import functools
import math

import jax
import jax.numpy as jnp
from jax import lax
from jax.experimental import pallas as pl
from jax.experimental.pallas import tpu as pltpu

D = 1024
POOL_WINDOWS = (2, 4, 8, 16)
POOL_W = 512
N_HEADS = 8
HEAD_DIM = 64
ATTN_W = 512
BLK = 256
N_BLK = 8
TOPK = 3
REL_BUCKETS = 32
REL_MAX_DIST = 128
MEM_HEADS = 4
MEM_HD = 128
MEM_W = 512
EPG = 8
N_EXPERTS = 32
FF = 256
DN_ALPHA = 2.0 ** 0.25
LN_EPS = 1e-5

LANE = 128
TOK_TILE = 512
SEG_ALIGN = 16
COMPACT_ROWS = 2 * TOK_TILE + N_EXPERTS * SEG_ALIGN
BIG_CHUNK = 2 * SEG_ALIGN
MAX_BIG = COMPACT_ROWS // BIG_CHUNK
CHUNK_TAB_W = 2 * MAX_BIG + 2 * N_EXPERTS
DISPATCH_MAIN_ROWS = 2 * TOK_TILE + 18 * SEG_ALIGN
FFN_TILE = 512
PROJ_CHAIN = 256
ATTN_PAIRS_PER_TRIP = 2
MOE_SUBS_PER_STEP = 2
FFN_CHAIN = 512
XS_DEPTH = 3
MERGE_TILE = 1024
HALO = 16
NEG = -1e30
LOG2E = math.log2(math.e)
VMEM_LIMIT = 56 * 1024 * 1024

F32 = jnp.float32
BF16 = jnp.bfloat16

_NT = (((1,), (1,)), ((), ()))


def _dot(a, b):
    return jnp.dot(a, b, preferred_element_type=F32)


def _dot_nt(a, b):
    return lax.dot_general(a, b, _NT, preferred_element_type=F32)


def _split(a):
    hi = a.astype(BF16)
    lo = (a - hi.astype(F32)).astype(BF16)
    return hi, lo


def _interleave(chains):
    results = [None] * len(chains)
    live = list(range(len(chains)))
    while live:
        for ci in list(live):
            try:
                next(chains[ci])
            except StopIteration as done:
                results[ci] = done.value
                live.remove(ci)
    return results


def _ln(x, g, b):
    mu = jnp.mean(x, axis=-1, keepdims=True)
    xc = x - mu
    var = jnp.mean(xc * xc, axis=-1, keepdims=True)
    return xc * lax.rsqrt(var + LN_EPS) * g + b


def _proj_kernel(x_ref, g_ref, b_ref, w_ref, wgrp_ref, pscale_ref,
                 ypool_ref, q_ref, k_ref, v_ref, ubuf, kbt, *, tm):
    s = pl.program_id(1)

    @pl.when(s == 0)
    def _():
        ubuf[0:HALO, :] = jnp.zeros((HALO, POOL_W), F32)
        kbt[...] = jnp.zeros_like(kbt)

    _interleave([_proj_chain(ci, s, tm, x_ref, g_ref, b_ref, w_ref, wgrp_ref, pscale_ref,
                             ypool_ref, q_ref, k_ref, v_ref, ubuf, kbt)
                 for ci in range(tm // PROJ_CHAIN)])
    ubuf[0:HALO, :] = ubuf[tm:tm + HALO, :]


def _proj_chain(ci, s, tm, x_ref, g_ref, b_ref, w_ref, wgrp_ref, pscale_ref,
                ypool_ref, q_ref, k_ref, v_ref, ubuf, kbt):
    n = PROJ_CHAIN
    r0 = ci * n
    rows = slice(r0, r0 + n)
    blk0 = s * (tm // BLK) + ci * (n // BLK)

    h = _ln(x_ref[0, rows, :], g_ref[...], b_ref[...])
    hb = h.astype(BF16)
    zu = _dot(hb, w_ref[:, 0:POOL_W])
    zq = _dot(hb, w_ref[:, POOL_W:POOL_W + ATTN_W])
    zk = _dot(hb, w_ref[:, POOL_W + ATTN_W:POOL_W + 2 * ATTN_W])
    zv = _dot(hb, w_ref[:, POOL_W + 2 * ATTN_W:])
    ubuf[HALO + r0:HALO + r0 + n, :] = zu

    r_io = lax.broadcasted_iota(jnp.int32, kbt.shape, 0)
    c_io = lax.broadcasted_iota(jnp.int32, kbt.shape, 1)
    head_match = (r_io >> 3) == (c_io >> 6)
    for bi in range(n // BLK):
        kmean = jnp.mean(zk[bi * BLK:(bi + 1) * BLK], axis=0, keepdims=True)
        kbt[...] = jnp.where(head_match & ((r_io & 7) == blk0 + bi), kmean, kbt[...])
    yield

    t_pos = s * tm + r0 + lax.broadcasted_iota(jnp.int32, (n, LANE), 0)
    for g, w in enumerate(POOL_WINDOWS):
        cols = slice(g * LANE, (g + 1) * LANE)
        ws = ubuf[HALO + r0:HALO + r0 + n, cols]
        for kk in range(1, w):
            ws = ws + ubuf[HALO + r0 - kk:HALO + r0 - kk + n, cols]
        cnt = jnp.minimum(t_pos + 1, w).astype(F32)
        y = ws / cnt - ubuf[HALO + r0:HALO + r0 + n, cols]
        yg = _dot(y.astype(BF16), wgrp_ref[g]) * pscale_ref[:, cols]
        ypool_ref[0, rows, cols] = yg.astype(ypool_ref.dtype)
    yield

    q_hi, q_lo = _split(zq)
    kb_hi, kb_lo = _split(kbt[...])
    g2 = _dot_nt(q_hi, jnp.concatenate([kb_hi, kb_lo], axis=0))
    gate = g2[:, 0:LANE] + g2[:, LANE:2 * LANE] + _dot_nt(q_lo, kb_hi)
    yield

    lane = lax.broadcasted_iota(jnp.int32, (n, LANE), 1)
    row = lax.broadcasted_iota(jnp.int32, (n, LANE), 0)
    n_l = lane & 7
    jrow = blk0 + (row >> 8)
    past = n_l < jrow
    gt = jnp.where(past, gate, -jnp.inf)
    cnt = jnp.zeros((n, LANE), F32)
    for sh in range(1, N_BLK):
        wrap = (n_l + sh) >= N_BLK
        gm = jnp.where(wrap, pltpu.roll(gt, N_BLK - sh, 1), pltpu.roll(gt, LANE - sh, 1))
        cnt = cnt + jnp.where(wrap, jnp.where(gm >= gt, 1.0, 0.0), jnp.where(gm > gt, 1.0, 0.0))
    keep = (past & (cnt < TOPK)) | (n_l == jrow)
    negmask = jnp.where(keep, 0.0, NEG)

    aug_lane = (lane >= HEAD_DIM) & (lane < HEAD_DIM + N_BLK)
    k_onehot = jnp.where(lane == HEAD_DIM + jrow, 1.0, 0.0)
    head_lane = lane < HEAD_DIM
    for hh in range(N_HEADS):
        cols = slice((hh // 2) * LANE, (hh // 2 + 1) * LANE)
        q_h, k_h = zq[:, cols], zk[:, cols]
        if hh % 2:
            q_h, k_h = pltpu.roll(q_h, HEAD_DIM, 1), pltpu.roll(k_h, HEAD_DIM, 1)
        m_h = jnp.where(aug_lane, pltpu.roll(negmask, HEAD_DIM - N_BLK * hh, 1), 0.0)
        q_ref[0, hh, rows, :] = jnp.where(head_lane, q_h, m_h).astype(q_ref.dtype)
        k_ref[0, hh, rows, :] = jnp.where(head_lane, k_h, k_onehot).astype(k_ref.dtype)
    for p in range(ATTN_W // LANE):
        v_ref[0, p, rows, :] = zv[:, p * LANE:(p + 1) * LANE].astype(v_ref.dtype)


def _attn_kernel(q_ref, k_ref, v_ref, tb_ref, o_ref):
    j = pl.program_id(1)
    lane = lax.broadcasted_iota(jnp.int32, (BLK, LANE), 1)

    def one_head(h, p, jj):
        q = q_ref[0, h]
        own0 = jj * BLK
        pieces = []
        if jj >= 2:
            bias = jnp.concatenate([tb_ref[1, h], tb_ref[0, h]], axis=1)
            s_all = _dot_nt(q, k_ref[0, h, 0:own0 + BLK, :])
            s_all = jnp.concatenate([s_all[:, 0:own0 - BLK], s_all[:, own0 - BLK:] + bias], axis=1)
            pieces.append((s_all, 0, own0 + BLK))
        elif jj == 1:
            bias = jnp.concatenate([tb_ref[1, h], tb_ref[0, h]], axis=1)
            s_near = _dot_nt(q, k_ref[0, h, 0:2 * BLK, :]) + bias
            pieces.append((s_near, 0, 2 * BLK))
        else:
            s_own = _dot_nt(q, k_ref[0, h, own0:own0 + BLK, :]) + tb_ref[0, h]
            pieces.append((s_own, own0, BLK))
        yield
        m = None
        for sc, _, _ in pieces:
            mm = jnp.max(sc, axis=1, keepdims=True)
            m = mm if m is None else jnp.maximum(m, mm)
        l = None
        probs = []
        for sc, start, size in pieces:
            e = jnp.exp2(sc - m)
            ls = jnp.sum(e, axis=1, keepdims=True)
            l = ls if l is None else l + ls
            probs.append((e.astype(BF16), start, size))
        yield
        acc = None
        for pb, start, size in probs:
            pv = _dot(pb, v_ref[0, p, start:start + size, :])
            acc = pv if acc is None else acc + pv
        return acc / l

    for jj in range(N_BLK):
        @pl.when(j == jj)
        def _(jj=jj):
            def group(gi, carry):
                pairs = [gi * ATTN_PAIRS_PER_TRIP + pi for pi in range(ATTN_PAIRS_PER_TRIP)]
                outs = _interleave([one_head(2 * p + hh, p, jj) for p in pairs for hh in range(2)])
                for pi, p in enumerate(pairs):
                    o_ref[0, p] = jnp.where(lane < HEAD_DIM, outs[2 * pi], outs[2 * pi + 1]).astype(o_ref.dtype)
                return carry
            lax.fori_loop(0, N_HEADS // 2 // ATTN_PAIRS_PER_TRIP, group, 0)


def _memkv_kernel(mem_ref, wk_ref, wv_ref, k_ref, v_ref):
    mb = mem_ref[0].astype(BF16)
    k_ref[0] = _dot(mb, wk_ref[...]).astype(k_ref.dtype)
    v_ref[0] = _dot(mb, wv_ref[...]).astype(v_ref.dtype)


def _merge_kernel(x_ref, ypool_ref, o_ref, kmem_ref, vmem_ref,
                  lng_ref, lnb_ref, wgl_ref, bgate_ref, wpu_ref, wau_ref, wout_ref,
                  ln1g_ref, ln1b_ref, wmq_ref, wmo_ref, ln2g_ref, ln2b_ref,
                  wrh_ref, br_ref,
                  h2_ref, h2b_ref, comb_ref, route_ref, route_t_ref, cnt_ref, *, tm):
    _interleave([_merge_chain(ci, x_ref, ypool_ref, o_ref, kmem_ref, vmem_ref,
                              lng_ref, lnb_ref, wgl_ref, bgate_ref, wpu_ref, wau_ref, wout_ref,
                              ln1g_ref, ln1b_ref, wmq_ref, wmo_ref, ln2g_ref, ln2b_ref,
                              wrh_ref, br_ref, h2_ref, h2b_ref, comb_ref, route_ref, route_t_ref, cnt_ref)
                 for ci in range(tm // TOK_TILE)])


def _merge_chain(ci, x_ref, ypool_ref, o_ref, kmem_ref, vmem_ref,
                 lng_ref, lnb_ref, wgl_ref, bgate_ref, wpu_ref, wau_ref, wout_ref,
                 ln1g_ref, ln1b_ref, wmq_ref, wmo_ref, ln2g_ref, ln2b_ref,
                 wrh_ref, br_ref, h2_ref, h2b_ref, comb_ref, route_ref, route_t_ref, cnt_ref):
    rows = slice(ci * TOK_TILE, (ci + 1) * TOK_TILE)
    n_rows = TOK_TILE
    h = _ln(x_ref[0, rows, :], lng_ref[...], lnb_ref[...])
    hb = h.astype(BF16)
    gl = _dot(hb, wgl_ref[...]) + bgate_ref[...]
    yield
    gates = 0.5 * jnp.tanh(0.5 * gl) + 0.5
    y_pool = _dot(ypool_ref[0, rows, :], wpu_ref[...])
    o_cat = jnp.concatenate([o_ref[0, p, rows, :] for p in range(ATTN_W // LANE)], axis=1)
    y_attn = _dot(o_cat, wau_ref[...])
    yield
    merged = gates[:, 0:D] * y_pool + gates[:, D:2 * D] * y_attn
    mix = _dot(merged.astype(BF16), wout_ref[...])
    yield
    h1 = _ln(DN_ALPHA * h + mix, ln1g_ref[...], ln1b_ref[...])

    qm = _dot(h1.astype(BF16), wmq_ref[...]).astype(BF16)
    yield
    outs = []
    for hd in range(MEM_HEADS):
        cols = slice(hd * MEM_HD, (hd + 1) * MEM_HD)
        sc = _dot_nt(qm[:, cols], kmem_ref[0, :, cols])
        m = jnp.max(sc, axis=1, keepdims=True)
        e = jnp.exp2(sc - m)
        l = jnp.sum(e, axis=1, keepdims=True)
        outs.append(_dot(e.astype(BF16), vmem_ref[0, :, cols]) / l)
    om = jnp.concatenate(outs, axis=1).astype(BF16)
    xa = _dot(om, wmo_ref[...])
    yield
    h2 = _ln(DN_ALPHA * h1 + xa, ln2g_ref[...], ln2b_ref[...])
    h2_ref[0, rows, :] = h2
    h2b_ref[0, rows, :] = h2.astype(BF16)

    x_hi, x_lo = _split(h2)
    r2 = _dot(x_hi, wrh_ref[...])
    r = r2[:, 0:LANE] + r2[:, LANE:2 * LANE] + _dot(x_lo, wrh_ref[:, 0:LANE]) + br_ref[...]
    yield
    lane = lax.broadcasted_iota(jnp.int32, (n_rows, LANE), 1)
    lane_f = lane.astype(F32)
    cmask = (lane >= N_EXPERTS) & (lane < 2 * N_EXPERTS)
    c = jnp.where(cmask, r, -jnp.inf)
    cmax = jnp.max(c, axis=1, keepdims=True)
    ce = jnp.exp(c - cmax)
    csum = jnp.sum(ce, axis=1, keepdims=True) * (1.0 / EPG)
    g_prob = 1.0 / csum
    grp_lane = ((lane & (N_EXPERTS - 1)) >> 3).astype(F32)
    gidx = jnp.min(jnp.where(cmask & (c == cmax), grp_lane, 99.0), axis=1, keepdims=True)
    fmask = (lane < N_EXPERTS) & (grp_lane == gidx)
    f = jnp.where(fmask, r, -jnp.inf)
    fmax = jnp.max(f, axis=1, keepdims=True)
    fe = jnp.exp(f - fmax)
    fsum = jnp.sum(fe, axis=1, keepdims=True)
    prob = fe / fsum
    p1 = jnp.max(prob, axis=1, keepdims=True)
    i1 = jnp.min(jnp.where(fmask & (prob == p1), lane_f, 999.0), axis=1, keepdims=True)
    rest = fmask & (lane_f != i1)
    prob2 = jnp.where(rest, prob, -1.0)
    p2 = jnp.max(prob2, axis=1, keepdims=True)
    i2 = jnp.min(jnp.where(rest & (prob2 == p2), lane_f, 999.0), axis=1, keepdims=True)
    den = p1 + p2
    comb = jnp.where(lane_f == i1, g_prob * (p1 / den),
                     jnp.where(lane_f == i2, g_prob * (p2 / den), 0.0))
    comb_ref[0, rows, :] = comb
    yield

    sel = jnp.where((lane_f == i1) | (lane_f == i2), 1.0, 0.0)
    cnt = jnp.sum(sel, axis=0, keepdims=True)
    pc = jnp.floor((cnt + (SEG_ALIGN - 1)) * (1.0 / SEG_ALIGN)) * SEG_ALIGN
    lane8 = lax.broadcasted_iota(jnp.int32, (8, LANE), 1)
    inc = jnp.broadcast_to(pc, (8, LANE))
    for sh in (1, 2, 4, 8, 16, 32, 64):
        inc = inc + jnp.where(lane8 >= sh, pltpu.roll(inc, sh, 1), 0.0)
    seg_start = inc[0:1] - pc
    t_row = lax.broadcasted_iota(jnp.int32, (n_rows, n_rows), 0)
    t_col = lax.broadcasted_iota(jnp.int32, (n_rows, n_rows), 1)
    earlier = jnp.where(t_row > t_col, 1.0, 0.0).astype(BF16)
    rank = _dot(earlier, sel.astype(BF16))
    pos = seg_start + rank
    d1 = jnp.sum(jnp.where(lane_f == i1, pos, 0.0), axis=1, keepdims=True)
    d2 = jnp.sum(jnp.where(lane_f == i2, pos, 0.0), axis=1, keepdims=True)
    route = jnp.where(lane == 0, d1, jnp.where(lane == 1, d2, 0.0))
    route_ref[0, rows, :] = route
    r_hi, r_lo = _split(route)
    eye = jnp.where(lax.broadcasted_iota(jnp.int32, (8, LANE), 0) == lane8, 1.0, 0.0).astype(BF16)
    route_t_ref[ci] = _dot_nt(eye, r_hi) + _dot_nt(eye, r_lo)
    cnt_ref[ci] = jnp.broadcast_to(pc, (8, LANE))


def _chunk_copies(sub, nch_ref, tab_ref, make_copy, act):
    base = sub * CHUNK_TAB_W
    for size, cnt_i, lo_off, go_off in ((BIG_CHUNK, 0, 0, MAX_BIG), (SEG_ALIGN, 1, 2 * MAX_BIG, 2 * MAX_BIG + N_EXPERTS)):
        def body(k, carry, size=size, lo_off=lo_off, go_off=go_off):
            lo = pl.multiple_of(tab_ref[base + lo_off + k], SEG_ALIGN)
            go = pl.multiple_of(tab_ref[base + go_off + k], SEG_ALIGN)
            act(make_copy(lo, go, size))
            return carry
        lax.fori_loop(0, nch_ref[2 * sub + cnt_i], body, 0)


def _dispatch_kernel(nch_ref, tab_ref, tail_ref, xb_ref, comb_ref, route_t_ref, xs_ref,
                     cbuf, zbuf, sem, zsem, *, n_steps, rows):
    i = pl.program_id(0)
    par = lax.rem(i, 2)
    n_u = MOE_SUBS_PER_STEP
    tm = TOK_TILE

    def copies(step, par_, act):
        for u in range(n_u):
            slot_ = par_ * n_u + u

            def mk(lo, go, size, slot_=slot_):
                return pltpu.make_async_copy(cbuf.at[slot_, pl.ds(lo, size), :],
                                             xs_ref.at[pl.ds(go, size), :], sem.at[slot_])
            _chunk_copies(step * n_u + u, nch_ref, tab_ref, mk, act)

    @pl.when(i >= 2)
    def _():
        copies(i - 2, par, lambda c: c.wait())

    def one_hot_rows(u, r_lo, r_hi):
        r_io = (r_lo + lax.broadcasted_iota(jnp.int32, (r_hi - r_lo, tm), 0)).astype(F32)
        d1 = route_t_ref[u, 0:1, :]
        d2 = route_t_ref[u, 1:2, :]
        return jnp.where((r_io == d1) | (r_io == d2), 1.0, 0.0).astype(BF16)

    def augmented(u):
        rs = slice(u * tm, (u + 1) * tm)
        comb = comb_ref[rs, :]
        c_hi = comb.astype(BF16).astype(F32)
        c_pack = (c_hi + pltpu.roll(comb - c_hi, N_EXPERTS, 1)).astype(BF16)
        return jnp.concatenate([xb_ref[rs, :], c_pack], axis=1)

    def chain(u):
        p_mat = one_hot_rows(u, 0, DISPATCH_MAIN_ROWS)
        x_aug = augmented(u)
        yield
        cbuf[par * n_u + u, 0:DISPATCH_MAIN_ROWS, :] = _dot(p_mat, x_aug).astype(BF16)

    _interleave([chain(u) for u in range(n_u)])

    for u in range(n_u):
        sub = i * n_u + u
        used_rows = nch_ref[2 * sub] * BIG_CHUNK + nch_ref[2 * sub + 1] * SEG_ALIGN

        @pl.when(used_rows > DISPATCH_MAIN_ROWS)
        def _(u=u):
            cbuf[par * n_u + u, DISPATCH_MAIN_ROWS:rows, :] = _dot(
                one_hot_rows(u, DISPATCH_MAIN_ROWS, rows), augmented(u)).astype(BF16)

    copies(i, par, lambda c: c.start())

    @pl.when(i == n_steps - 1)
    def _():
        if n_steps >= 2:
            copies(i - 1, 1 - par, lambda c: c.wait())
        copies(i, par, lambda c: c.wait())
        zbuf[...] = jnp.zeros_like(zbuf)

        def tails(act):
            def body(e, carry):
                st = tail_ref[e]
                n = tail_ref[N_EXPERTS + e]
                off = jnp.int32(0)
                size = zbuf.shape[0]
                while size >= SEG_ALIGN:
                    bit = (n & size) != 0

                    @pl.when(bit)
                    def _(size=size, off=off):
                        act(pltpu.make_async_copy(
                            zbuf.at[pl.ds(0, size), :],
                            xs_ref.at[pl.ds(pl.multiple_of(st + off, SEG_ALIGN), size), :], zsem))
                    off = off + jnp.where(bit, size, 0)
                    size //= 2
                return carry
            lax.fori_loop(0, N_EXPERTS, body, 0)
        tails(lambda c: c.start())
        tails(lambda c: c.wait())


def _ffn_kernel(texp_ref, nused_ref, xs_ref, wg_ref, wu_ref, wd_ref, ys_ref, wg_b, wu_b, wd_b, xbuf, xsem):
    t = pl.program_id(0)
    e = texp_ref[t]
    n_used = nused_ref[0]

    def fetch(u):
        slot = lax.rem(u, XS_DEPTH)
        return pltpu.make_async_copy(xs_ref.at[pl.ds(pl.multiple_of(u * FFN_TILE, FFN_TILE), FFN_TILE), :],
                                     xbuf.at[slot], xsem.at[slot])

    @pl.when(t == 0)
    def _():
        for u in range(XS_DEPTH - 1):
            @pl.when(u < n_used)
            def _(u=u):
                fetch(jnp.int32(u)).start()

    @pl.when(t + (XS_DEPTH - 1) < n_used)
    def _():
        fetch(t + (XS_DEPTH - 1)).start()

    @pl.when((t == 0) | (e != texp_ref[jnp.maximum(t - 1, 0)]))
    def _():
        wg_b[...] = wg_ref[0].astype(BF16)
        wu_b[...] = wu_ref[0].astype(BF16)
        wd_b[...] = wd_ref[0].astype(BF16)

    @pl.when(t < n_used)
    def _():
        fetch(t).wait()
        slot = lax.rem(t, XS_DEPTH)

        def chain(ci):
            rs = slice(ci * FFN_CHAIN, (ci + 1) * FFN_CHAIN)
            xrow = xbuf[slot, rs, 0:D]
            cw = xbuf[slot, rs, D:D + LANE].astype(F32)
            lane = lax.broadcasted_iota(jnp.int32, cw.shape, 1)
            c = jnp.sum(jnp.where((lane == e) | (lane == e + N_EXPERTS), cw, 0.0), axis=1, keepdims=True)
            a = _dot(xrow, wg_b[...])
            b = _dot(xrow, wu_b[...])
            yield
            hid = ((a * jax.nn.sigmoid(a)) * b * c).astype(BF16)
            yield
            ys_ref[rs, :] = _dot(hid, wd_b[...]).astype(ys_ref.dtype)

        _interleave([chain(ci) for ci in range(FFN_TILE // FFN_CHAIN)])


def _combine_kernel(nch_ref, tab_ref, route_ref, h2_ref, g_ref, b_ref, ys_ref, out_ref,
                    ybuf, sem, *, n_steps, rows):
    i = pl.program_id(0)
    par = lax.rem(i, 2)
    n_u = MOE_SUBS_PER_STEP
    tm = TOK_TILE

    def copies(step, par_, act):
        for u in range(n_u):
            slot_ = par_ * n_u + u

            def mk(lo, go, size, slot_=slot_):
                return pltpu.make_async_copy(ys_ref.at[pl.ds(go, size), :],
                                             ybuf.at[slot_, pl.ds(lo, size), :], sem.at[slot_])
            _chunk_copies(step * n_u + u, nch_ref, tab_ref, mk, act)

    @pl.when(i == 0)
    def _():
        ybuf[...] = jnp.zeros_like(ybuf)
        copies(0, 0, lambda c: c.start())

    @pl.when(i + 1 < n_steps)
    def _():
        copies(i + 1, 1 - par, lambda c: c.start())

    copies(i, par, lambda c: c.wait())

    def chain(u):
        rs = slice(u * tm, (u + 1) * tm)
        r_io = lax.broadcasted_iota(jnp.int32, (tm, rows), 1).astype(F32)
        d1 = route_ref[rs, 0:1]
        d2 = route_ref[rs, 1:2]
        p_t = jnp.where((r_io == d1) | (r_io == d2), 1.0, 0.0).astype(BF16)
        yield
        ff = _dot(p_t, ybuf[par * n_u + u])
        yield
        out_ref[rs, :] = _ln(DN_ALPHA * h2_ref[rs, :] + ff, g_ref[...], b_ref[...])

    _interleave([chain(u) for u in range(n_u)])


def _bias_kernel(tbl_ref, bkt_ref, out_ref):
    h = pl.program_id(0)
    far = tbl_ref[h, REL_BUCKETS - 1]
    for which in range(2):
        bk = bkt_ref[which]
        acc = jnp.where(bk < 0, NEG, 0.0)
        for kk in range(REL_BUCKETS):
            acc = jnp.where(bk == kk, (tbl_ref[h, kk] - far) * LOG2E, acc)
        out_ref[which, 0] = acc


def _rel_bucket_table(dist):
    max_exact = REL_BUCKETS // 2
    d = jnp.maximum(dist, 0)
    large = max_exact + (jnp.log(jnp.maximum(d, 1).astype(F32) / max_exact)
                         / math.log(REL_MAX_DIST / max_exact) * (REL_BUCKETS - max_exact)).astype(jnp.int32)
    large = jnp.minimum(large, REL_BUCKETS - 1)
    return jnp.where(d < max_exact, d, large)


def _const_spec(shape):
    nd = len(shape)
    return pl.BlockSpec(shape, lambda *_: (0,) * nd)


def kernel(x, mem, ln_in_g, ln_in_b, rel_bias, w_in, b_gate, w_pool_grp, pool_scale, w_pool_up, w_attn_up,
           w_mix_out, ln1_g, ln1_b, w_mq, w_mk, w_mv, w_mo, ln2_g, ln2_b, w_coarse, b_coarse, w_fine, b_fine,
           w_gate, w_up, w_down, ln3_g, ln3_b):
    B, S, _ = x.shape
    assert S == N_BLK * BLK and w_in.shape[0] == 1
    M = mem.shape[1]
    T = B * S
    tm = TOK_TILE

    wi = w_in[0]
    w_u = wi[:, 0:POOL_W]
    w_q = wi[:, POOL_W:POOL_W + ATTN_W] * (HEAD_DIM ** -0.5 * LOG2E)
    w_k = wi[:, POOL_W + ATTN_W:POOL_W + 2 * ATTN_W]
    w_v = wi[:, POOL_W + 2 * ATTN_W:POOL_W + 3 * ATTN_W]
    w_gl = wi[:, POOL_W + 3 * ATTN_W:]

    w1 = jnp.concatenate([w_u, w_q, w_k, w_v], axis=1).astype(BF16)
    row2 = lambda a: a.reshape(1, -1)

    iq = jnp.arange(BLK, dtype=jnp.int32)[:, None]
    ik = jnp.arange(BLK, dtype=jnp.int32)[None, :]
    d_own = iq - ik
    bkt = jnp.stack([jnp.where(d_own >= 0, _rel_bucket_table(d_own), -1), _rel_bucket_table(d_own + BLK)])
    t_bias = pl.pallas_call(
        _bias_kernel,
        grid=(N_HEADS,),
        in_specs=[pl.BlockSpec(memory_space=pltpu.SMEM), _const_spec((2, BLK, BLK))],
        out_specs=pl.BlockSpec((2, 1, BLK, BLK), lambda h: (0, h, 0, 0)),
        out_shape=jax.ShapeDtypeStruct((2, N_HEADS, BLK, BLK), F32),
        name="relbias_tiles",
    )(rel_bias.T, bkt)

    n_w1 = w1.shape[1]
    tmp = MERGE_TILE
    ypool, q_aug, k_aug, v_p = pl.pallas_call(
        functools.partial(_proj_kernel, tm=tmp),
        grid=(B, S // tmp),
        in_specs=[
            pl.BlockSpec((1, tmp, D), lambda b, s: (b, s, 0)),
            _const_spec((1, D)), _const_spec((1, D)),
            _const_spec((D, n_w1)),
            _const_spec((len(POOL_WINDOWS), LANE, LANE)),
            _const_spec((1, POOL_W)),
        ],
        out_specs=[
            pl.BlockSpec((1, tmp, POOL_W), lambda b, s: (b, s, 0)),
            pl.BlockSpec((1, N_HEADS, tmp, LANE), lambda b, s: (b, 0, s, 0)),
            pl.BlockSpec((1, N_HEADS, tmp, LANE), lambda b, s: (b, 0, s, 0)),
            pl.BlockSpec((1, ATTN_W // LANE, tmp, LANE), lambda b, s: (b, 0, s, 0)),
        ],
        out_shape=[
            jax.ShapeDtypeStruct((B, S, POOL_W), BF16),
            jax.ShapeDtypeStruct((B, N_HEADS, S, LANE), BF16),
            jax.ShapeDtypeStruct((B, N_HEADS, S, LANE), BF16),
            jax.ShapeDtypeStruct((B, ATTN_W // LANE, S, LANE), BF16),
        ],
        scratch_shapes=[pltpu.VMEM((HALO + tmp, POOL_W), F32), pltpu.VMEM((LANE, ATTN_W), F32)],
        compiler_params=pltpu.CompilerParams(dimension_semantics=("arbitrary", "arbitrary"),
                                             vmem_limit_bytes=VMEM_LIMIT),
        name="proj_pool_gate",
    )(x, row2(ln_in_g), row2(ln_in_b), w1, w_pool_grp[0].astype(BF16), row2(pool_scale[0]))

    o_attn = pl.pallas_call(
        _attn_kernel,
        grid=(B, N_BLK),
        in_specs=[
            pl.BlockSpec((1, N_HEADS, BLK, LANE), lambda b, j: (b, 0, j, 0)),
            pl.BlockSpec((1, N_HEADS, S, LANE), lambda b, j: (b, 0, 0, 0)),
            pl.BlockSpec((1, ATTN_W // LANE, S, LANE), lambda b, j: (b, 0, 0, 0)),
            _const_spec((2, N_HEADS, BLK, BLK)),
        ],
        out_specs=pl.BlockSpec((1, ATTN_W // LANE, BLK, LANE), lambda b, j: (b, 0, j, 0)),
        out_shape=jax.ShapeDtypeStruct((B, ATTN_W // LANE, S, LANE), BF16),
        compiler_params=pltpu.CompilerParams(dimension_semantics=("arbitrary", "arbitrary"),
                                             vmem_limit_bytes=VMEM_LIMIT),
        name="moba_attn",
    )(q_aug, k_aug, v_p, t_bias)

    kmem, vmem = pl.pallas_call(
        _memkv_kernel,
        grid=(B,),
        in_specs=[pl.BlockSpec((1, M, D), lambda b: (b, 0, 0)),
                  _const_spec((D, MEM_W)), _const_spec((D, MEM_W))],
        out_specs=[pl.BlockSpec((1, M, MEM_W), lambda b: (b, 0, 0)),
                   pl.BlockSpec((1, M, MEM_W), lambda b: (b, 0, 0))],
        out_shape=[jax.ShapeDtypeStruct((B, M, MEM_W), BF16)] * 2,
        compiler_params=pltpu.CompilerParams(dimension_semantics=("arbitrary",)),
        name="mem_kv",
    )(mem, w_mk[0].astype(BF16), w_mv[0].astype(BF16))

    w_r = jnp.concatenate([
        w_fine[0].reshape(D, N_EXPERTS),
        jnp.repeat(w_coarse[0], EPG, axis=1),
        jnp.zeros((D, LANE - 2 * N_EXPERTS), F32)], axis=1)
    b_r = jnp.concatenate([
        b_fine[0].reshape(N_EXPERTS), jnp.repeat(b_coarse[0], EPG),
        jnp.zeros((LANE - 2 * N_EXPERTS,), F32)]).reshape(1, LANE)
    w_r_hi = w_r.astype(BF16)
    w_r_lo = (w_r - w_r_hi.astype(F32)).astype(BF16)
    w_r_cat = jnp.concatenate([w_r_hi, w_r_lo], axis=1)

    n_sub = T // tm
    tmm = MERGE_TILE
    per = tmm // tm
    sub_idx = lambda b, s: b * (S // tmm) + s
    h2, h2b, comb, route, route_t, seg_cnt = pl.pallas_call(
        functools.partial(_merge_kernel, tm=tmm),
        grid=(B, S // tmm),
        in_specs=[
            pl.BlockSpec((1, tmm, D), lambda b, s: (b, s, 0)),
            pl.BlockSpec((1, tmm, POOL_W), lambda b, s: (b, s, 0)),
            pl.BlockSpec((1, ATTN_W // LANE, tmm, LANE), lambda b, s: (b, 0, s, 0)),
            pl.BlockSpec((1, M, MEM_W), lambda b, s: (b, 0, 0)),
            pl.BlockSpec((1, M, MEM_W), lambda b, s: (b, 0, 0)),
            _const_spec((1, D)), _const_spec((1, D)),
            _const_spec((D, 2 * D)), _const_spec((1, 2 * D)),
            _const_spec((POOL_W, D)), _const_spec((ATTN_W, D)), _const_spec((D, D)),
            _const_spec((1, D)), _const_spec((1, D)),
            _const_spec((D, MEM_W)), _const_spec((MEM_W, D)),
            _const_spec((1, D)), _const_spec((1, D)),
            _const_spec((D, 2 * LANE)), _const_spec((1, LANE)),
        ],
        out_specs=[
            pl.BlockSpec((1, tmm, D), lambda b, s: (b, s, 0)),
            pl.BlockSpec((1, tmm, D), lambda b, s: (b, s, 0)),
            pl.BlockSpec((1, tmm, LANE), lambda b, s: (b, s, 0)),
            pl.BlockSpec((1, tmm, LANE), lambda b, s: (b, s, 0)),
            pl.BlockSpec((per, 8, tm), lambda b, s: (sub_idx(b, s), 0, 0)),
            pl.BlockSpec((per, 8, LANE), lambda b, s: (sub_idx(b, s), 0, 0)),
        ],
        out_shape=[
            jax.ShapeDtypeStruct((B, S, D), F32),
            jax.ShapeDtypeStruct((B, S, D), BF16),
            jax.ShapeDtypeStruct((B, S, LANE), F32),
            jax.ShapeDtypeStruct((B, S, LANE), F32),
            jax.ShapeDtypeStruct((n_sub, 8, tm), F32),
            jax.ShapeDtypeStruct((n_sub, 8, LANE), F32),
        ],
        compiler_params=pltpu.CompilerParams(dimension_semantics=("arbitrary", "arbitrary"),
                                             vmem_limit_bytes=VMEM_LIMIT),
        name="merge_memattn_router",
    )(x, ypool, o_attn, kmem, vmem,
      row2(ln_in_g), row2(ln_in_b), w_gl.astype(BF16), row2(b_gate[0]),
      w_pool_up[0].astype(BF16), w_attn_up[0].astype(BF16), w_mix_out[0].astype(BF16),
      row2(ln1_g[0]), row2(ln1_b[0]),
      (w_mq[0] * (MEM_HD ** -0.5 * LOG2E)).astype(BF16), w_mo[0].astype(BF16),
      row2(ln2_g[0]), row2(ln2_b[0]),
      w_r_cat, b_r)

    pcs = seg_cnt[:, 0, :N_EXPERTS].astype(jnp.int32)
    tot = jnp.sum(pcs, axis=0)
    cap = ((tot + FFN_TILE - 1) // FFN_TILE) * FFN_TILE
    ends = jnp.cumsum(cap)
    base = ends - cap
    gs = base[None, :] + jnp.cumsum(pcs, axis=0) - pcs
    ls = jnp.cumsum(pcs, axis=1) - pcs
    n_sorted = n_sub * COMPACT_ROWS + N_EXPERTS * FFN_TILE
    n_ffn_tiles = n_sorted // FFN_TILE
    n_used = (ends[-1] // FFN_TILE).astype(jnp.int32)
    tile_row = jnp.arange(n_ffn_tiles, dtype=jnp.int32) * FFN_TILE
    tile_exp = jnp.sum(jnp.minimum(tile_row, ends[-1] - 1)[:, None] >= ends[None, :], axis=1).astype(jnp.int32)
    tails = jnp.concatenate([base + tot, cap - tot]).astype(jnp.int32)
    def kth_of(counts, n_slots, seg_off):
        end = jnp.cumsum(counts, axis=1)
        k = jnp.arange(n_slots, dtype=jnp.int32)
        exp_k = jnp.sum(k[None, :, None] >= end[:, None, :], axis=2)
        hot = exp_k[:, :, None] == jnp.arange(N_EXPERTS)[None, None, :]
        pick = lambda tbl: jnp.sum(jnp.where(hot, tbl[:, None, :], 0), axis=2)
        within = k[None, :] - pick(end - counts)
        return pick(ls + seg_off), pick(gs + seg_off), within

    n_big = pcs // BIG_CHUNK
    n_end = (pcs // SEG_ALIGN) % 2
    lo_b, go_b, m_b = kth_of(n_big, MAX_BIG, 0)
    lo_e, go_e, _ = kth_of(n_end, N_EXPERTS, pcs - SEG_ALIGN)
    chunk_tab = jnp.concatenate([lo_b + BIG_CHUNK * m_b, go_b + BIG_CHUNK * m_b, lo_e, go_e],
                                axis=1).astype(jnp.int32).reshape(-1)
    n_chunks = jnp.stack([jnp.sum(n_big, axis=1), jnp.sum(n_end, axis=1)], axis=1).astype(jnp.int32).reshape(-1)

    aug_w = D + LANE
    n_u = MOE_SUBS_PER_STEP
    n_steps = n_sub // n_u
    x_sorted = pl.pallas_call(
        functools.partial(_dispatch_kernel, n_steps=n_steps, rows=COMPACT_ROWS),
        grid_spec=pltpu.PrefetchScalarGridSpec(
            num_scalar_prefetch=3,
            grid=(n_steps,),
            in_specs=[
                pl.BlockSpec((n_u * tm, D), lambda i, *_: (i, 0)),
                pl.BlockSpec((n_u * tm, LANE), lambda i, *_: (i, 0)),
                pl.BlockSpec((n_u, 8, tm), lambda i, *_: (i, 0, 0)),
            ],
            out_specs=pl.BlockSpec(memory_space=pl.ANY),
            scratch_shapes=[
                pltpu.VMEM((2 * n_u, COMPACT_ROWS, aug_w), BF16),
                pltpu.VMEM((FFN_TILE // 2, aug_w), BF16),
                pltpu.SemaphoreType.DMA((2 * n_u,)),
                pltpu.SemaphoreType.DMA(()),
            ],
        ),
        out_shape=jax.ShapeDtypeStruct((n_sorted, aug_w), BF16),
        compiler_params=pltpu.CompilerParams(dimension_semantics=("arbitrary",), vmem_limit_bytes=VMEM_LIMIT),
        name="moe_dispatch",
    )(n_chunks, chunk_tab, tails, h2b.reshape(T, D), comb.reshape(T, LANE), route_t)

    used_tile = lambda t, texp, nused: (jnp.minimum(t, nused[0] - 1), 0)
    y_sorted = pl.pallas_call(
        _ffn_kernel,
        grid_spec=pltpu.PrefetchScalarGridSpec(
            num_scalar_prefetch=2,
            grid=(n_ffn_tiles,),
            in_specs=[
                pl.BlockSpec(memory_space=pl.ANY),
                pl.BlockSpec((1, D, FF), lambda t, texp, nused: (texp[t], 0, 0)),
                pl.BlockSpec((1, D, FF), lambda t, texp, nused: (texp[t], 0, 0)),
                pl.BlockSpec((1, FF, D), lambda t, texp, nused: (texp[t], 0, 0)),
            ],
            out_specs=pl.BlockSpec((FFN_TILE, D), used_tile),
            scratch_shapes=[pltpu.VMEM((D, FF), BF16), pltpu.VMEM((D, FF), BF16), pltpu.VMEM((FF, D), BF16),
                            pltpu.VMEM((XS_DEPTH, FFN_TILE, aug_w), BF16), pltpu.SemaphoreType.DMA((XS_DEPTH,))],
        ),
        out_shape=jax.ShapeDtypeStruct((n_sorted, D), BF16),
        compiler_params=pltpu.CompilerParams(dimension_semantics=("arbitrary",), vmem_limit_bytes=VMEM_LIMIT),
        name="moe_expert_ffn",
    )(tile_exp, n_used.reshape(1), x_sorted, w_gate[0], w_up[0], w_down[0])

    out = pl.pallas_call(
        functools.partial(_combine_kernel, n_steps=n_steps, rows=COMPACT_ROWS),
        grid_spec=pltpu.PrefetchScalarGridSpec(
            num_scalar_prefetch=2,
            grid=(n_steps,),
            in_specs=[
                pl.BlockSpec((n_u * tm, LANE), lambda i, *_: (i, 0)),
                pl.BlockSpec((n_u * tm, D), lambda i, *_: (i, 0)),
                pl.BlockSpec((1, D), lambda i, *_: (0, 0)),
                pl.BlockSpec((1, D), lambda i, *_: (0, 0)),
                pl.BlockSpec(memory_space=pl.ANY),
            ],
            out_specs=pl.BlockSpec((n_u * tm, D), lambda i, *_: (i, 0)),
            scratch_shapes=[
                pltpu.VMEM((2 * n_u, COMPACT_ROWS, D), BF16),
                pltpu.SemaphoreType.DMA((2 * n_u,)),
            ],
        ),
        out_shape=jax.ShapeDtypeStruct((T, D), F32),
        compiler_params=pltpu.CompilerParams(dimension_semantics=("arbitrary",), vmem_limit_bytes=VMEM_LIMIT),
        name="moe_combine_ln3",
    )(n_chunks, chunk_tab, route.reshape(T, LANE), h2.reshape(T, D), row2(ln3_g[0]), row2(ln3_b[0]), y_sorted)
    return out.reshape(B, S, D)
```

```python
import functools
import math

import jax
import jax.numpy as jnp
from jax import lax
from jax.experimental import pallas as pl
from jax.experimental.pallas import tpu as pltpu

D = 1024
POOL_WINDOWS = (2, 4, 8, 16)
POOL_W = 512
N_HEADS = 8
HEAD_DIM = 64
ATTN_W = 512
BLK = 256
N_BLK = 8
TOPK = 3
REL_BUCKETS = 32
REL_MAX_DIST = 128
MEM_HEADS = 4
MEM_HD = 128
MEM_W = 512
EPG = 8
N_EXPERTS = 32
FF = 256
DN_ALPHA = 2.0 ** 0.25
LN_EPS = 1e-5

LANE = 128
TOK_TILE = 512
SEG_ALIGN = 16
COMPACT_ROWS = 2 * TOK_TILE + N_EXPERTS * SEG_ALIGN
BIG_CHUNK = 2 * SEG_ALIGN
MAX_BIG = COMPACT_ROWS // BIG_CHUNK
CHUNK_TAB_W = 2 * MAX_BIG + 2 * N_EXPERTS
DISPATCH_MAIN_ROWS = 2 * TOK_TILE + 18 * SEG_ALIGN
FFN_TILE = 512
PROJ_CHAIN = 256
ATTN_PAIRS_PER_TRIP = 2
MOE_SUBS_PER_STEP = 2
FFN_CHAIN = 512
XS_DEPTH = 3
MERGE_TILE = 1024
HALO = 16
NEG = -1e30
LOG2E = math.log2(math.e)
VMEM_LIMIT = 56 * 1024 * 1024

F32 = jnp.float32
BF16 = jnp.bfloat16

_NT = (((1,), (1,)), ((), ()))


def _dot(a, b):
    return jnp.dot(a, b, preferred_element_type=F32)


def _dot_nt(a, b):
    return lax.dot_general(a, b, _NT, preferred_element_type=F32)


def _split(a):
    hi = a.astype(BF16)
    lo = (a - hi.astype(F32)).astype(BF16)
    return hi, lo


def _interleave(chains):
    results = [None] * len(chains)
    live = list(range(len(chains)))
    while live:
        for ci in list(live):
            try:
                next(chains[ci])
            except StopIteration as done:
                results[ci] = done.value
                live.remove(ci)
    return results


def _ln(x, g, b):
    mu = jnp.mean(x, axis=-1, keepdims=True)
    xc = x - mu
    var = jnp.mean(xc * xc, axis=-1, keepdims=True)
    return xc * lax.rsqrt(var + LN_EPS) * g + b


def _proj_kernel(x_ref, g_ref, b_ref, w_ref, wgrp_ref, pscale_ref,
                 ypool_ref, q_ref, k_ref, v_ref, ubuf, kbt, *, tm):
    s = pl.program_id(1)

    @pl.when(s == 0)
    def _():
        ubuf[0:HALO, :] = jnp.zeros((HALO, POOL_W), F32)
        kbt[...] = jnp.zeros_like(kbt)

    _interleave([_proj_chain(ci, s, tm, x_ref, g_ref, b_ref, w_ref, wgrp_ref, pscale_ref,
                             ypool_ref, q_ref, k_ref, v_ref, ubuf, kbt)
                 for ci in range(tm // PROJ_CHAIN)])
    ubuf[0:HALO, :] = ubuf[tm:tm + HALO, :]


def _proj_chain(ci, s, tm, x_ref, g_ref, b_ref, w_ref, wgrp_ref, pscale_ref,
                ypool_ref, q_ref, k_ref, v_ref, ubuf, kbt):
    n = PROJ_CHAIN
    r0 = ci * n
    rows = slice(r0, r0 + n)
    blk0 = s * (tm // BLK) + ci * (n // BLK)

    h = _ln(x_ref[0, rows, :], g_ref[...], b_ref[...])
    hb = h.astype(BF16)
    zu = _dot(hb, w_ref[:, 0:POOL_W])
    zq = _dot(hb, w_ref[:, POOL_W:POOL_W + ATTN_W])
    zk = _dot(hb, w_ref[:, POOL_W + ATTN_W:POOL_W + 2 * ATTN_W])
    zv = _dot(hb, w_ref[:, POOL_W + 2 * ATTN_W:])
    ubuf[HALO + r0:HALO + r0 + n, :] = zu

    r_io = lax.broadcasted_iota(jnp.int32, kbt.shape, 0)
    c_io = lax.broadcasted_iota(jnp.int32, kbt.shape, 1)
    head_match = (r_io >> 3) == (c_io >> 6)
    for bi in range(n // BLK):
        kmean = jnp.mean(zk[bi * BLK:(bi + 1) * BLK], axis=0, keepdims=True)
        kbt[...] = jnp.where(head_match & ((r_io & 7) == blk0 + bi), kmean, kbt[...])
    yield

    t_pos = s * tm + r0 + lax.broadcasted_iota(jnp.int32, (n, LANE), 0)
    for g, w in enumerate(POOL_WINDOWS):
        cols = slice(g * LANE, (g + 1) * LANE)
        ws = ubuf[HALO + r0:HALO + r0 + n, cols]
        for kk in range(1, w):
            ws = ws + ubuf[HALO + r0 - kk:HALO + r0 - kk + n, cols]
        cnt = jnp.minimum(t_pos + 1, w).astype(F32)
        y = ws / cnt - ubuf[HALO + r0:HALO + r0 + n, cols]
        yg = _dot(y.astype(BF16), wgrp_ref[g]) * pscale_ref[:, cols]
        ypool_ref[0, rows, cols] = yg.astype(ypool_ref.dtype)
    yield

    q_hi, q_lo = _split(zq)
    kb_hi, kb_lo = _split(kbt[...])
    g2 = _dot_nt(q_hi, jnp.concatenate([kb_hi, kb_lo], axis=0))
    gate = g2[:, 0:LANE] + g2[:, LANE:2 * LANE] + _dot_nt(q_lo, kb_hi)
    yield

    lane = lax.broadcasted_iota(jnp.int32, (n, LANE), 1)
    row = lax.broadcasted_iota(jnp.int32, (n, LANE), 0)
    n_l = lane & 7
    jrow = blk0 + (row >> 8)
    past = n_l < jrow
    gt = jnp.where(past, gate, -jnp.inf)
    cnt = jnp.zeros((n, LANE), F32)
    for sh in range(1, N_BLK):
        wrap = (n_l + sh) >= N_BLK
        gm = jnp.where(wrap, pltpu.roll(gt, N_BLK - sh, 1), pltpu.roll(gt, LANE - sh, 1))
        cnt = cnt + jnp.where(wrap, jnp.where(gm >= gt, 1.0, 0.0), jnp.where(gm > gt, 1.0, 0.0))
    keep = (past & (cnt < TOPK)) | (n_l == jrow)
    negmask = jnp.where(keep, 0.0, NEG)

    aug_lane = (lane >= HEAD_DIM) & (lane < HEAD_DIM + N_BLK)
    k_onehot = jnp.where(lane == HEAD_DIM + jrow, 1.0, 0.0)
    head_lane = lane < HEAD_DIM
    for hh in range(N_HEADS):
        cols = slice((hh // 2) * LANE, (hh // 2 + 1) * LANE)
        q_h, k_h = zq[:, cols], zk[:, cols]
        if hh % 2:
            q_h, k_h = pltpu.roll(q_h, HEAD_DIM, 1), pltpu.roll(k_h, HEAD_DIM, 1)
        m_h = jnp.where(aug_lane, pltpu.roll(negmask, HEAD_DIM - N_BLK * hh, 1), 0.0)
        q_ref[0, hh, rows, :] = jnp.where(head_lane, q_h, m_h).astype(q_ref.dtype)
        k_ref[0, hh, rows, :] = jnp.where(head_lane, k_h, k_onehot).astype(k_ref.dtype)
    for p in range(ATTN_W // LANE):
        v_ref[0, p, rows, :] = zv[:, p * LANE:(p + 1) * LANE].astype(v_ref.dtype)


def _attn_kernel(q_ref, k_ref, v_ref, tb_ref, o_ref):
    j = pl.program_id(1)
    lane = lax.broadcasted_iota(jnp.int32, (BLK, LANE), 1)

    def one_head(h, p, jj):
        q = q_ref[0, h]
        n_keys = (jj + 1) * BLK
        sc = _dot_nt(q, k_ref[0, h, 0:n_keys, :])
        if jj == 0:
            sc = sc + tb_ref[0, h]
        else:
            near = sc[:, n_keys - 2 * BLK:] + jnp.concatenate([tb_ref[1, h], tb_ref[0, h]], axis=1)
            sc = near if jj == 1 else jnp.concatenate([sc[:, 0:n_keys - 2 * BLK], near], axis=1)
        yield
        m = jnp.max(sc, axis=1, keepdims=True)
        e = jnp.exp2(sc - m)
        l = jnp.sum(e, axis=1, keepdims=True)
        yield
        return _dot(e.astype(BF16), v_ref[0, p, 0:n_keys, :]) / l

    for jj in range(N_BLK):
        @pl.when(j == jj)
        def _(jj=jj):
            def group(gi, carry):
                pairs = [gi * ATTN_PAIRS_PER_TRIP + pi for pi in range(ATTN_PAIRS_PER_TRIP)]
                outs = _interleave([one_head(2 * p + hh, p, jj) for p in pairs for hh in range(2)])
                for pi, p in enumerate(pairs):
                    o_ref[0, p] = jnp.where(lane < HEAD_DIM, outs[2 * pi], outs[2 * pi + 1]).astype(o_ref.dtype)
                return carry
            lax.fori_loop(0, N_HEADS // 2 // ATTN_PAIRS_PER_TRIP, group, 0)


def _memkv_kernel(mem_ref, wk_ref, wv_ref, k_ref, v_ref):
    mb = mem_ref[0].astype(BF16)
    k_ref[0] = _dot(mb, wk_ref[...]).astype(k_ref.dtype)
    v_ref[0] = _dot(mb, wv_ref[...]).astype(v_ref.dtype)


def _merge_kernel(x_ref, ypool_ref, o_ref, kmem_ref, vmem_ref,
                  lng_ref, lnb_ref, wgl_ref, bgate_ref, wpu_ref, wau_ref, wout_ref,
                  ln1g_ref, ln1b_ref, wmq_ref, wmo_ref, ln2g_ref, ln2b_ref,
                  wrh_ref, br_ref,
                  h2_ref, h2b_ref, comb_ref, route_ref, route_t_ref, cnt_ref, *, tm):
    _interleave([_merge_chain(ci, x_ref, ypool_ref, o_ref, kmem_ref, vmem_ref,
                              lng_ref, lnb_ref, wgl_ref, bgate_ref, wpu_ref, wau_ref, wout_ref,
                              ln1g_ref, ln1b_ref, wmq_ref, wmo_ref, ln2g_ref, ln2b_ref,
                              wrh_ref, br_ref, h2_ref, h2b_ref, comb_ref, route_ref, route_t_ref, cnt_ref)
                 for ci in range(tm // TOK_TILE)])


def _merge_chain(ci, x_ref, ypool_ref, o_ref, kmem_ref, vmem_ref,
                 lng_ref, lnb_ref, wgl_ref, bgate_ref, wpu_ref, wau_ref, wout_ref,
                 ln1g_ref, ln1b_ref, wmq_ref, wmo_ref, ln2g_ref, ln2b_ref,
                 wrh_ref, br_ref, h2_ref, h2b_ref, comb_ref, route_ref, route_t_ref, cnt_ref):
    rows = slice(ci * TOK_TILE, (ci + 1) * TOK_TILE)
    n_rows = TOK_TILE
    h = _ln(x_ref[0, rows, :], lng_ref[...], lnb_ref[...])
    hb = h.astype(BF16)
    gl = _dot(hb, wgl_ref[...]) + bgate_ref[...]
    yield
    gates = 0.5 * jnp.tanh(0.5 * gl) + 0.5
    y_pool = _dot(ypool_ref[0, rows, :], wpu_ref[...])
    o_cat = jnp.concatenate([o_ref[0, p, rows, :] for p in range(ATTN_W // LANE)], axis=1)
    y_attn = _dot(o_cat, wau_ref[...])
    yield
    merged = gates[:, 0:D] * y_pool + gates[:, D:2 * D] * y_attn
    mix = _dot(merged.astype(BF16), wout_ref[...])
    yield
    h1 = _ln(DN_ALPHA * h + mix, ln1g_ref[...], ln1b_ref[...])

    qm = _dot(h1.astype(BF16), wmq_ref[...]).astype(BF16)
    yield
    outs = []
    for hd in range(MEM_HEADS):
        cols = slice(hd * MEM_HD, (hd + 1) * MEM_HD)
        sc = _dot_nt(qm[:, cols], kmem_ref[0, :, cols])
        m = jnp.max(sc, axis=1, keepdims=True)
        e = jnp.exp2(sc - m)
        l = jnp.sum(e, axis=1, keepdims=True)
        outs.append(_dot(e.astype(BF16), vmem_ref[0, :, cols]) / l)
    om = jnp.concatenate(outs, axis=1).astype(BF16)
    xa = _dot(om, wmo_ref[...])
    yield
    h2 = _ln(DN_ALPHA * h1 + xa, ln2g_ref[...], ln2b_ref[...])
    h2_ref[0, rows, :] = h2
    h2b_ref[0, rows, :] = h2.astype(BF16)

    x_hi, x_lo = _split(h2)
    r2 = _dot(x_hi, wrh_ref[...])
    r = r2[:, 0:LANE] + r2[:, LANE:2 * LANE] + _dot(x_lo, wrh_ref[:, 0:LANE]) + br_ref[...]
    yield
    lane = lax.broadcasted_iota(jnp.int32, (n_rows, LANE), 1)
    lane_f = lane.astype(F32)
    cmask = (lane >= N_EXPERTS) & (lane < 2 * N_EXPERTS)
    c = jnp.where(cmask, r, -jnp.inf)
    cmax = jnp.max(c, axis=1, keepdims=True)
    ce = jnp.exp(c - cmax)
    csum = jnp.sum(ce, axis=1, keepdims=True) * (1.0 / EPG)
    g_prob = 1.0 / csum
    grp_lane = ((lane & (N_EXPERTS - 1)) >> 3).astype(F32)
    gidx = jnp.min(jnp.where(cmask & (c == cmax), grp_lane, 99.0), axis=1, keepdims=True)
    fmask = (lane < N_EXPERTS) & (grp_lane == gidx)
    f = jnp.where(fmask, r, -jnp.inf)
    fmax = jnp.max(f, axis=1, keepdims=True)
    fe = jnp.exp(f - fmax)
    fsum = jnp.sum(fe, axis=1, keepdims=True)
    prob = fe / fsum
    p1 = jnp.max(prob, axis=1, keepdims=True)
    i1 = jnp.min(jnp.where(fmask & (prob == p1), lane_f, 999.0), axis=1, keepdims=True)
    rest = fmask & (lane_f != i1)
    prob2 = jnp.where(rest, prob, -1.0)
    p2 = jnp.max(prob2, axis=1, keepdims=True)
    i2 = jnp.min(jnp.where(rest & (prob2 == p2), lane_f, 999.0), axis=1, keepdims=True)
    den = p1 + p2
    comb = jnp.where(lane_f == i1, g_prob * (p1 / den),
                     jnp.where(lane_f == i2, g_prob * (p2 / den), 0.0))
    comb_ref[0, rows, :] = comb
    yield

    sel = jnp.where((lane_f == i1) | (lane_f == i2), 1.0, 0.0)
    cnt = jnp.sum(sel, axis=0, keepdims=True)
    pc = jnp.floor((cnt + (SEG_ALIGN - 1)) * (1.0 / SEG_ALIGN)) * SEG_ALIGN
    lane8 = lax.broadcasted_iota(jnp.int32, (8, LANE), 1)
    inc = jnp.broadcast_to(pc, (8, LANE))
    for sh in (1, 2, 4, 8, 16, 32, 64):
        inc = inc + jnp.where(lane8 >= sh, pltpu.roll(inc, sh, 1), 0.0)
    seg_start = inc[0:1] - pc
    t_row = lax.broadcasted_iota(jnp.int32, (n_rows, n_rows), 0)
    t_col = lax.broadcasted_iota(jnp.int32, (n_rows, n_rows), 1)
    earlier = jnp.where(t_row > t_col, 1.0, 0.0).astype(BF16)
    rank = _dot(earlier, sel.astype(BF16))
    pos = seg_start + rank
    d1 = jnp.sum(jnp.where(lane_f == i1, pos, 0.0), axis=1, keepdims=True)
    d2 = jnp.sum(jnp.where(lane_f == i2, pos, 0.0), axis=1, keepdims=True)
    route = jnp.where(lane == 0, d1, jnp.where(lane == 1, d2, 0.0))
    route_ref[0, rows, :] = route
    r_hi, r_lo = _split(route)
    eye = jnp.where(lax.broadcasted_iota(jnp.int32, (8, LANE), 0) == lane8, 1.0, 0.0).astype(BF16)
    route_t_ref[ci] = _dot_nt(eye, r_hi) + _dot_nt(eye, r_lo)
    cnt_ref[ci] = jnp.broadcast_to(pc, (8, LANE))


def _chunk_copies(sub, nch_ref, tab_ref, make_copy, act):
    base = sub * CHUNK_TAB_W
    for size, cnt_i, lo_off, go_off in ((BIG_CHUNK, 0, 0, MAX_BIG), (SEG_ALIGN, 1, 2 * MAX_BIG, 2 * MAX_BIG + N_EXPERTS)):
        def body(k, carry, size=size, lo_off=lo_off, go_off=go_off):
            lo = pl.multiple_of(tab_ref[base + lo_off + k], SEG_ALIGN)
            go = pl.multiple_of(tab_ref[base + go_off + k], SEG_ALIGN)
            act(make_copy(lo, go, size))
            return carry
        lax.fori_loop(0, nch_ref[2 * sub + cnt_i], body, 0)


def _dispatch_kernel(nch_ref, tab_ref, tail_ref, xb_ref, comb_ref, route_t_ref, xs_ref,
                     cbuf, zbuf, sem, zsem, *, n_steps, rows):
    i = pl.program_id(0)
    par = lax.rem(i, 2)
    n_u = MOE_SUBS_PER_STEP
    tm = TOK_TILE

    def copies(step, par_, act):
        for u in range(n_u):
            slot_ = par_ * n_u + u

            def mk(lo, go, size, slot_=slot_):
                return pltpu.make_async_copy(cbuf.at[slot_, pl.ds(lo, size), :],
                                             xs_ref.at[pl.ds(go, size), :], sem.at[slot_])
            _chunk_copies(step * n_u + u, nch_ref, tab_ref, mk, act)

    @pl.when(i >= 2)
    def _():
        copies(i - 2, par, lambda c: c.wait())

    def one_hot_rows(u, r_lo, r_hi):
        r_io = (r_lo + lax.broadcasted_iota(jnp.int32, (r_hi - r_lo, tm), 0)).astype(F32)
        d1 = route_t_ref[u, 0:1, :]
        d2 = route_t_ref[u, 1:2, :]
        return jnp.where((r_io == d1) | (r_io == d2), 1.0, 0.0).astype(BF16)

    def augmented(u):
        rs = slice(u * tm, (u + 1) * tm)
        comb = comb_ref[rs, :]
        c_hi = comb.astype(BF16).astype(F32)
        c_pack = (c_hi + pltpu.roll(comb - c_hi, N_EXPERTS, 1)).astype(BF16)
        return jnp.concatenate([xb_ref[rs, :], c_pack], axis=1)

    def chain(u):
        p_mat = one_hot_rows(u, 0, DISPATCH_MAIN_ROWS)
        x_aug = augmented(u)
        yield
        cbuf[par * n_u + u, 0:DISPATCH_MAIN_ROWS, :] = _dot(p_mat, x_aug).astype(BF16)

    _interleave([chain(u) for u in range(n_u)])

    for u in range(n_u):
        sub = i * n_u + u
        used_rows = nch_ref[2 * sub] * BIG_CHUNK + nch_ref[2 * sub + 1] * SEG_ALIGN

        @pl.when(used_rows > DISPATCH_MAIN_ROWS)
        def _(u=u):
            cbuf[par * n_u + u, DISPATCH_MAIN_ROWS:rows, :] = _dot(
                one_hot_rows(u, DISPATCH_MAIN_ROWS, rows), augmented(u)).astype(BF16)

    copies(i, par, lambda c: c.start())

    @pl.when(i == n_steps - 1)
    def _():
        if n_steps >= 2:
            copies(i - 1, 1 - par, lambda c: c.wait())
        copies(i, par, lambda c: c.wait())
        zbuf[...] = jnp.zeros_like(zbuf)

        def tails(act):
            def body(e, carry):
                st = tail_ref[e]
                n = tail_ref[N_EXPERTS + e]
                off = jnp.int32(0)
                size = zbuf.shape[0]
                while size >= SEG_ALIGN:
                    bit = (n & size) != 0

                    @pl.when(bit)
                    def _(size=size, off=off):
                        act(pltpu.make_async_copy(
                            zbuf.at[pl.ds(0, size), :],
                            xs_ref.at[pl.ds(pl.multiple_of(st + off, SEG_ALIGN), size), :], zsem))
                    off = off + jnp.where(bit, size, 0)
                    size //= 2
                return carry
            lax.fori_loop(0, N_EXPERTS, body, 0)
        tails(lambda c: c.start())
        tails(lambda c: c.wait())


def _ffn_kernel(texp_ref, nused_ref, xs_ref, wg_ref, wu_ref, wd_ref, ys_ref, wg_b, wu_b, wd_b, xbuf, xsem):
    t = pl.program_id(0)
    e = texp_ref[t]
    n_used = nused_ref[0]

    def fetch(u):
        slot = lax.rem(u, XS_DEPTH)
        return pltpu.make_async_copy(xs_ref.at[pl.ds(pl.multiple_of(u * FFN_TILE, FFN_TILE), FFN_TILE), :],
                                     xbuf.at[slot], xsem.at[slot])

    @pl.when(t == 0)
    def _():
        for u in range(XS_DEPTH - 1):
            @pl.when(u < n_used)
            def _(u=u):
                fetch(jnp.int32(u)).start()

    @pl.when(t + (XS_DEPTH - 1) < n_used)
    def _():
        fetch(t + (XS_DEPTH - 1)).start()

    @pl.when((t == 0) | (e != texp_ref[jnp.maximum(t - 1, 0)]))
    def _():
        wg_b[...] = wg_ref[0].astype(BF16)
        wu_b[...] = wu_ref[0].astype(BF16)
        wd_b[...] = wd_ref[0].astype(BF16)

    @pl.when(t < n_used)
    def _():
        fetch(t).wait()
        slot = lax.rem(t, XS_DEPTH)

        def chain(ci):
            rs = slice(ci * FFN_CHAIN, (ci + 1) * FFN_CHAIN)
            xrow = xbuf[slot, rs, 0:D]
            cw = xbuf[slot, rs, D:D + LANE].astype(F32)
            lane = lax.broadcasted_iota(jnp.int32, cw.shape, 1)
            c = jnp.sum(jnp.where((lane == e) | (lane == e + N_EXPERTS), cw, 0.0), axis=1, keepdims=True)
            a = _dot(xrow, wg_b[...])
            b = _dot(xrow, wu_b[...])
            yield
            hid = ((a * jax.nn.sigmoid(a)) * b * c).astype(BF16)
            yield
            ys_ref[rs, :] = _dot(hid, wd_b[...]).astype(ys_ref.dtype)

        _interleave([chain(ci) for ci in range(FFN_TILE // FFN_CHAIN)])


def _combine_kernel(nch_ref, tab_ref, route_ref, h2_ref, g_ref, b_ref, ys_ref, out_ref,
                    ybuf, sem, *, n_steps, rows):
    i = pl.program_id(0)
    par = lax.rem(i, 2)
    n_u = MOE_SUBS_PER_STEP
    tm = TOK_TILE

    def copies(step, par_, act):
        for u in range(n_u):
            slot_ = par_ * n_u + u

            def mk(lo, go, size, slot_=slot_):
                return pltpu.make_async_copy(ys_ref.at[pl.ds(go, size), :],
                                             ybuf.at[slot_, pl.ds(lo, size), :], sem.at[slot_])
            _chunk_copies(step * n_u + u, nch_ref, tab_ref, mk, act)

    @pl.when(i == 0)
    def _():
        ybuf[...] = jnp.zeros_like(ybuf)
        copies(0, 0, lambda c: c.start())

    @pl.when(i + 1 < n_steps)
    def _():
        copies(i + 1, 1 - par, lambda c: c.start())

    copies(i, par, lambda c: c.wait())

    def chain(u):
        rs = slice(u * tm, (u + 1) * tm)
        r_io = lax.broadcasted_iota(jnp.int32, (tm, rows), 1).astype(F32)
        d1 = route_ref[rs, 0:1]
        d2 = route_ref[rs, 1:2]
        p_t = jnp.where((r_io == d1) | (r_io == d2), 1.0, 0.0).astype(BF16)
        yield
        ff = _dot(p_t, ybuf[par * n_u + u])
        yield
        out_ref[rs, :] = _ln(DN_ALPHA * h2_ref[rs, :] + ff, g_ref[...], b_ref[...])

    _interleave([chain(u) for u in range(n_u)])


def _bias_kernel(tbl_ref, bkt_ref, out_ref):
    h = pl.program_id(0)
    far = tbl_ref[h, REL_BUCKETS - 1]
    for which in range(2):
        bk = bkt_ref[which]
        acc = jnp.where(bk < 0, NEG, 0.0)
        for kk in range(REL_BUCKETS):
            acc = jnp.where(bk == kk, (tbl_ref[h, kk] - far) * LOG2E, acc)
        out_ref[which, 0] = acc


def _rel_bucket_table(dist):
    max_exact = REL_BUCKETS // 2
    d = jnp.maximum(dist, 0)
    large = max_exact + (jnp.log(jnp.maximum(d, 1).astype(F32) / max_exact)
                         / math.log(REL_MAX_DIST / max_exact) * (REL_BUCKETS - max_exact)).astype(jnp.int32)
    large = jnp.minimum(large, REL_BUCKETS - 1)
    return jnp.where(d < max_exact, d, large)


def _const_spec(shape):
    nd = len(shape)
    return pl.BlockSpec(shape, lambda *_: (0,) * nd)


def kernel(x, mem, ln_in_g, ln_in_b, rel_bias, w_in, b_gate, w_pool_grp, pool_scale, w_pool_up, w_attn_up,
           w_mix_out, ln1_g, ln1_b, w_mq, w_mk, w_mv, w_mo, ln2_g, ln2_b, w_coarse, b_coarse, w_fine, b_fine,
           w_gate, w_up, w_down, ln3_g, ln3_b):
    B, S, _ = x.shape
    assert S == N_BLK * BLK and w_in.shape[0] == 1
    M = mem.shape[1]
    T = B * S
    tm = TOK_TILE

    wi = w_in[0]
    w_u = wi[:, 0:POOL_W]
    w_q = wi[:, POOL_W:POOL_W + ATTN_W] * (HEAD_DIM ** -0.5 * LOG2E)
    w_k = wi[:, POOL_W + ATTN_W:POOL_W + 2 * ATTN_W]
    w_v = wi[:, POOL_W + 2 * ATTN_W:POOL_W + 3 * ATTN_W]
    w_gl = wi[:, POOL_W + 3 * ATTN_W:]

    w1 = jnp.concatenate([w_u, w_q, w_k, w_v], axis=1).astype(BF16)
    row2 = lambda a: a.reshape(1, -1)

    iq = jnp.arange(BLK, dtype=jnp.int32)[:, None]
    ik = jnp.arange(BLK, dtype=jnp.int32)[None, :]
    d_own = iq - ik
    bkt = jnp.stack([jnp.where(d_own >= 0, _rel_bucket_table(d_own), -1), _rel_bucket_table(d_own + BLK)])
    t_bias = pl.pallas_call(
        _bias_kernel,
        grid=(N_HEADS,),
        in_specs=[pl.BlockSpec(memory_space=pltpu.SMEM), _const_spec((2, BLK, BLK))],
        out_specs=pl.BlockSpec((2, 1, BLK, BLK), lambda h: (0, h, 0, 0)),
        out_shape=jax.ShapeDtypeStruct((2, N_HEADS, BLK, BLK), F32),
        name="relbias_tiles",
    )(rel_bias.T, bkt)

    n_w1 = w1.shape[1]
    tmp = MERGE_TILE
    ypool, q_aug, k_aug, v_p = pl.pallas_call(
        functools.partial(_proj_kernel, tm=tmp),
        grid=(B, S // tmp),
        in_specs=[
            pl.BlockSpec((1, tmp, D), lambda b, s: (b, s, 0)),
            _const_spec((1, D)), _const_spec((1, D)),
            _const_spec((D, n_w1)),
            _const_spec((len(POOL_WINDOWS), LANE, LANE)),
            _const_spec((1, POOL_W)),
        ],
        out_specs=[
            pl.BlockSpec((1, tmp, POOL_W), lambda b, s: (b, s, 0)),
            pl.BlockSpec((1, N_HEADS, tmp, LANE), lambda b, s: (b, 0, s, 0)),
            pl.BlockSpec((1, N_HEADS, tmp, LANE), lambda b, s: (b, 0, s, 0)),
            pl.BlockSpec((1, ATTN_W // LANE, tmp, LANE), lambda b, s: (b, 0, s, 0)),
        ],
        out_shape=[
            jax.ShapeDtypeStruct((B, S, POOL_W), BF16),
            jax.ShapeDtypeStruct((B, N_HEADS, S, LANE), BF16),
            jax.ShapeDtypeStruct((B, N_HEADS, S, LANE), BF16),
            jax.ShapeDtypeStruct((B, ATTN_W // LANE, S, LANE), BF16),
        ],
        scratch_shapes=[pltpu.VMEM((HALO + tmp, POOL_W), F32), pltpu.VMEM((LANE, ATTN_W), F32)],
        compiler_params=pltpu.CompilerParams(dimension_semantics=("arbitrary", "arbitrary"),
                                             vmem_limit_bytes=VMEM_LIMIT),
        name="proj_pool_gate",
    )(x, row2(ln_in_g), row2(ln_in_b), w1, w_pool_grp[0].astype(BF16), row2(pool_scale[0]))

    o_attn = pl.pallas_call(
        _attn_kernel,
        grid=(B, N_BLK),
        in_specs=[
            pl.BlockSpec((1, N_HEADS, BLK, LANE), lambda b, j: (b, 0, j, 0)),
            pl.BlockSpec((1, N_HEADS, S, LANE), lambda b, j: (b, 0, 0, 0)),
            pl.BlockSpec((1, ATTN_W // LANE, S, LANE), lambda b, j: (b, 0, 0, 0)),
            _const_spec((2, N_HEADS, BLK, BLK)),
        ],
        out_specs=pl.BlockSpec((1, ATTN_W // LANE, BLK, LANE), lambda b, j: (b, 0, j, 0)),
        out_shape=jax.ShapeDtypeStruct((B, ATTN_W // LANE, S, LANE), BF16),
        compiler_params=pltpu.CompilerParams(dimension_semantics=("arbitrary", "arbitrary"),
                                             vmem_limit_bytes=VMEM_LIMIT),
        name="moba_attn",
    )(q_aug, k_aug, v_p, t_bias)

    kmem, vmem = pl.pallas_call(
        _memkv_kernel,
        grid=(B,),
        in_specs=[pl.BlockSpec((1, M, D), lambda b: (b, 0, 0)),
                  _const_spec((D, MEM_W)), _const_spec((D, MEM_W))],
        out_specs=[pl.BlockSpec((1, M, MEM_W), lambda b: (b, 0, 0)),
                   pl.BlockSpec((1, M, MEM_W), lambda b: (b, 0, 0))],
        out_shape=[jax.ShapeDtypeStruct((B, M, MEM_W), BF16)] * 2,
        compiler_params=pltpu.CompilerParams(dimension_semantics=("arbitrary",)),
        name="mem_kv",
    )(mem, w_mk[0].astype(BF16), w_mv[0].astype(BF16))

    w_r = jnp.concatenate([
        w_fine[0].reshape(D, N_EXPERTS),
        jnp.repeat(w_coarse[0], EPG, axis=1),
        jnp.zeros((D, LANE - 2 * N_EXPERTS), F32)], axis=1)
    b_r = jnp.concatenate([
        b_fine[0].reshape(N_EXPERTS), jnp.repeat(b_coarse[0], EPG),
        jnp.zeros((LANE - 2 * N_EXPERTS,), F32)]).reshape(1, LANE)
    w_r_hi = w_r.astype(BF16)
    w_r_lo = (w_r - w_r_hi.astype(F32)).astype(BF16)
    w_r_cat = jnp.concatenate([w_r_hi, w_r_lo], axis=1)

    n_sub = T // tm
    tmm = MERGE_TILE
    per = tmm // tm
    sub_idx = lambda b, s: b * (S // tmm) + s
    h2, h2b, comb, route, route_t, seg_cnt = pl.pallas_call(
        functools.partial(_merge_kernel, tm=tmm),
        grid=(B, S // tmm),
        in_specs=[
            pl.BlockSpec((1, tmm, D), lambda b, s: (b, s, 0)),
            pl.BlockSpec((1, tmm, POOL_W), lambda b, s: (b, s, 0)),
            pl.BlockSpec((1, ATTN_W // LANE, tmm, LANE), lambda b, s: (b, 0, s, 0)),
            pl.BlockSpec((1, M, MEM_W), lambda b, s: (b, 0, 0)),
            pl.BlockSpec((1, M, MEM_W), lambda b, s: (b, 0, 0)),
            _const_spec((1, D)), _const_spec((1, D)),
            _const_spec((D, 2 * D)), _const_spec((1, 2 * D)),
            _const_spec((POOL_W, D)), _const_spec((ATTN_W, D)), _const_spec((D, D)),
            _const_spec((1, D)), _const_spec((1, D)),
            _const_spec((D, MEM_W)), _const_spec((MEM_W, D)),
            _const_spec((1, D)), _const_spec((1, D)),
            _const_spec((D, 2 * LANE)), _const_spec((1, LANE)),
        ],
        out_specs=[
            pl.BlockSpec((1, tmm, D), lambda b, s: (b, s, 0)),
            pl.BlockSpec((1, tmm, D), lambda b, s: (b, s, 0)),
            pl.BlockSpec((1, tmm, LANE), lambda b, s: (b, s, 0)),
            pl.BlockSpec((1, tmm, LANE), lambda b, s: (b, s, 0)),
            pl.BlockSpec((per, 8, tm), lambda b, s: (sub_idx(b, s), 0, 0)),
            pl.BlockSpec((per, 8, LANE), lambda b, s: (sub_idx(b, s), 0, 0)),
        ],
        out_shape=[
            jax.ShapeDtypeStruct((B, S, D), F32),
            jax.ShapeDtypeStruct((B, S, D), BF16),
            jax.ShapeDtypeStruct((B, S, LANE), F32),
            jax.ShapeDtypeStruct((B, S, LANE), F32),
            jax.ShapeDtypeStruct((n_sub, 8, tm), F32),
            jax.ShapeDtypeStruct((n_sub, 8, LANE), F32),
        ],
        compiler_params=pltpu.CompilerParams(dimension_semantics=("arbitrary", "arbitrary"),
                                             vmem_limit_bytes=VMEM_LIMIT),
        name="merge_memattn_router",
    )(x, ypool, o_attn, kmem, vmem,
      row2(ln_in_g), row2(ln_in_b), w_gl.astype(BF16), row2(b_gate[0]),
      w_pool_up[0].astype(BF16), w_attn_up[0].astype(BF16), w_mix_out[0].astype(BF16),
      row2(ln1_g[0]), row2(ln1_b[0]),
      (w_mq[0] * (MEM_HD ** -0.5 * LOG2E)).astype(BF16), w_mo[0].astype(BF16),
      row2(ln2_g[0]), row2(ln2_b[0]),
      w_r_cat, b_r)

    pcs = seg_cnt[:, 0, :N_EXPERTS].astype(jnp.int32)
    tot = jnp.sum(pcs, axis=0)
    cap = ((tot + FFN_TILE - 1) // FFN_TILE) * FFN_TILE
    ends = jnp.cumsum(cap)
    base = ends - cap
    gs = base[None, :] + jnp.cumsum(pcs, axis=0) - pcs
    ls = jnp.cumsum(pcs, axis=1) - pcs
    n_sorted = n_sub * COMPACT_ROWS + N_EXPERTS * FFN_TILE
    n_ffn_tiles = n_sorted // FFN_TILE
    n_used = (ends[-1] // FFN_TILE).astype(jnp.int32)
    tile_row = jnp.arange(n_ffn_tiles, dtype=jnp.int32) * FFN_TILE
    tile_exp = jnp.sum(jnp.minimum(tile_row, ends[-1] - 1)[:, None] >= ends[None, :], axis=1).astype(jnp.int32)
    tails = jnp.concatenate([base + tot, cap - tot]).astype(jnp.int32)
    def kth_of(counts, n_slots, seg_off):
        end = jnp.cumsum(counts, axis=1)
        k = jnp.arange(n_slots, dtype=jnp.int32)
        exp_k = jnp.sum(k[None, :, None] >= end[:, None, :], axis=2)
        hot = exp_k[:, :, None] == jnp.arange(N_EXPERTS)[None, None, :]
        pick = lambda tbl: jnp.sum(jnp.where(hot, tbl[:, None, :], 0), axis=2)
        within = k[None, :] - pick(end - counts)
        return pick(ls + seg_off), pick(gs + seg_off), within

    n_big = pcs // BIG_CHUNK
    n_end = (pcs // SEG_ALIGN) % 2
    lo_b, go_b, m_b = kth_of(n_big, MAX_BIG, 0)
    lo_e, go_e, _ = kth_of(n_end, N_EXPERTS, pcs - SEG_ALIGN)
    chunk_tab = jnp.concatenate([lo_b + BIG_CHUNK * m_b, go_b + BIG_CHUNK * m_b, lo_e, go_e],
                                axis=1).astype(jnp.int32).reshape(-1)
    n_chunks = jnp.stack([jnp.sum(n_big, axis=1), jnp.sum(n_end, axis=1)], axis=1).astype(jnp.int32).reshape(-1)

    aug_w = D + LANE
    n_u = MOE_SUBS_PER_STEP
    n_steps = n_sub // n_u
    x_sorted = pl.pallas_call(
        functools.partial(_dispatch_kernel, n_steps=n_steps, rows=COMPACT_ROWS),
        grid_spec=pltpu.PrefetchScalarGridSpec(
            num_scalar_prefetch=3,
            grid=(n_steps,),
            in_specs=[
                pl.BlockSpec((n_u * tm, D), lambda i, *_: (i, 0)),
                pl.BlockSpec((n_u * tm, LANE), lambda i, *_: (i, 0)),
                pl.BlockSpec((n_u, 8, tm), lambda i, *_: (i, 0, 0)),
            ],
            out_specs=pl.BlockSpec(memory_space=pl.ANY),
            scratch_shapes=[
                pltpu.VMEM((2 * n_u, COMPACT_ROWS, aug_w), BF16),
                pltpu.VMEM((FFN_TILE // 2, aug_w), BF16),
                pltpu.SemaphoreType.DMA((2 * n_u,)),
                pltpu.SemaphoreType.DMA(()),
            ],
        ),
        out_shape=jax.ShapeDtypeStruct((n_sorted, aug_w), BF16),
        compiler_params=pltpu.CompilerParams(dimension_semantics=("arbitrary",), vmem_limit_bytes=VMEM_LIMIT),
        name="moe_dispatch",
    )(n_chunks, chunk_tab, tails, h2b.reshape(T, D), comb.reshape(T, LANE), route_t)

    used_tile = lambda t, texp, nused: (jnp.minimum(t, nused[0] - 1), 0)
    y_sorted = pl.pallas_call(
        _ffn_kernel,
        grid_spec=pltpu.PrefetchScalarGridSpec(
            num_scalar_prefetch=2,
            grid=(n_ffn_tiles,),
            in_specs=[
                pl.BlockSpec(memory_space=pl.ANY),
                pl.BlockSpec((1, D, FF), lambda t, texp, nused: (texp[t], 0, 0)),
                pl.BlockSpec((1, D, FF), lambda t, texp, nused: (texp[t], 0, 0)),
                pl.BlockSpec((1, FF, D), lambda t, texp, nused: (texp[t], 0, 0)),
            ],
            out_specs=pl.BlockSpec((FFN_TILE, D), used_tile),
            scratch_shapes=[pltpu.VMEM((D, FF), BF16), pltpu.VMEM((D, FF), BF16), pltpu.VMEM((FF, D), BF16),
                            pltpu.VMEM((XS_DEPTH, FFN_TILE, aug_w), BF16), pltpu.SemaphoreType.DMA((XS_DEPTH,))],
        ),
        out_shape=jax.ShapeDtypeStruct((n_sorted, D), BF16),
        compiler_params=pltpu.CompilerParams(dimension_semantics=("arbitrary",), vmem_limit_bytes=VMEM_LIMIT),
        name="moe_expert_ffn",
    )(tile_exp, n_used.reshape(1), x_sorted, w_gate[0], w_up[0], w_down[0])

    out = pl.pallas_call(
        functools.partial(_combine_kernel, n_steps=n_steps, rows=COMPACT_ROWS),
        grid_spec=pltpu.PrefetchScalarGridSpec(
            num_scalar_prefetch=2,
            grid=(n_steps,),
            in_specs=[
                pl.BlockSpec((n_u * tm, LANE), lambda i, *_: (i, 0)),
                pl.BlockSpec((n_u * tm, D), lambda i, *_: (i, 0)),
                pl.BlockSpec((1, D), lambda i, *_: (0, 0)),
                pl.BlockSpec((1, D), lambda i, *_: (0, 0)),
                pl.BlockSpec(memory_space=pl.ANY),
            ],
            out_specs=pl.BlockSpec((n_u * tm, D), lambda i, *_: (i, 0)),
            scratch_shapes=[
                pltpu.VMEM((2 * n_u, COMPACT_ROWS, D), BF16),
                pltpu.SemaphoreType.DMA((2 * n_u,)),
            ],
        ),
        out_shape=jax.ShapeDtypeStruct((T, D), F32),
        compiler_params=pltpu.CompilerParams(dimension_semantics=("arbitrary",), vmem_limit_bytes=VMEM_LIMIT),
        name="moe_combine_ln3",
    )(n_chunks, chunk_tab, route.reshape(T, LANE), h2.reshape(T, D), row2(ln3_g[0]), row2(ln3_b[0]), y_sorted)
    return out.reshape(B, S, D)
```

```python
import functools
import math

import jax
import jax.numpy as jnp
from jax import lax
from jax.experimental import pallas as pl
from jax.experimental.pallas import tpu as pltpu

D = 1024
POOL_WINDOWS = (2, 4, 8, 16)
POOL_W = 512
N_HEADS = 8
HEAD_DIM = 64
ATTN_W = 512
BLK = 256
N_BLK = 8
TOPK = 3
REL_BUCKETS = 32
REL_MAX_DIST = 128
MEM_HEADS = 4
MEM_HD = 128
MEM_W = 512
EPG = 8
N_EXPERTS = 32
FF = 256
DN_ALPHA = 2.0 ** 0.25
LN_EPS = 1e-5

LANE = 128
TOK_TILE = 512
SEG_ALIGN = 16
COMPACT_ROWS = 2 * TOK_TILE + N_EXPERTS * SEG_ALIGN
BIG_CHUNK = 2 * SEG_ALIGN
MAX_BIG = COMPACT_ROWS // BIG_CHUNK
CHUNK_TAB_W = 2 * MAX_BIG + 2 * N_EXPERTS
DISPATCH_MAIN_ROWS = 2 * TOK_TILE + 18 * SEG_ALIGN
FFN_TILE = 512
PROJ_CHAIN = 256
ATTN_PAIRS_PER_TRIP = 2
MOE_SUBS_PER_STEP = 2
FFN_CHAIN = 512
XS_DEPTH = 3
MERGE_TILE = 1024
HALO = 16
NEG = -1e30
LOG2E = math.log2(math.e)
VMEM_LIMIT = 56 * 1024 * 1024

F32 = jnp.float32
BF16 = jnp.bfloat16

_NT = (((1,), (1,)), ((), ()))


def _dot(a, b):
    return jnp.dot(a, b, preferred_element_type=F32)


def _dot_nt(a, b):
    return lax.dot_general(a, b, _NT, preferred_element_type=F32)


def _split(a):
    hi = a.astype(BF16)
    lo = (a - hi.astype(F32)).astype(BF16)
    return hi, lo


def _interleave(chains):
    results = [None] * len(chains)
    live = list(range(len(chains)))
    while live:
        for ci in list(live):
            try:
                next(chains[ci])
            except StopIteration as done:
                results[ci] = done.value
                live.remove(ci)
    return results


def _ln(x, g, b):
    mu = jnp.mean(x, axis=-1, keepdims=True)
    xc = x - mu
    var = jnp.mean(xc * xc, axis=-1, keepdims=True)
    return xc * lax.rsqrt(var + LN_EPS) * g + b


def _proj_kernel(x_ref, g_ref, b_ref, w_ref, wgrp_ref, pscale_ref,
                 ypool_ref, q_ref, k_ref, v_ref, ubuf, kbt, *, tm):
    s = pl.program_id(1)

    @pl.when(s == 0)
    def _():
        ubuf[0:HALO, :] = jnp.zeros((HALO, POOL_W), F32)
        kbt[...] = jnp.zeros_like(kbt)

    _interleave([_proj_chain(ci, s, tm, x_ref, g_ref, b_ref, w_ref, wgrp_ref, pscale_ref,
                             ypool_ref, q_ref, k_ref, v_ref, ubuf, kbt)
                 for ci in range(tm // PROJ_CHAIN)])
    ubuf[0:HALO, :] = ubuf[tm:tm + HALO, :]


def _proj_chain(ci, s, tm, x_ref, g_ref, b_ref, w_ref, wgrp_ref, pscale_ref,
                ypool_ref, q_ref, k_ref, v_ref, ubuf, kbt):
    n = PROJ_CHAIN
    r0 = ci * n
    rows = slice(r0, r0 + n)
    blk0 = s * (tm // BLK) + ci * (n // BLK)

    h = _ln(x_ref[0, rows, :], g_ref[...], b_ref[...])
    hb = h.astype(BF16)
    zu = _dot(hb, w_ref[:, 0:POOL_W])
    zq = _dot(hb, w_ref[:, POOL_W:POOL_W + ATTN_W])
    zk = _dot(hb, w_ref[:, POOL_W + ATTN_W:POOL_W + 2 * ATTN_W])
    zv = _dot(hb, w_ref[:, POOL_W + 2 * ATTN_W:])
    ubuf[HALO + r0:HALO + r0 + n, :] = zu

    r_io = lax.broadcasted_iota(jnp.int32, kbt.shape, 0)
    c_io = lax.broadcasted_iota(jnp.int32, kbt.shape, 1)
    head_match = (r_io >> 3) == (c_io >> 6)
    for bi in range(n // BLK):
        kmean = jnp.mean(zk[bi * BLK:(bi + 1) * BLK], axis=0, keepdims=True)
        kbt[...] = jnp.where(head_match & ((r_io & 7) == blk0 + bi), kmean, kbt[...])
    yield

    t_pos = s * tm + r0 + lax.broadcasted_iota(jnp.int32, (n, LANE), 0)
    for g, w in enumerate(POOL_WINDOWS):
        cols = slice(g * LANE, (g + 1) * LANE)
        ws = ubuf[HALO + r0:HALO + r0 + n, cols]
        for kk in range(1, w):
            ws = ws + ubuf[HALO + r0 - kk:HALO + r0 - kk + n, cols]
        cnt = jnp.minimum(t_pos + 1, w).astype(F32)
        y = ws / cnt - ubuf[HALO + r0:HALO + r0 + n, cols]
        yg = _dot(y.astype(BF16), wgrp_ref[g]) * pscale_ref[:, cols]
        ypool_ref[0, rows, cols] = yg.astype(ypool_ref.dtype)
    yield

    q_hi, q_lo = _split(zq)
    kb_hi, kb_lo = _split(kbt[...])
    g2 = _dot_nt(q_hi, jnp.concatenate([kb_hi, kb_lo], axis=0))
    gate = g2[:, 0:LANE] + g2[:, LANE:2 * LANE] + _dot_nt(q_lo, kb_hi)
    yield

    lane = lax.broadcasted_iota(jnp.int32, (n, LANE), 1)
    row = lax.broadcasted_iota(jnp.int32, (n, LANE), 0)
    n_l = lane & 7
    jrow = blk0 + (row >> 8)
    past = n_l < jrow
    gt = jnp.where(past, gate, -jnp.inf)
    cnt = jnp.zeros((n, LANE), F32)
    for sh in range(1, N_BLK):
        wrap = (n_l + sh) >= N_BLK
        gm = jnp.where(wrap, pltpu.roll(gt, N_BLK - sh, 1), pltpu.roll(gt, LANE - sh, 1))
        cnt = cnt + jnp.where(wrap, jnp.where(gm >= gt, 1.0, 0.0), jnp.where(gm > gt, 1.0, 0.0))
    keep = (past & (cnt < TOPK)) | (n_l == jrow)
    negmask = jnp.where(keep, 0.0, NEG)

    aug_lane = (lane >= HEAD_DIM) & (lane < HEAD_DIM + N_BLK)
    k_onehot = jnp.where(lane == HEAD_DIM + jrow, 1.0, 0.0)
    head_lane = lane < HEAD_DIM
    for hh in range(N_HEADS):
        cols = slice((hh // 2) * LANE, (hh // 2 + 1) * LANE)
        q_h, k_h = zq[:, cols], zk[:, cols]
        if hh % 2:
            q_h, k_h = pltpu.roll(q_h, HEAD_DIM, 1), pltpu.roll(k_h, HEAD_DIM, 1)
        m_h = jnp.where(aug_lane, pltpu.roll(negmask, HEAD_DIM - N_BLK * hh, 1), 0.0)
        q_ref[0, hh, rows, :] = jnp.where(head_lane, q_h, m_h).astype(q_ref.dtype)
        k_ref[0, hh, rows, :] = jnp.where(head_lane, k_h, k_onehot).astype(k_ref.dtype)
    for p in range(ATTN_W // LANE):
        v_ref[0, p, rows, :] = zv[:, p * LANE:(p + 1) * LANE].astype(v_ref.dtype)


def _attn_kernel(q_ref, k_ref, v_ref, tb_ref, o_ref):
    j = pl.program_id(1)
    lane = lax.broadcasted_iota(jnp.int32, (BLK, LANE), 1)

    def one_head(h, p, jj):
        q = q_ref[0, h]
        n_keys = (jj + 1) * BLK
        sc = _dot_nt(q, k_ref[0, h, 0:n_keys, :])
        if jj == 0:
            sc = sc + tb_ref[0, h]
        else:
            near = sc[:, n_keys - 2 * BLK:] + jnp.concatenate([tb_ref[1, h], tb_ref[0, h]], axis=1)
            sc = near if jj == 1 else jnp.concatenate([sc[:, 0:n_keys - 2 * BLK], near], axis=1)
        yield
        m = jnp.max(sc, axis=1, keepdims=True)
        e = jnp.exp2(sc - m)
        l = jnp.sum(e, axis=1, keepdims=True)
        yield
        return _dot(e.astype(BF16), v_ref[0, p, 0:n_keys, :]) / l

    for jj in range(N_BLK):
        @pl.when(j == jj)
        def _(jj=jj):
            def group(gi, carry):
                pairs = [gi * ATTN_PAIRS_PER_TRIP + pi for pi in range(ATTN_PAIRS_PER_TRIP)]
                outs = _interleave([one_head(2 * p + hh, p, jj) for p in pairs for hh in range(2)])
                for pi, p in enumerate(pairs):
                    o_ref[0, p] = jnp.where(lane < HEAD_DIM, outs[2 * pi], outs[2 * pi + 1]).astype(o_ref.dtype)
                return carry
            lax.fori_loop(0, N_HEADS // 2 // ATTN_PAIRS_PER_TRIP, group, 0)


def _memkv_kernel(mem_ref, wk_ref, wv_ref, k_ref, v_ref):
    mb = mem_ref[0].astype(BF16)
    k_ref[0] = _dot(mb, wk_ref[...]).astype(k_ref.dtype)
    v_ref[0] = _dot(mb, wv_ref[...]).astype(v_ref.dtype)


def _merge_kernel(x_ref, ypool_ref, o_ref, kmem_ref, vmem_ref,
                  lng_ref, lnb_ref, wgl_ref, bgate_ref, wpu_ref, wau_ref, wout_ref,
                  ln1g_ref, ln1b_ref, wmq_ref, wmo_ref, ln2g_ref, ln2b_ref,
                  wrh_ref, br_ref,
                  h2_ref, h2b_ref, comb_ref, route_ref, route_t_ref, cnt_ref, *, tm):
    _interleave([_merge_chain(ci, x_ref, ypool_ref, o_ref, kmem_ref, vmem_ref,
                              lng_ref, lnb_ref, wgl_ref, bgate_ref, wpu_ref, wau_ref, wout_ref,
                              ln1g_ref, ln1b_ref, wmq_ref, wmo_ref, ln2g_ref, ln2b_ref,
                              wrh_ref, br_ref, h2_ref, h2b_ref, comb_ref, route_ref, route_t_ref, cnt_ref)
                 for ci in range(tm // TOK_TILE)])


def _merge_chain(ci, x_ref, ypool_ref, o_ref, kmem_ref, vmem_ref,
                 lng_ref, lnb_ref, wgl_ref, bgate_ref, wpu_ref, wau_ref, wout_ref,
                 ln1g_ref, ln1b_ref, wmq_ref, wmo_ref, ln2g_ref, ln2b_ref,
                 wrh_ref, br_ref, h2_ref, h2b_ref, comb_ref, route_ref, route_t_ref, cnt_ref):
    rows = slice(ci * TOK_TILE, (ci + 1) * TOK_TILE)
    n_rows = TOK_TILE
    h = _ln(x_ref[0, rows, :], lng_ref[...], lnb_ref[...])
    hb = h.astype(BF16)
    gl = _dot(hb, wgl_ref[...]) + bgate_ref[...]
    yield
    gates = 0.5 * jnp.tanh(0.5 * gl) + 0.5
    y_pool = _dot(ypool_ref[0, rows, :], wpu_ref[...])
    o_cat = jnp.concatenate([o_ref[0, p, rows, :] for p in range(ATTN_W // LANE)], axis=1)
    y_attn = _dot(o_cat, wau_ref[...])
    yield
    merged = gates[:, 0:D] * y_pool + gates[:, D:2 * D] * y_attn
    mix = _dot(merged.astype(BF16), wout_ref[...])
    yield
    h1 = _ln(DN_ALPHA * h + mix, ln1g_ref[...], ln1b_ref[...])

    qm = _dot(h1.astype(BF16), wmq_ref[...]).astype(BF16)
    yield
    outs = []
    for hd in range(MEM_HEADS):
        cols = slice(hd * MEM_HD, (hd + 1) * MEM_HD)
        sc = _dot_nt(qm[:, cols], kmem_ref[0, :, cols])
        m = jnp.max(sc, axis=1, keepdims=True)
        e = jnp.exp2(sc - m)
        l = jnp.sum(e, axis=1, keepdims=True)
        outs.append(_dot(e.astype(BF16), vmem_ref[0, :, cols]) / l)
    om = jnp.concatenate(outs, axis=1).astype(BF16)
    xa = _dot(om, wmo_ref[...])
    yield
    h2 = _ln(DN_ALPHA * h1 + xa, ln2g_ref[...], ln2b_ref[...])
    h2_ref[0, rows, :] = h2
    h2b_ref[0, rows, :] = h2.astype(BF16)

    x_hi, x_lo = _split(h2)
    r2 = _dot(x_hi, wrh_ref[...])
    r = r2[:, 0:LANE] + r2[:, LANE:2 * LANE] + _dot(x_lo, wrh_ref[:, 0:LANE]) + br_ref[...]
    yield
    lane = lax.broadcasted_iota(jnp.int32, (n_rows, LANE), 1)
    lane_f = lane.astype(F32)
    cmask = (lane >= N_EXPERTS) & (lane < 2 * N_EXPERTS)
    c = jnp.where(cmask, r, -jnp.inf)
    cmax = jnp.max(c, axis=1, keepdims=True)
    ce = jnp.exp(c - cmax)
    csum = jnp.sum(ce, axis=1, keepdims=True) * (1.0 / EPG)
    g_prob = 1.0 / csum
    grp_lane = ((lane & (N_EXPERTS - 1)) >> 3).astype(F32)
    gidx = jnp.min(jnp.where(cmask & (c == cmax), grp_lane, 99.0), axis=1, keepdims=True)
    fmask = (lane < N_EXPERTS) & (grp_lane == gidx)
    f = jnp.where(fmask, r, -jnp.inf)
    fmax = jnp.max(f, axis=1, keepdims=True)
    fe = jnp.exp(f - fmax)
    fsum = jnp.sum(fe, axis=1, keepdims=True)
    prob = fe / fsum
    p1 = jnp.max(prob, axis=1, keepdims=True)
    i1 = jnp.min(jnp.where(fmask & (prob == p1), lane_f, 999.0), axis=1, keepdims=True)
    rest = fmask & (lane_f != i1)
    prob2 = jnp.where(rest, prob, -1.0)
    p2 = jnp.max(prob2, axis=1, keepdims=True)
    i2 = jnp.min(jnp.where(rest & (prob2 == p2), lane_f, 999.0), axis=1, keepdims=True)
    den = p1 + p2
    comb = jnp.where(lane_f == i1, g_prob * (p1 / den),
                     jnp.where(lane_f == i2, g_prob * (p2 / den), 0.0))
    comb_ref[0, rows, :] = comb
    yield

    sel = jnp.where((lane_f == i1) | (lane_f == i2), 1.0, 0.0)
    cnt = jnp.sum(sel, axis=0, keepdims=True)
    pc = jnp.floor((cnt + (SEG_ALIGN - 1)) * (1.0 / SEG_ALIGN)) * SEG_ALIGN
    lane8 = lax.broadcasted_iota(jnp.int32, (8, LANE), 1)
    inc = jnp.broadcast_to(pc, (8, LANE))
    for sh in (1, 2, 4, 8, 16, 32, 64):
        inc = inc + jnp.where(lane8 >= sh, pltpu.roll(inc, sh, 1), 0.0)
    seg_start = inc[0:1] - pc
    t_row = lax.broadcasted_iota(jnp.int32, (n_rows, n_rows), 0)
    t_col = lax.broadcasted_iota(jnp.int32, (n_rows, n_rows), 1)
    earlier = jnp.where(t_row > t_col, 1.0, 0.0).astype(BF16)
    rank = _dot(earlier, sel.astype(BF16))
    pos = seg_start + rank
    d1 = jnp.sum(jnp.where(lane_f == i1, pos, 0.0), axis=1, keepdims=True)
    d2 = jnp.sum(jnp.where(lane_f == i2, pos, 0.0), axis=1, keepdims=True)
    route = jnp.where(lane == 0, d1, jnp.where(lane == 1, d2, 0.0))
    route_ref[0, rows, :] = route
    r_hi, r_lo = _split(route)
    eye = jnp.where(lax.broadcasted_iota(jnp.int32, (8, LANE), 0) == lane8, 1.0, 0.0).astype(BF16)
    route_t_ref[ci] = _dot_nt(eye, r_hi) + _dot_nt(eye, r_lo)
    cnt_ref[ci] = jnp.broadcast_to(pc, (8, LANE))


def _chunk_copies(sub, nch_ref, tab_ref, make_copy, act):
    base = sub * CHUNK_TAB_W
    for size, cnt_i, lo_off, go_off in ((BIG_CHUNK, 0, 0, MAX_BIG), (SEG_ALIGN, 1, 2 * MAX_BIG, 2 * MAX_BIG + N_EXPERTS)):
        def body(k, carry, size=size, lo_off=lo_off, go_off=go_off):
            lo = pl.multiple_of(tab_ref[base + lo_off + k], SEG_ALIGN)
            go = pl.multiple_of(tab_ref[base + go_off + k], SEG_ALIGN)
            act(make_copy(lo, go, size))
            return carry
        lax.fori_loop(0, nch_ref[2 * sub + cnt_i], body, 0)


def _dispatch_kernel(nch_ref, tab_ref, tail_ref, xb_ref, comb_ref, route_t_ref, xs_ref,
                     cbuf, zbuf, sem, zsem, *, n_steps, rows):
    i = pl.program_id(0)
    par = lax.rem(i, 2)
    n_u = MOE_SUBS_PER_STEP
    tm = TOK_TILE

    def copies(step, par_, act):
        for u in range(n_u):
            slot_ = par_ * n_u + u

            def mk(lo, go, size, slot_=slot_):
                return pltpu.make_async_copy(cbuf.at[slot_, pl.ds(lo, size), :],
                                             xs_ref.at[pl.ds(go, size), :], sem.at[slot_])
            _chunk_copies(step * n_u + u, nch_ref, tab_ref, mk, functools.partial(act, u % 2))

    @pl.when(i >= 2)
    def _():
        copies(i - 2, par, lambda prio, c: c.wait())

    def one_hot_rows(u, r_lo, r_hi):
        r_io = (r_lo + lax.broadcasted_iota(jnp.int32, (r_hi - r_lo, tm), 0)).astype(F32)
        d1 = route_t_ref[u, 0:1, :]
        d2 = route_t_ref[u, 1:2, :]
        return jnp.where((r_io == d1) | (r_io == d2), 1.0, 0.0).astype(BF16)

    def augmented(u):
        rs = slice(u * tm, (u + 1) * tm)
        comb = comb_ref[rs, :]
        c_hi = comb.astype(BF16).astype(F32)
        c_pack = (c_hi + pltpu.roll(comb - c_hi, N_EXPERTS, 1)).astype(BF16)
        return jnp.concatenate([xb_ref[rs, :], c_pack], axis=1)

    def chain(u):
        p_mat = one_hot_rows(u, 0, DISPATCH_MAIN_ROWS)
        x_aug = augmented(u)
        yield
        cbuf[par * n_u + u, 0:DISPATCH_MAIN_ROWS, :] = _dot(p_mat, x_aug).astype(BF16)

    _interleave([chain(u) for u in range(n_u)])

    for u in range(n_u):
        sub = i * n_u + u
        used_rows = nch_ref[2 * sub] * BIG_CHUNK + nch_ref[2 * sub + 1] * SEG_ALIGN

        @pl.when(used_rows > DISPATCH_MAIN_ROWS)
        def _(u=u):
            cbuf[par * n_u + u, DISPATCH_MAIN_ROWS:rows, :] = _dot(
                one_hot_rows(u, DISPATCH_MAIN_ROWS, rows), augmented(u)).astype(BF16)

    copies(i, par, lambda prio, c: c.start(priority=prio))

    @pl.when(i == n_steps - 1)
    def _():
        if n_steps >= 2:
            copies(i - 1, 1 - par, lambda prio, c: c.wait())
        copies(i, par, lambda prio, c: c.wait())
        zbuf[...] = jnp.zeros_like(zbuf)

        def tails(act):
            def body(e, carry):
                st = tail_ref[e]
                n = tail_ref[N_EXPERTS + e]
                off = jnp.int32(0)
                size = zbuf.shape[0]
                while size >= SEG_ALIGN:
                    bit = (n & size) != 0

                    @pl.when(bit)
                    def _(size=size, off=off):
                        act(pltpu.make_async_copy(
                            zbuf.at[pl.ds(0, size), :],
                            xs_ref.at[pl.ds(pl.multiple_of(st + off, SEG_ALIGN), size), :], zsem))
                    off = off + jnp.where(bit, size, 0)
                    size //= 2
                return carry
            lax.fori_loop(0, N_EXPERTS, body, 0)
        tails(lambda c: c.start())
        tails(lambda c: c.wait())


def _ffn_kernel(texp_ref, nused_ref, xs_ref, wg_ref, wu_ref, wd_ref, ys_ref, wg_b, wu_b, wd_b, xbuf, xsem):
    t = pl.program_id(0)
    e = texp_ref[t]
    n_used = nused_ref[0]

    def fetch(u):
        slot = lax.rem(u, XS_DEPTH)
        return pltpu.make_async_copy(xs_ref.at[pl.ds(pl.multiple_of(u * FFN_TILE, FFN_TILE), FFN_TILE), :],
                                     xbuf.at[slot], xsem.at[slot])

    @pl.when(t == 0)
    def _():
        for u in range(XS_DEPTH - 1):
            @pl.when(u < n_used)
            def _(u=u):
                fetch(jnp.int32(u)).start()

    @pl.when(t + (XS_DEPTH - 1) < n_used)
    def _():
        fetch(t + (XS_DEPTH - 1)).start()

    @pl.when((t == 0) | (e != texp_ref[jnp.maximum(t - 1, 0)]))
    def _():
        wg_b[...] = wg_ref[0].astype(BF16)
        wu_b[...] = wu_ref[0].astype(BF16)
        wd_b[...] = wd_ref[0].astype(BF16)

    @pl.when(t < n_used)
    def _():
        fetch(t).wait()
        slot = lax.rem(t, XS_DEPTH)

        def chain(ci):
            rs = slice(ci * FFN_CHAIN, (ci + 1) * FFN_CHAIN)
            xrow = xbuf[slot, rs, 0:D]
            cw = xbuf[slot, rs, D:D + LANE].astype(F32)
            lane = lax.broadcasted_iota(jnp.int32, cw.shape, 1)
            c = jnp.sum(jnp.where((lane == e) | (lane == e + N_EXPERTS), cw, 0.0), axis=1, keepdims=True)
            a = _dot(xrow, wg_b[...])
            b = _dot(xrow, wu_b[...])
            yield
            hid = ((a * jax.nn.sigmoid(a)) * b * c).astype(BF16)
            yield
            ys_ref[rs, :] = _dot(hid, wd_b[...]).astype(ys_ref.dtype)

        _interleave([chain(ci) for ci in range(FFN_TILE // FFN_CHAIN)])


def _combine_kernel(nch_ref, tab_ref, route_ref, h2_ref, g_ref, b_ref, ys_ref, out_ref,
                    ybuf, sem, *, n_steps, rows):
    i = pl.program_id(0)
    par = lax.rem(i, 2)
    n_u = MOE_SUBS_PER_STEP
    tm = TOK_TILE

    def copies(step, par_, act):
        for u in range(n_u):
            slot_ = par_ * n_u + u

            def mk(lo, go, size, slot_=slot_):
                return pltpu.make_async_copy(ys_ref.at[pl.ds(go, size), :],
                                             ybuf.at[slot_, pl.ds(lo, size), :], sem.at[slot_])
            _chunk_copies(step * n_u + u, nch_ref, tab_ref, mk, functools.partial(act, u % 2))

    @pl.when(i == 0)
    def _():
        ybuf[...] = jnp.zeros_like(ybuf)
        copies(0, 0, lambda prio, c: c.start(priority=prio))

    @pl.when(i + 1 < n_steps)
    def _():
        copies(i + 1, 1 - par, lambda prio, c: c.start(priority=prio))

    copies(i, par, lambda prio, c: c.wait())

    def chain(u):
        rs = slice(u * tm, (u + 1) * tm)
        r_io = lax.broadcasted_iota(jnp.int32, (tm, rows), 1).astype(F32)
        d1 = route_ref[rs, 0:1]
        d2 = route_ref[rs, 1:2]
        p_t = jnp.where((r_io == d1) | (r_io == d2), 1.0, 0.0).astype(BF16)
        yield
        ff = _dot(p_t, ybuf[par * n_u + u])
        yield
        out_ref[rs, :] = _ln(DN_ALPHA * h2_ref[rs, :] + ff, g_ref[...], b_ref[...])

    _interleave([chain(u) for u in range(n_u)])


def _bias_kernel(tbl_ref, bkt_ref, out_ref):
    h = pl.program_id(0)
    far = tbl_ref[h, REL_BUCKETS - 1]
    for which in range(2):
        bk = bkt_ref[which]
        acc = jnp.where(bk < 0, NEG, 0.0)
        for kk in range(REL_BUCKETS):
            acc = jnp.where(bk == kk, (tbl_ref[h, kk] - far) * LOG2E, acc)
        out_ref[which, 0] = acc


def _rel_bucket_table(dist):
    max_exact = REL_BUCKETS // 2
    d = jnp.maximum(dist, 0)
    large = max_exact + (jnp.log(jnp.maximum(d, 1).astype(F32) / max_exact)
                         / math.log(REL_MAX_DIST / max_exact) * (REL_BUCKETS - max_exact)).astype(jnp.int32)
    large = jnp.minimum(large, REL_BUCKETS - 1)
    return jnp.where(d < max_exact, d, large)


def _const_spec(shape):
    nd = len(shape)
    return pl.BlockSpec(shape, lambda *_: (0,) * nd)


def kernel(x, mem, ln_in_g, ln_in_b, rel_bias, w_in, b_gate, w_pool_grp, pool_scale, w_pool_up, w_attn_up,
           w_mix_out, ln1_g, ln1_b, w_mq, w_mk, w_mv, w_mo, ln2_g, ln2_b, w_coarse, b_coarse, w_fine, b_fine,
           w_gate, w_up, w_down, ln3_g, ln3_b):
    B, S, _ = x.shape
    assert S == N_BLK * BLK and w_in.shape[0] == 1
    M = mem.shape[1]
    T = B * S
    tm = TOK_TILE

    wi = w_in[0]
    w_u = wi[:, 0:POOL_W]
    w_q = wi[:, POOL_W:POOL_W + ATTN_W] * (HEAD_DIM ** -0.5 * LOG2E)
    w_k = wi[:, POOL_W + ATTN_W:POOL_W + 2 * ATTN_W]
    w_v = wi[:, POOL_W + 2 * ATTN_W:POOL_W + 3 * ATTN_W]
    w_gl = wi[:, POOL_W + 3 * ATTN_W:]

    w1 = jnp.concatenate([w_u, w_q, w_k, w_v], axis=1).astype(BF16)
    row2 = lambda a: a.reshape(1, -1)

    iq = jnp.arange(BLK, dtype=jnp.int32)[:, None]
    ik = jnp.arange(BLK, dtype=jnp.int32)[None, :]
    d_own = iq - ik
    bkt = jnp.stack([jnp.where(d_own >= 0, _rel_bucket_table(d_own), -1), _rel_bucket_table(d_own + BLK)])
    t_bias = pl.pallas_call(
        _bias_kernel,
        grid=(N_HEADS,),
        in_specs=[pl.BlockSpec(memory_space=pltpu.SMEM), _const_spec((2, BLK, BLK))],
        out_specs=pl.BlockSpec((2, 1, BLK, BLK), lambda h: (0, h, 0, 0)),
        out_shape=jax.ShapeDtypeStruct((2, N_HEADS, BLK, BLK), F32),
        name="relbias_tiles",
    )(rel_bias.T, bkt)

    n_w1 = w1.shape[1]
    tmp = MERGE_TILE
    ypool, q_aug, k_aug, v_p = pl.pallas_call(
        functools.partial(_proj_kernel, tm=tmp),
        grid=(B, S // tmp),
        in_specs=[
            pl.BlockSpec((1, tmp, D), lambda b, s: (b, s, 0)),
            _const_spec((1, D)), _const_spec((1, D)),
            _const_spec((D, n_w1)),
            _const_spec((len(POOL_WINDOWS), LANE, LANE)),
            _const_spec((1, POOL_W)),
        ],
        out_specs=[
            pl.BlockSpec((1, tmp, POOL_W), lambda b, s: (b, s, 0)),
            pl.BlockSpec((1, N_HEADS, tmp, LANE), lambda b, s: (b, 0, s, 0)),
            pl.BlockSpec((1, N_HEADS, tmp, LANE), lambda b, s: (b, 0, s, 0)),
            pl.BlockSpec((1, ATTN_W // LANE, tmp, LANE), lambda b, s: (b, 0, s, 0)),
        ],
        out_shape=[
            jax.ShapeDtypeStruct((B, S, POOL_W), BF16),
            jax.ShapeDtypeStruct((B, N_HEADS, S, LANE), BF16),
            jax.ShapeDtypeStruct((B, N_HEADS, S, LANE), BF16),
            jax.ShapeDtypeStruct((B, ATTN_W // LANE, S, LANE), BF16),
        ],
        scratch_shapes=[pltpu.VMEM((HALO + tmp, POOL_W), F32), pltpu.VMEM((LANE, ATTN_W), F32)],
        compiler_params=pltpu.CompilerParams(dimension_semantics=("arbitrary", "arbitrary"),
                                             vmem_limit_bytes=VMEM_LIMIT),
        name="proj_pool_gate",
    )(x, row2(ln_in_g), row2(ln_in_b), w1, w_pool_grp[0].astype(BF16), row2(pool_scale[0]))

    o_attn = pl.pallas_call(
        _attn_kernel,
        grid=(B, N_BLK),
        in_specs=[
            pl.BlockSpec((1, N_HEADS, BLK, LANE), lambda b, j: (b, 0, j, 0)),
            pl.BlockSpec((1, N_HEADS, S, LANE), lambda b, j: (b, 0, 0, 0)),
            pl.BlockSpec((1, ATTN_W // LANE, S, LANE), lambda b, j: (b, 0, 0, 0)),
            _const_spec((2, N_HEADS, BLK, BLK)),
        ],
        out_specs=pl.BlockSpec((1, ATTN_W // LANE, BLK, LANE), lambda b, j: (b, 0, j, 0)),
        out_shape=jax.ShapeDtypeStruct((B, ATTN_W // LANE, S, LANE), BF16),
        compiler_params=pltpu.CompilerParams(dimension_semantics=("arbitrary", "arbitrary"),
                                             vmem_limit_bytes=VMEM_LIMIT),
        name="moba_attn",
    )(q_aug, k_aug, v_p, t_bias)

    kmem, vmem = pl.pallas_call(
        _memkv_kernel,
        grid=(B,),
        in_specs=[pl.BlockSpec((1, M, D), lambda b: (b, 0, 0)),
                  _const_spec((D, MEM_W)), _const_spec((D, MEM_W))],
        out_specs=[pl.BlockSpec((1, M, MEM_W), lambda b: (b, 0, 0)),
                   pl.BlockSpec((1, M, MEM_W), lambda b: (b, 0, 0))],
        out_shape=[jax.ShapeDtypeStruct((B, M, MEM_W), BF16)] * 2,
        compiler_params=pltpu.CompilerParams(dimension_semantics=("arbitrary",)),
        name="mem_kv",
    )(mem, w_mk[0].astype(BF16), w_mv[0].astype(BF16))

    w_r = jnp.concatenate([
        w_fine[0].reshape(D, N_EXPERTS),
        jnp.repeat(w_coarse[0], EPG, axis=1),
        jnp.zeros((D, LANE - 2 * N_EXPERTS), F32)], axis=1)
    b_r = jnp.concatenate([
        b_fine[0].reshape(N_EXPERTS), jnp.repeat(b_coarse[0], EPG),
        jnp.zeros((LANE - 2 * N_EXPERTS,), F32)]).reshape(1, LANE)
    w_r_hi = w_r.astype(BF16)
    w_r_lo = (w_r - w_r_hi.astype(F32)).astype(BF16)
    w_r_cat = jnp.concatenate([w_r_hi, w_r_lo], axis=1)

    n_sub = T // tm
    tmm = MERGE_TILE
    per = tmm // tm
    sub_idx = lambda b, s: b * (S // tmm) + s
    h2, h2b, comb, route, route_t, seg_cnt = pl.pallas_call(
        functools.partial(_merge_kernel, tm=tmm),
        grid=(B, S // tmm),
        in_specs=[
            pl.BlockSpec((1, tmm, D), lambda b, s: (b, s, 0)),
            pl.BlockSpec((1, tmm, POOL_W), lambda b, s: (b, s, 0)),
            pl.BlockSpec((1, ATTN_W // LANE, tmm, LANE), lambda b, s: (b, 0, s, 0)),
            pl.BlockSpec((1, M, MEM_W), lambda b, s: (b, 0, 0)),
            pl.BlockSpec((1, M, MEM_W), lambda b, s: (b, 0, 0)),
            _const_spec((1, D)), _const_spec((1, D)),
            _const_spec((D, 2 * D)), _const_spec((1, 2 * D)),
            _const_spec((POOL_W, D)), _const_spec((ATTN_W, D)), _const_spec((D, D)),
            _const_spec((1, D)), _const_spec((1, D)),
            _const_spec((D, MEM_W)), _const_spec((MEM_W, D)),
            _const_spec((1, D)), _const_spec((1, D)),
            _const_spec((D, 2 * LANE)), _const_spec((1, LANE)),
        ],
        out_specs=[
            pl.BlockSpec((1, tmm, D), lambda b, s: (b, s, 0)),
            pl.BlockSpec((1, tmm, D), lambda b, s: (b, s, 0)),
            pl.BlockSpec((1, tmm, LANE), lambda b, s: (b, s, 0)),
            pl.BlockSpec((1, tmm, LANE), lambda b, s: (b, s, 0)),
            pl.BlockSpec((per, 8, tm), lambda b, s: (sub_idx(b, s), 0, 0)),
            pl.BlockSpec((per, 8, LANE), lambda b, s: (sub_idx(b, s), 0, 0)),
        ],
        out_shape=[
            jax.ShapeDtypeStruct((B, S, D), F32),
            jax.ShapeDtypeStruct((B, S, D), BF16),
            jax.ShapeDtypeStruct((B, S, LANE), F32),
            jax.ShapeDtypeStruct((B, S, LANE), F32),
            jax.ShapeDtypeStruct((n_sub, 8, tm), F32),
            jax.ShapeDtypeStruct((n_sub, 8, LANE), F32),
        ],
        compiler_params=pltpu.CompilerParams(dimension_semantics=("arbitrary", "arbitrary"),
                                             vmem_limit_bytes=VMEM_LIMIT),
        name="merge_memattn_router",
    )(x, ypool, o_attn, kmem, vmem,
      row2(ln_in_g), row2(ln_in_b), w_gl.astype(BF16), row2(b_gate[0]),
      w_pool_up[0].astype(BF16), w_attn_up[0].astype(BF16), w_mix_out[0].astype(BF16),
      row2(ln1_g[0]), row2(ln1_b[0]),
      (w_mq[0] * (MEM_HD ** -0.5 * LOG2E)).astype(BF16), w_mo[0].astype(BF16),
      row2(ln2_g[0]), row2(ln2_b[0]),
      w_r_cat, b_r)

    pcs = seg_cnt[:, 0, :N_EXPERTS].astype(jnp.int32)
    tot = jnp.sum(pcs, axis=0)
    cap = ((tot + FFN_TILE - 1) // FFN_TILE) * FFN_TILE
    ends = jnp.cumsum(cap)
    base = ends - cap
    gs = base[None, :] + jnp.cumsum(pcs, axis=0) - pcs
    ls = jnp.cumsum(pcs, axis=1) - pcs
    n_sorted = n_sub * COMPACT_ROWS + N_EXPERTS * FFN_TILE
    n_ffn_tiles = n_sorted // FFN_TILE
    n_used = (ends[-1] // FFN_TILE).astype(jnp.int32)
    tile_row = jnp.arange(n_ffn_tiles, dtype=jnp.int32) * FFN_TILE
    tile_exp = jnp.sum(jnp.minimum(tile_row, ends[-1] - 1)[:, None] >= ends[None, :], axis=1).astype(jnp.int32)
    tails = jnp.concatenate([base + tot, cap - tot]).astype(jnp.int32)
    def kth_of(counts, n_slots, seg_off):
        end = jnp.cumsum(counts, axis=1)
        k = jnp.arange(n_slots, dtype=jnp.int32)
        exp_k = jnp.sum(k[None, :, None] >= end[:, None, :], axis=2)
        hot = exp_k[:, :, None] == jnp.arange(N_EXPERTS)[None, None, :]
        pick = lambda tbl: jnp.sum(jnp.where(hot, tbl[:, None, :], 0), axis=2)
        within = k[None, :] - pick(end - counts)
        return pick(ls + seg_off), pick(gs + seg_off), within

    n_big = pcs // BIG_CHUNK
    n_end = (pcs // SEG_ALIGN) % 2
    lo_b, go_b, m_b = kth_of(n_big, MAX_BIG, 0)
    lo_e, go_e, _ = kth_of(n_end, N_EXPERTS, pcs - SEG_ALIGN)
    chunk_tab = jnp.concatenate([lo_b + BIG_CHUNK * m_b, go_b + BIG_CHUNK * m_b, lo_e, go_e],
                                axis=1).astype(jnp.int32).reshape(-1)
    n_chunks = jnp.stack([jnp.sum(n_big, axis=1), jnp.sum(n_end, axis=1)], axis=1).astype(jnp.int32).reshape(-1)

    aug_w = D + LANE
    n_u = MOE_SUBS_PER_STEP
    n_steps = n_sub // n_u
    x_sorted = pl.pallas_call(
        functools.partial(_dispatch_kernel, n_steps=n_steps, rows=COMPACT_ROWS),
        grid_spec=pltpu.PrefetchScalarGridSpec(
            num_scalar_prefetch=3,
            grid=(n_steps,),
            in_specs=[
                pl.BlockSpec((n_u * tm, D), lambda i, *_: (i, 0)),
                pl.BlockSpec((n_u * tm, LANE), lambda i, *_: (i, 0)),
                pl.BlockSpec((n_u, 8, tm), lambda i, *_: (i, 0, 0)),
            ],
            out_specs=pl.BlockSpec(memory_space=pl.ANY),
            scratch_shapes=[
                pltpu.VMEM((2 * n_u, COMPACT_ROWS, aug_w), BF16),
                pltpu.VMEM((FFN_TILE // 2, aug_w), BF16),
                pltpu.SemaphoreType.DMA((2 * n_u,)),
                pltpu.SemaphoreType.DMA(()),
            ],
        ),
        out_shape=jax.ShapeDtypeStruct((n_sorted, aug_w), BF16),
        compiler_params=pltpu.CompilerParams(dimension_semantics=("arbitrary",), vmem_limit_bytes=VMEM_LIMIT),
        name="moe_dispatch",
    )(n_chunks, chunk_tab, tails, h2b.reshape(T, D), comb.reshape(T, LANE), route_t)

    used_tile = lambda t, texp, nused: (jnp.minimum(t, nused[0] - 1), 0)
    y_sorted = pl.pallas_call(
        _ffn_kernel,
        grid_spec=pltpu.PrefetchScalarGridSpec(
            num_scalar_prefetch=2,
            grid=(n_ffn_tiles,),
            in_specs=[
                pl.BlockSpec(memory_space=pl.ANY),
                pl.BlockSpec((1, D, FF), lambda t, texp, nused: (texp[t], 0, 0)),
                pl.BlockSpec((1, D, FF), lambda t, texp, nused: (texp[t], 0, 0)),
                pl.BlockSpec((1, FF, D), lambda t, texp, nused: (texp[t], 0, 0)),
            ],
            out_specs=pl.BlockSpec((FFN_TILE, D), used_tile),
            scratch_shapes=[pltpu.VMEM((D, FF), BF16), pltpu.VMEM((D, FF), BF16), pltpu.VMEM((FF, D), BF16),
                            pltpu.VMEM((XS_DEPTH, FFN_TILE, aug_w), BF16), pltpu.SemaphoreType.DMA((XS_DEPTH,))],
        ),
        out_shape=jax.ShapeDtypeStruct((n_sorted, D), BF16),
        compiler_params=pltpu.CompilerParams(dimension_semantics=("arbitrary",), vmem_limit_bytes=VMEM_LIMIT),
        name="moe_expert_ffn",
    )(tile_exp, n_used.reshape(1), x_sorted, w_gate[0], w_up[0], w_down[0])

    out = pl.pallas_call(
        functools.partial(_combine_kernel, n_steps=n_steps, rows=COMPACT_ROWS),
        grid_spec=pltpu.PrefetchScalarGridSpec(
            num_scalar_prefetch=2,
            grid=(n_steps,),
            in_specs=[
                pl.BlockSpec((n_u * tm, LANE), lambda i, *_: (i, 0)),
                pl.BlockSpec((n_u * tm, D), lambda i, *_: (i, 0)),
                pl.BlockSpec((1, D), lambda i, *_: (0, 0)),
                pl.BlockSpec((1, D), lambda i, *_: (0, 0)),
                pl.BlockSpec(memory_space=pl.ANY),
            ],
            out_specs=pl.BlockSpec((n_u * tm, D), lambda i, *_: (i, 0)),
            scratch_shapes=[
                pltpu.VMEM((2 * n_u, COMPACT_ROWS, D), BF16),
                pltpu.SemaphoreType.DMA((2 * n_u,)),
            ],
        ),
        out_shape=jax.ShapeDtypeStruct((T, D), F32),
        compiler_params=pltpu.CompilerParams(dimension_semantics=("arbitrary",), vmem_limit_bytes=VMEM_LIMIT),
        name="moe_combine_ln3",
    )(n_chunks, chunk_tab, route.reshape(T, LANE), h2.reshape(T, D), row2(ln3_g[0]), row2(ln3_b[0]), y_sorted)
    return out.reshape(B, S, D)
```
